```python
import jax, jax.numpy as jnp
from jax import lax
import numpy as np

D_MODEL = 4096
BATCH = 1
SEQ = 8192
DEPTH = 1

GRID_W = 64
CTX_LEN = 256
D_MIX = D_MODEL
D_RWKV = D_MIX // 2
D_HGRN = D_MIX - D_RWKV
RWKV_HEAD = 64
RWKV_HEADS = D_RWKV // RWKV_HEAD
HGRN_EXPAND = 128
HGRN_HEADS = D_HGRN // HGRN_EXPAND
HGRN_VAL = D_HGRN // HGRN_HEADS
D_DECAY_LORA = max(32, int(round(1.8 * D_RWKV ** 0.5 / 32)) * 32)
D_AAA_LORA = max(32, int(round(1.8 * D_RWKV ** 0.5 / 32)) * 32)
D_GATE_LORA = max(32, int(round(0.6 * D_RWKV ** 0.8 / 32)) * 32)
D_FF = 4 * D_MODEL
HGRN_CHUNK = 64
N_MOD = 6
NORM_EPS = 1e-6
GN_EPS = 64e-5
RWKV_SPLITS = (D_RWKV, D_RWKV, D_RWKV, 2 * D_DECAY_LORA, 2 * D_AAA_LORA, D_GATE_LORA)
RWKV_COLS = sum(RWKV_SPLITS)
HGRN_SPLITS = (D_HGRN, 2 * D_HGRN, D_HGRN, D_HGRN)
HGRN_COLS = sum(HGRN_SPLITS)
IN_COLS = RWKV_COLS + HGRN_COLS

kernel_name = "hybrid_rwkv7_hgrn2_prefix_dit_layer"


def _split(t, sizes):
    idx = np.cumsum(sizes)[:-1].tolist()
    return jnp.split(t, idx, axis=-1)


def _rms_norm(t, g):
    tf = t.astype(jnp.float32)
    tf = tf * lax.rsqrt(jnp.mean(tf * tf, axis=-1, keepdims=True) + NORM_EPS)
    return (tf * g).astype(t.dtype)


def _modulate(h, shift, scale):
    return h * (1 + scale) + shift


def _qshift_grid(t, rows):
    B, L, C = t.shape
    t5 = t.reshape(B, rows, GRID_W, C // 4, 4)
    tp = jnp.pad(t5, ((0, 0), (1, 1), (1, 1), (0, 0), (0, 0)))
    left = tp[:, 1:-1, :-2, :, 0]
    right = tp[:, 1:-1, 2:, :, 1]
    up = tp[:, :-2, 1:-1, :, 2]
    down = tp[:, 2:, 1:-1, :, 3]
    return jnp.stack([left, right, up, down], axis=-1).reshape(B, L, C)


def _qshift_seq(t):
    B, L, C = t.shape
    t4 = t.reshape(B, L, C // 4, 4)
    tp = jnp.pad(t4, ((0, 0), (1, 1), (0, 0), (0, 0)))
    prev, nxt = tp[:, :-2], tp[:, 2:]
    return jnp.stack([prev[..., 0], nxt[..., 1], prev[..., 2], nxt[..., 3]], axis=-1).reshape(B, L, C)


def _rwkv7_features(p, shifted, mu, w0, w2, a0, a2, k_k, k_a):
    m = p + mu * (shifted - p)
    B, L, _ = m.shape
    r, k, v, wl, al, gl = _split(m, RWKV_SPLITS)
    hd = lambda t: t.reshape(B, L, RWKV_HEADS, RWKV_HEAD)
    kkf = hd(k * k_k).astype(jnp.float32)
    kk = kkf * lax.rsqrt(jnp.sum(kkf * kkf, axis=-1, keepdims=True) + 1e-12)
    wd = jnp.einsum('bldr,drc->bldc', jnp.tanh(wl.reshape(B, L, 2, D_DECAY_LORA)), w2) + w0
    log_w = -jax.nn.softplus(-wd.astype(jnp.float32)) - 0.5
    decay = jnp.exp(-jnp.exp(log_w))
    a = jax.nn.sigmoid(jnp.einsum('bldr,drc->bldc', al.reshape(B, L, 2, D_AAA_LORA), a2) + a0)
    dirs = []
    for d in range(2):
        a_d = a[:, :, d]
        k_d = k * (1 + (a_d - 1) * k_a)
        dirs.append((hd(r), hd(decay[:, :, d]), hd(k_d), hd(v), -kk, kk * hd(a_d)))
    return dirs, (hd(r), hd(k), hd(v), gl)


def _rwkv7_scan(r, w, k, v, a, b, s0):
    def step(S, inp):
        r_t, w_t, k_t, v_t, a_t, b_t = inp
        sa = jnp.einsum('bhvk,bhk->bhv', S, a_t)
        S = S * w_t[:, :, None, :] + sa[..., None] * b_t[:, :, None, :] + v_t[..., None] * k_t[:, :, None, :]
        return S, jnp.einsum('bhvk,bhk->bhv', S, r_t)
    xs = tuple(jnp.moveaxis(t.astype(jnp.float32), 1, 0) for t in (r, w, k, v, a, b))
    S, y = lax.scan(step, s0, xs)
    return jnp.moveaxis(y, 0, 1), S


def _rwkv7_out(y, r, k, v, gl, r_k, ln_g, ln_b, g2):
    B, L, H, N = y.shape
    mean = jnp.mean(y, axis=-1, keepdims=True)
    var = jnp.mean(jnp.square(y - mean), axis=-1, keepdims=True)
    yn = ((y - mean) * lax.rsqrt(var + GN_EPS)).reshape(B, L, D_RWKV) * ln_g + ln_b
    bonus = (jnp.sum(r * k * r_k, axis=-1, keepdims=True) * v).reshape(B, L, D_RWKV)
    gate = jax.nn.sigmoid(gl) @ g2
    return ((yn + bonus) * gate).astype(v.dtype)


def _hgrn2_features(p, lb):
    B, L, _ = p.shape
    q, f, i, g = _split(p, HGRN_SPLITS)
    hd = lambda t: t.reshape(B, L, HGRN_HEADS, -1)
    qh = hd(jax.nn.silu(q))
    fd = lb + (1 - lb) * jax.nn.sigmoid(f.reshape(B, L, 2, D_HGRN).astype(jnp.float32))
    dirs = [(qh, hd(1 - fd[:, :, d]), hd(i), hd(jnp.log(fd[:, :, d]))) for d in range(2)]
    return dirs, g


def _hgrn2_chunk_scan(q, k, v, log_f, s0):
    B, L, H, K = q.shape
    V = v.shape[-1]
    n = L // HGRN_CHUNK
    def chunks(t):
        return t.astype(jnp.float32).reshape(B, n, HGRN_CHUNK, H, t.shape[-1]).transpose(1, 0, 3, 2, 4)
    lower_tri = jnp.tril(jnp.ones((HGRN_CHUNK, HGRN_CHUNK), dtype=bool))[:, :, None]
    def step(S, inp):
        qc, kc, vc, gc = inp
        b = jnp.cumsum(gc, axis=2)
        rel = jnp.exp(jnp.where(lower_tri, b[:, :, :, None, :] - b[:, :, None, :, :], -jnp.inf))
        scores = jnp.einsum('bhtk,bhsk,bhtsk->bhts', qc, kc, rel)
        o = jnp.einsum('bhts,bhsv->bhtv', scores, vc) + jnp.einsum('bhtk,bhkv->bhtv', qc * jnp.exp(b), S)
        b_end = b[:, :, -1:, :]
        S = S * jnp.exp(b_end)[:, :, 0, :, None] + jnp.einsum('bhsk,bhsv->bhkv', kc * jnp.exp(b_end - b), vc)
        return S, o
    S, o = lax.scan(step, s0, (chunks(q), chunks(k), chunks(v), chunks(log_f)))
    return o.transpose(1, 0, 3, 2, 4).reshape(B, L, H, V), S


def _hgrn2_out(o, g, norm_g):
    B, L, H, V = o.shape
    on = o * lax.rsqrt(jnp.mean(o * o, axis=-1, keepdims=True) + NORM_EPS) * norm_g
    return (on.reshape(B, L, D_HGRN) * jax.nn.silu(g)).astype(g.dtype)


def _bidir_with_prefix(scan_fn, ctx_dirs, lat_dirs, s0):
    y_lat, y_ctx = [], []
    for d, rev in enumerate((False, True)):
        cin, lin = ctx_dirs[d], lat_dirs[d]
        if rev:
            cin = tuple(jnp.flip(t, 1) for t in cin)
            lin = tuple(jnp.flip(t, 1) for t in lin)
        yc, s_ctx = scan_fn(*cin, s0)
        yl, _ = scan_fn(*lin, s_ctx)
        if rev:
            yc, yl = jnp.flip(yc, 1), jnp.flip(yl, 1)
        y_lat.append(yl)
        y_ctx.append(yc)
    return y_lat[0] + y_lat[1], y_ctx[0] + y_ctx[1]


def _sq_relu_mlp(h, w1, w2):
    return jnp.square(jax.nn.relu(h @ w1)) @ w2


def setup_inputs(seed: int = 0) -> dict:
    key = jax.random.key(seed)
    ks = iter(jax.random.split(key, 40))
    nrm = lambda shape, scale: jax.random.normal(next(ks), shape, jnp.float32) * scale
    gain = lambda shape: 1.0 + nrm(shape, 0.02)
    unif = lambda shape, lo, hi: jax.random.uniform(next(ks), shape, jnp.float32, lo, hi)
    return {
        "x": nrm((BATCH, SEQ, D_MODEL), 1.0),
        "c": nrm((BATCH, D_MODEL), 1.0),
        "ctx": nrm((BATCH, CTX_LEN, D_MODEL), 1.0),
        "c_ctx": nrm((D_MODEL,), 1.0),
        "w_ada": nrm((DEPTH, D_MODEL, N_MOD * D_MODEL), 0.3 * D_MODEL ** -0.5),
        "b_ada": nrm((DEPTH, N_MOD * D_MODEL), 0.01),
        "g_mix_pre": gain((DEPTH, D_MODEL)),
        "g_mix_post": gain((DEPTH, D_MODEL)),
        "g_ffn_pre": gain((DEPTH, D_MODEL)),
        "g_ffn_post": gain((DEPTH, D_MODEL)),
        "w_in": nrm((DEPTH, D_MODEL, IN_COLS), D_MODEL ** -0.5),
        "mu_shift": unif((DEPTH, RWKV_COLS), 0.2, 0.8),
        "w0": unif((DEPTH, 2, D_RWKV), -6.0, 0.0),
        "w2": nrm((DEPTH, 2, D_DECAY_LORA, D_RWKV), 0.1 * D_DECAY_LORA ** -0.5),
        "a0": nrm((DEPTH, 2, D_RWKV), 0.1),
        "a2": nrm((DEPTH, 2, D_AAA_LORA, D_RWKV), 0.1 * D_AAA_LORA ** -0.5),
        "g2": nrm((DEPTH, D_GATE_LORA, D_RWKV), D_GATE_LORA ** -0.5),
        "k_k": 0.85 + nrm((DEPTH, D_RWKV), 0.02),
        "k_a": gain((DEPTH, D_RWKV)),
        "r_k": nrm((DEPTH, RWKV_HEADS, RWKV_HEAD), 0.1),
        "ln_x_g": gain((DEPTH, D_RWKV)),
        "ln_x_b": nrm((DEPTH, D_RWKV), 0.01),
        "hgrn_lb_logits": nrm((DEPTH + 1, 2, D_HGRN), 0.1),
        "hgrn_norm_g": gain((DEPTH, HGRN_VAL)),
        "w_out": nrm((DEPTH, D_MIX, D_MODEL), D_MIX ** -0.5),
        "w_ff1": nrm((DEPTH, D_MODEL, D_FF), D_MODEL ** -0.5),
        "w_ff2": nrm((DEPTH, D_FF, D_MODEL), D_FF ** -0.5),
    }


def reference(x, c, ctx, c_ctx, w_ada, b_ada, g_mix_pre, g_mix_post, g_ffn_pre, g_ffn_post, w_in,
              mu_shift, w0, w2, a0, a2, g2, k_k, k_a, r_k, ln_x_g, ln_x_b, hgrn_lb_logits,
              hgrn_norm_g, w_out, w_ff1, w_ff2):
    B = x.shape[0]
    rows = x.shape[1] // GRID_W
    lower_bounds = jnp.cumsum(jax.nn.softmax(hgrn_lb_logits.astype(jnp.float32), axis=0), axis=0)
    for l in range(DEPTH):
        last = l == DEPTH - 1
        mod_x = jax.nn.silu(c) @ w_ada[l] + b_ada[l]
        mod_c = jax.nn.silu(c_ctx) @ w_ada[l] + b_ada[l]
        sh1, sc1, gt1, sh2, sc2, gt2 = jnp.split(mod_x[:, None, :], N_MOD, axis=-1)
        csh1, csc1, cgt1, csh2, csc2, cgt2 = jnp.split(mod_c, N_MOD, axis=-1)

        px = _modulate(_rms_norm(x, g_mix_pre[l]), sh1, sc1) @ w_in[l]
        pc = _modulate(_rms_norm(ctx, g_mix_pre[l]), csh1, csc1) @ w_in[l]
        pax, phx = px[..., :RWKV_COLS], px[..., RWKV_COLS:]
        pac, phc = pc[..., :RWKV_COLS], pc[..., RWKV_COLS:]

        rw = (mu_shift[l], w0[l], w2[l], a0[l], a2[l], k_k[l], k_a[l])
        dirs_x, aux_x = _rwkv7_features(pax, _qshift_grid(pax, rows), *rw)
        dirs_c, aux_c = _rwkv7_features(pac, _qshift_seq(pac), *rw)
        s0_rwkv = jnp.zeros((B, RWKV_HEADS, RWKV_HEAD, RWKV_HEAD), jnp.float32)
        ya_x, ya_c = _bidir_with_prefix(_rwkv7_scan, dirs_c, dirs_x, s0_rwkv)

        hdirs_x, hg_x = _hgrn2_features(phx, lower_bounds[l])
        hdirs_c, hg_c = _hgrn2_features(phc, lower_bounds[l])
        s0_hgrn = jnp.zeros((B, HGRN_HEADS, HGRN_EXPAND, HGRN_VAL), jnp.float32)
        yh_x, yh_c = _bidir_with_prefix(_hgrn2_chunk_scan, hdirs_c, hdirs_x, s0_hgrn)

        rw_out = (r_k[l], ln_x_g[l], ln_x_b[l], g2[l])
        ux = jnp.concatenate([_rwkv7_out(ya_x, *aux_x, *rw_out),
                              _hgrn2_out(yh_x, hg_x, hgrn_norm_g[l])], axis=-1) @ w_out[l]
        x = x + gt1 * _rms_norm(ux, g_mix_post[l])

        hx = _modulate(_rms_norm(x, g_ffn_pre[l]), sh2, sc2)
        x = x + gt2 * _rms_norm(_sq_relu_mlp(hx, w_ff1[l], w_ff2[l]), g_ffn_post[l])

        if not last:
            uc = jnp.concatenate([_rwkv7_out(ya_c, *aux_c, *rw_out),
                                  _hgrn2_out(yh_c, hg_c, hgrn_norm_g[l])], axis=-1) @ w_out[l]
            ctx = ctx + cgt1 * _rms_norm(uc, g_mix_post[l])
            hc = _modulate(_rms_norm(ctx, g_ffn_pre[l]), csh2, csc2)
            ctx = ctx + cgt2 * _rms_norm(_sq_relu_mlp(hc, w_ff1[l], w_ff2[l]), g_ffn_post[l])
    return x
```

```python
import functools

import jax
import jax.numpy as jnp
import numpy as np
from jax import lax
from jax.experimental import pallas as pl
from jax.experimental.pallas import tpu as pltpu

F32 = jnp.float32
BF16 = jnp.bfloat16
HI = lax.Precision.HIGHEST

LANE = 128
GRID_W = 64
CHUNK = GRID_W
RWKV_HEAD = 64
HGRN_HEAD = 128
SUB = 16
GROUP_W = 256
NORM_EPS = 1e-6
GN_EPS = 64e-5
EXP_M05 = float(np.exp(-0.5))
VMEM_LIMIT = 56 * 1024 * 1024


def _pick(n, target, unit=LANE):
    best = None
    for m in range(unit, min(n, target) + 1, unit):
        if n % m == 0:
            best = m
    return best if best is not None else n


def _sigmoid(z):
    return 1.0 / (1.0 + jnp.exp(-z))


def _dot(a, b, prec=HI):
    return jnp.dot(a, b, precision=prec, preferred_element_type=F32)


def _dot_nt(a, b, prec=HI):
    return lax.dot_general(a, b, (((1,), (1,)), ((), ())), precision=prec, preferred_element_type=F32)


def _dot_tn(a, b, prec=HI):
    return lax.dot_general(a, b, (((0,), (0,)), ((), ())), precision=prec, preferred_element_type=F32)


def _params(sem, vmem=None):
    return pltpu.CompilerParams(dimension_semantics=sem, vmem_limit_bytes=vmem)


def _ada_kernel(c_ref, w_ref, b_ref, o_ref):
    cv = c_ref[...]
    o_ref[...] = _dot(cv * _sigmoid(cv), w_ref[...]) + b_ref[...]


def _ada(cvec, w, b):
    rows, d = cvec.shape
    n = w.shape[1]
    tn = _pick(n, 512)
    return pl.pallas_call(
        _ada_kernel,
        grid=(n // tn,),
        in_specs=[pl.BlockSpec((rows, d), lambda j: (0, 0)),
                  pl.BlockSpec((d, tn), lambda j: (0, j)),
                  pl.BlockSpec((1, tn), lambda j: (0, j))],
        out_specs=pl.BlockSpec((rows, tn), lambda j: (0, j)),
        out_shape=jax.ShapeDtypeStruct((rows, n), F32),
        compiler_params=_params(("arbitrary",), VMEM_LIMIT),
        name="ada_mod",
    )(cvec, w, b.reshape(1, n))


def _prep_kernel(ctx_ref, x_ref, g_ref, mod_ref, o_ref, *, ncb, d):
    is_ctx = pl.program_id(0) < ncb
    rows = jnp.where(is_ctx, ctx_ref[...], x_ref[...])
    ms = jnp.mean(rows * rows, axis=-1, keepdims=True)
    hn = rows * lax.rsqrt(ms + NORM_EPS) * g_ref[...]
    shift = jnp.where(is_ctx, mod_ref[1:2, 0:d], mod_ref[0:1, 0:d])
    scale = jnp.where(is_ctx, mod_ref[1:2, d:2 * d], mod_ref[0:1, d:2 * d])
    o_ref[...] = (hn * (1.0 + scale) + shift).astype(o_ref.dtype)


def _prep(ctx2, x2, g, mod):
    lc, d = ctx2.shape
    l = x2.shape[0]
    tb = _pick(int(np.gcd(lc, l)), 256, 8)
    ncb = lc // tb
    return pl.pallas_call(
        functools.partial(_prep_kernel, ncb=ncb, d=d),
        grid=((lc + l) // tb,),
        in_specs=[pl.BlockSpec((tb, d), lambda i: (jnp.minimum(i, ncb - 1), 0)),
                  pl.BlockSpec((tb, d), lambda i: (jnp.maximum(i - ncb, 0), 0)),
                  pl.BlockSpec((1, d), lambda i: (0, 0)),
                  pl.BlockSpec(mod.shape, lambda i: (0, 0))],
        out_specs=pl.BlockSpec((tb, d), lambda i: (i, 0)),
        out_shape=jax.ShapeDtypeStruct((lc + l, d), BF16),
        compiler_params=_params(("arbitrary",)),
        name="norm_modulate",
    )(ctx2, x2, g.reshape(1, d), mod)


def _mm_kernel(x_ref, w_ref, o_ref, *, act):
    acc = jnp.dot(x_ref[...], w_ref[...], preferred_element_type=F32)
    if act == "relu2":
        acc = jnp.square(jnp.maximum(acc, 0.0))
    o_ref[...] = acc.astype(o_ref.dtype)


def _mm(x, w, out_dtype, act=None, tm_target=1024, tn_target=1024, name="matmul"):
    m, k = x.shape
    n = w.shape[1]
    tm = _pick(m, tm_target)
    tn = _pick(n, tn_target)
    return pl.pallas_call(
        functools.partial(_mm_kernel, act=act),
        grid=(m // tm, n // tn),
        in_specs=[pl.BlockSpec((tm, k), lambda i, j: (i, 0)),
                  pl.BlockSpec((k, tn), lambda i, j: (0, j))],
        out_specs=pl.BlockSpec((tm, tn), lambda i, j: (i, j)),
        out_shape=jax.ShapeDtypeStruct((m, n), out_dtype),
        compiler_params=_params(("arbitrary", "arbitrary"), VMEM_LIMIT),
        name=name,
    )(x, w)


def _mmk_kernel(x_ref, w_ref, o_ref, acc_ref):
    kk = pl.program_id(2)
    part = jnp.dot(x_ref[...], w_ref[...], preferred_element_type=F32)

    @pl.when(kk == 0)
    def _():
        acc_ref[...] = part

    @pl.when(kk > 0)
    def _():
        acc_ref[...] = acc_ref[...] + part

    @pl.when(kk == pl.num_programs(2) - 1)
    def _():
        o_ref[...] = acc_ref[...]


def _mm_ksplit(x, w, tm_target=1024, tn_target=1024, tk_target=2048, name="matmul_k"):
    m, k = x.shape
    n = w.shape[1]
    tm, tn, tk = _pick(m, tm_target), _pick(n, tn_target), _pick(k, tk_target)
    return pl.pallas_call(
        _mmk_kernel,
        grid=(m // tm, n // tn, k // tk),
        in_specs=[pl.BlockSpec((tm, tk), lambda i, j, kk: (i, kk)),
                  pl.BlockSpec((tk, tn), lambda i, j, kk: (kk, j))],
        out_specs=pl.BlockSpec((tm, tn), lambda i, j, kk: (i, j)),
        out_shape=jax.ShapeDtypeStruct((m, n), F32),
        scratch_shapes=[pltpu.VMEM((tm, tn), F32)],
        compiler_params=_params(("arbitrary", "arbitrary", "arbitrary"), VMEM_LIMIT),
        name=name,
    )(x, w)


def _rwkv_feat_kernel(pp_ref, pc_ref, pn_ref, sel_ref, mu_ref, w0_ref, w2_ref, a0_ref, a2_ref, g2_ref,
                      kk_ref, ka_ref, rk_ref, hs_ref, hst_ref,
                      r_o, v_o, a_o, lw0_o, lw1_o, k0_o, k1_o, b0_o, b1_o, bonus_o, gate_o,
                      *, ncb, nrows, dr, rd, ra, rg):
    i = pl.program_id(0)
    is_ctx = i < ncb
    xi = i - ncb
    cur, prv, nxt = pc_ref[...], pp_ref[...], pn_ref[...]
    c = cur.shape[0]
    row = lax.broadcasted_iota(jnp.int32, cur.shape, 0)
    prev_last = jnp.where(is_ctx & (i > 0), prv[c - 1:c, :], 0.0)
    next_first = jnp.where(is_ctx & (i < ncb - 1), nxt[0:1, :], 0.0)
    before = jnp.where(row == 0, prev_last, pltpu.roll(cur, 1, 0))
    after = jnp.where(row == c - 1, next_first, pltpu.roll(cur, c - 1, 0))
    above = jnp.where(is_ctx, before, jnp.where(xi > 0, prv, 0.0))
    below = jnp.where(is_ctx, after, jnp.where(xi < nrows - 1, nxt, 0.0))
    sel = sel_ref[...]
    shifted = jnp.where(sel == 0, before, jnp.where(sel == 1, after, jnp.where(sel == 2, above, below)))
    m = cur + mu_ref[...] * (shifted - cur)

    r, k, v = m[:, 0:dr], m[:, dr:2 * dr], m[:, 2 * dr:3 * dr]
    o = 3 * dr
    wl = (m[:, o:o + rd], m[:, o + rd:o + 2 * rd])
    o += 2 * rd
    al = (m[:, o:o + ra], m[:, o + ra:o + 2 * ra])
    o += 2 * ra
    gl = m[:, o:o + rg]

    hs, hst = hs_ref[...], hst_ref[...]

    def headsum(z):
        return _dot(_dot(z, hs), hst)

    kkf = k * kk_ref[...]
    kk = kkf * lax.rsqrt(headsum(kkf * kkf) + 1e-12)
    r_o[...] = r
    v_o[...] = v
    a_o[...] = -kk
    for d, (lw_o, k_o, b_o) in enumerate(((lw0_o, k0_o, b0_o), (lw1_o, k1_o, b1_o))):
        wd = _dot(jnp.tanh(wl[d]), w2_ref[d]) + w0_ref[d:d + 1, :]
        lw_o[...] = -EXP_M05 * _sigmoid(wd)
        ad = _sigmoid(_dot(al[d], a2_ref[d]) + a0_ref[d:d + 1, :])
        k_o[...] = k * (1.0 + (ad - 1.0) * ka_ref[...])
        b_o[...] = kk * ad
    bonus_o[...] = headsum(r * k * rk_ref[...]) * v
    gate_o[...] = _dot(_sigmoid(gl), g2_ref[...])


def _rwkv_features(p_r, sel, mu, w0, w2p, a0, a2p, g2p, k_k, k_a, r_k, lc, l):
    n, nr = p_r.shape
    dr = k_k.shape[0]
    rd, ra, rg = w2p.shape[1], a2p.shape[1], g2p.shape[0]
    nb, ncb = n // CHUNK, lc // CHUNK
    nheads = dr // RWKV_HEAD
    hs = (np.arange(dr)[:, None] // RWKV_HEAD == np.arange(LANE)[None, :]).astype(np.float32)
    assert nheads <= LANE
    full = lambda a: pl.BlockSpec(a.shape, lambda i: (0,) * a.ndim)
    row1 = lambda a: a.reshape(1, -1)
    consts = [sel, mu, w0, w2p, a0, a2p, g2p, row1(k_k), row1(k_a), row1(r_k), jnp.asarray(hs), jnp.asarray(hs.T)]
    blk = (CHUNK, nr)
    outs = pl.pallas_call(
        functools.partial(_rwkv_feat_kernel, ncb=ncb, nrows=l // CHUNK, dr=dr, rd=rd, ra=ra, rg=rg),
        grid=(nb,),
        in_specs=[pl.BlockSpec(blk, lambda i: (jnp.maximum(i - 1, 0), 0)),
                  pl.BlockSpec(blk, lambda i: (i, 0)),
                  pl.BlockSpec(blk, lambda i: (jnp.minimum(i + 1, nb - 1), 0))] + [full(a) for a in consts],
        out_specs=[pl.BlockSpec((CHUNK, dr), lambda i: (i, 0))] * 11,
        out_shape=[jax.ShapeDtypeStruct((n, dr), F32)] * 11,
        compiler_params=_params(("arbitrary",), VMEM_LIMIT),
        name="rwkv_features",
    )(p_r, p_r, p_r, *consts)
    return outs


def _block_diag_mask(n, w):
    r = lax.broadcasted_iota(jnp.int32, (n, n), 0)
    c = lax.broadcasted_iota(jnp.int32, (n, n), 1)
    return (r // w) == (c // w)


def _tri(c, rev):
    t = lax.broadcasted_iota(jnp.int32, (c, c), 0)
    s = lax.broadcasted_iota(jnp.int32, (c, c), 1)
    return ((s >= t) if rev else (s <= t)).astype(F32)


def _rwkv_chunk(r, lw, k, v, a, b, s_vk, rev):
    c, w = r.shape
    nh = w // c
    bc = _dot(_tri(c, rev), lw)
    e_in, e_ex, e_ng = jnp.exp(bc), jnp.exp(bc - lw), jnp.exp(-bc)
    at, rt, bt, kt = a * e_ex, r * e_in, b * e_ng, k * e_ng
    ar = jnp.concatenate([at, rt], axis=0)
    bd = _block_diag_mask(w, c)

    def expand(z):
        return jnp.where(bd, jnp.concatenate([z] * nh, axis=0), 0.0)

    gb = _dot_nt(ar, expand(bt))
    gk = _dot_nt(ar, expand(kt))
    t_i = lax.broadcasted_iota(jnp.int32, (c, w), 0)
    s_i = lax.broadcasted_iota(jnp.int32, (c, w), 1) % c
    strict = (s_i > t_i) if rev else (s_i < t_i)
    incl = (s_i >= t_i) if rev else (s_i <= t_i)
    l_ab = jnp.where(strict, gb[:c], 0.0)
    m_rb = jnp.where(incl, gb[c:], 0.0)
    l_ak = jnp.where(strict, gk[:c], 0.0)
    m_rk = jnp.where(incl, gk[c:], 0.0)
    t_m = (s_i == t_i).astype(F32) + l_ab
    p_m = _dot(l_ab, expand(l_ab))
    for _ in range(int(np.log2(c)) - 2):
        tp = _dot(jnp.concatenate([t_m, p_m], axis=0), expand(p_m))
        t_m = t_m + tp[:c]
        p_m = tp[c:]
    t_m = t_m + _dot(t_m, expand(p_m))

    ars = _dot_nt(ar, s_vk)
    lm = _dot(jnp.concatenate([l_ak, m_rk], axis=0), expand(v))
    u = _dot(t_m, expand(ars[:c] + lm[:c]))
    y = ars[c:] + _dot(m_rb, expand(u)) + lm[c:]
    ds = _dot_tn(jnp.concatenate([u, v], axis=0), jnp.concatenate([bt, kt], axis=0))
    last = bc[0:1] if rev else bc[c - 1:c]
    s_new = (s_vk + jnp.where(bd, ds, 0.0)) * jnp.exp(last)
    return y, s_new


def _rwkv_scan_kernel(rf, lwf, kf, vf, af, bf, rr, lwr, kr, vr, ar, br, yf_o, yr_o, sf, sr):
    @pl.when(pl.program_id(1) == 0)
    def _():
        sf[...] = jnp.zeros_like(sf)
        sr[...] = jnp.zeros_like(sr)

    y, s_new = _rwkv_chunk(rf[...], lwf[...], kf[...], vf[...], af[...], bf[...], sf[...], rev=False)
    yf_o[...] = y
    sf[...] = s_new
    y, s_new = _rwkv_chunk(rr[...], lwr[...], kr[...], vr[...], ar[...], br[...], sr[...], rev=True)
    yr_o[...] = y
    sr[...] = s_new


def _rev_chunk(c, ncb, nb):
    return jnp.where(c < ncb, ncb - 1 - c, nb - 1 - (c - ncb))


def _rwkv_scan(r, v, a, lw0, lw1, k0, k1, b0, b1, lc):
    n, dr = r.shape
    nb, ncb, ng = n // CHUNK, lc // CHUNK, dr // GROUP_W
    fwd = pl.BlockSpec((CHUNK, GROUP_W), lambda g, c: (c, g))
    rev = pl.BlockSpec((CHUNK, GROUP_W), lambda g, c: (_rev_chunk(c, ncb, nb), g))
    return pl.pallas_call(
        _rwkv_scan_kernel,
        grid=(ng, nb),
        in_specs=[fwd] * 6 + [rev] * 6,
        out_specs=[fwd, rev],
        out_shape=[jax.ShapeDtypeStruct((n, dr), F32)] * 2,
        scratch_shapes=[pltpu.VMEM((GROUP_W, GROUP_W), F32)] * 2,
        compiler_params=_params(("arbitrary", "arbitrary")),
        name="rwkv_scan",
    )(r, lw0, k0, v, a, b0, r, lw1, k1, v, a, b1)


def _hgrn_chunk(q, k, v, lf, s_vk, rev):
    c, w = q.shape
    nsb = c // SUB
    bc = _dot(_tri(c, rev), lf)
    last = bc[0:1] if rev else bc[c - 1:c]
    bdh = _block_diag_mask(w, HGRN_HEAD)
    o_inter = _dot_nt(q * jnp.exp(bc), s_vk)
    s_new = s_vk * jnp.exp(last) + jnp.where(bdh, _dot_tn(v, k * jnp.exp(last - bc)), 0.0)

    lane2 = lax.broadcasted_iota(jnp.int32, (c, w), 1)
    row2 = lax.broadcasted_iota(jnp.int32, (c, w), 0)
    zpad = jnp.zeros((HGRN_HEAD - c, w), F32)

    def expand(z):
        parts = []
        for h in range(w // HGRN_HEAD):
            parts += [jnp.where(lane2 // HGRN_HEAD == h, z, 0.0), zpad]
        return jnp.concatenate(parts, axis=0)

    ones_bd = bdh.astype(F32)
    t3 = lax.broadcasted_iota(jnp.int32, (SUB, SUB, w), 0)
    s3 = lax.broadcasted_iota(jnp.int32, (SUB, SUB, w), 1)
    l3 = lax.broadcasted_iota(jnp.int32, (SUB, SUB, w), 2) % HGRN_HEAD
    causal = (s3 >= t3) if rev else (s3 <= t3)
    a_rows = []
    for i in range(nsb):
        lo, hi = SUB * i, SUB * (i + 1)
        qi, ki, bi = q[lo:hi], k[lo:hi], bc[lo:hi]
        d3 = bi[:, None, :] - bi[None, :, :]
        x3 = jnp.where(causal, (qi[:, None, :] * ki[None, :, :]) * jnp.exp(jnp.minimum(d3, 0.0)), 0.0)
        r3 = _dot(x3.reshape(SUB * SUB, w), ones_bd).reshape(SUB, SUB, w)
        a_i = jnp.sum(jnp.where(l3 == s3 + lo, r3, 0.0), axis=1)
        if (not rev and i > 0) or (rev and i < nsb - 1):
            beta = bc[hi:hi + 1] if rev else bc[lo - 1:lo]
            earlier = (row2 >= hi) if rev else (row2 < lo)
            kp = jnp.where(earlier, k * jnp.exp(jnp.minimum(beta - bc, 0.0)), 0.0)
            a_i = a_i + _dot_nt(qi * jnp.exp(bi - beta), expand(kp))
        a_rows.append(a_i)
    o = o_inter + _dot(jnp.concatenate(a_rows, axis=0), expand(v))
    return o, s_new


def _hgrn_scan_kernel(qf, ff, vf, qr, fr, vr, lg_ref, of_o, or_o, sf, sr, *, layer):
    @pl.when(pl.program_id(1) == 0)
    def _():
        sf[...] = jnp.zeros_like(sf)
        sr[...] = jnp.zeros_like(sr)

    lg = lg_ref[...]
    e = jnp.exp(lg - jnp.max(lg, axis=0, keepdims=True))
    lb = jnp.sum(e[:layer + 1], axis=0) / jnp.sum(e, axis=0)
    for d, (q_ref, f_ref, v_ref, o_ref, s_ref) in enumerate(((qf, ff, vf, of_o, sf), (qr, fr, vr, or_o, sr))):
        lbd = lb[d:d + 1, :]
        fd = lbd + (1.0 - lbd) * _sigmoid(f_ref[...])
        qv = q_ref[...]
        o, s_new = _hgrn_chunk(qv * _sigmoid(qv), 1.0 - fd, v_ref[...], jnp.log(fd), s_ref[...], rev=(d == 1))
        o_ref[...] = o
        s_ref[...] = s_new


def _hgrn_scan(p_h, lb_logits, lc, layer):
    n = p_h.shape[0]
    dh = lb_logits.shape[2]
    nb, ncb, ng = n // CHUNK, lc // CHUNK, dh // GROUP_W
    blk = (CHUNK, GROUP_W)
    fwd = lambda sec: pl.BlockSpec(blk, lambda g, c: (c, sec * ng + g))
    rev = lambda sec: pl.BlockSpec(blk, lambda g, c: (_rev_chunk(c, ncb, nb), sec * ng + g))
    return pl.pallas_call(
        functools.partial(_hgrn_scan_kernel, layer=layer),
        grid=(ng, nb),
        in_specs=[fwd(0), fwd(1), fwd(3), rev(0), rev(2), rev(3),
                  pl.BlockSpec((lb_logits.shape[0], 2, GROUP_W), lambda g, c: (0, 0, g))],
        out_specs=[pl.BlockSpec(blk, lambda g, c: (c, g)),
                   pl.BlockSpec(blk, lambda g, c: (_rev_chunk(c, ncb, nb), g))],
        out_shape=[jax.ShapeDtypeStruct((n, dh), F32)] * 2,
        scratch_shapes=[pltpu.VMEM((GROUP_W, GROUP_W), F32)] * 2,
        compiler_params=_params(("arbitrary", "arbitrary")),
        name="hgrn_scan",
    )(p_h, p_h, p_h, p_h, p_h, p_h, lb_logits)


def _mix_out_kernel(yf, yr, bonus, gate, of, orv, g_ref, lng, lnb, hng, hs_ref, hst_ref, u_o, *, dr, dh):
    hs, hst = hs_ref[...], hst_ref[...]

    def headmean(z):
        return _dot(_dot(z, hs), hst) * (1.0 / RWKV_HEAD)

    y = yf[...] + yr[...]
    yc = y - headmean(y)
    yn = yc * lax.rsqrt(headmean(yc * yc) + GN_EPS) * lng[...] + lnb[...]
    u_o[:, 0:dr] = ((yn + bonus[...]) * gate[...]).astype(u_o.dtype)
    o = of[...] + orv[...]
    g = g_ref[...]
    sg = g * _sigmoid(g)
    for h in range(dh // HGRN_HEAD):
        sl = slice(h * HGRN_HEAD, (h + 1) * HGRN_HEAD)
        oh = o[:, sl]
        on = oh * lax.rsqrt(jnp.mean(oh * oh, axis=-1, keepdims=True) + NORM_EPS) * hng[...]
        u_o[:, dr + h * HGRN_HEAD:dr + (h + 1) * HGRN_HEAD] = (on * sg[:, sl]).astype(u_o.dtype)


def _mix_out(yf, yr, bonus, gate, of, orv, p_h, ln_g, ln_b, hg, lc, l):
    dr, dh = yf.shape[1], of.shape[1]
    tb = _pick(int(np.gcd(lc, l)), 128, 8)
    off = lc // tb
    hs = (np.arange(dr)[:, None] // RWKV_HEAD == np.arange(LANE)[None, :]).astype(np.float32)
    rows = lambda wd: pl.BlockSpec((tb, wd), lambda i: (i + off, 0))
    full = lambda a: pl.BlockSpec(a.shape, lambda i: (0,) * a.ndim)
    consts = [ln_g.reshape(1, dr), ln_b.reshape(1, dr), hg.reshape(1, HGRN_HEAD), jnp.asarray(hs), jnp.asarray(hs.T)]
    return pl.pallas_call(
        functools.partial(_mix_out_kernel, dr=dr, dh=dh),
        grid=(l // tb,),
        in_specs=[rows(dr)] * 4 + [rows(dh)] * 2 + [pl.BlockSpec((tb, dh), lambda i: (i + off, 4))]
                 + [full(a) for a in consts],
        out_specs=pl.BlockSpec((tb, dr + dh), lambda i: (i, 0)),
        out_shape=jax.ShapeDtypeStruct((l, dr + dh), BF16),
        compiler_params=_params(("arbitrary",), VMEM_LIMIT),
        name="mix_out",
    )(yf, yr, bonus, gate, of, orv, p_h, *consts)


def _res1_kernel(ux_ref, x_ref, mod_ref, gpost_ref, gpre_ref, x1_o, h2_o, *, d):
    ux = ux_ref[...]
    nrm = ux * lax.rsqrt(jnp.mean(ux * ux, axis=-1, keepdims=True) + NORM_EPS) * gpost_ref[...]
    x1 = x_ref[...] + mod_ref[0:1, 2 * d:3 * d] * nrm
    x1_o[...] = x1
    hn = x1 * lax.rsqrt(jnp.mean(x1 * x1, axis=-1, keepdims=True) + NORM_EPS) * gpre_ref[...]
    h2_o[...] = (hn * (1.0 + mod_ref[0:1, 4 * d:5 * d]) + mod_ref[0:1, 3 * d:4 * d]).astype(h2_o.dtype)


def _res2_kernel(m_ref, x1_ref, mod_ref, gpost_ref, o_ref, *, d):
    mv = m_ref[...]
    nrm = mv * lax.rsqrt(jnp.mean(mv * mv, axis=-1, keepdims=True) + NORM_EPS) * gpost_ref[...]
    o_ref[...] = x1_ref[...] + mod_ref[0:1, 5 * d:6 * d] * nrm


def _res1(ux, x2, mod, g_post, g_pre):
    l, d = x2.shape
    tb = _pick(l, 256, 8)
    rows = pl.BlockSpec((tb, d), lambda i: (i, 0))
    full = lambda a: pl.BlockSpec(a.shape, lambda i: (0,) * a.ndim)
    gp, gq = g_post.reshape(1, d), g_pre.reshape(1, d)
    return pl.pallas_call(
        functools.partial(_res1_kernel, d=d),
        grid=(l // tb,),
        in_specs=[rows, rows, full(mod), full(gp), full(gq)],
        out_specs=[rows, rows],
        out_shape=[jax.ShapeDtypeStruct((l, d), F32), jax.ShapeDtypeStruct((l, d), BF16)],
        compiler_params=_params(("arbitrary",), VMEM_LIMIT),
        name="residual_mix",
    )(ux, x2, mod, gp, gq)


def _res2(mlp, x1, mod, g_post):
    l, d = x1.shape
    tb = _pick(l, 256, 8)
    rows = pl.BlockSpec((tb, d), lambda i: (i, 0))
    full = lambda a: pl.BlockSpec(a.shape, lambda i: (0,) * a.ndim)
    gp = g_post.reshape(1, d)
    return pl.pallas_call(
        functools.partial(_res2_kernel, d=d),
        grid=(l // tb,),
        in_specs=[rows, rows, full(mod), full(gp)],
        out_specs=rows,
        out_shape=jax.ShapeDtypeStruct((l, d), F32),
        compiler_params=_params(("arbitrary",), VMEM_LIMIT),
        name="residual_ffn",
    )(mlp, x1, mod, gp)


def _pad_cols(a, width):
    return jnp.pad(a, ((0, 0), (0, width - a.shape[1])))


def _pad_rows(a, height):
    return jnp.pad(a, ((0, height - a.shape[0]), (0, 0)))


def _round_up(n, m):
    return (n + m - 1) // m * m


def kernel(x, c, ctx, c_ctx, w_ada, b_ada, g_mix_pre, g_mix_post, g_ffn_pre, g_ffn_post, w_in, mu_shift, w0, w2, a0,
           a2, g2, k_k, k_a, r_k, ln_x_g, ln_x_b, hgrn_lb_logits, hgrn_norm_g, w_out, w_ff1, w_ff2):
    assert x.shape[0] == 1 and w_in.shape[0] == 1, "single batch, single layer"
    layer = 0
    x2, ctx2 = x[0], ctx[0]
    l, d = x2.shape
    lc = ctx2.shape[0]
    dr, dh = k_k.shape[1], hgrn_lb_logits.shape[2]
    rd_raw, ra_raw, rg_raw = w2.shape[2], a2.shape[2], g2.shape[1]
    rd, ra, rg = (_round_up(v, LANE) for v in (rd_raw, ra_raw, rg_raw))
    assert l % GRID_W == 0 and lc % CHUNK == 0 and dr % GROUP_W == 0 and dh % GROUP_W == 0

    cvec = jnp.concatenate([c, c_ctx[None, :], jnp.zeros((6, d), F32)], axis=0)
    mod = _ada(cvec, w_ada[layer], b_ada[layer])

    wi = w_in[layer]
    mu = mu_shift[layer][None, :]
    sel = (jnp.arange(mu.shape[1], dtype=jnp.int32) % 4)[None, :]
    o = 3 * dr
    cuts = [(0, o, o)]
    for raw, padded in ((rd_raw, rd), (rd_raw, rd), (ra_raw, ra), (ra_raw, ra), (rg_raw, rg)):
        cuts.append((o, o + raw, padded))
        o += raw
    rw_cols = o
    regroup = lambda a: jnp.concatenate([_pad_cols(a[:, s:e], wd) for s, e, wd in cuts], axis=1)
    w_r = regroup(wi[:, :rw_cols]).astype(BF16)
    w_h = wi[:, rw_cols:].astype(BF16)
    mu_p, sel_p = regroup(mu), regroup(sel)
    w2p = jnp.stack([_pad_rows(w2[layer, dd], rd) for dd in range(2)])
    a2p = jnp.stack([_pad_rows(a2[layer, dd], ra) for dd in range(2)])
    g2p = _pad_rows(g2[layer], rg)

    h = _prep(ctx2, x2, g_mix_pre[layer], mod)
    p_r = _mm(h, w_r, F32, tm_target=768, name="in_proj_rwkv")
    p_h = _mm(h, w_h, F32, tm_target=768, name="in_proj_hgrn")

    (r, v, a, lw0, lw1, k0, k1, b0, b1, bonus, gate) = _rwkv_features(
        p_r, sel_p, mu_p, w0[layer], w2p, a0[layer], a2p, g2p, k_k[layer], k_a[layer], r_k[layer].reshape(-1), lc, l)
    yf, yr = _rwkv_scan(r, v, a, lw0, lw1, k0, k1, b0, b1, lc)
    of, orv = _hgrn_scan(p_h, hgrn_lb_logits, lc, layer)

    u = _mix_out(yf, yr, bonus, gate, of, orv, p_h, ln_x_g[layer], ln_x_b[layer], hgrn_norm_g[layer], lc, l)
    ux = _mm(u, w_out[layer].astype(BF16), F32, name="out_proj")
    x1, h2 = _res1(ux, x2, mod, g_mix_post[layer], g_ffn_pre[layer])
    act = _mm(h2, w_ff1[layer].astype(BF16), BF16, act="relu2", name="ffn_up")
    mlp = _mm_ksplit(act, w_ff2[layer].astype(BF16), name="ffn_down")
    out = _res2(mlp, x1, mod, g_ffn_post[layer])
    return out[None]
```

```python
import functools

import jax
import jax.numpy as jnp
import numpy as np
from jax import lax
from jax.experimental import pallas as pl
from jax.experimental.pallas import tpu as pltpu

F32 = jnp.float32
BF16 = jnp.bfloat16
PROJ_DTYPE = jnp.bfloat16
HI = lax.Precision.HIGHEST

LANE = 128
GRID_W = 64
CHUNK = GRID_W
RWKV_HEAD = 64
HGRN_HEAD = 128
SUB = 16
GROUP_W = 256
NORM_EPS = 1e-6
GN_EPS = 64e-5
EXP_M05 = float(np.exp(-0.5))
VMEM_LIMIT = 56 * 1024 * 1024


def _pick(n, target, unit=LANE):
    best = None
    for m in range(unit, min(n, target) + 1, unit):
        if n % m == 0:
            best = m
    return best if best is not None else n


def _sigmoid(z):
    return 1.0 / (1.0 + jnp.exp(-z))


def _dot(a, b, prec=HI):
    return jnp.dot(a, b, precision=prec, preferred_element_type=F32)


def _dot_nt(a, b, prec=HI):
    return lax.dot_general(a, b, (((1,), (1,)), ((), ())), precision=prec, preferred_element_type=F32)


def _dot_tn(a, b, prec=HI):
    return lax.dot_general(a, b, (((0,), (0,)), ((), ())), precision=prec, preferred_element_type=F32)


def _mxu(a, b):
    return jnp.dot(a, b, preferred_element_type=F32)


def _mxu_nt(a, b):
    return lax.dot_general(a, b, (((1,), (1,)), ((), ())), preferred_element_type=F32)


def _mxu_tn(a, b):
    return lax.dot_general(a, b, (((0,), (0,)), ((), ())), preferred_element_type=F32)


def _params(sem, vmem=None):
    return pltpu.CompilerParams(dimension_semantics=sem, vmem_limit_bytes=vmem)


def _ada_kernel(c_ref, w_ref, b_ref, o_ref):
    cv = c_ref[...]
    o_ref[...] = _dot(cv * _sigmoid(cv), w_ref[...]) + b_ref[...]


def _ada(cvec, w, b):
    rows, d = cvec.shape
    n = w.shape[1]
    tn = _pick(n, 512)
    return pl.pallas_call(
        _ada_kernel,
        grid=(n // tn,),
        in_specs=[pl.BlockSpec((rows, d), lambda j: (0, 0)),
                  pl.BlockSpec((d, tn), lambda j: (0, j)),
                  pl.BlockSpec((1, tn), lambda j: (0, j))],
        out_specs=pl.BlockSpec((rows, tn), lambda j: (0, j)),
        out_shape=jax.ShapeDtypeStruct((rows, n), F32),
        compiler_params=_params(("arbitrary",), VMEM_LIMIT),
        name="ada_mod",
    )(cvec, w, b.reshape(1, n))


def _prep_kernel(ctx_ref, x_ref, g_ref, mod_ref, o_ref, *, ncb, d):
    is_ctx = pl.program_id(0) < ncb
    rows = jnp.where(is_ctx, ctx_ref[...], x_ref[...])
    ms = jnp.mean(rows * rows, axis=-1, keepdims=True)
    hn = rows * lax.rsqrt(ms + NORM_EPS) * g_ref[...]
    shift = jnp.where(is_ctx, mod_ref[1:2, 0:d], mod_ref[0:1, 0:d])
    scale = jnp.where(is_ctx, mod_ref[1:2, d:2 * d], mod_ref[0:1, d:2 * d])
    o_ref[...] = (hn * (1.0 + scale) + shift).astype(o_ref.dtype)


def _prep(ctx2, x2, g, mod):
    lc, d = ctx2.shape
    l = x2.shape[0]
    tb = _pick(int(np.gcd(lc, l)), 256, 8)
    ncb = lc // tb
    return pl.pallas_call(
        functools.partial(_prep_kernel, ncb=ncb, d=d),
        grid=((lc + l) // tb,),
        in_specs=[pl.BlockSpec((tb, d), lambda i: (jnp.minimum(i, ncb - 1), 0)),
                  pl.BlockSpec((tb, d), lambda i: (jnp.maximum(i - ncb, 0), 0)),
                  pl.BlockSpec((1, d), lambda i: (0, 0)),
                  pl.BlockSpec(mod.shape, lambda i: (0, 0))],
        out_specs=pl.BlockSpec((tb, d), lambda i: (i, 0)),
        out_shape=jax.ShapeDtypeStruct((lc + l, d), PROJ_DTYPE),
        compiler_params=_params(("arbitrary",)),
        name="norm_modulate",
    )(ctx2, x2, g.reshape(1, d), mod)


def _mm_kernel(x_ref, w_ref, o_ref, *, act):
    acc = jnp.dot(x_ref[...], w_ref[...], preferred_element_type=F32)
    if act == "relu2":
        acc = jnp.square(jnp.maximum(acc, 0.0))
    o_ref[...] = acc.astype(o_ref.dtype)


def _mm(x, w, out_dtype, act=None, tm_target=1024, tn_target=1024, name="matmul"):
    m, k = x.shape
    n = w.shape[1]
    tm = _pick(m, tm_target)
    tn = _pick(n, tn_target)
    return pl.pallas_call(
        functools.partial(_mm_kernel, act=act),
        grid=(m // tm, n // tn),
        in_specs=[pl.BlockSpec((tm, k), lambda i, j: (i, 0)),
                  pl.BlockSpec((k, tn), lambda i, j: (0, j))],
        out_specs=pl.BlockSpec((tm, tn), lambda i, j: (i, j)),
        out_shape=jax.ShapeDtypeStruct((m, n), out_dtype),
        compiler_params=_params(("arbitrary", "arbitrary"), VMEM_LIMIT),
        name=name,
    )(x, w)


def _mmk_kernel(x_ref, w_ref, o_ref, acc_ref):
    kk = pl.program_id(2)
    part = jnp.dot(x_ref[...], w_ref[...], preferred_element_type=F32)

    @pl.when(kk == 0)
    def _():
        acc_ref[...] = part

    @pl.when(kk > 0)
    def _():
        acc_ref[...] = acc_ref[...] + part

    @pl.when(kk == pl.num_programs(2) - 1)
    def _():
        o_ref[...] = acc_ref[...]


def _mm_ksplit(x, w, tm_target=1024, tn_target=1024, tk_target=2048, name="matmul_k"):
    m, k = x.shape
    n = w.shape[1]
    tm, tn, tk = _pick(m, tm_target), _pick(n, tn_target), _pick(k, tk_target)
    return pl.pallas_call(
        _mmk_kernel,
        grid=(m // tm, n // tn, k // tk),
        in_specs=[pl.BlockSpec((tm, tk), lambda i, j, kk: (i, kk)),
                  pl.BlockSpec((tk, tn), lambda i, j, kk: (kk, j))],
        out_specs=pl.BlockSpec((tm, tn), lambda i, j, kk: (i, j)),
        out_shape=jax.ShapeDtypeStruct((m, n), F32),
        scratch_shapes=[pltpu.VMEM((tm, tn), F32)],
        compiler_params=_params(("arbitrary", "arbitrary", "arbitrary"), VMEM_LIMIT),
        name=name,
    )(x, w)


def _rwkv_feat_kernel(pp_ref, pc_ref, pn_ref, sel_ref, mu_ref, w0_ref, w2_ref, a0_ref, a2_ref, g2_ref,
                      kk_ref, ka_ref, rk_ref, hs_ref, hst_ref,
                      r_o, v_o, a_o, lw0_o, lw1_o, k0_o, k1_o, b0_o, b1_o, bonus_o, gate_o,
                      *, ncb, nrows, dr, rd, ra, rg):
    i = pl.program_id(0)
    is_ctx = i < ncb
    xi = i - ncb
    cur, prv, nxt = pc_ref[...], pp_ref[...], pn_ref[...]
    c = cur.shape[0]
    row = lax.broadcasted_iota(jnp.int32, cur.shape, 0)
    prev_last = jnp.where(is_ctx & (i > 0), prv[c - 1:c, :], 0.0)
    next_first = jnp.where(is_ctx & (i < ncb - 1), nxt[0:1, :], 0.0)
    before = jnp.where(row == 0, prev_last, pltpu.roll(cur, 1, 0))
    after = jnp.where(row == c - 1, next_first, pltpu.roll(cur, c - 1, 0))
    above = jnp.where(is_ctx, before, jnp.where(xi > 0, prv, 0.0))
    below = jnp.where(is_ctx, after, jnp.where(xi < nrows - 1, nxt, 0.0))
    sel = sel_ref[...]
    shifted = jnp.where(sel == 0, before, jnp.where(sel == 1, after, jnp.where(sel == 2, above, below)))
    m = cur + mu_ref[...] * (shifted - cur)

    r, k, v = m[:, 0:dr], m[:, dr:2 * dr], m[:, 2 * dr:3 * dr]
    o = 3 * dr
    wl = (m[:, o:o + rd], m[:, o + rd:o + 2 * rd])
    o += 2 * rd
    al = (m[:, o:o + ra], m[:, o + ra:o + 2 * ra])
    o += 2 * ra
    gl = m[:, o:o + rg]

    hs, hst = hs_ref[...], hst_ref[...]

    def headsum(z):
        return _dot(_dot(z, hs), hst)

    kkf = k * kk_ref[...]
    kk = kkf * lax.rsqrt(headsum(kkf * kkf) + 1e-12)
    r_o[...] = r
    v_o[...] = v
    a_o[...] = -kk
    for d, (lw_o, k_o, b_o) in enumerate(((lw0_o, k0_o, b0_o), (lw1_o, k1_o, b1_o))):
        wd = _dot(jnp.tanh(wl[d]), w2_ref[d]) + w0_ref[d:d + 1, :]
        lw_o[...] = -EXP_M05 * _sigmoid(wd)
        ad = _sigmoid(_dot(al[d], a2_ref[d]) + a0_ref[d:d + 1, :])
        k_o[...] = k * (1.0 + (ad - 1.0) * ka_ref[...])
        b_o[...] = kk * ad
    bonus_o[...] = headsum(r * k * rk_ref[...]) * v
    gate_o[...] = _dot(_sigmoid(gl), g2_ref[...])


def _rwkv_features(p_r, sel, mu, w0, w2p, a0, a2p, g2p, k_k, k_a, r_k, lc, l):
    n, nr = p_r.shape
    dr = k_k.shape[0]
    rd, ra, rg = w2p.shape[1], a2p.shape[1], g2p.shape[0]
    nb, ncb = n // CHUNK, lc // CHUNK
    nheads = dr // RWKV_HEAD
    hs = (np.arange(dr)[:, None] // RWKV_HEAD == np.arange(LANE)[None, :]).astype(np.float32)
    assert nheads <= LANE
    full = lambda a: pl.BlockSpec(a.shape, lambda i: (0,) * a.ndim)
    row1 = lambda a: a.reshape(1, -1)
    consts = [sel, mu, w0, w2p, a0, a2p, g2p, row1(k_k), row1(k_a), row1(r_k), jnp.asarray(hs), jnp.asarray(hs.T)]
    blk = (CHUNK, nr)
    outs = pl.pallas_call(
        functools.partial(_rwkv_feat_kernel, ncb=ncb, nrows=l // CHUNK, dr=dr, rd=rd, ra=ra, rg=rg),
        grid=(nb,),
        in_specs=[pl.BlockSpec(blk, lambda i: (jnp.maximum(i - 1, 0), 0)),
                  pl.BlockSpec(blk, lambda i: (i, 0)),
                  pl.BlockSpec(blk, lambda i: (jnp.minimum(i + 1, nb - 1), 0))] + [full(a) for a in consts],
        out_specs=[pl.BlockSpec((CHUNK, dr), lambda i: (i, 0))] * 11,
        out_shape=[jax.ShapeDtypeStruct((n, dr), F32)] * 11,
        compiler_params=_params(("arbitrary",), VMEM_LIMIT),
        name="rwkv_features",
    )(p_r, p_r, p_r, *consts)
    return outs


def _block_diag_mask(n, w):
    r = lax.broadcasted_iota(jnp.int32, (n, n), 0)
    c = lax.broadcasted_iota(jnp.int32, (n, n), 1)
    return (r // w) == (c // w)


def _cumsum_rows(z, rev):
    c = z.shape[0]
    t = lax.broadcasted_iota(jnp.int32, (c, c), 0)
    s = lax.broadcasted_iota(jnp.int32, (c, c), 1)
    tri = ((s >= t) if rev else (s <= t)).astype(BF16)
    hi = z.astype(BF16)
    rest = z - hi.astype(F32)
    mid = rest.astype(BF16)
    lo = (rest - mid.astype(F32)).astype(BF16)
    return _mxu(tri, hi) + _mxu(tri, mid) + _mxu(tri, lo)


def _rwkv_chunk(r, lw, k, v, a, b, s_vk, rev):
    c, w = r.shape
    nh = w // c
    bc = _cumsum_rows(lw, rev)
    e_in, e_ex, e_ng = jnp.exp(bc), jnp.exp(bc - lw), jnp.exp(-bc)
    at, rt, bt, kt = a * e_ex, r * e_in, b * e_ng, k * e_ng
    ar = jnp.concatenate([at, rt], axis=0).astype(BF16)
    bd = _block_diag_mask(w, c)

    def expand(z):
        return jnp.where(bd, jnp.concatenate([z] * nh, axis=0), jnp.zeros((), z.dtype))

    def packed_mm(lhs, rhs):
        return _mxu(lhs.astype(BF16), expand(rhs.astype(BF16)))

    bt16, kt16 = bt.astype(BF16), kt.astype(BF16)
    gb = _mxu_nt(ar, expand(bt16))
    gk = _mxu_nt(ar, expand(kt16))
    t_i = lax.broadcasted_iota(jnp.int32, (c, w), 0)
    s_i = lax.broadcasted_iota(jnp.int32, (c, w), 1) % c
    strict = (s_i > t_i) if rev else (s_i < t_i)
    incl = (s_i >= t_i) if rev else (s_i <= t_i)
    l_ab = jnp.where(strict, gb[:c], 0.0)
    m_rb = jnp.where(incl, gb[c:], 0.0)
    l_ak = jnp.where(strict, gk[:c], 0.0)
    m_rk = jnp.where(incl, gk[c:], 0.0)
    t_m = (s_i == t_i).astype(F32) + l_ab
    p_m = packed_mm(l_ab, l_ab)
    for _ in range(int(np.log2(c)) - 2):
        tp = packed_mm(jnp.concatenate([t_m, p_m], axis=0), p_m)
        t_m = t_m + tp[:c]
        p_m = tp[c:]
    t_m = t_m + packed_mm(t_m, p_m)

    ars = _mxu_nt(ar, s_vk.astype(BF16))
    lm = _mxu(jnp.concatenate([l_ak, m_rk], axis=0).astype(BF16), expand(v.astype(BF16)))
    u = packed_mm(t_m, ars[:c] + lm[:c])
    u16 = u.astype(BF16)
    y = ars[c:] + _mxu(m_rb.astype(BF16), expand(u16)) + lm[c:]
    ds = _mxu_tn(jnp.concatenate([u16, v.astype(BF16)], axis=0), jnp.concatenate([bt16, kt16], axis=0))
    last = bc[0:1] if rev else bc[c - 1:c]
    s_new = (s_vk + jnp.where(bd, ds, 0.0)) * jnp.exp(last)
    return y, s_new


def _rwkv_scan_kernel(rf, lwf, kf, vf, af, bf, rr, lwr, kr, vr, ar, br, yf_o, yr_o, sf, sr):
    @pl.when(pl.program_id(1) == 0)
    def _():
        sf[...] = jnp.zeros_like(sf)
        sr[...] = jnp.zeros_like(sr)

    y, s_new = _rwkv_chunk(rf[...], lwf[...], kf[...], vf[...], af[...], bf[...], sf[...], rev=False)
    yf_o[...] = y
    sf[...] = s_new
    y, s_new = _rwkv_chunk(rr[...], lwr[...], kr[...], vr[...], ar[...], br[...], sr[...], rev=True)
    yr_o[...] = y
    sr[...] = s_new


def _rev_chunk(c, ncb, nb):
    return jnp.where(c < ncb, ncb - 1 - c, nb - 1 - (c - ncb))


def _rwkv_scan(r, v, a, lw0, lw1, k0, k1, b0, b1, lc):
    n, dr = r.shape
    nb, ncb, ng = n // CHUNK, lc // CHUNK, dr // GROUP_W
    fwd = pl.BlockSpec((CHUNK, GROUP_W), lambda g, c: (c, g))
    rev = pl.BlockSpec((CHUNK, GROUP_W), lambda g, c: (_rev_chunk(c, ncb, nb), g))
    return pl.pallas_call(
        _rwkv_scan_kernel,
        grid=(ng, nb),
        in_specs=[fwd] * 6 + [rev] * 6,
        out_specs=[fwd, rev],
        out_shape=[jax.ShapeDtypeStruct((n, dr), F32)] * 2,
        scratch_shapes=[pltpu.VMEM((GROUP_W, GROUP_W), F32)] * 2,
        compiler_params=_params(("arbitrary", "arbitrary")),
        name="rwkv_scan",
    )(r, lw0, k0, v, a, b0, r, lw1, k1, v, a, b1)


def _hgrn_chunk(q, k, v, lf, s_vk, rev):
    c, w = q.shape
    nsb = c // SUB
    bc = _cumsum_rows(lf, rev)
    last = bc[0:1] if rev else bc[c - 1:c]
    bdh = _block_diag_mask(w, HGRN_HEAD)
    v16 = v.astype(BF16)
    o_inter = _mxu_nt((q * jnp.exp(bc)).astype(BF16), s_vk.astype(BF16))
    s_new = s_vk * jnp.exp(last) + jnp.where(bdh, _mxu_tn(v16, (k * jnp.exp(last - bc)).astype(BF16)), 0.0)

    lane2 = lax.broadcasted_iota(jnp.int32, (c, w), 1)
    row2 = lax.broadcasted_iota(jnp.int32, (c, w), 0)
    zpad = jnp.zeros((HGRN_HEAD - c, w), BF16)

    def expand(z):
        parts = []
        for h in range(w // HGRN_HEAD):
            parts += [jnp.where(lane2 // HGRN_HEAD == h, z, jnp.zeros((), BF16)), zpad]
        return jnp.concatenate(parts, axis=0)

    ones_bd = bdh.astype(BF16)
    t3 = lax.broadcasted_iota(jnp.int32, (SUB, SUB, w), 0)
    s3 = lax.broadcasted_iota(jnp.int32, (SUB, SUB, w), 1)
    l3 = lax.broadcasted_iota(jnp.int32, (SUB, SUB, w), 2) % HGRN_HEAD
    causal = (s3 >= t3) if rev else (s3 <= t3)
    a_rows = []
    for i in range(nsb):
        lo, hi = SUB * i, SUB * (i + 1)
        qi, ki, bi = q[lo:hi], k[lo:hi], bc[lo:hi]
        d3 = bi[:, None, :] - bi[None, :, :]
        x3 = jnp.where(causal, (qi[:, None, :] * ki[None, :, :]) * jnp.exp(jnp.minimum(d3, 0.0)), 0.0)
        r3 = _mxu(x3.reshape(SUB * SUB, w).astype(BF16), ones_bd).reshape(SUB, SUB, w)
        a_i = jnp.sum(jnp.where(l3 == s3 + lo, r3, 0.0), axis=1)
        if (not rev and i > 0) or (rev and i < nsb - 1):
            beta = bc[hi:hi + 1] if rev else bc[lo - 1:lo]
            earlier = (row2 >= hi) if rev else (row2 < lo)
            kp = jnp.where(earlier, k * jnp.exp(jnp.minimum(beta - bc, 0.0)), 0.0)
            a_i = a_i + _mxu_nt((qi * jnp.exp(bi - beta)).astype(BF16), expand(kp.astype(BF16)))
        a_rows.append(a_i)
    o = o_inter + _mxu(jnp.concatenate(a_rows, axis=0).astype(BF16), expand(v16))
    return o, s_new


def _hgrn_scan_kernel(qf, ff, vf, qr, fr, vr, lg_ref, of_o, or_o, sf, sr, *, layer):
    @pl.when(pl.program_id(1) == 0)
    def _():
        sf[...] = jnp.zeros_like(sf)
        sr[...] = jnp.zeros_like(sr)

    lg = lg_ref[...]
    e = jnp.exp(lg - jnp.max(lg, axis=0, keepdims=True))
    lb = jnp.sum(e[:layer + 1], axis=0) / jnp.sum(e, axis=0)
    for d, (q_ref, f_ref, v_ref, o_ref, s_ref) in enumerate(((qf, ff, vf, of_o, sf), (qr, fr, vr, or_o, sr))):
        lbd = lb[d:d + 1, :]
        fd = lbd + (1.0 - lbd) * _sigmoid(f_ref[...])
        qv = q_ref[...]
        o, s_new = _hgrn_chunk(qv * _sigmoid(qv), 1.0 - fd, v_ref[...], jnp.log(fd), s_ref[...], rev=(d == 1))
        o_ref[...] = o
        s_ref[...] = s_new


def _hgrn_scan(p_h, lb_logits, lc, layer):
    n = p_h.shape[0]
    dh = lb_logits.shape[2]
    nb, ncb, ng = n // CHUNK, lc // CHUNK, dh // GROUP_W
    blk = (CHUNK, GROUP_W)
    fwd = lambda sec: pl.BlockSpec(blk, lambda g, c: (c, sec * ng + g))
    rev = lambda sec: pl.BlockSpec(blk, lambda g, c: (_rev_chunk(c, ncb, nb), sec * ng + g))
    return pl.pallas_call(
        functools.partial(_hgrn_scan_kernel, layer=layer),
        grid=(ng, nb),
        in_specs=[fwd(0), fwd(1), fwd(3), rev(0), rev(2), rev(3),
                  pl.BlockSpec((lb_logits.shape[0], 2, GROUP_W), lambda g, c: (0, 0, g))],
        out_specs=[pl.BlockSpec(blk, lambda g, c: (c, g)),
                   pl.BlockSpec(blk, lambda g, c: (_rev_chunk(c, ncb, nb), g))],
        out_shape=[jax.ShapeDtypeStruct((n, dh), F32)] * 2,
        scratch_shapes=[pltpu.VMEM((GROUP_W, GROUP_W), F32)] * 2,
        compiler_params=_params(("arbitrary", "arbitrary")),
        name="hgrn_scan",
    )(p_h, p_h, p_h, p_h, p_h, p_h, lb_logits)


def _mix_out_kernel(yf, yr, bonus, gate, of, orv, g_ref, lng, lnb, hng, hs_ref, hst_ref, u_o, *, dr, dh):
    hs, hst = hs_ref[...], hst_ref[...]

    def headmean(z):
        return _dot(_dot(z, hs), hst) * (1.0 / RWKV_HEAD)

    y = yf[...] + yr[...]
    yc = y - headmean(y)
    yn = yc * lax.rsqrt(headmean(yc * yc) + GN_EPS) * lng[...] + lnb[...]
    u_o[:, 0:dr] = ((yn + bonus[...]) * gate[...]).astype(u_o.dtype)
    o = of[...] + orv[...]
    g = g_ref[...]
    sg = g * _sigmoid(g)
    for h in range(dh // HGRN_HEAD):
        sl = slice(h * HGRN_HEAD, (h + 1) * HGRN_HEAD)
        oh = o[:, sl]
        on = oh * lax.rsqrt(jnp.mean(oh * oh, axis=-1, keepdims=True) + NORM_EPS) * hng[...]
        u_o[:, dr + h * HGRN_HEAD:dr + (h + 1) * HGRN_HEAD] = (on * sg[:, sl]).astype(u_o.dtype)


def _mix_out(yf, yr, bonus, gate, of, orv, p_h, ln_g, ln_b, hg, lc, l):
    dr, dh = yf.shape[1], of.shape[1]
    tb = _pick(int(np.gcd(lc, l)), 128, 8)
    off = lc // tb
    hs = (np.arange(dr)[:, None] // RWKV_HEAD == np.arange(LANE)[None, :]).astype(np.float32)
    rows = lambda wd: pl.BlockSpec((tb, wd), lambda i: (i + off, 0))
    full = lambda a: pl.BlockSpec(a.shape, lambda i: (0,) * a.ndim)
    consts = [ln_g.reshape(1, dr), ln_b.reshape(1, dr), hg.reshape(1, HGRN_HEAD), jnp.asarray(hs), jnp.asarray(hs.T)]
    return pl.pallas_call(
        functools.partial(_mix_out_kernel, dr=dr, dh=dh),
        grid=(l // tb,),
        in_specs=[rows(dr)] * 4 + [rows(dh)] * 2 + [pl.BlockSpec((tb, dh), lambda i: (i + off, 4))]
                 + [full(a) for a in consts],
        out_specs=pl.BlockSpec((tb, dr + dh), lambda i: (i, 0)),
        out_shape=jax.ShapeDtypeStruct((l, dr + dh), PROJ_DTYPE),
        compiler_params=_params(("arbitrary",), VMEM_LIMIT),
        name="mix_out",
    )(yf, yr, bonus, gate, of, orv, p_h, *consts)


def _res1_kernel(ux_ref, x_ref, mod_ref, gpost_ref, gpre_ref, x1_o, h2_o, *, d):
    ux = ux_ref[...]
    nrm = ux * lax.rsqrt(jnp.mean(ux * ux, axis=-1, keepdims=True) + NORM_EPS) * gpost_ref[...]
    x1 = x_ref[...] + mod_ref[0:1, 2 * d:3 * d] * nrm
    x1_o[...] = x1
    hn = x1 * lax.rsqrt(jnp.mean(x1 * x1, axis=-1, keepdims=True) + NORM_EPS) * gpre_ref[...]
    h2_o[...] = (hn * (1.0 + mod_ref[0:1, 4 * d:5 * d]) + mod_ref[0:1, 3 * d:4 * d]).astype(h2_o.dtype)


def _res2_kernel(m_ref, x1_ref, mod_ref, gpost_ref, o_ref, *, d):
    mv = m_ref[...]
    nrm = mv * lax.rsqrt(jnp.mean(mv * mv, axis=-1, keepdims=True) + NORM_EPS) * gpost_ref[...]
    o_ref[...] = x1_ref[...] + mod_ref[0:1, 5 * d:6 * d] * nrm


def _res1(ux, x2, mod, g_post, g_pre):
    l, d = x2.shape
    tb = _pick(l, 256, 8)
    rows = pl.BlockSpec((tb, d), lambda i: (i, 0))
    full = lambda a: pl.BlockSpec(a.shape, lambda i: (0,) * a.ndim)
    gp, gq = g_post.reshape(1, d), g_pre.reshape(1, d)
    return pl.pallas_call(
        functools.partial(_res1_kernel, d=d),
        grid=(l // tb,),
        in_specs=[rows, rows, full(mod), full(gp), full(gq)],
        out_specs=[rows, rows],
        out_shape=[jax.ShapeDtypeStruct((l, d), F32), jax.ShapeDtypeStruct((l, d), PROJ_DTYPE)],
        compiler_params=_params(("arbitrary",), VMEM_LIMIT),
        name="residual_mix",
    )(ux, x2, mod, gp, gq)


def _res2(mlp, x1, mod, g_post):
    l, d = x1.shape
    tb = _pick(l, 256, 8)
    rows = pl.BlockSpec((tb, d), lambda i: (i, 0))
    full = lambda a: pl.BlockSpec(a.shape, lambda i: (0,) * a.ndim)
    gp = g_post.reshape(1, d)
    return pl.pallas_call(
        functools.partial(_res2_kernel, d=d),
        grid=(l // tb,),
        in_specs=[rows, rows, full(mod), full(gp)],
        out_specs=rows,
        out_shape=jax.ShapeDtypeStruct((l, d), F32),
        compiler_params=_params(("arbitrary",), VMEM_LIMIT),
        name="residual_ffn",
    )(mlp, x1, mod, gp)


def _pad_cols(a, width):
    return jnp.pad(a, ((0, 0), (0, width - a.shape[1])))


def _pad_rows(a, height):
    return jnp.pad(a, ((0, height - a.shape[0]), (0, 0)))


def _round_up(n, m):
    return (n + m - 1) // m * m


def kernel(x, c, ctx, c_ctx, w_ada, b_ada, g_mix_pre, g_mix_post, g_ffn_pre, g_ffn_post, w_in, mu_shift, w0, w2, a0,
           a2, g2, k_k, k_a, r_k, ln_x_g, ln_x_b, hgrn_lb_logits, hgrn_norm_g, w_out, w_ff1, w_ff2):
    assert x.shape[0] == 1 and w_in.shape[0] == 1, "single batch, single layer"
    layer = 0
    x2, ctx2 = x[0], ctx[0]
    l, d = x2.shape
    lc = ctx2.shape[0]
    dr, dh = k_k.shape[1], hgrn_lb_logits.shape[2]
    rd_raw, ra_raw, rg_raw = w2.shape[2], a2.shape[2], g2.shape[1]
    rd, ra, rg = (_round_up(v, LANE) for v in (rd_raw, ra_raw, rg_raw))
    assert l % GRID_W == 0 and lc % CHUNK == 0 and dr % GROUP_W == 0 and dh % GROUP_W == 0

    cvec = jnp.concatenate([c, c_ctx[None, :], jnp.zeros((6, d), F32)], axis=0)
    mod = _ada(cvec, w_ada[layer], b_ada[layer])

    wi = w_in[layer]
    mu = mu_shift[layer][None, :]
    sel = (jnp.arange(mu.shape[1], dtype=jnp.int32) % 4)[None, :]
    o = 3 * dr
    cuts = [(0, o, o)]
    for raw, padded in ((rd_raw, rd), (rd_raw, rd), (ra_raw, ra), (ra_raw, ra), (rg_raw, rg)):
        cuts.append((o, o + raw, padded))
        o += raw
    rw_cols = o
    regroup = lambda a: jnp.concatenate([_pad_cols(a[:, s:e], wd) for s, e, wd in cuts], axis=1)
    w_r = regroup(wi[:, :rw_cols]).astype(PROJ_DTYPE)
    w_h = wi[:, rw_cols:].astype(PROJ_DTYPE)
    mu_p, sel_p = regroup(mu), regroup(sel)
    w2p = jnp.stack([_pad_rows(w2[layer, dd], rd) for dd in range(2)])
    a2p = jnp.stack([_pad_rows(a2[layer, dd], ra) for dd in range(2)])
    g2p = _pad_rows(g2[layer], rg)

    h = _prep(ctx2, x2, g_mix_pre[layer], mod)
    p_r = _mm(h, w_r, F32, tm_target=768, name="in_proj_rwkv")
    p_h = _mm(h, w_h, F32, tm_target=768, name="in_proj_hgrn")

    (r, v, a, lw0, lw1, k0, k1, b0, b1, bonus, gate) = _rwkv_features(
        p_r, sel_p, mu_p, w0[layer], w2p, a0[layer], a2p, g2p, k_k[layer], k_a[layer], r_k[layer].reshape(-1), lc, l)
    yf, yr = _rwkv_scan(r, v, a, lw0, lw1, k0, k1, b0, b1, lc)
    of, orv = _hgrn_scan(p_h, hgrn_lb_logits, lc, layer)

    u = _mix_out(yf, yr, bonus, gate, of, orv, p_h, ln_x_g[layer], ln_x_b[layer], hgrn_norm_g[layer], lc, l)
    ux = _mm(u, w_out[layer].astype(PROJ_DTYPE), F32, name="out_proj")
    x1, h2 = _res1(ux, x2, mod, g_mix_post[layer], g_ffn_pre[layer])
    act = _mm(h2, w_ff1[layer].astype(PROJ_DTYPE), PROJ_DTYPE, act="relu2", name="ffn_up")
    mlp = _mm_ksplit(act, w_ff2[layer].astype(PROJ_DTYPE), name="ffn_down")
    out = _res2(mlp, x1, mod, g_ffn_post[layer])
    return out[None]
```

```python
import functools

import jax
import jax.numpy as jnp
import numpy as np
from jax import lax
from jax.experimental import pallas as pl
from jax.experimental.pallas import tpu as pltpu

F32 = jnp.float32
BF16 = jnp.bfloat16
PROJ_DTYPE = jnp.bfloat16
HI = lax.Precision.HIGHEST

LANE = 128
GRID_W = 64
CHUNK = GRID_W
RWKV_HEAD = 64
HGRN_HEAD = 128
SUB = 16
GROUP_W = 256
RWKV_GROUPS_PER_STEP = 4
HGRN_GROUPS_PER_STEP = 4
MAX_CHUNK_LOG_DECAY = 60.0
NORM_EPS = 1e-6
GN_EPS = 64e-5
EXP_M05 = float(np.exp(-0.5))
VMEM_LIMIT = 56 * 1024 * 1024


def _pick(n, target, unit=LANE):
    best = None
    for m in range(unit, min(n, target) + 1, unit):
        if n % m == 0:
            best = m
    return best if best is not None else n


def _sigmoid(z):
    return 1.0 / (1.0 + jnp.exp(-z))


def _dot(a, b, prec=HI):
    return jnp.dot(a, b, precision=prec, preferred_element_type=F32)


def _dot_nt(a, b, prec=HI):
    return lax.dot_general(a, b, (((1,), (1,)), ((), ())), precision=prec, preferred_element_type=F32)


def _dot_tn(a, b, prec=HI):
    return lax.dot_general(a, b, (((0,), (0,)), ((), ())), precision=prec, preferred_element_type=F32)


def _mxu(a, b):
    return jnp.dot(a, b, preferred_element_type=F32)


def _mxu_nt(a, b):
    return lax.dot_general(a, b, (((1,), (1,)), ((), ())), preferred_element_type=F32)


def _mxu_tn(a, b):
    return lax.dot_general(a, b, (((0,), (0,)), ((), ())), preferred_element_type=F32)


def _params(sem, vmem=None):
    return pltpu.CompilerParams(dimension_semantics=sem, vmem_limit_bytes=vmem)


def _ada_kernel(c_ref, w_ref, b_ref, o_ref):
    cv = c_ref[...]
    o_ref[...] = _dot(cv * _sigmoid(cv), w_ref[...]) + b_ref[...]


def _ada(cvec, w, b):
    rows, d = cvec.shape
    n = w.shape[1]
    tn = _pick(n, 512)
    return pl.pallas_call(
        _ada_kernel,
        grid=(n // tn,),
        in_specs=[pl.BlockSpec((rows, d), lambda j: (0, 0)),
                  pl.BlockSpec((d, tn), lambda j: (0, j)),
                  pl.BlockSpec((1, tn), lambda j: (0, j))],
        out_specs=pl.BlockSpec((rows, tn), lambda j: (0, j)),
        out_shape=jax.ShapeDtypeStruct((rows, n), F32),
        compiler_params=_params(("arbitrary",), VMEM_LIMIT),
        name="ada_mod",
    )(cvec, w, b.reshape(1, n))


def _prep_kernel(ctx_ref, x_ref, g_ref, mod_ref, o_ref, *, ncb, d):
    is_ctx = pl.program_id(0) < ncb
    rows = jnp.where(is_ctx, ctx_ref[...], x_ref[...])
    ms = jnp.mean(rows * rows, axis=-1, keepdims=True)
    hn = rows * lax.rsqrt(ms + NORM_EPS) * g_ref[...]
    shift = jnp.where(is_ctx, mod_ref[1:2, 0:d], mod_ref[0:1, 0:d])
    scale = jnp.where(is_ctx, mod_ref[1:2, d:2 * d], mod_ref[0:1, d:2 * d])
    o_ref[...] = (hn * (1.0 + scale) + shift).astype(o_ref.dtype)


def _prep(ctx2, x2, g, mod):
    lc, d = ctx2.shape
    l = x2.shape[0]
    tb = _pick(int(np.gcd(lc, l)), 256, 8)
    ncb = lc // tb
    return pl.pallas_call(
        functools.partial(_prep_kernel, ncb=ncb, d=d),
        grid=((lc + l) // tb,),
        in_specs=[pl.BlockSpec((tb, d), lambda i: (jnp.minimum(i, ncb - 1), 0)),
                  pl.BlockSpec((tb, d), lambda i: (jnp.maximum(i - ncb, 0), 0)),
                  pl.BlockSpec((1, d), lambda i: (0, 0)),
                  pl.BlockSpec(mod.shape, lambda i: (0, 0))],
        out_specs=pl.BlockSpec((tb, d), lambda i: (i, 0)),
        out_shape=jax.ShapeDtypeStruct((lc + l, d), PROJ_DTYPE),
        compiler_params=_params(("arbitrary",)),
        name="norm_modulate",
    )(ctx2, x2, g.reshape(1, d), mod)


def _mm_kernel(x_ref, w_ref, o_ref, *, act):
    acc = jnp.dot(x_ref[...], w_ref[...], preferred_element_type=F32)
    if act == "relu2":
        acc = jnp.square(jnp.maximum(acc, 0.0))
    o_ref[...] = acc.astype(o_ref.dtype)


def _mm(x, w, out_dtype, act=None, tm_target=1024, tn_target=1024, name="matmul"):
    m, k = x.shape
    n = w.shape[1]
    tm = _pick(m, tm_target)
    tn = _pick(n, tn_target)
    return pl.pallas_call(
        functools.partial(_mm_kernel, act=act),
        grid=(m // tm, n // tn),
        in_specs=[pl.BlockSpec((tm, k), lambda i, j: (i, 0)),
                  pl.BlockSpec((k, tn), lambda i, j: (0, j))],
        out_specs=pl.BlockSpec((tm, tn), lambda i, j: (i, j)),
        out_shape=jax.ShapeDtypeStruct((m, n), out_dtype),
        compiler_params=_params(("arbitrary", "arbitrary"), VMEM_LIMIT),
        name=name,
    )(x, w)


def _mmk_kernel(x_ref, w_ref, o_ref, acc_ref):
    kk = pl.program_id(2)
    part = jnp.dot(x_ref[...], w_ref[...], preferred_element_type=F32)

    @pl.when(kk == 0)
    def _():
        acc_ref[...] = part

    @pl.when(kk > 0)
    def _():
        acc_ref[...] = acc_ref[...] + part

    @pl.when(kk == pl.num_programs(2) - 1)
    def _():
        o_ref[...] = acc_ref[...]


def _mm_ksplit(x, w, tm_target=1024, tn_target=1024, tk_target=2048, name="matmul_k"):
    m, k = x.shape
    n = w.shape[1]
    tm, tn, tk = _pick(m, tm_target), _pick(n, tn_target), _pick(k, tk_target)
    return pl.pallas_call(
        _mmk_kernel,
        grid=(m // tm, n // tn, k // tk),
        in_specs=[pl.BlockSpec((tm, tk), lambda i, j, kk: (i, kk)),
                  pl.BlockSpec((tk, tn), lambda i, j, kk: (kk, j))],
        out_specs=pl.BlockSpec((tm, tn), lambda i, j, kk: (i, j)),
        out_shape=jax.ShapeDtypeStruct((m, n), F32),
        scratch_shapes=[pltpu.VMEM((tm, tn), F32)],
        compiler_params=_params(("arbitrary", "arbitrary", "arbitrary"), VMEM_LIMIT),
        name=name,
    )(x, w)


def _rwkv_feat_kernel(pp_ref, pc_ref, pn_ref, sel_ref, mu_ref, w0_ref, w2_ref, a0_ref, a2_ref, g2_ref,
                      kk_ref, ka_ref, rk_ref, hs_ref, hst_ref,
                      r_o, v_o, a_o, lw0_o, lw1_o, k0_o, k1_o, b0_o, b1_o, bonus_o, gate_o,
                      *, ncb, nrows, dr, rd, ra, rg):
    i = pl.program_id(0)
    is_ctx = i < ncb
    xi = i - ncb
    cur, prv, nxt = pc_ref[...], pp_ref[...], pn_ref[...]
    c = cur.shape[0]
    row = lax.broadcasted_iota(jnp.int32, cur.shape, 0)
    prev_last = jnp.where(is_ctx & (i > 0), prv[c - 1:c, :], 0.0)
    next_first = jnp.where(is_ctx & (i < ncb - 1), nxt[0:1, :], 0.0)
    before = jnp.where(row == 0, prev_last, pltpu.roll(cur, 1, 0))
    after = jnp.where(row == c - 1, next_first, pltpu.roll(cur, c - 1, 0))
    above = jnp.where(is_ctx, before, jnp.where(xi > 0, prv, 0.0))
    below = jnp.where(is_ctx, after, jnp.where(xi < nrows - 1, nxt, 0.0))
    sel = sel_ref[...]
    shifted = jnp.where(sel == 0, before, jnp.where(sel == 1, after, jnp.where(sel == 2, above, below)))
    m = cur + mu_ref[...] * (shifted - cur)

    r, k, v = m[:, 0:dr], m[:, dr:2 * dr], m[:, 2 * dr:3 * dr]
    o = 3 * dr
    wl = (m[:, o:o + rd], m[:, o + rd:o + 2 * rd])
    o += 2 * rd
    al = (m[:, o:o + ra], m[:, o + ra:o + 2 * ra])
    o += 2 * ra
    gl = m[:, o:o + rg]

    hs, hst = hs_ref[...], hst_ref[...]

    def headsum(z):
        return _dot(_dot(z, hs), hst)

    kkf = k * kk_ref[...]
    kk = kkf * lax.rsqrt(headsum(kkf * kkf) + 1e-12)
    r_o[...] = r
    v_o[...] = v
    a_o[...] = -kk
    for d, (lw_o, k_o, b_o) in enumerate(((lw0_o, k0_o, b0_o), (lw1_o, k1_o, b1_o))):
        wd = _dot(jnp.tanh(wl[d]), w2_ref[d]) + w0_ref[d:d + 1, :]
        lw_o[...] = -EXP_M05 * _sigmoid(wd)
        ad = _sigmoid(_dot(al[d], a2_ref[d]) + a0_ref[d:d + 1, :])
        k_o[...] = k * (1.0 + (ad - 1.0) * ka_ref[...])
        b_o[...] = kk * ad
    bonus_o[...] = headsum(r * k * rk_ref[...]) * v
    gate_o[...] = _dot(_sigmoid(gl), g2_ref[...])


def _rwkv_features(p_r, sel, mu, w0, w2p, a0, a2p, g2p, k_k, k_a, r_k, lc, l):
    n, nr = p_r.shape
    dr = k_k.shape[0]
    rd, ra, rg = w2p.shape[1], a2p.shape[1], g2p.shape[0]
    nb, ncb = n // CHUNK, lc // CHUNK
    nheads = dr // RWKV_HEAD
    hs = (np.arange(dr)[:, None] // RWKV_HEAD == np.arange(LANE)[None, :]).astype(np.float32)
    assert nheads <= LANE
    full = lambda a: pl.BlockSpec(a.shape, lambda i: (0,) * a.ndim)
    row1 = lambda a: a.reshape(1, -1)
    consts = [sel, mu, w0, w2p, a0, a2p, g2p, row1(k_k), row1(k_a), row1(r_k), jnp.asarray(hs), jnp.asarray(hs.T)]
    blk = (CHUNK, nr)
    outs = pl.pallas_call(
        functools.partial(_rwkv_feat_kernel, ncb=ncb, nrows=l // CHUNK, dr=dr, rd=rd, ra=ra, rg=rg),
        grid=(nb,),
        in_specs=[pl.BlockSpec(blk, lambda i: (jnp.maximum(i - 1, 0), 0)),
                  pl.BlockSpec(blk, lambda i: (i, 0)),
                  pl.BlockSpec(blk, lambda i: (jnp.minimum(i + 1, nb - 1), 0))] + [full(a) for a in consts],
        out_specs=[pl.BlockSpec((CHUNK, dr), lambda i: (i, 0))] * 11,
        out_shape=[jax.ShapeDtypeStruct((n, dr), F32)] * 11,
        compiler_params=_params(("arbitrary",), VMEM_LIMIT),
        name="rwkv_features",
    )(p_r, p_r, p_r, *consts)
    return outs


def _block_diag_mask(n, w):
    r = lax.broadcasted_iota(jnp.int32, (n, n), 0)
    c = lax.broadcasted_iota(jnp.int32, (n, n), 1)
    return (r // w) == (c // w)


def _cumsum_rows(z, rev):
    c = z.shape[0]
    t = lax.broadcasted_iota(jnp.int32, (c, c), 0)
    s = lax.broadcasted_iota(jnp.int32, (c, c), 1)
    tri = ((s >= t) if rev else (s <= t)).astype(BF16)
    hi = z.astype(BF16)
    rest = z - hi.astype(F32)
    mid = rest.astype(BF16)
    lo = (rest - mid.astype(F32)).astype(BF16)
    return _mxu(tri, hi) + _mxu(tri, mid) + _mxu(tri, lo)


def _rwkv_chunks(chains):
    c, w = chains[0][0].shape
    nh = w // c
    every = range(len(chains))
    revs = [ch[7] for ch in chains]
    bd = _block_diag_mask(w, c)
    t_i = lax.broadcasted_iota(jnp.int32, (c, w), 0)
    s_i = lax.broadcasted_iota(jnp.int32, (c, w), 1) % c
    eye = (s_i == t_i).astype(F32)
    strict = {False: s_i < t_i, True: s_i > t_i}
    incl = {False: s_i <= t_i, True: s_i >= t_i}

    def expand(z):
        return jnp.where(bd, jnp.concatenate([z] * nh, axis=0), jnp.zeros((), z.dtype))

    def packed_mm(lhs, rhs):
        return _mxu(lhs.astype(BF16), expand(rhs.astype(BF16)))

    bc = [_cumsum_rows(ch[1], ch[7]) for ch in chains]
    ar, bk, v16 = [], [], []
    for i, (r, lw, k, v, a, b, _, _) in enumerate(chains):
        e_in, e_ex, e_ng = jnp.exp(bc[i]), jnp.exp(bc[i] - lw), jnp.exp(-bc[i])
        ar.append(jnp.concatenate([a * e_ex, r * e_in], axis=0).astype(BF16))
        bk.append(jnp.concatenate([b * e_ng, k * e_ng], axis=0).astype(BF16))
        v16.append(v.astype(BF16))
    gb = [_mxu_nt(ar[i], expand(bk[i][:c])) for i in every]
    gk = [_mxu_nt(ar[i], expand(bk[i][c:])) for i in every]
    l_ab = [jnp.where(strict[revs[i]], gb[i][:c], 0.0) for i in every]
    m_rb = [jnp.where(incl[revs[i]], gb[i][c:], 0.0) for i in every]
    lmk = [jnp.concatenate([jnp.where(strict[revs[i]], gk[i][:c], 0.0),
                            jnp.where(incl[revs[i]], gk[i][c:], 0.0)], axis=0) for i in every]
    t_m = [eye + l_ab[i] for i in every]
    p_m = [packed_mm(l_ab[i], l_ab[i]) for i in every]
    for _ in range(int(np.log2(c)) - 2):
        tp = [packed_mm(jnp.concatenate([t_m[i], p_m[i]], axis=0), p_m[i]) for i in every]
        t_m = [t_m[i] + tp[i][:c] for i in every]
        p_m = [tp[i][c:] for i in every]
    t_m = [t_m[i] + packed_mm(t_m[i], p_m[i]) for i in every]

    ars = [_mxu_nt(ar[i], chains[i][6].astype(BF16)) for i in every]
    lm = [_mxu(lmk[i].astype(BF16), expand(v16[i])) for i in every]
    u16 = [packed_mm(t_m[i], ars[i][:c] + lm[i][:c]).astype(BF16) for i in every]
    ys = [ars[i][c:] + _mxu(m_rb[i].astype(BF16), expand(u16[i])) + lm[i][c:] for i in every]
    ds = [_mxu_tn(jnp.concatenate([u16[i], v16[i]], axis=0), bk[i]) for i in every]
    s_new = []
    for i in every:
        last = bc[i][0:1] if revs[i] else bc[i][c - 1:c]
        s_new.append((chains[i][6] + jnp.where(bd, ds[i], 0.0)) * jnp.exp(last))
    return ys, s_new


def _rwkv_scan_kernel(rf, lwf, kf, vf, af, bf, rr, lwr, kr, vr, ar, br, yf_o, yr_o, s_ref):
    @pl.when(pl.program_id(1) == 0)
    def _():
        s_ref[...] = jnp.zeros_like(s_ref)

    ngs = rf.shape[1] // GROUP_W
    chains = []
    for d, refs in enumerate(((rf, lwf, kf, vf, af, bf), (rr, lwr, kr, vr, ar, br))):
        for g in range(ngs):
            sl = slice(g * GROUP_W, (g + 1) * GROUP_W)
            chains.append(tuple(ref[:, sl] for ref in refs) + (s_ref[d, g], d == 1))
    ys, s_new = _rwkv_chunks(chains)
    for d, o_ref in enumerate((yf_o, yr_o)):
        for g in range(ngs):
            o_ref[:, g * GROUP_W:(g + 1) * GROUP_W] = ys[d * ngs + g]
            s_ref[d, g] = s_new[d * ngs + g]


def _rev_chunk(c, ncb, nb):
    return jnp.where(c < ncb, ncb - 1 - c, nb - 1 - (c - ncb))


def _rwkv_scan(r, v, a, lw0, lw1, k0, k1, b0, b1, lc):
    n, dr = r.shape
    nb, ncb = n // CHUNK, lc // CHUNK
    bw = _pick(dr, RWKV_GROUPS_PER_STEP * GROUP_W, GROUP_W)
    fwd = pl.BlockSpec((CHUNK, bw), lambda g, c: (c, g))
    rev = pl.BlockSpec((CHUNK, bw), lambda g, c: (_rev_chunk(c, ncb, nb), g))
    return pl.pallas_call(
        _rwkv_scan_kernel,
        grid=(dr // bw, nb),
        in_specs=[fwd] * 6 + [rev] * 6,
        out_specs=[fwd, rev],
        out_shape=[jax.ShapeDtypeStruct((n, dr), F32)] * 2,
        scratch_shapes=[pltpu.VMEM((2, bw // GROUP_W, GROUP_W, GROUP_W), F32)],
        compiler_params=_params(("arbitrary", "arbitrary")),
        name="rwkv_scan",
    )(r, lw0, k0, v, a, b0, r, lw1, k1, v, a, b1)


def _hgrn_chunk(q, k, v, lf, s_vk, rev):
    c, w = q.shape
    nsb = c // SUB
    bc = _cumsum_rows(lf, rev)
    last = bc[0:1] if rev else bc[c - 1:c]
    bdh = _block_diag_mask(w, HGRN_HEAD)
    v16 = v.astype(BF16)
    o_inter = _mxu_nt((q * jnp.exp(bc)).astype(BF16), s_vk.astype(BF16))
    s_new = s_vk * jnp.exp(last) + jnp.where(bdh, _mxu_tn(v16, (k * jnp.exp(last - bc)).astype(BF16)), 0.0)

    lane2 = lax.broadcasted_iota(jnp.int32, (c, w), 1)
    row2 = lax.broadcasted_iota(jnp.int32, (c, w), 0)
    zpad = jnp.zeros((HGRN_HEAD - c, w), BF16)

    def expand(z):
        parts = []
        for h in range(w // HGRN_HEAD):
            parts += [jnp.where(lane2 // HGRN_HEAD == h, z, jnp.zeros((), BF16)), zpad]
        return jnp.concatenate(parts, axis=0)

    ones_bd = bdh.astype(BF16)
    t3 = lax.broadcasted_iota(jnp.int32, (SUB, SUB, w), 0)
    s3 = lax.broadcasted_iota(jnp.int32, (SUB, SUB, w), 1)
    l3 = lax.broadcasted_iota(jnp.int32, (SUB, SUB, w), 2) % HGRN_HEAD
    causal = (s3 >= t3) if rev else (s3 <= t3)
    a_rows = []
    for i in range(nsb):
        lo, hi = SUB * i, SUB * (i + 1)
        qi, ki, bi = q[lo:hi], k[lo:hi], bc[lo:hi]
        d3 = bi[:, None, :] - bi[None, :, :]
        x3 = jnp.where(causal, (qi[:, None, :] * ki[None, :, :]) * jnp.exp(jnp.minimum(d3, 0.0)), 0.0)
        r3 = _mxu(x3.reshape(SUB * SUB, w).astype(BF16), ones_bd).reshape(SUB, SUB, w)
        a_i = jnp.sum(jnp.where(l3 == s3 + lo, r3, 0.0), axis=1)
        if (not rev and i > 0) or (rev and i < nsb - 1):
            beta = bc[hi:hi + 1] if rev else bc[lo - 1:lo]
            earlier = (row2 >= hi) if rev else (row2 < lo)
            kp = jnp.where(earlier, k * jnp.exp(jnp.minimum(beta - bc, 0.0)), 0.0)
            a_i = a_i + _mxu_nt((qi * jnp.exp(bi - beta)).astype(BF16), expand(kp.astype(BF16)))
        a_rows.append(a_i)
    o = o_inter + _mxu(jnp.concatenate(a_rows, axis=0).astype(BF16), expand(v16))
    return o, s_new


def _hgrn_chunks_bounded(chains):
    c, w = chains[0][0].shape
    every = range(len(chains))
    revs = [ch[5] for ch in chains]
    bdh = _block_diag_mask(w, HGRN_HEAD)
    lane2 = lax.broadcasted_iota(jnp.int32, (c, w), 1)
    t_i = lax.broadcasted_iota(jnp.int32, (c, w), 0)
    s_i = lane2 % HGRN_HEAD
    incl = {False: s_i <= t_i, True: s_i >= t_i}
    zpad = jnp.zeros((HGRN_HEAD - c, w), BF16)

    def expand(z):
        parts = []
        for h in range(w // HGRN_HEAD):
            parts += [jnp.where(lane2 // HGRN_HEAD == h, z, jnp.zeros((), BF16)), zpad]
        return jnp.concatenate(parts, axis=0)

    bc = [_cumsum_rows(ch[3], ch[5]) for ch in chains]
    last = [bc[i][0:1] if revs[i] else bc[i][c - 1:c] for i in every]
    qt = [(chains[i][0] * jnp.exp(bc[i])).astype(BF16) for i in every]
    kt = [(chains[i][1] * jnp.exp(-bc[i])).astype(BF16) for i in every]
    kd = [(chains[i][1] * jnp.exp(last[i] - bc[i])).astype(BF16) for i in every]
    v16 = [chains[i][2].astype(BF16) for i in every]
    o_inter = [_mxu_nt(qt[i], chains[i][4].astype(BF16)) for i in every]
    a = [jnp.where(incl[revs[i]], _mxu_nt(qt[i], expand(kt[i])), 0.0).astype(BF16) for i in every]
    o = [o_inter[i] + _mxu(a[i], expand(v16[i])) for i in every]
    ds = [_mxu_tn(v16[i], kd[i]) for i in every]
    s_new = [chains[i][4] * jnp.exp(last[i]) + jnp.where(bdh, ds[i], 0.0) for i in every]
    return o, s_new


def _hgrn_scan_kernel(qf, ff, vf, qr, fr, vr, lg_ref, of_o, or_o, s_ref, *, layer):
    @pl.when(pl.program_id(1) == 0)
    def _():
        s_ref[...] = jnp.zeros_like(s_ref)

    lg = lg_ref[...]
    e = jnp.exp(lg - jnp.max(lg, axis=0, keepdims=True))
    lb = jnp.sum(e[:layer + 1], axis=0) / jnp.sum(e, axis=0)
    ngs = qf.shape[1] // GROUP_W
    chains = []
    min_total = None
    for d, (q_ref, f_ref, v_ref) in enumerate(((qf, ff, vf), (qr, fr, vr))):
        fd = lb[d:d + 1, :] + (1.0 - lb[d:d + 1, :]) * _sigmoid(f_ref[...])
        qv = q_ref[...]
        qh, kh, vh, lf = qv * _sigmoid(qv), 1.0 - fd, v_ref[...], jnp.log(fd)
        total = jnp.min(jnp.sum(lf, axis=0, keepdims=True))
        min_total = total if min_total is None else jnp.minimum(min_total, total)
        for g in range(ngs):
            sl = slice(g * GROUP_W, (g + 1) * GROUP_W)
            chains.append((qh[:, sl], kh[:, sl], vh[:, sl], lf[:, sl], s_ref[d, g], d == 1))

    def emit(o, s_new):
        for d, o_ref in enumerate((of_o, or_o)):
            for g in range(ngs):
                o_ref[:, g * GROUP_W:(g + 1) * GROUP_W] = o[d * ngs + g]
                s_ref[d, g] = s_new[d * ngs + g]

    bounded = min_total >= -MAX_CHUNK_LOG_DECAY

    @pl.when(bounded)
    def _():
        emit(*_hgrn_chunks_bounded(chains))

    @pl.when(jnp.logical_not(bounded))
    def _():
        res = [_hgrn_chunk(*ch) for ch in chains]
        emit([r[0] for r in res], [r[1] for r in res])


def _hgrn_scan(p_h, lb_logits, lc, layer):
    n = p_h.shape[0]
    dh = lb_logits.shape[2]
    nb, ncb = n // CHUNK, lc // CHUNK
    bw = _pick(dh, HGRN_GROUPS_PER_STEP * GROUP_W, GROUP_W)
    ng = dh // bw
    blk = (CHUNK, bw)
    fwd = lambda sec: pl.BlockSpec(blk, lambda g, c: (c, sec * ng + g))
    rev = lambda sec: pl.BlockSpec(blk, lambda g, c: (_rev_chunk(c, ncb, nb), sec * ng + g))
    return pl.pallas_call(
        functools.partial(_hgrn_scan_kernel, layer=layer),
        grid=(ng, nb),
        in_specs=[fwd(0), fwd(1), fwd(3), rev(0), rev(2), rev(3),
                  pl.BlockSpec((lb_logits.shape[0], 2, bw), lambda g, c: (0, 0, g))],
        out_specs=[pl.BlockSpec(blk, lambda g, c: (c, g)),
                   pl.BlockSpec(blk, lambda g, c: (_rev_chunk(c, ncb, nb), g))],
        out_shape=[jax.ShapeDtypeStruct((n, dh), F32)] * 2,
        scratch_shapes=[pltpu.VMEM((2, bw // GROUP_W, GROUP_W, GROUP_W), F32)],
        compiler_params=_params(("arbitrary", "arbitrary")),
        name="hgrn_scan",
    )(p_h, p_h, p_h, p_h, p_h, p_h, lb_logits)


def _mix_out_kernel(yf, yr, bonus, gate, of, orv, g_ref, lng, lnb, hng, hs_ref, hst_ref, u_o, *, dr, dh):
    hs, hst = hs_ref[...], hst_ref[...]

    def headmean(z):
        return _dot(_dot(z, hs), hst) * (1.0 / RWKV_HEAD)

    y = yf[...] + yr[...]
    yc = y - headmean(y)
    yn = yc * lax.rsqrt(headmean(yc * yc) + GN_EPS) * lng[...] + lnb[...]
    u_o[:, 0:dr] = ((yn + bonus[...]) * gate[...]).astype(u_o.dtype)
    o = of[...] + orv[...]
    g = g_ref[...]
    sg = g * _sigmoid(g)
    for h in range(dh // HGRN_HEAD):
        sl = slice(h * HGRN_HEAD, (h + 1) * HGRN_HEAD)
        oh = o[:, sl]
        on = oh * lax.rsqrt(jnp.mean(oh * oh, axis=-1, keepdims=True) + NORM_EPS) * hng[...]
        u_o[:, dr + h * HGRN_HEAD:dr + (h + 1) * HGRN_HEAD] = (on * sg[:, sl]).astype(u_o.dtype)


def _mix_out(yf, yr, bonus, gate, of, orv, p_h, ln_g, ln_b, hg, lc, l):
    dr, dh = yf.shape[1], of.shape[1]
    tb = _pick(int(np.gcd(lc, l)), 128, 8)
    off = lc // tb
    hs = (np.arange(dr)[:, None] // RWKV_HEAD == np.arange(LANE)[None, :]).astype(np.float32)
    rows = lambda wd: pl.BlockSpec((tb, wd), lambda i: (i + off, 0))
    full = lambda a: pl.BlockSpec(a.shape, lambda i: (0,) * a.ndim)
    consts = [ln_g.reshape(1, dr), ln_b.reshape(1, dr), hg.reshape(1, HGRN_HEAD), jnp.asarray(hs), jnp.asarray(hs.T)]
    return pl.pallas_call(
        functools.partial(_mix_out_kernel, dr=dr, dh=dh),
        grid=(l // tb,),
        in_specs=[rows(dr)] * 4 + [rows(dh)] * 2 + [pl.BlockSpec((tb, dh), lambda i: (i + off, 4))]
                 + [full(a) for a in consts],
        out_specs=pl.BlockSpec((tb, dr + dh), lambda i: (i, 0)),
        out_shape=jax.ShapeDtypeStruct((l, dr + dh), PROJ_DTYPE),
        compiler_params=_params(("arbitrary",), VMEM_LIMIT),
        name="mix_out",
    )(yf, yr, bonus, gate, of, orv, p_h, *consts)


def _res1_kernel(ux_ref, x_ref, mod_ref, gpost_ref, gpre_ref, x1_o, h2_o, *, d):
    ux = ux_ref[...]
    nrm = ux * lax.rsqrt(jnp.mean(ux * ux, axis=-1, keepdims=True) + NORM_EPS) * gpost_ref[...]
    x1 = x_ref[...] + mod_ref[0:1, 2 * d:3 * d] * nrm
    x1_o[...] = x1
    hn = x1 * lax.rsqrt(jnp.mean(x1 * x1, axis=-1, keepdims=True) + NORM_EPS) * gpre_ref[...]
    h2_o[...] = (hn * (1.0 + mod_ref[0:1, 4 * d:5 * d]) + mod_ref[0:1, 3 * d:4 * d]).astype(h2_o.dtype)


def _res2_kernel(m_ref, x1_ref, mod_ref, gpost_ref, o_ref, *, d):
    mv = m_ref[...]
    nrm = mv * lax.rsqrt(jnp.mean(mv * mv, axis=-1, keepdims=True) + NORM_EPS) * gpost_ref[...]
    o_ref[...] = x1_ref[...] + mod_ref[0:1, 5 * d:6 * d] * nrm


def _res1(ux, x2, mod, g_post, g_pre):
    l, d = x2.shape
    tb = _pick(l, 256, 8)
    rows = pl.BlockSpec((tb, d), lambda i: (i, 0))
    full = lambda a: pl.BlockSpec(a.shape, lambda i: (0,) * a.ndim)
    gp, gq = g_post.reshape(1, d), g_pre.reshape(1, d)
    return pl.pallas_call(
        functools.partial(_res1_kernel, d=d),
        grid=(l // tb,),
        in_specs=[rows, rows, full(mod), full(gp), full(gq)],
        out_specs=[rows, rows],
        out_shape=[jax.ShapeDtypeStruct((l, d), F32), jax.ShapeDtypeStruct((l, d), PROJ_DTYPE)],
        compiler_params=_params(("arbitrary",), VMEM_LIMIT),
        name="residual_mix",
    )(ux, x2, mod, gp, gq)


def _res2(mlp, x1, mod, g_post):
    l, d = x1.shape
    tb = _pick(l, 256, 8)
    rows = pl.BlockSpec((tb, d), lambda i: (i, 0))
    full = lambda a: pl.BlockSpec(a.shape, lambda i: (0,) * a.ndim)
    gp = g_post.reshape(1, d)
    return pl.pallas_call(
        functools.partial(_res2_kernel, d=d),
        grid=(l // tb,),
        in_specs=[rows, rows, full(mod), full(gp)],
        out_specs=rows,
        out_shape=jax.ShapeDtypeStruct((l, d), F32),
        compiler_params=_params(("arbitrary",), VMEM_LIMIT),
        name="residual_ffn",
    )(mlp, x1, mod, gp)


def _pad_cols(a, width):
    return jnp.pad(a, ((0, 0), (0, width - a.shape[1])))


def _pad_rows(a, height):
    return jnp.pad(a, ((0, height - a.shape[0]), (0, 0)))


def _round_up(n, m):
    return (n + m - 1) // m * m


def kernel(x, c, ctx, c_ctx, w_ada, b_ada, g_mix_pre, g_mix_post, g_ffn_pre, g_ffn_post, w_in, mu_shift, w0, w2, a0,
           a2, g2, k_k, k_a, r_k, ln_x_g, ln_x_b, hgrn_lb_logits, hgrn_norm_g, w_out, w_ff1, w_ff2):
    assert x.shape[0] == 1 and w_in.shape[0] == 1, "single batch, single layer"
    layer = 0
    x2, ctx2 = x[0], ctx[0]
    l, d = x2.shape
    lc = ctx2.shape[0]
    dr, dh = k_k.shape[1], hgrn_lb_logits.shape[2]
    rd_raw, ra_raw, rg_raw = w2.shape[2], a2.shape[2], g2.shape[1]
    rd, ra, rg = (_round_up(v, LANE) for v in (rd_raw, ra_raw, rg_raw))
    assert l % GRID_W == 0 and lc % CHUNK == 0 and dr % GROUP_W == 0 and dh % GROUP_W == 0

    cvec = jnp.concatenate([c, c_ctx[None, :], jnp.zeros((6, d), F32)], axis=0)
    mod = _ada(cvec, w_ada[layer], b_ada[layer])

    wi = w_in[layer]
    mu = mu_shift[layer][None, :]
    sel = (jnp.arange(mu.shape[1], dtype=jnp.int32) % 4)[None, :]
    o = 3 * dr
    cuts = [(0, o, o)]
    for raw, padded in ((rd_raw, rd), (rd_raw, rd), (ra_raw, ra), (ra_raw, ra), (rg_raw, rg)):
        cuts.append((o, o + raw, padded))
        o += raw
    rw_cols = o
    regroup = lambda a: jnp.concatenate([_pad_cols(a[:, s:e], wd) for s, e, wd in cuts], axis=1)
    w_r = regroup(wi[:, :rw_cols]).astype(PROJ_DTYPE)
    w_h = wi[:, rw_cols:].astype(PROJ_DTYPE)
    mu_p, sel_p = regroup(mu), regroup(sel)
    w2p = jnp.stack([_pad_rows(w2[layer, dd], rd) for dd in range(2)])
    a2p = jnp.stack([_pad_rows(a2[layer, dd], ra) for dd in range(2)])
    g2p = _pad_rows(g2[layer], rg)

    h = _prep(ctx2, x2, g_mix_pre[layer], mod)
    p_r = _mm(h, w_r, F32, tm_target=768, name="in_proj_rwkv")
    p_h = _mm(h, w_h, F32, tm_target=768, name="in_proj_hgrn")

    (r, v, a, lw0, lw1, k0, k1, b0, b1, bonus, gate) = _rwkv_features(
        p_r, sel_p, mu_p, w0[layer], w2p, a0[layer], a2p, g2p, k_k[layer], k_a[layer], r_k[layer].reshape(-1), lc, l)
    yf, yr = _rwkv_scan(r, v, a, lw0, lw1, k0, k1, b0, b1, lc)
    of, orv = _hgrn_scan(p_h, hgrn_lb_logits, lc, layer)

    u = _mix_out(yf, yr, bonus, gate, of, orv, p_h, ln_x_g[layer], ln_x_b[layer], hgrn_norm_g[layer], lc, l)
    ux = _mm(u, w_out[layer].astype(PROJ_DTYPE), F32, name="out_proj")
    x1, h2 = _res1(ux, x2, mod, g_mix_post[layer], g_ffn_pre[layer])
    act = _mm(h2, w_ff1[layer].astype(PROJ_DTYPE), PROJ_DTYPE, act="relu2", name="ffn_up")
    mlp = _mm_ksplit(act, w_ff2[layer].astype(PROJ_DTYPE), name="ffn_down")
    out = _res2(mlp, x1, mod, g_ffn_post[layer])
    return out[None]
```

```python
import functools

import jax
import jax.numpy as jnp
import numpy as np
from jax import lax
from jax.experimental import pallas as pl
from jax.experimental.pallas import tpu as pltpu

F32 = jnp.float32
BF16 = jnp.bfloat16
PROJ_DTYPE = jnp.bfloat16

LANE = 128
GRID_W = 64
CHUNK = GRID_W
RWKV_HEAD = 64
HGRN_HEAD = 128
SUB = 16
GROUP_W = 256
RWKV_GROUPS_PER_STEP = 8
HGRN_GROUPS_PER_STEP = 4
MAX_CHUNK_LOG_DECAY = 60.0
NORM_EPS = 1e-6
GN_EPS = 64e-5
EXP_M05 = float(np.exp(-0.5))
VMEM_LIMIT = 56 * 1024 * 1024


def _pick(n, target, unit=LANE):
    best = None
    for m in range(unit, min(n, target) + 1, unit):
        if n % m == 0:
            best = m
    return best if best is not None else n


def _sigmoid(z):
    return 1.0 / (1.0 + jnp.exp(-z))


def _mxu(a, b):
    return jnp.dot(a, b, preferred_element_type=F32)


def _mxu_nt(a, b):
    return lax.dot_general(a, b, (((1,), (1,)), ((), ())), preferred_element_type=F32)


def _mxu_tn(a, b):
    return lax.dot_general(a, b, (((0,), (0,)), ((), ())), preferred_element_type=F32)


def _split3(z):
    hi = z.astype(BF16)
    rest = z - hi.astype(F32)
    mid = rest.astype(BF16)
    return hi, mid, (rest - mid.astype(F32)).astype(BF16)


def _exact_mxu(z, sel):
    hi, mid, lo = _split3(z)
    return _mxu(hi, sel) + _mxu(mid, sel) + _mxu(lo, sel)


def _head_sum(z, hs, hst):
    return _exact_mxu(_exact_mxu(z, hs), hst)


def _head_indicators(dr):
    hs = np.arange(dr)[:, None] // RWKV_HEAD == np.arange(LANE)[None, :]
    return jnp.asarray(hs, BF16), jnp.asarray(hs.T, BF16)


def _params(sem, vmem=None):
    return pltpu.CompilerParams(dimension_semantics=sem, vmem_limit_bytes=vmem)


def _ada_kernel(c_ref, w_ref, b_ref, o_ref):
    cv = c_ref[...]
    rows = cv.shape[0]
    hi, mid, lo = (t.astype(F32) for t in _split3(cv * _sigmoid(cv)))
    s3 = jnp.concatenate([hi, mid, lo], axis=0).astype(BF16)
    w = w_ref[...]
    w_hi = w.astype(BF16)
    w_lo = (w - w_hi.astype(F32)).astype(BF16)
    p = _mxu(s3, w_hi)
    q = _mxu(s3[:2 * rows], w_lo)
    o_ref[...] = (p[:rows] + p[rows:2 * rows] + p[2 * rows:] + q[:rows] + q[rows:]) + b_ref[...]


def _ada(cvec, w, b):
    rows, d = cvec.shape
    n = w.shape[1]
    tn = _pick(n, 512)
    return pl.pallas_call(
        _ada_kernel,
        grid=(n // tn,),
        in_specs=[pl.BlockSpec((rows, d), lambda j: (0, 0)),
                  pl.BlockSpec((d, tn), lambda j: (0, j)),
                  pl.BlockSpec((1, tn), lambda j: (0, j))],
        out_specs=pl.BlockSpec((rows, tn), lambda j: (0, j)),
        out_shape=jax.ShapeDtypeStruct((rows, n), F32),
        compiler_params=_params(("arbitrary",), VMEM_LIMIT),
        name="ada_mod",
    )(cvec, w, b.reshape(1, n))


def _prep_kernel(ctx_ref, x_ref, g_ref, mod_ref, o_ref, *, ncb, d):
    is_ctx = pl.program_id(0) < ncb
    rows = jnp.where(is_ctx, ctx_ref[...], x_ref[...])
    ms = jnp.mean(rows * rows, axis=-1, keepdims=True)
    hn = rows * lax.rsqrt(ms + NORM_EPS) * g_ref[...]
    shift = jnp.where(is_ctx, mod_ref[1:2, 0:d], mod_ref[0:1, 0:d])
    scale = jnp.where(is_ctx, mod_ref[1:2, d:2 * d], mod_ref[0:1, d:2 * d])
    o_ref[...] = (hn * (1.0 + scale) + shift).astype(o_ref.dtype)


def _prep(ctx2, x2, g, mod):
    lc, d = ctx2.shape
    l = x2.shape[0]
    tb = _pick(int(np.gcd(lc, l)), 256, 8)
    ncb = lc // tb
    return pl.pallas_call(
        functools.partial(_prep_kernel, ncb=ncb, d=d),
        grid=((lc + l) // tb,),
        in_specs=[pl.BlockSpec((tb, d), lambda i: (jnp.minimum(i, ncb - 1), 0)),
                  pl.BlockSpec((tb, d), lambda i: (jnp.maximum(i - ncb, 0), 0)),
                  pl.BlockSpec((1, d), lambda i: (0, 0)),
                  pl.BlockSpec(mod.shape, lambda i: (0, 0))],
        out_specs=pl.BlockSpec((tb, d), lambda i: (i, 0)),
        out_shape=jax.ShapeDtypeStruct((lc + l, d), PROJ_DTYPE),
        compiler_params=_params(("arbitrary",)),
        name="norm_modulate",
    )(ctx2, x2, g.reshape(1, d), mod)


def _mm_kernel(x_ref, w_ref, o_ref, *, act):
    acc = jnp.dot(x_ref[...], w_ref[...], preferred_element_type=F32)
    if act == "relu2":
        acc = jnp.square(jnp.maximum(acc, 0.0))
    o_ref[...] = acc.astype(o_ref.dtype)


def _mm(x, w, out_dtype, act=None, tm_target=1024, tn_target=1024, name="matmul"):
    m, k = x.shape
    n = w.shape[1]
    tm = _pick(m, tm_target)
    tn = _pick(n, tn_target)
    return pl.pallas_call(
        functools.partial(_mm_kernel, act=act),
        grid=(m // tm, n // tn),
        in_specs=[pl.BlockSpec((tm, k), lambda i, j: (i, 0), pipeline_mode=pl.Buffered(1)),
                  pl.BlockSpec((k, tn), lambda i, j: (0, j))],
        out_specs=pl.BlockSpec((tm, tn), lambda i, j: (i, j)),
        out_shape=jax.ShapeDtypeStruct((m, n), out_dtype),
        compiler_params=_params(("arbitrary", "arbitrary"), VMEM_LIMIT),
        name=name,
    )(x, w)


def _mmk_kernel(x_ref, w_ref, o_ref, acc_ref):
    kk = pl.program_id(2)
    part = jnp.dot(x_ref[...], w_ref[...], preferred_element_type=F32)

    @pl.when(kk == 0)
    def _():
        acc_ref[...] = part

    @pl.when(kk > 0)
    def _():
        acc_ref[...] = acc_ref[...] + part

    @pl.when(kk == pl.num_programs(2) - 1)
    def _():
        o_ref[...] = acc_ref[...]


def _mm_ksplit(x, w, tm_target=1024, tn_target=1024, tk_target=2048, name="matmul_k"):
    m, k = x.shape
    n = w.shape[1]
    tm, tn, tk = _pick(m, tm_target), _pick(n, tn_target), _pick(k, tk_target)
    return pl.pallas_call(
        _mmk_kernel,
        grid=(m // tm, n // tn, k // tk),
        in_specs=[pl.BlockSpec((tm, tk), lambda i, j, kk: (i, kk)),
                  pl.BlockSpec((tk, tn), lambda i, j, kk: (kk, j))],
        out_specs=pl.BlockSpec((tm, tn), lambda i, j, kk: (i, j)),
        out_shape=jax.ShapeDtypeStruct((m, n), F32),
        scratch_shapes=[pltpu.VMEM((tm, tn), F32)],
        compiler_params=_params(("arbitrary", "arbitrary", "arbitrary"), VMEM_LIMIT),
        name=name,
    )(x, w)


def _rwkv_feat_kernel(pp_ref, pc_ref, pn_ref, sel_ref, mu_ref, w0_ref, w2_ref, a0_ref, a2_ref, g2_ref,
                      kk_ref, ka_ref, rk_ref, hs_ref, hst_ref,
                      r_o, v_o, a_o, lw0_o, lw1_o, k0_o, k1_o, b0_o, b1_o, bonus_o, gate_o,
                      *, ncb, nrows, dr, rd, ra, rg):
    i = pl.program_id(0)
    is_ctx = i < ncb
    xi = i - ncb
    cur, prv, nxt = pc_ref[...], pp_ref[...], pn_ref[...]
    c = cur.shape[0]
    row = lax.broadcasted_iota(jnp.int32, cur.shape, 0)
    prev_last = jnp.where(is_ctx & (i > 0), prv[c - 1:c, :], 0.0)
    next_first = jnp.where(is_ctx & (i < ncb - 1), nxt[0:1, :], 0.0)
    before = jnp.where(row == 0, prev_last, pltpu.roll(cur, 1, 0))
    after = jnp.where(row == c - 1, next_first, pltpu.roll(cur, c - 1, 0))
    above = jnp.where(is_ctx, before, jnp.where(xi > 0, prv, 0.0))
    below = jnp.where(is_ctx, after, jnp.where(xi < nrows - 1, nxt, 0.0))
    sel = sel_ref[...]
    shifted = jnp.where(sel == 0, before, jnp.where(sel == 1, after, jnp.where(sel == 2, above, below)))
    m = cur + mu_ref[...] * (shifted - cur)

    r, k, v = m[:, 0:dr], m[:, dr:2 * dr], m[:, 2 * dr:3 * dr]
    o = 3 * dr
    wl = (m[:, o:o + rd], m[:, o + rd:o + 2 * rd])
    o += 2 * rd
    al = (m[:, o:o + ra], m[:, o + ra:o + 2 * ra])
    o += 2 * ra
    gl = m[:, o:o + rg]

    hs, hst = hs_ref[...], hst_ref[...]

    def headsum(z):
        return _head_sum(z, hs, hst)

    kkf = k * kk_ref[...]
    kk = kkf * lax.rsqrt(headsum(kkf * kkf) + 1e-12)
    r_o[...] = r
    v_o[...] = v
    a_o[...] = -kk
    for d, (lw_o, k_o, b_o) in enumerate(((lw0_o, k0_o, b0_o), (lw1_o, k1_o, b1_o))):
        wd = _mxu(jnp.tanh(wl[d]).astype(BF16), w2_ref[d]) + w0_ref[d:d + 1, :]
        lw_o[...] = -EXP_M05 * _sigmoid(wd)
        ad = _sigmoid(_mxu(al[d].astype(BF16), a2_ref[d]) + a0_ref[d:d + 1, :])
        k_o[...] = k * (1.0 + (ad - 1.0) * ka_ref[...])
        b_o[...] = kk * ad
    bonus_o[...] = headsum(r * k * rk_ref[...]) * v
    gate_o[...] = _mxu(_sigmoid(gl).astype(BF16), g2_ref[...])


def _rwkv_features(p_r, sel, mu, w0, w2p, a0, a2p, g2p, k_k, k_a, r_k, lc, l):
    n, nr = p_r.shape
    dr = k_k.shape[0]
    rd, ra, rg = w2p.shape[1], a2p.shape[1], g2p.shape[0]
    nb, ncb = n // CHUNK, lc // CHUNK
    nheads = dr // RWKV_HEAD
    assert nheads <= LANE
    hs, hst = _head_indicators(dr)
    full = lambda a: pl.BlockSpec(a.shape, lambda i: (0,) * a.ndim)
    row1 = lambda a: a.reshape(1, -1)
    consts = [sel, mu, w0, w2p.astype(BF16), a0, a2p.astype(BF16), g2p.astype(BF16), row1(k_k), row1(k_a),
              row1(r_k), hs, hst]
    blk = (CHUNK, nr)
    outs = pl.pallas_call(
        functools.partial(_rwkv_feat_kernel, ncb=ncb, nrows=l // CHUNK, dr=dr, rd=rd, ra=ra, rg=rg),
        grid=(nb,),
        in_specs=[pl.BlockSpec(blk, lambda i: (jnp.maximum(i - 1, 0), 0)),
                  pl.BlockSpec(blk, lambda i: (i, 0)),
                  pl.BlockSpec(blk, lambda i: (jnp.minimum(i + 1, nb - 1), 0))] + [full(a) for a in consts],
        out_specs=[pl.BlockSpec((CHUNK, dr), lambda i: (i, 0))] * 11,
        out_shape=[jax.ShapeDtypeStruct((n, dr), F32)] * 11,
        compiler_params=_params(("arbitrary",), VMEM_LIMIT),
        name="rwkv_features",
    )(p_r, p_r, p_r, *consts)
    return outs


def _block_diag_mask(n, w):
    r = lax.broadcasted_iota(jnp.int32, (n, n), 0)
    c = lax.broadcasted_iota(jnp.int32, (n, n), 1)
    return (r // w) == (c // w)


def _cumsum_rows(z, rev):
    c = z.shape[0]
    t = lax.broadcasted_iota(jnp.int32, (c, c), 0)
    s = lax.broadcasted_iota(jnp.int32, (c, c), 1)
    tri = ((s >= t) if rev else (s <= t)).astype(BF16)
    hi, mid, lo = _split3(z)
    return _mxu(tri, hi) + _mxu(tri, mid) + _mxu(tri, lo)


def _rwkv_chunks(chains):
    c, w = chains[0][0].shape
    nh = w // c
    every = range(len(chains))
    revs = [ch[7] for ch in chains]
    bd = _block_diag_mask(w, c)
    t_i = lax.broadcasted_iota(jnp.int32, (c, w), 0)
    s_i = lax.broadcasted_iota(jnp.int32, (c, w), 1) % c
    eye = (s_i == t_i).astype(F32)
    strict = {False: s_i < t_i, True: s_i > t_i}
    incl = {False: s_i <= t_i, True: s_i >= t_i}

    def expand(z):
        return jnp.where(bd, jnp.concatenate([z] * nh, axis=0), jnp.zeros((), z.dtype))

    def packed_mm(lhs, rhs):
        return _mxu(lhs.astype(BF16), expand(rhs.astype(BF16)))

    bc = [_cumsum_rows(ch[1], ch[7]) for ch in chains]
    ar, bk, v16 = [], [], []
    for i, (r, lw, k, v, a, b, _, _) in enumerate(chains):
        e_in, e_ex, e_ng = jnp.exp(bc[i]), jnp.exp(bc[i] - lw), jnp.exp(-bc[i])
        ar.append(jnp.concatenate([a * e_ex, r * e_in], axis=0).astype(BF16))
        bk.append(jnp.concatenate([b * e_ng, k * e_ng], axis=0).astype(BF16))
        v16.append(v.astype(BF16))
    gb = [_mxu_nt(ar[i], expand(bk[i][:c])) for i in every]
    gk = [_mxu_nt(ar[i], expand(bk[i][c:])) for i in every]
    l_ab = [jnp.where(strict[revs[i]], gb[i][:c], 0.0) for i in every]
    m_rb = [jnp.where(incl[revs[i]], gb[i][c:], 0.0) for i in every]
    lmk = [jnp.concatenate([jnp.where(strict[revs[i]], gk[i][:c], 0.0),
                            jnp.where(incl[revs[i]], gk[i][c:], 0.0)], axis=0) for i in every]
    t_m = [eye + l_ab[i] for i in every]
    p_m = [packed_mm(l_ab[i], l_ab[i]) for i in every]
    for _ in range(int(np.log2(c)) - 2):
        tp = [packed_mm(jnp.concatenate([t_m[i], p_m[i]], axis=0), p_m[i]) for i in every]
        t_m = [t_m[i] + tp[i][:c] for i in every]
        p_m = [tp[i][c:] for i in every]
    t_m = [t_m[i] + packed_mm(t_m[i], p_m[i]) for i in every]

    ars = [_mxu_nt(ar[i], chains[i][6].astype(BF16)) for i in every]
    lm = [_mxu(lmk[i].astype(BF16), expand(v16[i])) for i in every]
    u16 = [packed_mm(t_m[i], ars[i][:c] + lm[i][:c]).astype(BF16) for i in every]
    ys = [ars[i][c:] + _mxu(m_rb[i].astype(BF16), expand(u16[i])) + lm[i][c:] for i in every]
    ds = [_mxu_tn(jnp.concatenate([u16[i], v16[i]], axis=0), bk[i]) for i in every]
    s_new = []
    for i in every:
        last = bc[i][0:1] if revs[i] else bc[i][c - 1:c]
        s_new.append((chains[i][6] + jnp.where(bd, ds[i], 0.0)) * jnp.exp(last))
    return ys, s_new


def _rwkv_scan_kernel(rf, lwf, kf, vf, af, bf, rr, lwr, kr, vr, ar, br, yf_o, yr_o, s_ref):
    @pl.when(pl.program_id(1) == 0)
    def _():
        s_ref[...] = jnp.zeros_like(s_ref)

    ngs = rf.shape[1] // GROUP_W
    chains = []
    for d, refs in enumerate(((rf, lwf, kf, vf, af, bf), (rr, lwr, kr, vr, ar, br))):
        for g in range(ngs):
            sl = slice(g * GROUP_W, (g + 1) * GROUP_W)
            chains.append(tuple(ref[:, sl] for ref in refs) + (s_ref[d, g], d == 1))
    ys, s_new = _rwkv_chunks(chains)
    for d, o_ref in enumerate((yf_o, yr_o)):
        for g in range(ngs):
            o_ref[:, g * GROUP_W:(g + 1) * GROUP_W] = ys[d * ngs + g]
            s_ref[d, g] = s_new[d * ngs + g]


def _rev_chunk(c, ncb, nb):
    return jnp.where(c < ncb, ncb - 1 - c, nb - 1 - (c - ncb))


def _rwkv_scan(r, v, a, lw0, lw1, k0, k1, b0, b1, lc):
    n, dr = r.shape
    nb, ncb = n // CHUNK, lc // CHUNK
    bw = _pick(dr, RWKV_GROUPS_PER_STEP * GROUP_W, GROUP_W)
    fwd = pl.BlockSpec((CHUNK, bw), lambda g, c: (c, g))
    rev = pl.BlockSpec((CHUNK, bw), lambda g, c: (_rev_chunk(c, ncb, nb), g))
    return pl.pallas_call(
        _rwkv_scan_kernel,
        grid=(dr // bw, nb),
        in_specs=[fwd] * 6 + [rev] * 6,
        out_specs=[fwd, rev],
        out_shape=[jax.ShapeDtypeStruct((n, dr), F32)] * 2,
        scratch_shapes=[pltpu.VMEM((2, bw // GROUP_W, GROUP_W, GROUP_W), F32)],
        compiler_params=_params(("arbitrary", "arbitrary")),
        name="rwkv_scan",
    )(r, lw0, k0, v, a, b0, r, lw1, k1, v, a, b1)


def _hgrn_chunk(q, k, v, lf, s_vk, rev):
    c, w = q.shape
    nsb = c // SUB
    bc = _cumsum_rows(lf, rev)
    last = bc[0:1] if rev else bc[c - 1:c]
    bdh = _block_diag_mask(w, HGRN_HEAD)
    v16 = v.astype(BF16)
    o_inter = _mxu_nt((q * jnp.exp(bc)).astype(BF16), s_vk.astype(BF16))
    s_new = s_vk * jnp.exp(last) + jnp.where(bdh, _mxu_tn(v16, (k * jnp.exp(last - bc)).astype(BF16)), 0.0)

    lane2 = lax.broadcasted_iota(jnp.int32, (c, w), 1)
    row2 = lax.broadcasted_iota(jnp.int32, (c, w), 0)
    zpad = jnp.zeros((HGRN_HEAD - c, w), BF16)

    def expand(z):
        parts = []
        for h in range(w // HGRN_HEAD):
            parts += [jnp.where(lane2 // HGRN_HEAD == h, z, jnp.zeros((), BF16)), zpad]
        return jnp.concatenate(parts, axis=0)

    ones_bd = bdh.astype(BF16)
    t3 = lax.broadcasted_iota(jnp.int32, (SUB, SUB, w), 0)
    s3 = lax.broadcasted_iota(jnp.int32, (SUB, SUB, w), 1)
    l3 = lax.broadcasted_iota(jnp.int32, (SUB, SUB, w), 2) % HGRN_HEAD
    causal = (s3 >= t3) if rev else (s3 <= t3)
    a_rows = []
    for i in range(nsb):
        lo, hi = SUB * i, SUB * (i + 1)
        qi, ki, bi = q[lo:hi], k[lo:hi], bc[lo:hi]
        d3 = bi[:, None, :] - bi[None, :, :]
        x3 = jnp.where(causal, (qi[:, None, :] * ki[None, :, :]) * jnp.exp(jnp.minimum(d3, 0.0)), 0.0)
        r3 = _mxu(x3.reshape(SUB * SUB, w).astype(BF16), ones_bd).reshape(SUB, SUB, w)
        a_i = jnp.sum(jnp.where(l3 == s3 + lo, r3, 0.0), axis=1)
        if (not rev and i > 0) or (rev and i < nsb - 1):
            beta = bc[hi:hi + 1] if rev else bc[lo - 1:lo]
            earlier = (row2 >= hi) if rev else (row2 < lo)
            kp = jnp.where(earlier, k * jnp.exp(jnp.minimum(beta - bc, 0.0)), 0.0)
            a_i = a_i + _mxu_nt((qi * jnp.exp(bi - beta)).astype(BF16), expand(kp.astype(BF16)))
        a_rows.append(a_i)
    o = o_inter + _mxu(jnp.concatenate(a_rows, axis=0).astype(BF16), expand(v16))
    return o, s_new


def _hgrn_chunks_bounded(chains):
    c, w = chains[0][0].shape
    every = range(len(chains))
    revs = [ch[5] for ch in chains]
    bdh = _block_diag_mask(w, HGRN_HEAD)
    lane2 = lax.broadcasted_iota(jnp.int32, (c, w), 1)
    t_i = lax.broadcasted_iota(jnp.int32, (c, w), 0)
    s_i = lane2 % HGRN_HEAD
    incl = {False: s_i <= t_i, True: s_i >= t_i}
    zpad = jnp.zeros((HGRN_HEAD - c, w), BF16)

    def expand(z):
        parts = []
        for h in range(w // HGRN_HEAD):
            parts += [jnp.where(lane2 // HGRN_HEAD == h, z, jnp.zeros((), BF16)), zpad]
        return jnp.concatenate(parts, axis=0)

    bc = [_cumsum_rows(ch[3], ch[5]) for ch in chains]
    last = [bc[i][0:1] if revs[i] else bc[i][c - 1:c] for i in every]
    qt = [(chains[i][0] * jnp.exp(bc[i])).astype(BF16) for i in every]
    kt = [(chains[i][1] * jnp.exp(-bc[i])).astype(BF16) for i in every]
    kd = [(chains[i][1] * jnp.exp(last[i] - bc[i])).astype(BF16) for i in every]
    v16 = [chains[i][2].astype(BF16) for i in every]
    o_inter = [_mxu_nt(qt[i], chains[i][4].astype(BF16)) for i in every]
    a = [jnp.where(incl[revs[i]], _mxu_nt(qt[i], expand(kt[i])), 0.0).astype(BF16) for i in every]
    o = [o_inter[i] + _mxu(a[i], expand(v16[i])) for i in every]
    ds = [_mxu_tn(v16[i], kd[i]) for i in every]
    s_new = [chains[i][4] * jnp.exp(last[i]) + jnp.where(bdh, ds[i], 0.0) for i in every]
    return o, s_new


def _hgrn_scan_kernel(qf, ff, vf, qr, fr, vr, lg_ref, of_o, or_o, s_ref, *, layer):
    @pl.when(pl.program_id(1) == 0)
    def _():
        s_ref[...] = jnp.zeros_like(s_ref)

    lg = lg_ref[...]
    e = jnp.exp(lg - jnp.max(lg, axis=0, keepdims=True))
    lb = jnp.sum(e[:layer + 1], axis=0) / jnp.sum(e, axis=0)
    ngs = qf.shape[1] // GROUP_W
    chains = []
    min_total = None
    for d, (q_ref, f_ref, v_ref) in enumerate(((qf, ff, vf), (qr, fr, vr))):
        fd = lb[d:d + 1, :] + (1.0 - lb[d:d + 1, :]) * _sigmoid(f_ref[...])
        qv = q_ref[...]
        qh, kh, vh, lf = qv * _sigmoid(qv), 1.0 - fd, v_ref[...], jnp.log(fd)
        total = jnp.min(jnp.sum(lf, axis=0, keepdims=True))
        min_total = total if min_total is None else jnp.minimum(min_total, total)
        for g in range(ngs):
            sl = slice(g * GROUP_W, (g + 1) * GROUP_W)
            chains.append((qh[:, sl], kh[:, sl], vh[:, sl], lf[:, sl], s_ref[d, g], d == 1))

    def emit(o, s_new):
        for d, o_ref in enumerate((of_o, or_o)):
            for g in range(ngs):
                o_ref[:, g * GROUP_W:(g + 1) * GROUP_W] = o[d * ngs + g]
                s_ref[d, g] = s_new[d * ngs + g]

    bounded = min_total >= -MAX_CHUNK_LOG_DECAY

    @pl.when(bounded)
    def _():
        emit(*_hgrn_chunks_bounded(chains))

    @pl.when(jnp.logical_not(bounded))
    def _():
        res = [_hgrn_chunk(*ch) for ch in chains]
        emit([r[0] for r in res], [r[1] for r in res])


def _hgrn_scan(p_h, lb_logits, lc, layer):
    n = p_h.shape[0]
    dh = lb_logits.shape[2]
    nb, ncb = n // CHUNK, lc // CHUNK
    bw = _pick(dh, HGRN_GROUPS_PER_STEP * GROUP_W, GROUP_W)
    ng = dh // bw
    blk = (CHUNK, bw)
    fwd = lambda sec: pl.BlockSpec(blk, lambda g, c: (c, sec * ng + g))
    rev = lambda sec: pl.BlockSpec(blk, lambda g, c: (_rev_chunk(c, ncb, nb), sec * ng + g))
    return pl.pallas_call(
        functools.partial(_hgrn_scan_kernel, layer=layer),
        grid=(ng, nb),
        in_specs=[fwd(0), fwd(1), fwd(3), rev(0), rev(2), rev(3),
                  pl.BlockSpec((lb_logits.shape[0], 2, bw), lambda g, c: (0, 0, g))],
        out_specs=[pl.BlockSpec(blk, lambda g, c: (c, g)),
                   pl.BlockSpec(blk, lambda g, c: (_rev_chunk(c, ncb, nb), g))],
        out_shape=[jax.ShapeDtypeStruct((n, dh), F32)] * 2,
        scratch_shapes=[pltpu.VMEM((2, bw // GROUP_W, GROUP_W, GROUP_W), F32)],
        compiler_params=_params(("arbitrary", "arbitrary")),
        name="hgrn_scan",
    )(p_h, p_h, p_h, p_h, p_h, p_h, lb_logits)


def _mix_out_kernel(yf, yr, bonus, gate, of, orv, g_ref, lng, lnb, hng, hs_ref, hst_ref, u_o, *, dr, dh):
    hs, hst = hs_ref[...], hst_ref[...]

    def headmean(z):
        return _head_sum(z, hs, hst) * (1.0 / RWKV_HEAD)

    y = yf[...] + yr[...]
    yc = y - headmean(y)
    yn = yc * lax.rsqrt(headmean(yc * yc) + GN_EPS) * lng[...] + lnb[...]
    u_o[:, 0:dr] = ((yn + bonus[...]) * gate[...]).astype(u_o.dtype)
    o = of[...] + orv[...]
    g = g_ref[...]
    sg = g * _sigmoid(g)
    for h in range(dh // HGRN_HEAD):
        sl = slice(h * HGRN_HEAD, (h + 1) * HGRN_HEAD)
        oh = o[:, sl]
        on = oh * lax.rsqrt(jnp.mean(oh * oh, axis=-1, keepdims=True) + NORM_EPS) * hng[...]
        u_o[:, dr + h * HGRN_HEAD:dr + (h + 1) * HGRN_HEAD] = (on * sg[:, sl]).astype(u_o.dtype)


def _mix_out(yf, yr, bonus, gate, of, orv, p_h, ln_g, ln_b, hg, lc, l):
    dr, dh = yf.shape[1], of.shape[1]
    tb = _pick(int(np.gcd(lc, l)), 128, 8)
    off = lc // tb
    hs, hst = _head_indicators(dr)
    rows = lambda wd: pl.BlockSpec((tb, wd), lambda i: (i + off, 0))
    full = lambda a: pl.BlockSpec(a.shape, lambda i: (0,) * a.ndim)
    consts = [ln_g.reshape(1, dr), ln_b.reshape(1, dr), hg.reshape(1, HGRN_HEAD), hs, hst]
    return pl.pallas_call(
        functools.partial(_mix_out_kernel, dr=dr, dh=dh),
        grid=(l // tb,),
        in_specs=[rows(dr)] * 4 + [rows(dh)] * 2 + [pl.BlockSpec((tb, dh), lambda i: (i + off, 4))]
                 + [full(a) for a in consts],
        out_specs=pl.BlockSpec((tb, dr + dh), lambda i: (i, 0)),
        out_shape=jax.ShapeDtypeStruct((l, dr + dh), PROJ_DTYPE),
        compiler_params=_params(("arbitrary",), VMEM_LIMIT),
        name="mix_out",
    )(yf, yr, bonus, gate, of, orv, p_h, *consts)


def _res1_kernel(ux_ref, x_ref, mod_ref, gpost_ref, gpre_ref, x1_o, h2_o, *, d):
    ux = ux_ref[...]
    nrm = ux * lax.rsqrt(jnp.mean(ux * ux, axis=-1, keepdims=True) + NORM_EPS) * gpost_ref[...]
    x1 = x_ref[...] + mod_ref[0:1, 2 * d:3 * d] * nrm
    x1_o[...] = x1
    hn = x1 * lax.rsqrt(jnp.mean(x1 * x1, axis=-1, keepdims=True) + NORM_EPS) * gpre_ref[...]
    h2_o[...] = (hn * (1.0 + mod_ref[0:1, 4 * d:5 * d]) + mod_ref[0:1, 3 * d:4 * d]).astype(h2_o.dtype)


def _res2_kernel(m_ref, x1_ref, mod_ref, gpost_ref, o_ref, *, d):
    mv = m_ref[...]
    nrm = mv * lax.rsqrt(jnp.mean(mv * mv, axis=-1, keepdims=True) + NORM_EPS) * gpost_ref[...]
    o_ref[...] = x1_ref[...] + mod_ref[0:1, 5 * d:6 * d] * nrm


def _res1(ux, x2, mod, g_post, g_pre):
    l, d = x2.shape
    tb = _pick(l, 256, 8)
    rows = pl.BlockSpec((tb, d), lambda i: (i, 0))
    full = lambda a: pl.BlockSpec(a.shape, lambda i: (0,) * a.ndim)
    gp, gq = g_post.reshape(1, d), g_pre.reshape(1, d)
    return pl.pallas_call(
        functools.partial(_res1_kernel, d=d),
        grid=(l // tb,),
        in_specs=[rows, rows, full(mod), full(gp), full(gq)],
        out_specs=[rows, rows],
        out_shape=[jax.ShapeDtypeStruct((l, d), F32), jax.ShapeDtypeStruct((l, d), PROJ_DTYPE)],
        compiler_params=_params(("arbitrary",), VMEM_LIMIT),
        name="residual_mix",
    )(ux, x2, mod, gp, gq)


def _res2(mlp, x1, mod, g_post):
    l, d = x1.shape
    tb = _pick(l, 256, 8)
    rows = pl.BlockSpec((tb, d), lambda i: (i, 0))
    full = lambda a: pl.BlockSpec(a.shape, lambda i: (0,) * a.ndim)
    gp = g_post.reshape(1, d)
    return pl.pallas_call(
        functools.partial(_res2_kernel, d=d),
        grid=(l // tb,),
        in_specs=[rows, rows, full(mod), full(gp)],
        out_specs=rows,
        out_shape=jax.ShapeDtypeStruct((l, d), F32),
        compiler_params=_params(("arbitrary",), VMEM_LIMIT),
        name="residual_ffn",
    )(mlp, x1, mod, gp)


def _pad_cols(a, width):
    return jnp.pad(a, ((0, 0), (0, width - a.shape[1])))


def _pad_rows(a, height):
    return jnp.pad(a, ((0, height - a.shape[0]), (0, 0)))


def _round_up(n, m):
    return (n + m - 1) // m * m


def kernel(x, c, ctx, c_ctx, w_ada, b_ada, g_mix_pre, g_mix_post, g_ffn_pre, g_ffn_post, w_in, mu_shift, w0, w2, a0,
           a2, g2, k_k, k_a, r_k, ln_x_g, ln_x_b, hgrn_lb_logits, hgrn_norm_g, w_out, w_ff1, w_ff2):
    assert x.shape[0] == 1 and w_in.shape[0] == 1, "single batch, single layer"
    layer = 0
    x2, ctx2 = x[0], ctx[0]
    l, d = x2.shape
    lc = ctx2.shape[0]
    dr, dh = k_k.shape[1], hgrn_lb_logits.shape[2]
    rd_raw, ra_raw, rg_raw = w2.shape[2], a2.shape[2], g2.shape[1]
    rd, ra, rg = (_round_up(v, LANE) for v in (rd_raw, ra_raw, rg_raw))
    assert l % GRID_W == 0 and lc % CHUNK == 0 and dr % GROUP_W == 0 and dh % GROUP_W == 0

    cvec = jnp.concatenate([c, c_ctx[None, :], jnp.zeros((6, d), F32)], axis=0)
    mod = _ada(cvec, w_ada[layer], b_ada[layer])

    wi = w_in[layer]
    mu = mu_shift[layer][None, :]
    sel = (jnp.arange(mu.shape[1], dtype=jnp.int32) % 4)[None, :]
    o = 3 * dr
    cuts = [(0, o, o)]
    for raw, padded in ((rd_raw, rd), (rd_raw, rd), (ra_raw, ra), (ra_raw, ra), (rg_raw, rg)):
        cuts.append((o, o + raw, padded))
        o += raw
    rw_cols = o
    regroup = lambda a: jnp.concatenate([_pad_cols(a[:, s:e], wd) for s, e, wd in cuts], axis=1)
    w_r = regroup(wi[:, :rw_cols]).astype(PROJ_DTYPE)
    w_h = wi[:, rw_cols:].astype(PROJ_DTYPE)
    mu_p, sel_p = regroup(mu), regroup(sel)
    w2p = jnp.stack([_pad_rows(w2[layer, dd], rd) for dd in range(2)])
    a2p = jnp.stack([_pad_rows(a2[layer, dd], ra) for dd in range(2)])
    g2p = _pad_rows(g2[layer], rg)

    h = _prep(ctx2, x2, g_mix_pre[layer], mod)
    p_r = _mm(h, w_r, F32, tm_target=768, tn_target=1152, name="in_proj_rwkv")
    p_h = _mm(h, w_h, F32, tm_target=768, tn_target=2048, name="in_proj_hgrn")

    (r, v, a, lw0, lw1, k0, k1, b0, b1, bonus, gate) = _rwkv_features(
        p_r, sel_p, mu_p, w0[layer], w2p, a0[layer], a2p, g2p, k_k[layer], k_a[layer], r_k[layer].reshape(-1), lc, l)
    yf, yr = _rwkv_scan(r, v, a, lw0, lw1, k0, k1, b0, b1, lc)
    of, orv = _hgrn_scan(p_h, hgrn_lb_logits, lc, layer)

    u = _mix_out(yf, yr, bonus, gate, of, orv, p_h, ln_x_g[layer], ln_x_b[layer], hgrn_norm_g[layer], lc, l)
    ux = _mm(u, w_out[layer].astype(PROJ_DTYPE), F32, name="out_proj")
    x1, h2 = _res1(ux, x2, mod, g_mix_post[layer], g_ffn_pre[layer])
    act = _mm(h2, w_ff1[layer].astype(PROJ_DTYPE), PROJ_DTYPE, act="relu2", tn_target=2048, name="ffn_up")
    mlp = _mm_ksplit(act, w_ff2[layer].astype(PROJ_DTYPE), tk_target=4096, name="ffn_down")
    out = _res2(mlp, x1, mod, g_ffn_post[layer])
    return out[None]
```

```python
import functools

import jax
import jax.numpy as jnp
import numpy as np
from jax import lax
from jax.experimental import pallas as pl
from jax.experimental.pallas import tpu as pltpu

F32 = jnp.float32
BF16 = jnp.bfloat16
PROJ_DTYPE = jnp.bfloat16

LANE = 128
GRID_W = 64
CHUNK = GRID_W
RWKV_HEAD = 64
HGRN_HEAD = 128
SUB = 16
GROUP_W = 256
RWKV_GROUPS_PER_STEP = 8
HGRN_GROUPS_PER_STEP = 8
MAX_CHUNK_LOG_DECAY = 60.0
NORM_EPS = 1e-6
GN_EPS = 64e-5
EXP_M05 = float(np.exp(-0.5))
VMEM_LIMIT = 56 * 1024 * 1024


def _pick(n, target, unit=LANE):
    best = None
    for m in range(unit, min(n, target) + 1, unit):
        if n % m == 0:
            best = m
    return best if best is not None else n


def _sigmoid(z):
    return 1.0 / (1.0 + jnp.exp(-z))


def _mxu(a, b):
    return jnp.dot(a, b, preferred_element_type=F32)


def _mxu_nt(a, b):
    return lax.dot_general(a, b, (((1,), (1,)), ((), ())), preferred_element_type=F32)


def _mxu_tn(a, b):
    return lax.dot_general(a, b, (((0,), (0,)), ((), ())), preferred_element_type=F32)


def _split3(z):
    hi = z.astype(BF16)
    rest = z - hi.astype(F32)
    mid = rest.astype(BF16)
    return hi, mid, (rest - mid.astype(F32)).astype(BF16)


def _exact_mxu(z, sel):
    hi, mid, lo = _split3(z)
    return _mxu(hi, sel) + _mxu(mid, sel) + _mxu(lo, sel)


def _head_sum(z, hs, hst):
    return _exact_mxu(_exact_mxu(z, hs), hst)


def _head_indicators(dr):
    hs = np.arange(dr)[:, None] // RWKV_HEAD == np.arange(LANE)[None, :]
    return jnp.asarray(hs, BF16), jnp.asarray(hs.T, BF16)


def _params(sem, vmem=None):
    return pltpu.CompilerParams(dimension_semantics=sem, vmem_limit_bytes=vmem)


def _ada_kernel(c_ref, w_ref, b_ref, o_ref):
    cv = c_ref[...]
    rows = cv.shape[0]
    hi, mid, lo = (t.astype(F32) for t in _split3(cv * _sigmoid(cv)))
    s3 = jnp.concatenate([hi, mid, lo], axis=0).astype(BF16)
    w = w_ref[...]
    w_hi = w.astype(BF16)
    w_lo = (w - w_hi.astype(F32)).astype(BF16)
    p = _mxu(s3, w_hi)
    q = _mxu(s3[:2 * rows], w_lo)
    o_ref[...] = (p[:rows] + p[rows:2 * rows] + p[2 * rows:] + q[:rows] + q[rows:]) + b_ref[...]


def _ada(cvec, w, b):
    rows, d = cvec.shape
    n = w.shape[1]
    tn = _pick(n, 512)
    return pl.pallas_call(
        _ada_kernel,
        grid=(n // tn,),
        in_specs=[pl.BlockSpec((rows, d), lambda j: (0, 0)),
                  pl.BlockSpec((d, tn), lambda j: (0, j)),
                  pl.BlockSpec((1, tn), lambda j: (0, j))],
        out_specs=pl.BlockSpec((rows, tn), lambda j: (0, j)),
        out_shape=jax.ShapeDtypeStruct((rows, n), F32),
        compiler_params=_params(("arbitrary",), VMEM_LIMIT),
        name="ada_mod",
    )(cvec, w, b.reshape(1, n))


def _prep_kernel(ctx_ref, x_ref, g_ref, mod_ref, o_ref, *, ncb, d):
    is_ctx = pl.program_id(0) < ncb
    rows = jnp.where(is_ctx, ctx_ref[...], x_ref[...])
    ms = jnp.mean(rows * rows, axis=-1, keepdims=True)
    hn = rows * lax.rsqrt(ms + NORM_EPS) * g_ref[...]
    shift = jnp.where(is_ctx, mod_ref[1:2, 0:d], mod_ref[0:1, 0:d])
    scale = jnp.where(is_ctx, mod_ref[1:2, d:2 * d], mod_ref[0:1, d:2 * d])
    o_ref[...] = (hn * (1.0 + scale) + shift).astype(o_ref.dtype)


def _prep(ctx2, x2, g, mod):
    lc, d = ctx2.shape
    l = x2.shape[0]
    tb = _pick(int(np.gcd(lc, l)), 256, 8)
    ncb = lc // tb
    return pl.pallas_call(
        functools.partial(_prep_kernel, ncb=ncb, d=d),
        grid=((lc + l) // tb,),
        in_specs=[pl.BlockSpec((tb, d), lambda i: (jnp.minimum(i, ncb - 1), 0)),
                  pl.BlockSpec((tb, d), lambda i: (jnp.maximum(i - ncb, 0), 0)),
                  pl.BlockSpec((1, d), lambda i: (0, 0)),
                  pl.BlockSpec(mod.shape, lambda i: (0, 0))],
        out_specs=pl.BlockSpec((tb, d), lambda i: (i, 0)),
        out_shape=jax.ShapeDtypeStruct((lc + l, d), PROJ_DTYPE),
        compiler_params=_params(("arbitrary",)),
        name="norm_modulate",
    )(ctx2, x2, g.reshape(1, d), mod)


def _mm_kernel(x_ref, w_ref, o_ref, *, act):
    acc = jnp.dot(x_ref[...], w_ref[...], preferred_element_type=F32)
    if act == "relu2":
        acc = jnp.square(jnp.maximum(acc, 0.0))
    o_ref[...] = acc.astype(o_ref.dtype)


def _mm(x, w, out_dtype, act=None, tm_target=1024, tn_target=1024, name="matmul"):
    m, k = x.shape
    n = w.shape[1]
    tm = _pick(m, tm_target)
    tn = _pick(n, tn_target)
    return pl.pallas_call(
        functools.partial(_mm_kernel, act=act),
        grid=(m // tm, n // tn),
        in_specs=[pl.BlockSpec((tm, k), lambda i, j: (i, 0)),
                  pl.BlockSpec((k, tn), lambda i, j: (0, j))],
        out_specs=pl.BlockSpec((tm, tn), lambda i, j: (i, j)),
        out_shape=jax.ShapeDtypeStruct((m, n), out_dtype),
        compiler_params=_params(("arbitrary", "arbitrary"), VMEM_LIMIT),
        name=name,
    )(x, w)


def _ffn_down_kernel(a_ref, w_ref, x1_hbm, mod_ref, g_ref, o_ref, x1_buf, sem, *, d):
    i, kk = pl.program_id(0), pl.program_id(1)
    tm = x1_buf.shape[0]
    x1_copy = pltpu.make_async_copy(x1_hbm.at[pl.ds(pl.multiple_of(i * tm, tm), tm), :], x1_buf, sem)

    @pl.when(kk == 0)
    def _():
        x1_copy.start()
        o_ref[...] = jnp.zeros_like(o_ref)

    o_ref[...] += jnp.dot(a_ref[...], w_ref[...], preferred_element_type=F32)

    @pl.when(kk == pl.num_programs(1) - 1)
    def _():
        x1_copy.wait()
        gate = mod_ref[0:1, 5 * d:6 * d] * g_ref[...]
        rb = _pick(tm, 128, 8)
        for r0 in range(0, tm, rb):
            mv = o_ref[r0:r0 + rb, :]
            scale = lax.rsqrt(jnp.mean(mv * mv, axis=-1, keepdims=True) + NORM_EPS)
            o_ref[r0:r0 + rb, :] = x1_buf[r0:r0 + rb, :] + gate * (mv * scale)


def _ffn_down_residual(act, w, x1, mod, g_post, tm_target=512, tk_target=1024):
    m, k = act.shape
    d = w.shape[1]
    tm, tk = _pick(m, tm_target, 8), _pick(k, tk_target)
    full = lambda a: pl.BlockSpec(a.shape, lambda i, kk: (0,) * a.ndim)
    gp = g_post.reshape(1, d)
    return pl.pallas_call(
        functools.partial(_ffn_down_kernel, d=d),
        grid=(m // tm, k // tk),
        in_specs=[pl.BlockSpec((tm, tk), lambda i, kk: (i, kk)),
                  pl.BlockSpec((tk, d), lambda i, kk: (kk, 0)),
                  pl.BlockSpec(memory_space=pl.ANY), full(mod), full(gp)],
        out_specs=pl.BlockSpec((tm, d), lambda i, kk: (i, 0)),
        out_shape=jax.ShapeDtypeStruct((m, d), F32),
        scratch_shapes=[pltpu.VMEM((tm, d), F32), pltpu.SemaphoreType.DMA(())],
        compiler_params=_params(("arbitrary", "arbitrary"), VMEM_LIMIT),
        name="ffn_down",
    )(act, w, x1, mod, gp)


def _rwkv_feat_kernel(pp_ref, pc_ref, pn_ref, sel_ref, mu_ref, w0_ref, w2_ref, a0_ref, a2_ref, g2_ref,
                      kk_ref, ka_ref, rk_ref, hs_ref, hst_ref,
                      r_o, v_o, a_o, lw0_o, lw1_o, k0_o, k1_o, b0_o, b1_o, bonus_o, gate_o,
                      *, ncb, nrows, dr, rd, ra, rg):
    i = pl.program_id(0)
    is_ctx = i < ncb
    xi = i - ncb
    cur, prv, nxt = pc_ref[...], pp_ref[...], pn_ref[...]
    c = cur.shape[0]
    row = lax.broadcasted_iota(jnp.int32, cur.shape, 0)
    prev_last = jnp.where(is_ctx & (i > 0), prv[c - 1:c, :], 0.0)
    next_first = jnp.where(is_ctx & (i < ncb - 1), nxt[0:1, :], 0.0)
    before = jnp.where(row == 0, prev_last, pltpu.roll(cur, 1, 0))
    after = jnp.where(row == c - 1, next_first, pltpu.roll(cur, c - 1, 0))
    above = jnp.where(is_ctx, before, jnp.where(xi > 0, prv, 0.0))
    below = jnp.where(is_ctx, after, jnp.where(xi < nrows - 1, nxt, 0.0))
    sel = sel_ref[...]
    shifted = jnp.where(sel == 0, before, jnp.where(sel == 1, after, jnp.where(sel == 2, above, below)))
    m = cur + mu_ref[...] * (shifted - cur)

    r, k, v = m[:, 0:dr], m[:, dr:2 * dr], m[:, 2 * dr:3 * dr]
    o = 3 * dr
    wl = (m[:, o:o + rd], m[:, o + rd:o + 2 * rd])
    o += 2 * rd
    al = (m[:, o:o + ra], m[:, o + ra:o + 2 * ra])
    o += 2 * ra
    gl = m[:, o:o + rg]

    hs, hst = hs_ref[...], hst_ref[...]

    def headsum(z):
        return _head_sum(z, hs, hst)

    kkf = k * kk_ref[...]
    kk = kkf * lax.rsqrt(headsum(kkf * kkf) + 1e-12)
    r_o[...] = r
    v_o[...] = v
    a_o[...] = -kk
    for d, (lw_o, k_o, b_o) in enumerate(((lw0_o, k0_o, b0_o), (lw1_o, k1_o, b1_o))):
        wd = _mxu(jnp.tanh(wl[d]).astype(BF16), w2_ref[d]) + w0_ref[d:d + 1, :]
        lw_o[...] = -EXP_M05 * _sigmoid(wd)
        ad = _sigmoid(_mxu(al[d].astype(BF16), a2_ref[d]) + a0_ref[d:d + 1, :])
        k_o[...] = k * (1.0 + (ad - 1.0) * ka_ref[...])
        b_o[...] = kk * ad
    bonus_o[...] = headsum(r * k * rk_ref[...]) * v
    gate_o[...] = _mxu(_sigmoid(gl).astype(BF16), g2_ref[...])


def _rwkv_features(p_r, sel, mu, w0, w2p, a0, a2p, g2p, k_k, k_a, r_k, lc, l):
    n, nr = p_r.shape
    dr = k_k.shape[0]
    rd, ra, rg = w2p.shape[1], a2p.shape[1], g2p.shape[0]
    nb, ncb = n // CHUNK, lc // CHUNK
    nheads = dr // RWKV_HEAD
    assert nheads <= LANE
    hs, hst = _head_indicators(dr)
    full = lambda a: pl.BlockSpec(a.shape, lambda i: (0,) * a.ndim)
    row1 = lambda a: a.reshape(1, -1)
    consts = [sel, mu, w0, w2p.astype(BF16), a0, a2p.astype(BF16), g2p.astype(BF16), row1(k_k), row1(k_a),
              row1(r_k), hs, hst]
    blk = (CHUNK, nr)
    outs = pl.pallas_call(
        functools.partial(_rwkv_feat_kernel, ncb=ncb, nrows=l // CHUNK, dr=dr, rd=rd, ra=ra, rg=rg),
        grid=(nb,),
        in_specs=[pl.BlockSpec(blk, lambda i: (jnp.maximum(i - 1, 0), 0)),
                  pl.BlockSpec(blk, lambda i: (i, 0)),
                  pl.BlockSpec(blk, lambda i: (jnp.minimum(i + 1, nb - 1), 0))] + [full(a) for a in consts],
        out_specs=[pl.BlockSpec((CHUNK, dr), lambda i: (i, 0))] * 11,
        out_shape=[jax.ShapeDtypeStruct((n, dr), F32)] * 11,
        compiler_params=_params(("arbitrary",), VMEM_LIMIT),
        name="rwkv_features",
    )(p_r, p_r, p_r, *consts)
    return outs


def _block_diag_mask(n, w):
    r = lax.broadcasted_iota(jnp.int32, (n, n), 0)
    c = lax.broadcasted_iota(jnp.int32, (n, n), 1)
    return (r // w) == (c // w)


def _cumsum_rows(z, rev):
    c = z.shape[0]
    t = lax.broadcasted_iota(jnp.int32, (c, c), 0)
    s = lax.broadcasted_iota(jnp.int32, (c, c), 1)
    tri = ((s >= t) if rev else (s <= t)).astype(BF16)
    hi, mid, lo = _split3(z)
    return _mxu(tri, hi) + _mxu(tri, mid) + _mxu(tri, lo)


def _rwkv_chunks(chains):
    c, w = chains[0][0].shape
    nh = w // c
    every = range(len(chains))
    revs = [ch[7] for ch in chains]
    bd = _block_diag_mask(w, c)
    t_i = lax.broadcasted_iota(jnp.int32, (c, w), 0)
    s_i = lax.broadcasted_iota(jnp.int32, (c, w), 1) % c
    eye = (s_i == t_i).astype(F32)
    strict = {False: s_i < t_i, True: s_i > t_i}
    incl = {False: s_i <= t_i, True: s_i >= t_i}

    def expand(z):
        return jnp.where(bd, jnp.concatenate([z] * nh, axis=0), jnp.zeros((), z.dtype))

    def packed_mm(lhs, rhs):
        return _mxu(lhs.astype(BF16), expand(rhs.astype(BF16)))

    bc = [_cumsum_rows(ch[1], ch[7]) for ch in chains]
    ar, bk, v16 = [], [], []
    for i, (r, lw, k, v, a, b, _, _) in enumerate(chains):
        e_in, e_ex, e_ng = jnp.exp(bc[i]), jnp.exp(bc[i] - lw), jnp.exp(-bc[i])
        ar.append(jnp.concatenate([a * e_ex, r * e_in], axis=0).astype(BF16))
        bk.append(jnp.concatenate([b * e_ng, k * e_ng], axis=0).astype(BF16))
        v16.append(v.astype(BF16))
    gb = [_mxu_nt(ar[i], expand(bk[i][:c])) for i in every]
    gk = [_mxu_nt(ar[i], expand(bk[i][c:])) for i in every]
    l_ab = [jnp.where(strict[revs[i]], gb[i][:c], 0.0) for i in every]
    m_rb = [jnp.where(incl[revs[i]], gb[i][c:], 0.0) for i in every]
    lmk = [jnp.concatenate([jnp.where(strict[revs[i]], gk[i][:c], 0.0),
                            jnp.where(incl[revs[i]], gk[i][c:], 0.0)], axis=0) for i in every]
    t_m = [eye + l_ab[i] for i in every]
    p_m = [packed_mm(l_ab[i], l_ab[i]) for i in every]
    for _ in range(int(np.log2(c)) - 2):
        tp = [packed_mm(jnp.concatenate([t_m[i], p_m[i]], axis=0), p_m[i]) for i in every]
        t_m = [t_m[i] + tp[i][:c] for i in every]
        p_m = [tp[i][c:] for i in every]
    t_m = [t_m[i] + packed_mm(t_m[i], p_m[i]) for i in every]

    ars = [_mxu_nt(ar[i], chains[i][6].astype(BF16)) for i in every]
    lm = [_mxu(lmk[i].astype(BF16), expand(v16[i])) for i in every]
    u16 = [packed_mm(t_m[i], ars[i][:c] + lm[i][:c]).astype(BF16) for i in every]
    ys = [ars[i][c:] + _mxu(m_rb[i].astype(BF16), expand(u16[i])) + lm[i][c:] for i in every]
    ds = [_mxu_tn(jnp.concatenate([u16[i], v16[i]], axis=0), bk[i]) for i in every]
    s_new = []
    for i in every:
        last = bc[i][0:1] if revs[i] else bc[i][c - 1:c]
        s_new.append((chains[i][6] + jnp.where(bd, ds[i], 0.0)) * jnp.exp(last))
    return ys, s_new


def _rwkv_scan_kernel(rf, lwf, kf, vf, af, bf, rr, lwr, kr, vr, ar, br, yf_o, yr_o, s_ref):
    @pl.when(pl.program_id(1) == 0)
    def _():
        s_ref[...] = jnp.zeros_like(s_ref)

    ngs = rf.shape[1] // GROUP_W
    chains = []
    for d, refs in enumerate(((rf, lwf, kf, vf, af, bf), (rr, lwr, kr, vr, ar, br))):
        for g in range(ngs):
            sl = slice(g * GROUP_W, (g + 1) * GROUP_W)
            chains.append(tuple(ref[:, sl] for ref in refs) + (s_ref[d, g], d == 1))
    ys, s_new = _rwkv_chunks(chains)
    for d, o_ref in enumerate((yf_o, yr_o)):
        for g in range(ngs):
            o_ref[:, g * GROUP_W:(g + 1) * GROUP_W] = ys[d * ngs + g]
            s_ref[d, g] = s_new[d * ngs + g]


def _rev_chunk(c, ncb, nb):
    return jnp.where(c < ncb, ncb - 1 - c, nb - 1 - (c - ncb))


def _rwkv_scan(r, v, a, lw0, lw1, k0, k1, b0, b1, lc):
    n, dr = r.shape
    nb, ncb = n // CHUNK, lc // CHUNK
    bw = _pick(dr, RWKV_GROUPS_PER_STEP * GROUP_W, GROUP_W)
    fwd = pl.BlockSpec((CHUNK, bw), lambda g, c: (c, g))
    rev = pl.BlockSpec((CHUNK, bw), lambda g, c: (_rev_chunk(c, ncb, nb), g))
    return pl.pallas_call(
        _rwkv_scan_kernel,
        grid=(dr // bw, nb),
        in_specs=[fwd] * 6 + [rev] * 6,
        out_specs=[fwd, rev],
        out_shape=[jax.ShapeDtypeStruct((n, dr), F32)] * 2,
        scratch_shapes=[pltpu.VMEM((2, bw // GROUP_W, GROUP_W, GROUP_W), F32)],
        compiler_params=_params(("arbitrary", "arbitrary")),
        name="rwkv_scan",
    )(r, lw0, k0, v, a, b0, r, lw1, k1, v, a, b1)


def _hgrn_chunk(q, k, v, lf, s_vk, rev):
    c, w = q.shape
    nsb = c // SUB
    bc = _cumsum_rows(lf, rev)
    last = bc[0:1] if rev else bc[c - 1:c]
    bdh = _block_diag_mask(w, HGRN_HEAD)
    v16 = v.astype(BF16)
    o_inter = _mxu_nt((q * jnp.exp(bc)).astype(BF16), s_vk.astype(BF16))
    s_new = s_vk * jnp.exp(last) + jnp.where(bdh, _mxu_tn(v16, (k * jnp.exp(last - bc)).astype(BF16)), 0.0)

    lane2 = lax.broadcasted_iota(jnp.int32, (c, w), 1)
    row2 = lax.broadcasted_iota(jnp.int32, (c, w), 0)
    zpad = jnp.zeros((HGRN_HEAD - c, w), BF16)

    def expand(z):
        parts = []
        for h in range(w // HGRN_HEAD):
            parts += [jnp.where(lane2 // HGRN_HEAD == h, z, jnp.zeros((), BF16)), zpad]
        return jnp.concatenate(parts, axis=0)

    ones_bd = bdh.astype(BF16)
    t3 = lax.broadcasted_iota(jnp.int32, (SUB, SUB, w), 0)
    s3 = lax.broadcasted_iota(jnp.int32, (SUB, SUB, w), 1)
    l3 = lax.broadcasted_iota(jnp.int32, (SUB, SUB, w), 2) % HGRN_HEAD
    causal = (s3 >= t3) if rev else (s3 <= t3)
    a_rows = []
    for i in range(nsb):
        lo, hi = SUB * i, SUB * (i + 1)
        qi, ki, bi = q[lo:hi], k[lo:hi], bc[lo:hi]
        d3 = bi[:, None, :] - bi[None, :, :]
        x3 = jnp.where(causal, (qi[:, None, :] * ki[None, :, :]) * jnp.exp(jnp.minimum(d3, 0.0)), 0.0)
        r3 = _mxu(x3.reshape(SUB * SUB, w).astype(BF16), ones_bd).reshape(SUB, SUB, w)
        a_i = jnp.sum(jnp.where(l3 == s3 + lo, r3, 0.0), axis=1)
        if (not rev and i > 0) or (rev and i < nsb - 1):
            beta = bc[hi:hi + 1] if rev else bc[lo - 1:lo]
            earlier = (row2 >= hi) if rev else (row2 < lo)
            kp = jnp.where(earlier, k * jnp.exp(jnp.minimum(beta - bc, 0.0)), 0.0)
            a_i = a_i + _mxu_nt((qi * jnp.exp(bi - beta)).astype(BF16), expand(kp.astype(BF16)))
        a_rows.append(a_i)
    o = o_inter + _mxu(jnp.concatenate(a_rows, axis=0).astype(BF16), expand(v16))
    return o, s_new


def _hgrn_chunks_bounded(chains):
    c, w = chains[0][0].shape
    every = range(len(chains))
    revs = [ch[5] for ch in chains]
    bdh = _block_diag_mask(w, HGRN_HEAD)
    lane2 = lax.broadcasted_iota(jnp.int32, (c, w), 1)
    t_i = lax.broadcasted_iota(jnp.int32, (c, w), 0)
    s_i = lane2 % HGRN_HEAD
    incl = {False: s_i <= t_i, True: s_i >= t_i}
    zpad = jnp.zeros((HGRN_HEAD - c, w), BF16)

    def expand(z):
        parts = []
        for h in range(w // HGRN_HEAD):
            parts += [jnp.where(lane2 // HGRN_HEAD == h, z, jnp.zeros((), BF16)), zpad]
        return jnp.concatenate(parts, axis=0)

    bc = [_cumsum_rows(ch[3], ch[5]) for ch in chains]
    last = [bc[i][0:1] if revs[i] else bc[i][c - 1:c] for i in every]
    qt = [(chains[i][0] * jnp.exp(bc[i])).astype(BF16) for i in every]
    kt = [(chains[i][1] * jnp.exp(-bc[i])).astype(BF16) for i in every]
    kd = [(chains[i][1] * jnp.exp(last[i] - bc[i])).astype(BF16) for i in every]
    v16 = [chains[i][2].astype(BF16) for i in every]
    o_inter = [_mxu_nt(qt[i], chains[i][4].astype(BF16)) for i in every]
    a = [jnp.where(incl[revs[i]], _mxu_nt(qt[i], expand(kt[i])), 0.0).astype(BF16) for i in every]
    o = [o_inter[i] + _mxu(a[i], expand(v16[i])) for i in every]
    ds = [_mxu_tn(v16[i], kd[i]) for i in every]
    s_new = [chains[i][4] * jnp.exp(last[i]) + jnp.where(bdh, ds[i], 0.0) for i in every]
    return o, s_new


def _hgrn_scan_kernel(qf, ff, vf, qr, fr, vr, lg_ref, of_o, or_o, s_ref, *, layer):
    @pl.when(pl.program_id(1) == 0)
    def _():
        s_ref[...] = jnp.zeros_like(s_ref)

    lg = lg_ref[...]
    e = jnp.exp(lg - jnp.max(lg, axis=0, keepdims=True))
    lb = jnp.sum(e[:layer + 1], axis=0) / jnp.sum(e, axis=0)
    ngs = qf.shape[1] // GROUP_W
    chains = []
    min_total = None
    for d, (q_ref, f_ref, v_ref) in enumerate(((qf, ff, vf), (qr, fr, vr))):
        fd = lb[d:d + 1, :] + (1.0 - lb[d:d + 1, :]) * _sigmoid(f_ref[...])
        qv = q_ref[...]
        qh, kh, vh, lf = qv * _sigmoid(qv), 1.0 - fd, v_ref[...], jnp.log(fd)
        total = jnp.min(jnp.sum(lf, axis=0, keepdims=True))
        min_total = total if min_total is None else jnp.minimum(min_total, total)
        for g in range(ngs):
            sl = slice(g * GROUP_W, (g + 1) * GROUP_W)
            chains.append((qh[:, sl], kh[:, sl], vh[:, sl], lf[:, sl], s_ref[d, g], d == 1))

    def emit(o, s_new):
        for d, o_ref in enumerate((of_o, or_o)):
            for g in range(ngs):
                o_ref[:, g * GROUP_W:(g + 1) * GROUP_W] = o[d * ngs + g]
                s_ref[d, g] = s_new[d * ngs + g]

    bounded = min_total >= -MAX_CHUNK_LOG_DECAY

    @pl.when(bounded)
    def _():
        emit(*_hgrn_chunks_bounded(chains))

    @pl.when(jnp.logical_not(bounded))
    def _():
        res = [_hgrn_chunk(*ch) for ch in chains]
        emit([r[0] for r in res], [r[1] for r in res])


def _hgrn_scan(p_h, lb_logits, lc, layer):
    n = p_h.shape[0]
    dh = lb_logits.shape[2]
    nb, ncb = n // CHUNK, lc // CHUNK
    bw = _pick(dh, HGRN_GROUPS_PER_STEP * GROUP_W, GROUP_W)
    ng = dh // bw
    blk = (CHUNK, bw)
    fwd = lambda sec: pl.BlockSpec(blk, lambda g, c: (c, sec * ng + g))
    rev = lambda sec: pl.BlockSpec(blk, lambda g, c: (_rev_chunk(c, ncb, nb), sec * ng + g))
    return pl.pallas_call(
        functools.partial(_hgrn_scan_kernel, layer=layer),
        grid=(ng, nb),
        in_specs=[fwd(0), fwd(1), fwd(3), rev(0), rev(2), rev(3),
                  pl.BlockSpec((lb_logits.shape[0], 2, bw), lambda g, c: (0, 0, g))],
        out_specs=[pl.BlockSpec(blk, lambda g, c: (c, g)),
                   pl.BlockSpec(blk, lambda g, c: (_rev_chunk(c, ncb, nb), g))],
        out_shape=[jax.ShapeDtypeStruct((n, dh), F32)] * 2,
        scratch_shapes=[pltpu.VMEM((2, bw // GROUP_W, GROUP_W, GROUP_W), F32)],
        compiler_params=_params(("arbitrary", "arbitrary")),
        name="hgrn_scan",
    )(p_h, p_h, p_h, p_h, p_h, p_h, lb_logits)


def _mix_out_kernel(yf, yr, bonus, gate, of, orv, g_ref, lng, lnb, hng, hs_ref, hst_ref, u_o, *, dr, dh):
    hs, hst = hs_ref[...], hst_ref[...]

    def headmean(z):
        return _head_sum(z, hs, hst) * (1.0 / RWKV_HEAD)

    y = yf[...] + yr[...]
    yc = y - headmean(y)
    yn = yc * lax.rsqrt(headmean(yc * yc) + GN_EPS) * lng[...] + lnb[...]
    u_o[:, 0:dr] = ((yn + bonus[...]) * gate[...]).astype(u_o.dtype)
    o = of[...] + orv[...]
    g = g_ref[...]
    sg = g * _sigmoid(g)
    for h in range(dh // HGRN_HEAD):
        sl = slice(h * HGRN_HEAD, (h + 1) * HGRN_HEAD)
        oh = o[:, sl]
        on = oh * lax.rsqrt(jnp.mean(oh * oh, axis=-1, keepdims=True) + NORM_EPS) * hng[...]
        u_o[:, dr + h * HGRN_HEAD:dr + (h + 1) * HGRN_HEAD] = (on * sg[:, sl]).astype(u_o.dtype)


def _mix_out(yf, yr, bonus, gate, of, orv, p_h, ln_g, ln_b, hg, lc, l):
    dr, dh = yf.shape[1], of.shape[1]
    tb = _pick(int(np.gcd(lc, l)), 128, 8)
    off = lc // tb
    hs, hst = _head_indicators(dr)
    rows = lambda wd: pl.BlockSpec((tb, wd), lambda i: (i + off, 0))
    full = lambda a: pl.BlockSpec(a.shape, lambda i: (0,) * a.ndim)
    consts = [ln_g.reshape(1, dr), ln_b.reshape(1, dr), hg.reshape(1, HGRN_HEAD), hs, hst]
    return pl.pallas_call(
        functools.partial(_mix_out_kernel, dr=dr, dh=dh),
        grid=(l // tb,),
        in_specs=[rows(dr)] * 4 + [rows(dh)] * 2 + [pl.BlockSpec((tb, dh), lambda i: (i + off, 4))]
                 + [full(a) for a in consts],
        out_specs=pl.BlockSpec((tb, dr + dh), lambda i: (i, 0)),
        out_shape=jax.ShapeDtypeStruct((l, dr + dh), PROJ_DTYPE),
        compiler_params=_params(("arbitrary",), VMEM_LIMIT),
        name="mix_out",
    )(yf, yr, bonus, gate, of, orv, p_h, *consts)


def _res1_kernel(ux_ref, x_ref, mod_ref, gpost_ref, gpre_ref, x1_o, h2_o, *, d):
    ux = ux_ref[...]
    nrm = ux * lax.rsqrt(jnp.mean(ux * ux, axis=-1, keepdims=True) + NORM_EPS) * gpost_ref[...]
    x1 = x_ref[...] + mod_ref[0:1, 2 * d:3 * d] * nrm
    x1_o[...] = x1
    hn = x1 * lax.rsqrt(jnp.mean(x1 * x1, axis=-1, keepdims=True) + NORM_EPS) * gpre_ref[...]
    h2_o[...] = (hn * (1.0 + mod_ref[0:1, 4 * d:5 * d]) + mod_ref[0:1, 3 * d:4 * d]).astype(h2_o.dtype)


def _res1(ux, x2, mod, g_post, g_pre):
    l, d = x2.shape
    tb = _pick(l, 256, 8)
    rows = pl.BlockSpec((tb, d), lambda i: (i, 0))
    full = lambda a: pl.BlockSpec(a.shape, lambda i: (0,) * a.ndim)
    gp, gq = g_post.reshape(1, d), g_pre.reshape(1, d)
    return pl.pallas_call(
        functools.partial(_res1_kernel, d=d),
        grid=(l // tb,),
        in_specs=[rows, rows, full(mod), full(gp), full(gq)],
        out_specs=[rows, rows],
        out_shape=[jax.ShapeDtypeStruct((l, d), F32), jax.ShapeDtypeStruct((l, d), PROJ_DTYPE)],
        compiler_params=_params(("arbitrary",), VMEM_LIMIT),
        name="residual_mix",
    )(ux, x2, mod, gp, gq)


def _pad_cols(a, width):
    return jnp.pad(a, ((0, 0), (0, width - a.shape[1])))


def _pad_rows(a, height):
    return jnp.pad(a, ((0, height - a.shape[0]), (0, 0)))


def _round_up(n, m):
    return (n + m - 1) // m * m


def kernel(x, c, ctx, c_ctx, w_ada, b_ada, g_mix_pre, g_mix_post, g_ffn_pre, g_ffn_post, w_in, mu_shift, w0, w2, a0,
           a2, g2, k_k, k_a, r_k, ln_x_g, ln_x_b, hgrn_lb_logits, hgrn_norm_g, w_out, w_ff1, w_ff2):
    assert x.shape[0] == 1 and w_in.shape[0] == 1, "single batch, single layer"
    layer = 0
    x2, ctx2 = x[0], ctx[0]
    l, d = x2.shape
    lc = ctx2.shape[0]
    dr, dh = k_k.shape[1], hgrn_lb_logits.shape[2]
    rd_raw, ra_raw, rg_raw = w2.shape[2], a2.shape[2], g2.shape[1]
    rd, ra, rg = (_round_up(v, LANE) for v in (rd_raw, ra_raw, rg_raw))
    assert l % GRID_W == 0 and lc % CHUNK == 0 and dr % GROUP_W == 0 and dh % GROUP_W == 0

    cvec = jnp.concatenate([c, c_ctx[None, :], jnp.zeros((6, d), F32)], axis=0)
    mod = _ada(cvec, w_ada[layer], b_ada[layer])

    wi = w_in[layer]
    mu = mu_shift[layer][None, :]
    sel = (jnp.arange(mu.shape[1], dtype=jnp.int32) % 4)[None, :]
    o = 3 * dr
    cuts = [(0, o, o)]
    for raw, padded in ((rd_raw, rd), (rd_raw, rd), (ra_raw, ra), (ra_raw, ra), (rg_raw, rg)):
        cuts.append((o, o + raw, padded))
        o += raw
    rw_cols = o
    regroup = lambda a: jnp.concatenate([_pad_cols(a[:, s:e], wd) for s, e, wd in cuts], axis=1)
    w_r = regroup(wi[:, :rw_cols]).astype(PROJ_DTYPE)
    w_h = wi[:, rw_cols:].astype(PROJ_DTYPE)
    mu_p, sel_p = regroup(mu), regroup(sel)
    w2p = jnp.stack([_pad_rows(w2[layer, dd], rd) for dd in range(2)])
    a2p = jnp.stack([_pad_rows(a2[layer, dd], ra) for dd in range(2)])
    g2p = _pad_rows(g2[layer], rg)

    h = _prep(ctx2, x2, g_mix_pre[layer], mod)
    p_r = _mm(h, w_r, F32, tm_target=768, name="in_proj_rwkv")
    p_h = _mm(h, w_h, F32, tm_target=768, name="in_proj_hgrn")

    (r, v, a, lw0, lw1, k0, k1, b0, b1, bonus, gate) = _rwkv_features(
        p_r, sel_p, mu_p, w0[layer], w2p, a0[layer], a2p, g2p, k_k[layer], k_a[layer], r_k[layer].reshape(-1), lc, l)
    yf, yr = _rwkv_scan(r, v, a, lw0, lw1, k0, k1, b0, b1, lc)
    of, orv = _hgrn_scan(p_h, hgrn_lb_logits, lc, layer)

    u = _mix_out(yf, yr, bonus, gate, of, orv, p_h, ln_x_g[layer], ln_x_b[layer], hgrn_norm_g[layer], lc, l)
    ux = _mm(u, w_out[layer].astype(PROJ_DTYPE), F32, name="out_proj")
    x1, h2 = _res1(ux, x2, mod, g_mix_post[layer], g_ffn_pre[layer])
    act = _mm(h2, w_ff1[layer].astype(PROJ_DTYPE), PROJ_DTYPE, act="relu2", name="ffn_up")
    out = _ffn_down_residual(act, w_ff2[layer].astype(PROJ_DTYPE), x1, mod, g_ffn_post[layer])
    return out[None]
```

```python
import functools

import jax
import jax.numpy as jnp
import numpy as np
from jax import lax
from jax.experimental import pallas as pl
from jax.experimental.pallas import tpu as pltpu

F32 = jnp.float32
BF16 = jnp.bfloat16
PROJ_DTYPE = jnp.bfloat16

LANE = 128
GRID_W = 64
CHUNK = GRID_W
RWKV_HEAD = 64
HGRN_HEAD = 128
SUB = 16
GROUP_W = 256
RWKV_GROUPS_PER_STEP = 8
HGRN_GROUPS_PER_STEP = 8
MAX_CHUNK_LOG_DECAY = 60.0
NORM_EPS = 1e-6
GN_EPS = 64e-5
EXP_M05 = float(np.exp(-0.5))
VMEM_LIMIT = 56 * 1024 * 1024


def _pick(n, target, unit=LANE):
    best = None
    for m in range(unit, min(n, target) + 1, unit):
        if n % m == 0:
            best = m
    return best if best is not None else n


def _sigmoid(z):
    return 1.0 / (1.0 + jnp.exp(-z))


def _mxu(a, b):
    return jnp.dot(a, b, preferred_element_type=F32)


def _mxu_nt(a, b):
    return lax.dot_general(a, b, (((1,), (1,)), ((), ())), preferred_element_type=F32)


def _mxu_tn(a, b):
    return lax.dot_general(a, b, (((0,), (0,)), ((), ())), preferred_element_type=F32)


def _split3(z):
    hi = z.astype(BF16)
    rest = z - hi.astype(F32)
    mid = rest.astype(BF16)
    return hi, mid, (rest - mid.astype(F32)).astype(BF16)


def _exact_mxu(z, sel):
    hi, mid, lo = _split3(z)
    return _mxu(hi, sel) + _mxu(mid, sel) + _mxu(lo, sel)


def _head_sum(z, hs, hst):
    return _exact_mxu(_exact_mxu(z, hs), hst)


def _head_indicators(dr):
    hs = np.arange(dr)[:, None] // RWKV_HEAD == np.arange(LANE)[None, :]
    return jnp.asarray(hs, BF16), jnp.asarray(hs.T, BF16)


def _params(sem, vmem=None):
    return pltpu.CompilerParams(dimension_semantics=sem, vmem_limit_bytes=vmem)


def _ada_kernel(c_ref, w_ref, b_ref, o_ref):
    cv = c_ref[...]
    rows = cv.shape[0]
    hi, mid, lo = (t.astype(F32) for t in _split3(cv * _sigmoid(cv)))
    s3 = jnp.concatenate([hi, mid, lo], axis=0).astype(BF16)
    w = w_ref[...]
    w_hi = w.astype(BF16)
    w_lo = (w - w_hi.astype(F32)).astype(BF16)
    p = _mxu(s3, w_hi)
    q = _mxu(s3[:2 * rows], w_lo)
    o_ref[...] = (p[:rows] + p[rows:2 * rows] + p[2 * rows:] + q[:rows] + q[rows:]) + b_ref[...]


def _ada(cvec, w, b):
    rows, d = cvec.shape
    n = w.shape[1]
    tn = _pick(n, 512)
    return pl.pallas_call(
        _ada_kernel,
        grid=(n // tn,),
        in_specs=[pl.BlockSpec((rows, d), lambda j: (0, 0)),
                  pl.BlockSpec((d, tn), lambda j: (0, j)),
                  pl.BlockSpec((1, tn), lambda j: (0, j))],
        out_specs=pl.BlockSpec((rows, tn), lambda j: (0, j)),
        out_shape=jax.ShapeDtypeStruct((rows, n), F32),
        compiler_params=_params(("arbitrary",), VMEM_LIMIT),
        name="ada_mod",
    )(cvec, w, b.reshape(1, n))


def _prep_kernel(ctx_ref, x_ref, g_ref, mod_ref, o_ref, *, ncb, d):
    is_ctx = pl.program_id(0) < ncb
    rows = jnp.where(is_ctx, ctx_ref[...], x_ref[...])
    ms = jnp.mean(rows * rows, axis=-1, keepdims=True)
    hn = rows * lax.rsqrt(ms + NORM_EPS) * g_ref[...]
    shift = jnp.where(is_ctx, mod_ref[1:2, 0:d], mod_ref[0:1, 0:d])
    scale = jnp.where(is_ctx, mod_ref[1:2, d:2 * d], mod_ref[0:1, d:2 * d])
    o_ref[...] = (hn * (1.0 + scale) + shift).astype(o_ref.dtype)


def _prep(ctx2, x2, g, mod):
    lc, d = ctx2.shape
    l = x2.shape[0]
    tb = _pick(int(np.gcd(lc, l)), 256, 8)
    ncb = lc // tb
    return pl.pallas_call(
        functools.partial(_prep_kernel, ncb=ncb, d=d),
        grid=((lc + l) // tb,),
        in_specs=[pl.BlockSpec((tb, d), lambda i: (jnp.minimum(i, ncb - 1), 0)),
                  pl.BlockSpec((tb, d), lambda i: (jnp.maximum(i - ncb, 0), 0)),
                  pl.BlockSpec((1, d), lambda i: (0, 0)),
                  pl.BlockSpec(mod.shape, lambda i: (0, 0))],
        out_specs=pl.BlockSpec((tb, d), lambda i: (i, 0)),
        out_shape=jax.ShapeDtypeStruct((lc + l, d), PROJ_DTYPE),
        compiler_params=_params(("arbitrary",)),
        name="norm_modulate",
    )(ctx2, x2, g.reshape(1, d), mod)


def _mm_kernel(x_ref, w_ref, o_ref, *, act):
    acc = jnp.dot(x_ref[...], w_ref[...], preferred_element_type=F32)
    if act == "relu2":
        acc = jnp.square(jnp.maximum(acc, 0.0))
    o_ref[...] = acc.astype(o_ref.dtype)


def _mm(x, w, out_dtype, act=None, tm_target=1024, tn_target=1024, name="matmul"):
    m, k = x.shape
    n = w.shape[1]
    tm = _pick(m, tm_target)
    tn = _pick(n, tn_target)
    return pl.pallas_call(
        functools.partial(_mm_kernel, act=act),
        grid=(m // tm, n // tn),
        in_specs=[pl.BlockSpec((tm, k), lambda i, j: (i, 0)),
                  pl.BlockSpec((k, tn), lambda i, j: (0, j))],
        out_specs=pl.BlockSpec((tm, tn), lambda i, j: (i, j)),
        out_shape=jax.ShapeDtypeStruct((m, n), out_dtype),
        compiler_params=_params(("arbitrary", "arbitrary"), VMEM_LIMIT),
        name=name,
    )(x, w)


def _ffn_down_kernel(a_ref, w_ref, x1_hbm, mod_ref, g_ref, o_ref, x1_buf, sem, *, d):
    i, kk = pl.program_id(0), pl.program_id(1)
    tm = x1_buf.shape[0]
    x1_copy = pltpu.make_async_copy(x1_hbm.at[pl.ds(pl.multiple_of(i * tm, tm), tm), :], x1_buf, sem)

    @pl.when(kk == 0)
    def _():
        x1_copy.start()
        o_ref[...] = jnp.zeros_like(o_ref)

    o_ref[...] += jnp.dot(a_ref[...], w_ref[...], preferred_element_type=F32)

    @pl.when(kk == pl.num_programs(1) - 1)
    def _():
        x1_copy.wait()
        gate = mod_ref[0:1, 5 * d:6 * d] * g_ref[...]
        rb = _pick(tm, 128, 8)
        for r0 in range(0, tm, rb):
            mv = o_ref[r0:r0 + rb, :]
            scale = lax.rsqrt(jnp.mean(mv * mv, axis=-1, keepdims=True) + NORM_EPS)
            o_ref[r0:r0 + rb, :] = x1_buf[r0:r0 + rb, :] + gate * (mv * scale)


def _ffn_down_residual(act, w, x1, mod, g_post, tm_target=512, tk_target=1024):
    m, k = act.shape
    d = w.shape[1]
    tm, tk = _pick(m, tm_target, 8), _pick(k, tk_target)
    full = lambda a: pl.BlockSpec(a.shape, lambda i, kk: (0,) * a.ndim)
    gp = g_post.reshape(1, d)
    return pl.pallas_call(
        functools.partial(_ffn_down_kernel, d=d),
        grid=(m // tm, k // tk),
        in_specs=[pl.BlockSpec((tm, tk), lambda i, kk: (i, kk)),
                  pl.BlockSpec((tk, d), lambda i, kk: (kk, 0)),
                  pl.BlockSpec(memory_space=pl.ANY), full(mod), full(gp)],
        out_specs=pl.BlockSpec((tm, d), lambda i, kk: (i, 0)),
        out_shape=jax.ShapeDtypeStruct((m, d), F32),
        scratch_shapes=[pltpu.VMEM((tm, d), F32), pltpu.SemaphoreType.DMA(())],
        compiler_params=_params(("arbitrary", "arbitrary"), VMEM_LIMIT),
        name="ffn_down",
    )(act, w, x1, mod, gp)


def _rwkv_feat_kernel(pp_ref, pc_ref, pn_ref, sel_ref, mu_ref, w0_ref, w2_ref, a0_ref, a2_ref, g2_ref,
                      kk_ref, ka_ref, rk_ref, hs_ref, hst_ref,
                      r_o, v_o, a_o, lw0_o, lw1_o, k0_o, k1_o, b0_o, b1_o, bonus_o, gate_o,
                      *, ncb, nrows, dr, rd, ra, rg):
    i = pl.program_id(0)
    is_ctx = i < ncb
    xi = i - ncb
    cur, prv, nxt = pc_ref[...], pp_ref[...], pn_ref[...]
    c = cur.shape[0]
    row = lax.broadcasted_iota(jnp.int32, cur.shape, 0)
    prev_last = jnp.where(is_ctx & (i > 0), prv[c - 1:c, :], 0.0)
    next_first = jnp.where(is_ctx & (i < ncb - 1), nxt[0:1, :], 0.0)
    before = jnp.where(row == 0, prev_last, pltpu.roll(cur, 1, 0))
    after = jnp.where(row == c - 1, next_first, pltpu.roll(cur, c - 1, 0))
    above = jnp.where(is_ctx, before, jnp.where(xi > 0, prv, 0.0))
    below = jnp.where(is_ctx, after, jnp.where(xi < nrows - 1, nxt, 0.0))
    sel = sel_ref[...]
    shifted = jnp.where(sel == 0, before, jnp.where(sel == 1, after, jnp.where(sel == 2, above, below)))
    m = cur + mu_ref[...] * (shifted - cur)

    r, k, v = m[:, 0:dr], m[:, dr:2 * dr], m[:, 2 * dr:3 * dr]
    o = 3 * dr
    wl = (m[:, o:o + rd], m[:, o + rd:o + 2 * rd])
    o += 2 * rd
    al = (m[:, o:o + ra], m[:, o + ra:o + 2 * ra])
    o += 2 * ra
    gl = m[:, o:o + rg]

    hs, hst = hs_ref[...], hst_ref[...]

    def headsum(z):
        return _head_sum(z, hs, hst)

    kkf = k * kk_ref[...]
    kk = kkf * lax.rsqrt(headsum(kkf * kkf) + 1e-12)
    r_o[...] = r
    v_o[...] = v
    a_o[...] = -kk
    for d, (lw_o, k_o, b_o) in enumerate(((lw0_o, k0_o, b0_o), (lw1_o, k1_o, b1_o))):
        wd = _mxu(jnp.tanh(wl[d]).astype(BF16), w2_ref[d]) + w0_ref[d:d + 1, :]
        lw_o[...] = -EXP_M05 * _sigmoid(wd)
        ad = _sigmoid(_mxu(al[d].astype(BF16), a2_ref[d]) + a0_ref[d:d + 1, :])
        k_o[...] = k * (1.0 + (ad - 1.0) * ka_ref[...])
        b_o[...] = kk * ad
    bonus_o[...] = headsum(r * k * rk_ref[...]) * v
    gate_o[...] = _mxu(_sigmoid(gl).astype(BF16), g2_ref[...])


def _rwkv_features(p_r, sel, mu, w0, w2p, a0, a2p, g2p, k_k, k_a, r_k, lc, l):
    n, nr = p_r.shape
    dr = k_k.shape[0]
    rd, ra, rg = w2p.shape[1], a2p.shape[1], g2p.shape[0]
    nb, ncb = n // CHUNK, lc // CHUNK
    nheads = dr // RWKV_HEAD
    assert nheads <= LANE
    hs, hst = _head_indicators(dr)
    full = lambda a: pl.BlockSpec(a.shape, lambda i: (0,) * a.ndim)
    row1 = lambda a: a.reshape(1, -1)
    consts = [sel, mu, w0, w2p.astype(BF16), a0, a2p.astype(BF16), g2p.astype(BF16), row1(k_k), row1(k_a),
              row1(r_k), hs, hst]
    blk = (CHUNK, nr)
    outs = pl.pallas_call(
        functools.partial(_rwkv_feat_kernel, ncb=ncb, nrows=l // CHUNK, dr=dr, rd=rd, ra=ra, rg=rg),
        grid=(nb,),
        in_specs=[pl.BlockSpec(blk, lambda i: (jnp.maximum(i - 1, 0), 0)),
                  pl.BlockSpec(blk, lambda i: (i, 0)),
                  pl.BlockSpec(blk, lambda i: (jnp.minimum(i + 1, nb - 1), 0))] + [full(a) for a in consts],
        out_specs=[pl.BlockSpec((CHUNK, dr), lambda i: (i, 0))] * 11,
        out_shape=[jax.ShapeDtypeStruct((n, dr), F32)] * 11,
        compiler_params=_params(("arbitrary",), VMEM_LIMIT),
        name="rwkv_features",
    )(p_r, p_r, p_r, *consts)
    return outs


def _block_diag_mask(n, w):
    r = lax.broadcasted_iota(jnp.int32, (n, n), 0)
    c = lax.broadcasted_iota(jnp.int32, (n, n), 1)
    return (r // w) == (c // w)


def _cumsum_rows(z, rev):
    c = z.shape[0]
    t = lax.broadcasted_iota(jnp.int32, (c, c), 0)
    s = lax.broadcasted_iota(jnp.int32, (c, c), 1)
    tri = ((s >= t) if rev else (s <= t)).astype(BF16)
    hi, mid, lo = _split3(z)
    return _mxu(tri, hi) + _mxu(tri, mid) + _mxu(tri, lo)


def _rwkv_chunks(chains):
    c, w = chains[0][0].shape
    nh = w // c
    every = range(len(chains))
    revs = [ch[7] for ch in chains]
    bd = _block_diag_mask(w, c)
    t_i = lax.broadcasted_iota(jnp.int32, (c, w), 0)
    s_i = lax.broadcasted_iota(jnp.int32, (c, w), 1) % c
    eye = (s_i == t_i).astype(F32)
    strict = {False: s_i < t_i, True: s_i > t_i}
    incl = {False: s_i <= t_i, True: s_i >= t_i}

    def expand(z):
        return jnp.where(bd, jnp.concatenate([z] * nh, axis=0), jnp.zeros((), z.dtype))

    def packed_mm(lhs, rhs):
        return _mxu(lhs.astype(BF16), expand(rhs.astype(BF16)))

    bc = [_cumsum_rows(ch[1], ch[7]) for ch in chains]
    ar, bk, v16 = [], [], []
    for i, (r, lw, k, v, a, b, _, _) in enumerate(chains):
        e_in, e_ex, e_ng = jnp.exp(bc[i]), jnp.exp(bc[i] - lw), jnp.exp(-bc[i])
        ar.append(jnp.concatenate([a * e_ex, r * e_in], axis=0).astype(BF16))
        bk.append(jnp.concatenate([b * e_ng, k * e_ng], axis=0).astype(BF16))
        v16.append(v.astype(BF16))
    gb = [_mxu_nt(ar[i], expand(bk[i][:c])) for i in every]
    gk = [_mxu_nt(ar[i], expand(bk[i][c:])) for i in every]
    l_ab = [jnp.where(strict[revs[i]], gb[i][:c], 0.0) for i in every]
    m_rb = [jnp.where(incl[revs[i]], gb[i][c:], 0.0) for i in every]
    lmk = [jnp.concatenate([jnp.where(strict[revs[i]], gk[i][:c], 0.0),
                            jnp.where(incl[revs[i]], gk[i][c:], 0.0)], axis=0) for i in every]
    t_m = [eye + l_ab[i] for i in every]
    p_m = [packed_mm(l_ab[i], l_ab[i]) for i in every]
    for _ in range(int(np.log2(c)) - 2):
        tp = [packed_mm(jnp.concatenate([t_m[i], p_m[i]], axis=0), p_m[i]) for i in every]
        t_m = [t_m[i] + tp[i][:c] for i in every]
        p_m = [tp[i][c:] for i in every]
    t_m = [t_m[i] + packed_mm(t_m[i], p_m[i]) for i in every]

    ars = [_mxu_nt(ar[i], chains[i][6].astype(BF16)) for i in every]
    lm = [_mxu(lmk[i].astype(BF16), expand(v16[i])) for i in every]
    u16 = [packed_mm(t_m[i], ars[i][:c] + lm[i][:c]).astype(BF16) for i in every]
    ys = [ars[i][c:] + _mxu(m_rb[i].astype(BF16), expand(u16[i])) + lm[i][c:] for i in every]
    ds = [_mxu_tn(jnp.concatenate([u16[i], v16[i]], axis=0), bk[i]) for i in every]
    s_new = []
    for i in every:
        last = bc[i][0:1] if revs[i] else bc[i][c - 1:c]
        s_new.append((chains[i][6] + jnp.where(bd, ds[i], 0.0)) * jnp.exp(last))
    return ys, s_new


def _cast_jobs(weights, ng, nb):
    specs, shapes = [], []
    for w in weights:
        rows, cols = w.shape
        n = max(m for m in range(1, ng * nb + 1) if rows % m == 0 and (rows // m) % 16 == 0)
        specs.append(pl.BlockSpec((rows // n, cols), lambda g, c, n=n: (jnp.minimum(g * nb + c, n - 1), 0)))
        shapes.append(jax.ShapeDtypeStruct(w.shape, PROJ_DTYPE))
    return specs, shapes


def _run_cast_jobs(src_refs, dst_refs):
    for src, dst in zip(src_refs, dst_refs):
        dst[...] = src[...].astype(dst.dtype)


def _rwkv_scan_kernel(rf, lwf, kf, vf, af, bf, rr, lwr, kr, vr, ar, br, *rest):
    n_cast = (len(rest) - 3) // 2
    yf_o, yr_o = rest[n_cast:n_cast + 2]
    s_ref = rest[-1]
    _run_cast_jobs(rest[:n_cast], rest[n_cast + 2:-1])

    @pl.when(pl.program_id(1) == 0)
    def _():
        s_ref[...] = jnp.zeros_like(s_ref)

    ngs = rf.shape[1] // GROUP_W
    chains = []
    for d, refs in enumerate(((rf, lwf, kf, vf, af, bf), (rr, lwr, kr, vr, ar, br))):
        for g in range(ngs):
            sl = slice(g * GROUP_W, (g + 1) * GROUP_W)
            chains.append(tuple(ref[:, sl] for ref in refs) + (s_ref[d, g], d == 1))
    ys, s_new = _rwkv_chunks(chains)
    for d, o_ref in enumerate((yf_o, yr_o)):
        for g in range(ngs):
            o_ref[:, g * GROUP_W:(g + 1) * GROUP_W] = ys[d * ngs + g]
            s_ref[d, g] = s_new[d * ngs + g]


def _rev_chunk(c, ncb, nb):
    return jnp.where(c < ncb, ncb - 1 - c, nb - 1 - (c - ncb))


def _rwkv_scan(r, v, a, lw0, lw1, k0, k1, b0, b1, lc, cast_weights):
    n, dr = r.shape
    nb, ncb = n // CHUNK, lc // CHUNK
    bw = _pick(dr, RWKV_GROUPS_PER_STEP * GROUP_W, GROUP_W)
    fwd = pl.BlockSpec((CHUNK, bw), lambda g, c: (c, g))
    rev = pl.BlockSpec((CHUNK, bw), lambda g, c: (_rev_chunk(c, ncb, nb), g))
    cast_specs, cast_shapes = _cast_jobs(cast_weights, dr // bw, nb)
    return pl.pallas_call(
        _rwkv_scan_kernel,
        grid=(dr // bw, nb),
        in_specs=[fwd] * 6 + [rev] * 6 + cast_specs,
        out_specs=[fwd, rev] + cast_specs,
        out_shape=[jax.ShapeDtypeStruct((n, dr), F32)] * 2 + cast_shapes,
        scratch_shapes=[pltpu.VMEM((2, bw // GROUP_W, GROUP_W, GROUP_W), F32)],
        compiler_params=_params(("arbitrary", "arbitrary"), VMEM_LIMIT),
        name="rwkv_scan",
    )(r, lw0, k0, v, a, b0, r, lw1, k1, v, a, b1, *cast_weights)


def _hgrn_chunk(q, k, v, lf, s_vk, rev):
    c, w = q.shape
    nsb = c // SUB
    bc = _cumsum_rows(lf, rev)
    last = bc[0:1] if rev else bc[c - 1:c]
    bdh = _block_diag_mask(w, HGRN_HEAD)
    v16 = v.astype(BF16)
    o_inter = _mxu_nt((q * jnp.exp(bc)).astype(BF16), s_vk.astype(BF16))
    s_new = s_vk * jnp.exp(last) + jnp.where(bdh, _mxu_tn(v16, (k * jnp.exp(last - bc)).astype(BF16)), 0.0)

    lane2 = lax.broadcasted_iota(jnp.int32, (c, w), 1)
    row2 = lax.broadcasted_iota(jnp.int32, (c, w), 0)
    zpad = jnp.zeros((HGRN_HEAD - c, w), BF16)

    def expand(z):
        parts = []
        for h in range(w // HGRN_HEAD):
            parts += [jnp.where(lane2 // HGRN_HEAD == h, z, jnp.zeros((), BF16)), zpad]
        return jnp.concatenate(parts, axis=0)

    ones_bd = bdh.astype(BF16)
    t3 = lax.broadcasted_iota(jnp.int32, (SUB, SUB, w), 0)
    s3 = lax.broadcasted_iota(jnp.int32, (SUB, SUB, w), 1)
    l3 = lax.broadcasted_iota(jnp.int32, (SUB, SUB, w), 2) % HGRN_HEAD
    causal = (s3 >= t3) if rev else (s3 <= t3)
    a_rows = []
    for i in range(nsb):
        lo, hi = SUB * i, SUB * (i + 1)
        qi, ki, bi = q[lo:hi], k[lo:hi], bc[lo:hi]
        d3 = bi[:, None, :] - bi[None, :, :]
        x3 = jnp.where(causal, (qi[:, None, :] * ki[None, :, :]) * jnp.exp(jnp.minimum(d3, 0.0)), 0.0)
        r3 = _mxu(x3.reshape(SUB * SUB, w).astype(BF16), ones_bd).reshape(SUB, SUB, w)
        a_i = jnp.sum(jnp.where(l3 == s3 + lo, r3, 0.0), axis=1)
        if (not rev and i > 0) or (rev and i < nsb - 1):
            beta = bc[hi:hi + 1] if rev else bc[lo - 1:lo]
            earlier = (row2 >= hi) if rev else (row2 < lo)
            kp = jnp.where(earlier, k * jnp.exp(jnp.minimum(beta - bc, 0.0)), 0.0)
            a_i = a_i + _mxu_nt((qi * jnp.exp(bi - beta)).astype(BF16), expand(kp.astype(BF16)))
        a_rows.append(a_i)
    o = o_inter + _mxu(jnp.concatenate(a_rows, axis=0).astype(BF16), expand(v16))
    return o, s_new


def _hgrn_chunks_bounded(chains):
    c, w = chains[0][0].shape
    every = range(len(chains))
    revs = [ch[5] for ch in chains]
    bdh = _block_diag_mask(w, HGRN_HEAD)
    lane2 = lax.broadcasted_iota(jnp.int32, (c, w), 1)
    t_i = lax.broadcasted_iota(jnp.int32, (c, w), 0)
    s_i = lane2 % HGRN_HEAD
    incl = {False: s_i <= t_i, True: s_i >= t_i}
    zpad = jnp.zeros((HGRN_HEAD - c, w), BF16)

    def expand(z):
        parts = []
        for h in range(w // HGRN_HEAD):
            parts += [jnp.where(lane2 // HGRN_HEAD == h, z, jnp.zeros((), BF16)), zpad]
        return jnp.concatenate(parts, axis=0)

    bc = [_cumsum_rows(ch[3], ch[5]) for ch in chains]
    last = [bc[i][0:1] if revs[i] else bc[i][c - 1:c] for i in every]
    qt = [(chains[i][0] * jnp.exp(bc[i])).astype(BF16) for i in every]
    kt = [(chains[i][1] * jnp.exp(-bc[i])).astype(BF16) for i in every]
    kd = [(chains[i][1] * jnp.exp(last[i] - bc[i])).astype(BF16) for i in every]
    v16 = [chains[i][2].astype(BF16) for i in every]
    o_inter = [_mxu_nt(qt[i], chains[i][4].astype(BF16)) for i in every]
    a = [jnp.where(incl[revs[i]], _mxu_nt(qt[i], expand(kt[i])), 0.0).astype(BF16) for i in every]
    o = [o_inter[i] + _mxu(a[i], expand(v16[i])) for i in every]
    ds = [_mxu_tn(v16[i], kd[i]) for i in every]
    s_new = [chains[i][4] * jnp.exp(last[i]) + jnp.where(bdh, ds[i], 0.0) for i in every]
    return o, s_new


def _hgrn_scan_kernel(qf, ff, vf, qr, fr, vr, lg_ref, *rest, layer):
    n_cast = (len(rest) - 3) // 2
    of_o, or_o = rest[n_cast:n_cast + 2]
    s_ref = rest[-1]
    _run_cast_jobs(rest[:n_cast], rest[n_cast + 2:-1])

    @pl.when(pl.program_id(1) == 0)
    def _():
        s_ref[...] = jnp.zeros_like(s_ref)

    lg = lg_ref[...]
    e = jnp.exp(lg - jnp.max(lg, axis=0, keepdims=True))
    lb = jnp.sum(e[:layer + 1], axis=0) / jnp.sum(e, axis=0)
    ngs = qf.shape[1] // GROUP_W
    chains = []
    min_total = None
    for d, (q_ref, f_ref, v_ref) in enumerate(((qf, ff, vf), (qr, fr, vr))):
        fd = lb[d:d + 1, :] + (1.0 - lb[d:d + 1, :]) * _sigmoid(f_ref[...])
        qv = q_ref[...]
        qh, kh, vh, lf = qv * _sigmoid(qv), 1.0 - fd, v_ref[...], jnp.log(fd)
        total = jnp.min(jnp.sum(lf, axis=0, keepdims=True))
        min_total = total if min_total is None else jnp.minimum(min_total, total)
        for g in range(ngs):
            sl = slice(g * GROUP_W, (g + 1) * GROUP_W)
            chains.append((qh[:, sl], kh[:, sl], vh[:, sl], lf[:, sl], s_ref[d, g], d == 1))

    def emit(o, s_new):
        for d, o_ref in enumerate((of_o, or_o)):
            for g in range(ngs):
                o_ref[:, g * GROUP_W:(g + 1) * GROUP_W] = o[d * ngs + g]
                s_ref[d, g] = s_new[d * ngs + g]

    bounded = min_total >= -MAX_CHUNK_LOG_DECAY

    @pl.when(bounded)
    def _():
        emit(*_hgrn_chunks_bounded(chains))

    @pl.when(jnp.logical_not(bounded))
    def _():
        res = [_hgrn_chunk(*ch) for ch in chains]
        emit([r[0] for r in res], [r[1] for r in res])


def _hgrn_scan(p_h, lb_logits, lc, layer, cast_weights):
    n = p_h.shape[0]
    dh = lb_logits.shape[2]
    nb, ncb = n // CHUNK, lc // CHUNK
    bw = _pick(dh, HGRN_GROUPS_PER_STEP * GROUP_W, GROUP_W)
    ng = dh // bw
    blk = (CHUNK, bw)
    fwd = lambda sec: pl.BlockSpec(blk, lambda g, c: (c, sec * ng + g))
    rev = lambda sec: pl.BlockSpec(blk, lambda g, c: (_rev_chunk(c, ncb, nb), sec * ng + g))
    cast_specs, cast_shapes = _cast_jobs(cast_weights, ng, nb)
    return pl.pallas_call(
        functools.partial(_hgrn_scan_kernel, layer=layer),
        grid=(ng, nb),
        in_specs=[fwd(0), fwd(1), fwd(3), rev(0), rev(2), rev(3),
                  pl.BlockSpec((lb_logits.shape[0], 2, bw), lambda g, c: (0, 0, g))] + cast_specs,
        out_specs=[pl.BlockSpec(blk, lambda g, c: (c, g)),
                   pl.BlockSpec(blk, lambda g, c: (_rev_chunk(c, ncb, nb), g))] + cast_specs,
        out_shape=[jax.ShapeDtypeStruct((n, dh), F32)] * 2 + cast_shapes,
        scratch_shapes=[pltpu.VMEM((2, bw // GROUP_W, GROUP_W, GROUP_W), F32)],
        compiler_params=_params(("arbitrary", "arbitrary"), VMEM_LIMIT),
        name="hgrn_scan",
    )(p_h, p_h, p_h, p_h, p_h, p_h, lb_logits, *cast_weights)


def _mix_out_kernel(yf, yr, bonus, gate, of, orv, g_ref, lng, lnb, hng, hs_ref, hst_ref, u_o, *, dr, dh):
    hs, hst = hs_ref[...], hst_ref[...]

    def headmean(z):
        return _head_sum(z, hs, hst) * (1.0 / RWKV_HEAD)

    y = yf[...] + yr[...]
    yc = y - headmean(y)
    yn = yc * lax.rsqrt(headmean(yc * yc) + GN_EPS) * lng[...] + lnb[...]
    u_o[:, 0:dr] = ((yn + bonus[...]) * gate[...]).astype(u_o.dtype)
    o = of[...] + orv[...]
    g = g_ref[...]
    sg = g * _sigmoid(g)
    for h in range(dh // HGRN_HEAD):
        sl = slice(h * HGRN_HEAD, (h + 1) * HGRN_HEAD)
        oh = o[:, sl]
        on = oh * lax.rsqrt(jnp.mean(oh * oh, axis=-1, keepdims=True) + NORM_EPS) * hng[...]
        u_o[:, dr + h * HGRN_HEAD:dr + (h + 1) * HGRN_HEAD] = (on * sg[:, sl]).astype(u_o.dtype)


def _mix_out(yf, yr, bonus, gate, of, orv, p_h, ln_g, ln_b, hg, lc, l):
    dr, dh = yf.shape[1], of.shape[1]
    tb = _pick(int(np.gcd(lc, l)), 128, 8)
    off = lc // tb
    hs, hst = _head_indicators(dr)
    rows = lambda wd: pl.BlockSpec((tb, wd), lambda i: (i + off, 0))
    full = lambda a: pl.BlockSpec(a.shape, lambda i: (0,) * a.ndim)
    consts = [ln_g.reshape(1, dr), ln_b.reshape(1, dr), hg.reshape(1, HGRN_HEAD), hs, hst]
    return pl.pallas_call(
        functools.partial(_mix_out_kernel, dr=dr, dh=dh),
        grid=(l // tb,),
        in_specs=[rows(dr)] * 4 + [rows(dh)] * 2 + [pl.BlockSpec((tb, dh), lambda i: (i + off, 4))]
                 + [full(a) for a in consts],
        out_specs=pl.BlockSpec((tb, dr + dh), lambda i: (i, 0)),
        out_shape=jax.ShapeDtypeStruct((l, dr + dh), PROJ_DTYPE),
        compiler_params=_params(("arbitrary",), VMEM_LIMIT),
        name="mix_out",
    )(yf, yr, bonus, gate, of, orv, p_h, *consts)


def _res1_kernel(ux_ref, x_ref, mod_ref, gpost_ref, gpre_ref, x1_o, h2_o, *, d):
    ux = ux_ref[...]
    nrm = ux * lax.rsqrt(jnp.mean(ux * ux, axis=-1, keepdims=True) + NORM_EPS) * gpost_ref[...]
    x1 = x_ref[...] + mod_ref[0:1, 2 * d:3 * d] * nrm
    x1_o[...] = x1
    hn = x1 * lax.rsqrt(jnp.mean(x1 * x1, axis=-1, keepdims=True) + NORM_EPS) * gpre_ref[...]
    h2_o[...] = (hn * (1.0 + mod_ref[0:1, 4 * d:5 * d]) + mod_ref[0:1, 3 * d:4 * d]).astype(h2_o.dtype)


def _res1(ux, x2, mod, g_post, g_pre):
    l, d = x2.shape
    tb = _pick(l, 256, 8)
    rows = pl.BlockSpec((tb, d), lambda i: (i, 0))
    full = lambda a: pl.BlockSpec(a.shape, lambda i: (0,) * a.ndim)
    gp, gq = g_post.reshape(1, d), g_pre.reshape(1, d)
    return pl.pallas_call(
        functools.partial(_res1_kernel, d=d),
        grid=(l // tb,),
        in_specs=[rows, rows, full(mod), full(gp), full(gq)],
        out_specs=[rows, rows],
        out_shape=[jax.ShapeDtypeStruct((l, d), F32), jax.ShapeDtypeStruct((l, d), PROJ_DTYPE)],
        compiler_params=_params(("arbitrary",), VMEM_LIMIT),
        name="residual_mix",
    )(ux, x2, mod, gp, gq)


def _pad_cols(a, width):
    return jnp.pad(a, ((0, 0), (0, width - a.shape[1])))


def _pad_rows(a, height):
    return jnp.pad(a, ((0, height - a.shape[0]), (0, 0)))


def _round_up(n, m):
    return (n + m - 1) // m * m


def kernel(x, c, ctx, c_ctx, w_ada, b_ada, g_mix_pre, g_mix_post, g_ffn_pre, g_ffn_post, w_in, mu_shift, w0, w2, a0,
           a2, g2, k_k, k_a, r_k, ln_x_g, ln_x_b, hgrn_lb_logits, hgrn_norm_g, w_out, w_ff1, w_ff2):
    assert x.shape[0] == 1 and w_in.shape[0] == 1, "single batch, single layer"
    layer = 0
    x2, ctx2 = x[0], ctx[0]
    l, d = x2.shape
    lc = ctx2.shape[0]
    dr, dh = k_k.shape[1], hgrn_lb_logits.shape[2]
    rd_raw, ra_raw, rg_raw = w2.shape[2], a2.shape[2], g2.shape[1]
    rd, ra, rg = (_round_up(v, LANE) for v in (rd_raw, ra_raw, rg_raw))
    assert l % GRID_W == 0 and lc % CHUNK == 0 and dr % GROUP_W == 0 and dh % GROUP_W == 0

    cvec = jnp.concatenate([c, c_ctx[None, :], jnp.zeros((6, d), F32)], axis=0)
    mod = _ada(cvec, w_ada[layer], b_ada[layer])

    wi = w_in[layer]
    mu = mu_shift[layer][None, :]
    sel = (jnp.arange(mu.shape[1], dtype=jnp.int32) % 4)[None, :]
    o = 3 * dr
    cuts = [(0, o, o)]
    for raw, padded in ((rd_raw, rd), (rd_raw, rd), (ra_raw, ra), (ra_raw, ra), (rg_raw, rg)):
        cuts.append((o, o + raw, padded))
        o += raw
    rw_cols = o
    regroup = lambda a: jnp.concatenate([_pad_cols(a[:, s:e], wd) for s, e, wd in cuts], axis=1)
    w_r = regroup(wi[:, :rw_cols]).astype(PROJ_DTYPE)
    w_h = wi[:, rw_cols:].astype(PROJ_DTYPE)
    mu_p, sel_p = regroup(mu), regroup(sel)
    w2p = jnp.stack([_pad_rows(w2[layer, dd], rd) for dd in range(2)])
    a2p = jnp.stack([_pad_rows(a2[layer, dd], ra) for dd in range(2)])
    g2p = _pad_rows(g2[layer], rg)

    h = _prep(ctx2, x2, g_mix_pre[layer], mod)
    p_r = _mm(h, w_r, F32, tm_target=768, name="in_proj_rwkv")
    p_h = _mm(h, w_h, F32, tm_target=768, name="in_proj_hgrn")

    (r, v, a, lw0, lw1, k0, k1, b0, b1, bonus, gate) = _rwkv_features(
        p_r, sel_p, mu_p, w0[layer], w2p, a0[layer], a2p, g2p, k_k[layer], k_a[layer], r_k[layer].reshape(-1), lc, l)
    yf, yr, w_up, w_down = _rwkv_scan(r, v, a, lw0, lw1, k0, k1, b0, b1, lc, [w_ff1[layer], w_ff2[layer]])
    of, orv, w_o = _hgrn_scan(p_h, hgrn_lb_logits, lc, layer, [w_out[layer]])

    u = _mix_out(yf, yr, bonus, gate, of, orv, p_h, ln_x_g[layer], ln_x_b[layer], hgrn_norm_g[layer], lc, l)
    ux = _mm(u, w_o, F32, name="out_proj")
    x1, h2 = _res1(ux, x2, mod, g_mix_post[layer], g_ffn_pre[layer])
    act = _mm(h2, w_up, PROJ_DTYPE, act="relu2", name="ffn_up")
    out = _ffn_down_residual(act, w_down, x1, mod, g_ffn_post[layer])
    return out[None]
```

```python
import functools

import jax
import jax.numpy as jnp
import numpy as np
from jax import lax
from jax.experimental import pallas as pl
from jax.experimental.pallas import tpu as pltpu

F32 = jnp.float32
BF16 = jnp.bfloat16
PROJ_DTYPE = jnp.bfloat16

LANE = 128
GRID_W = 64
CHUNK = GRID_W
RWKV_HEAD = 64
HGRN_HEAD = 128
SUB = 16
GROUP_W = 256
RWKV_GROUPS_PER_STEP = 8
HGRN_GROUPS_PER_STEP = 8
MAX_CHUNK_LOG_DECAY = 60.0
NORM_EPS = 1e-6
GN_EPS = 64e-5
EXP_M05 = float(np.exp(-0.5))
VMEM_LIMIT = 56 * 1024 * 1024


def _pick(n, target, unit=LANE):
    best = None
    for m in range(unit, min(n, target) + 1, unit):
        if n % m == 0:
            best = m
    return best if best is not None else n


def _sigmoid(z):
    return 1.0 / (1.0 + jnp.exp(-z))


def _mxu(a, b):
    return jnp.dot(a, b, preferred_element_type=F32)


def _mxu_nt(a, b):
    return lax.dot_general(a, b, (((1,), (1,)), ((), ())), preferred_element_type=F32)


def _mxu_tn(a, b):
    return lax.dot_general(a, b, (((0,), (0,)), ((), ())), preferred_element_type=F32)


def _split3(z):
    hi = z.astype(BF16)
    rest = z - hi.astype(F32)
    mid = rest.astype(BF16)
    return hi, mid, (rest - mid.astype(F32)).astype(BF16)


def _select_sum(z, sel):
    hi, mid, _ = _split3(z)
    return _mxu(hi, sel) + _mxu(mid, sel)


def _head_sum(z, hs, hst):
    return _select_sum(_select_sum(z, hs), hst)


def _head_indicators(dr):
    hs = np.arange(dr)[:, None] // RWKV_HEAD == np.arange(LANE)[None, :]
    return jnp.asarray(hs, BF16), jnp.asarray(hs.T, BF16)


def _params(sem, vmem=None):
    return pltpu.CompilerParams(dimension_semantics=sem, vmem_limit_bytes=vmem)


def _ada_kernel(c_ref, w_ref, b_ref, o_ref):
    cv = c_ref[...]
    rows = cv.shape[0]
    hi, mid, lo = (t.astype(F32) for t in _split3(cv * _sigmoid(cv)))
    s3 = jnp.concatenate([hi, mid, lo], axis=0).astype(BF16)
    w = w_ref[...]
    w_hi = w.astype(BF16)
    w_lo = (w - w_hi.astype(F32)).astype(BF16)
    p = _mxu(s3, w_hi)
    q = _mxu(s3[:2 * rows], w_lo)
    o_ref[...] = (p[:rows] + p[rows:2 * rows] + p[2 * rows:] + q[:rows] + q[rows:]) + b_ref[...]


def _ada(cvec, w, b):
    rows, d = cvec.shape
    n = w.shape[1]
    tn = _pick(n, 512)
    return pl.pallas_call(
        _ada_kernel,
        grid=(n // tn,),
        in_specs=[pl.BlockSpec((rows, d), lambda j: (0, 0)),
                  pl.BlockSpec((d, tn), lambda j: (0, j)),
                  pl.BlockSpec((1, tn), lambda j: (0, j))],
        out_specs=pl.BlockSpec((rows, tn), lambda j: (0, j)),
        out_shape=jax.ShapeDtypeStruct((rows, n), F32),
        compiler_params=_params(("arbitrary",), VMEM_LIMIT),
        name="ada_mod",
    )(cvec, w, b.reshape(1, n))


def _prep_kernel(ctx_ref, x_ref, g_ref, mod_ref, o_ref, *, ncb, d):
    is_ctx = pl.program_id(0) < ncb
    rows = jnp.where(is_ctx, ctx_ref[...], x_ref[...])
    ms = jnp.mean(rows * rows, axis=-1, keepdims=True)
    hn = rows * lax.rsqrt(ms + NORM_EPS) * g_ref[...]
    shift = jnp.where(is_ctx, mod_ref[1:2, 0:d], mod_ref[0:1, 0:d])
    scale = jnp.where(is_ctx, mod_ref[1:2, d:2 * d], mod_ref[0:1, d:2 * d])
    o_ref[...] = (hn * (1.0 + scale) + shift).astype(o_ref.dtype)


def _prep(ctx2, x2, g, mod):
    lc, d = ctx2.shape
    l = x2.shape[0]
    tb = _pick(int(np.gcd(lc, l)), 256, 8)
    ncb = lc // tb
    return pl.pallas_call(
        functools.partial(_prep_kernel, ncb=ncb, d=d),
        grid=((lc + l) // tb,),
        in_specs=[pl.BlockSpec((tb, d), lambda i: (jnp.minimum(i, ncb - 1), 0)),
                  pl.BlockSpec((tb, d), lambda i: (jnp.maximum(i - ncb, 0), 0)),
                  pl.BlockSpec((1, d), lambda i: (0, 0)),
                  pl.BlockSpec(mod.shape, lambda i: (0, 0))],
        out_specs=pl.BlockSpec((tb, d), lambda i: (i, 0)),
        out_shape=jax.ShapeDtypeStruct((lc + l, d), PROJ_DTYPE),
        compiler_params=_params(("arbitrary",)),
        name="norm_modulate",
    )(ctx2, x2, g.reshape(1, d), mod)


def _mm_kernel(x_ref, w_ref, o_ref, *, act):
    acc = jnp.dot(x_ref[...], w_ref[...], preferred_element_type=F32)
    if act == "relu2":
        acc = jnp.square(jnp.maximum(acc, 0.0))
    o_ref[...] = acc.astype(o_ref.dtype)


def _mm(x, w, out_dtype, act=None, tm_target=1024, tn_target=1024, name="matmul"):
    m, k = x.shape
    n = w.shape[1]
    tm = _pick(m, tm_target)
    tn = _pick(n, tn_target)
    return pl.pallas_call(
        functools.partial(_mm_kernel, act=act),
        grid=(m // tm, n // tn),
        in_specs=[pl.BlockSpec((tm, k), lambda i, j: (i, 0)),
                  pl.BlockSpec((k, tn), lambda i, j: (0, j))],
        out_specs=pl.BlockSpec((tm, tn), lambda i, j: (i, j)),
        out_shape=jax.ShapeDtypeStruct((m, n), out_dtype),
        compiler_params=_params(("arbitrary", "arbitrary"), VMEM_LIMIT),
        name=name,
    )(x, w)


def _ffn_down_kernel(a_ref, w_ref, x1_hbm, mod_ref, g_ref, o_ref, x1_buf, sem, *, d):
    i, kk = pl.program_id(0), pl.program_id(1)
    tm = x1_buf.shape[0]
    x1_copy = pltpu.make_async_copy(x1_hbm.at[pl.ds(pl.multiple_of(i * tm, tm), tm), :], x1_buf, sem)

    @pl.when(kk == 0)
    def _():
        x1_copy.start()
        o_ref[...] = jnp.zeros_like(o_ref)

    o_ref[...] += jnp.dot(a_ref[...], w_ref[...], preferred_element_type=F32)

    @pl.when(kk == pl.num_programs(1) - 1)
    def _():
        x1_copy.wait()
        gate = mod_ref[0:1, 5 * d:6 * d] * g_ref[...]
        rb = _pick(tm, 128, 8)
        for r0 in range(0, tm, rb):
            mv = o_ref[r0:r0 + rb, :]
            scale = lax.rsqrt(jnp.mean(mv * mv, axis=-1, keepdims=True) + NORM_EPS)
            o_ref[r0:r0 + rb, :] = x1_buf[r0:r0 + rb, :] + gate * (mv * scale)


def _ffn_down_residual(act, w, x1, mod, g_post, tm_target=512, tk_target=1024):
    m, k = act.shape
    d = w.shape[1]
    tm, tk = _pick(m, tm_target, 8), _pick(k, tk_target)
    full = lambda a: pl.BlockSpec(a.shape, lambda i, kk: (0,) * a.ndim)
    gp = g_post.reshape(1, d)
    return pl.pallas_call(
        functools.partial(_ffn_down_kernel, d=d),
        grid=(m // tm, k // tk),
        in_specs=[pl.BlockSpec((tm, tk), lambda i, kk: (i, kk)),
                  pl.BlockSpec((tk, d), lambda i, kk: (kk, 0)),
                  pl.BlockSpec(memory_space=pl.ANY), full(mod), full(gp)],
        out_specs=pl.BlockSpec((tm, d), lambda i, kk: (i, 0)),
        out_shape=jax.ShapeDtypeStruct((m, d), F32),
        scratch_shapes=[pltpu.VMEM((tm, d), F32), pltpu.SemaphoreType.DMA(())],
        compiler_params=_params(("arbitrary", "arbitrary"), VMEM_LIMIT),
        name="ffn_down",
    )(act, w, x1, mod, gp)


RING = 4


def _rwkv_feat_kernel(p_hbm, sel_ref, mu_ref, w0_ref, w2_ref, a0_ref, a2_ref, g2_ref,
                      kk_ref, ka_ref, rk_ref, hs_ref, hst_ref,
                      r_o, v_o, a_o, lw0_o, lw1_o, k0_o, k1_o, b0_o, b1_o, bonus_o, gate_o, ring, sems,
                      *, ncb, nrows, dr, rd, ra, rg):
    i = pl.program_id(0)
    nb = pl.num_programs(0)
    c = ring.shape[1]

    def fetch(blk):
        slot = blk % RING
        return pltpu.make_async_copy(p_hbm.at[pl.ds(pl.multiple_of(blk * c, c), c), :], ring.at[slot], sems.at[slot])

    @pl.when(i == 0)
    def _():
        fetch(0).start()
        fetch(1).start()
        fetch(2).start()
        fetch(0).wait()
        fetch(1).wait()

    @pl.when((i > 0) & (i + 1 < nb))
    def _():
        fetch(i + 1).wait()

    @pl.when((i > 0) & (i + 2 < nb))
    def _():
        fetch(i + 2).start()

    is_ctx = i < ncb
    xi = i - ncb
    cur = ring[i % RING]
    prv = ring[jnp.maximum(i - 1, 0) % RING]
    nxt = ring[jnp.minimum(i + 1, nb - 1) % RING]
    row = lax.broadcasted_iota(jnp.int32, cur.shape, 0)
    prev_last = jnp.where(is_ctx & (i > 0), prv[c - 1:c, :], 0.0)
    next_first = jnp.where(is_ctx & (i < ncb - 1), nxt[0:1, :], 0.0)
    before = jnp.where(row == 0, prev_last, pltpu.roll(cur, 1, 0))
    after = jnp.where(row == c - 1, next_first, pltpu.roll(cur, c - 1, 0))
    above = jnp.where(is_ctx, before, jnp.where(xi > 0, prv, 0.0))
    below = jnp.where(is_ctx, after, jnp.where(xi < nrows - 1, nxt, 0.0))
    sel = sel_ref[...]
    shifted = jnp.where(sel == 0, before, jnp.where(sel == 1, after, jnp.where(sel == 2, above, below)))
    m = cur + mu_ref[...] * (shifted - cur)

    r, k, v = m[:, 0:dr], m[:, dr:2 * dr], m[:, 2 * dr:3 * dr]
    o = 3 * dr
    wl = (m[:, o:o + rd], m[:, o + rd:o + 2 * rd])
    o += 2 * rd
    al = (m[:, o:o + ra], m[:, o + ra:o + 2 * ra])
    o += 2 * ra
    gl = m[:, o:o + rg]

    hs, hst = hs_ref[...], hst_ref[...]

    def headsum(z):
        return _head_sum(z, hs, hst)

    kkf = k * kk_ref[...]
    kk = kkf * lax.rsqrt(headsum(kkf * kkf) + 1e-12)
    r_o[...] = r
    v_o[...] = v
    a_o[...] = -kk
    for d, (lw_o, k_o, b_o) in enumerate(((lw0_o, k0_o, b0_o), (lw1_o, k1_o, b1_o))):
        wd = _mxu(jnp.tanh(wl[d]).astype(BF16), w2_ref[d]) + w0_ref[d:d + 1, :]
        lw_o[...] = -EXP_M05 * _sigmoid(wd)
        ad = _sigmoid(_mxu(al[d].astype(BF16), a2_ref[d]) + a0_ref[d:d + 1, :])
        k_o[...] = k * (1.0 + (ad - 1.0) * ka_ref[...])
        b_o[...] = kk * ad
    bonus_o[...] = headsum(r * k * rk_ref[...]) * v
    gate_o[...] = _mxu(_sigmoid(gl).astype(BF16), g2_ref[...])


def _rwkv_features(p_r, sel, mu, w0, w2p, a0, a2p, g2p, k_k, k_a, r_k, lc, l):
    n, nr = p_r.shape
    dr = k_k.shape[0]
    rd, ra, rg = w2p.shape[1], a2p.shape[1], g2p.shape[0]
    nb, ncb = n // CHUNK, lc // CHUNK
    nheads = dr // RWKV_HEAD
    assert nheads <= LANE
    hs, hst = _head_indicators(dr)
    full = lambda a: pl.BlockSpec(a.shape, lambda i: (0,) * a.ndim)
    row1 = lambda a: a.reshape(1, -1)
    consts = [sel, mu, w0, w2p.astype(BF16), a0, a2p.astype(BF16), g2p.astype(BF16), row1(k_k), row1(k_a),
              row1(r_k), hs, hst]
    assert nb >= 3
    outs = pl.pallas_call(
        functools.partial(_rwkv_feat_kernel, ncb=ncb, nrows=l // CHUNK, dr=dr, rd=rd, ra=ra, rg=rg),
        grid=(nb,),
        in_specs=[pl.BlockSpec(memory_space=pl.ANY)] + [full(a) for a in consts],
        out_specs=[pl.BlockSpec((CHUNK, dr), lambda i: (i, 0))] * 11,
        out_shape=[jax.ShapeDtypeStruct((n, dr), F32)] * 11,
        scratch_shapes=[pltpu.VMEM((RING, CHUNK, nr), F32), pltpu.SemaphoreType.DMA((RING,))],
        compiler_params=_params(("arbitrary",), VMEM_LIMIT),
        name="rwkv_features",
    )(p_r, *consts)
    return outs


def _block_diag_mask(n, w):
    r = lax.broadcasted_iota(jnp.int32, (n, n), 0)
    c = lax.broadcasted_iota(jnp.int32, (n, n), 1)
    return (r // w) == (c // w)


def _cumsum_rows(z, rev):
    c = z.shape[0]
    t = lax.broadcasted_iota(jnp.int32, (c, c), 0)
    s = lax.broadcasted_iota(jnp.int32, (c, c), 1)
    tri = ((s >= t) if rev else (s <= t)).astype(BF16)
    hi, mid, lo = _split3(z)
    return _mxu(tri, hi) + _mxu(tri, mid) + _mxu(tri, lo)


def _rwkv_chunks(chains):
    c, w = chains[0][0].shape
    nh = w // c
    every = range(len(chains))
    revs = [ch[7] for ch in chains]
    bd = _block_diag_mask(w, c)
    t_i = lax.broadcasted_iota(jnp.int32, (c, w), 0)
    s_i = lax.broadcasted_iota(jnp.int32, (c, w), 1) % c
    eye = (s_i == t_i).astype(F32)
    strict = {False: s_i < t_i, True: s_i > t_i}
    incl = {False: s_i <= t_i, True: s_i >= t_i}

    def expand(z):
        return jnp.where(bd, jnp.concatenate([z] * nh, axis=0), jnp.zeros((), z.dtype))

    def packed_mm(lhs, rhs):
        return _mxu(lhs.astype(BF16), expand(rhs.astype(BF16)))

    bc = [_cumsum_rows(ch[1], ch[7]) for ch in chains]
    ar, bk, v16 = [], [], []
    for i, (r, lw, k, v, a, b, _, _) in enumerate(chains):
        e_in, e_ex, e_ng = jnp.exp(bc[i]), jnp.exp(bc[i] - lw), jnp.exp(-bc[i])
        ar.append(jnp.concatenate([a * e_ex, r * e_in], axis=0).astype(BF16))
        bk.append(jnp.concatenate([b * e_ng, k * e_ng], axis=0).astype(BF16))
        v16.append(v.astype(BF16))
    gb = [_mxu_nt(ar[i], expand(bk[i][:c])) for i in every]
    gk = [_mxu_nt(ar[i], expand(bk[i][c:])) for i in every]
    l_ab = [jnp.where(strict[revs[i]], gb[i][:c], 0.0) for i in every]
    m_rb = [jnp.where(incl[revs[i]], gb[i][c:], 0.0) for i in every]
    lmk = [jnp.concatenate([jnp.where(strict[revs[i]], gk[i][:c], 0.0),
                            jnp.where(incl[revs[i]], gk[i][c:], 0.0)], axis=0) for i in every]
    t_m = [eye + l_ab[i] for i in every]
    p_m = [packed_mm(l_ab[i], l_ab[i]) for i in every]
    for _ in range(int(np.log2(c)) - 2):
        tp = [packed_mm(jnp.concatenate([t_m[i], p_m[i]], axis=0), p_m[i]) for i in every]
        t_m = [t_m[i] + tp[i][:c] for i in every]
        p_m = [tp[i][c:] for i in every]
    t_m = [t_m[i] + packed_mm(t_m[i], p_m[i]) for i in every]

    ars = [_mxu_nt(ar[i], chains[i][6].astype(BF16)) for i in every]
    lm = [_mxu(lmk[i].astype(BF16), expand(v16[i])) for i in every]
    u16 = [packed_mm(t_m[i], ars[i][:c] + lm[i][:c]).astype(BF16) for i in every]
    ys = [ars[i][c:] + _mxu(m_rb[i].astype(BF16), expand(u16[i])) + lm[i][c:] for i in every]
    ds = [_mxu_tn(jnp.concatenate([u16[i], v16[i]], axis=0), bk[i]) for i in every]
    s_new = []
    for i in every:
        last = bc[i][0:1] if revs[i] else bc[i][c - 1:c]
        s_new.append((chains[i][6] + jnp.where(bd, ds[i], 0.0)) * jnp.exp(last))
    return ys, s_new


def _cast_jobs(weights, ng, nb):
    in_specs, out_specs, shapes = [], [], []
    for w, col0, cols in weights:
        rows = w.shape[0]
        n = max(m for m in range(1, ng * nb + 1) if rows % m == 0 and (rows // m) % 16 == 0)
        slab = lambda g, c, n=n: jnp.minimum(g * nb + c, n - 1)
        in_specs.append(pl.BlockSpec((pl.Element(rows // n), pl.Element(cols)),
                                     lambda g, c, s=slab, h=rows // n, col0=col0: (s(g, c) * h, col0)))
        out_specs.append(pl.BlockSpec((rows // n, cols), lambda g, c, s=slab: (s(g, c), 0)))
        shapes.append(jax.ShapeDtypeStruct((rows, cols), PROJ_DTYPE))
    return in_specs, out_specs, shapes


def _run_cast_jobs(src_refs, dst_refs):
    for src, dst in zip(src_refs, dst_refs):
        dst[...] = src[...].astype(dst.dtype)


def _rwkv_scan_kernel(rf, lwf, kf, vf, af, bf, rr, lwr, kr, vr, ar, br, *rest):
    n_cast = (len(rest) - 3) // 2
    yf_o, yr_o = rest[n_cast:n_cast + 2]
    s_ref = rest[-1]
    _run_cast_jobs(rest[:n_cast], rest[n_cast + 2:-1])

    @pl.when(pl.program_id(1) == 0)
    def _():
        s_ref[...] = jnp.zeros_like(s_ref)

    ngs = rf.shape[1] // GROUP_W
    chains = []
    for d, refs in enumerate(((rf, lwf, kf, vf, af, bf), (rr, lwr, kr, vr, ar, br))):
        for g in range(ngs):
            sl = slice(g * GROUP_W, (g + 1) * GROUP_W)
            chains.append(tuple(ref[:, sl] for ref in refs) + (s_ref[d, g], d == 1))
    ys, s_new = _rwkv_chunks(chains)
    for d, o_ref in enumerate((yf_o, yr_o)):
        for g in range(ngs):
            o_ref[:, g * GROUP_W:(g + 1) * GROUP_W] = ys[d * ngs + g]
            s_ref[d, g] = s_new[d * ngs + g]


def _rev_chunk(c, ncb, nb):
    return jnp.where(c < ncb, ncb - 1 - c, nb - 1 - (c - ncb))


def _rwkv_scan(r, v, a, lw0, lw1, k0, k1, b0, b1, lc, cast_weights):
    n, dr = r.shape
    nb, ncb = n // CHUNK, lc // CHUNK
    bw = _pick(dr, RWKV_GROUPS_PER_STEP * GROUP_W, GROUP_W)
    fwd = pl.BlockSpec((CHUNK, bw), lambda g, c: (c, g))
    rev = pl.BlockSpec((CHUNK, bw), lambda g, c: (_rev_chunk(c, ncb, nb), g))
    cast_in, cast_out, cast_shapes = _cast_jobs(cast_weights, dr // bw, nb)
    return pl.pallas_call(
        _rwkv_scan_kernel,
        grid=(dr // bw, nb),
        in_specs=[fwd] * 6 + [rev] * 6 + cast_in,
        out_specs=[fwd, rev] + cast_out,
        out_shape=[jax.ShapeDtypeStruct((n, dr), F32)] * 2 + cast_shapes,
        scratch_shapes=[pltpu.VMEM((2, bw // GROUP_W, GROUP_W, GROUP_W), F32)],
        compiler_params=_params(("arbitrary", "arbitrary"), VMEM_LIMIT),
        name="rwkv_scan",
    )(r, lw0, k0, v, a, b0, r, lw1, k1, v, a, b1, *[w for w, _, _ in cast_weights])


def _hgrn_chunk(q, k, v, lf, s_vk, rev):
    c, w = q.shape
    nsb = c // SUB
    bc = _cumsum_rows(lf, rev)
    last = bc[0:1] if rev else bc[c - 1:c]
    bdh = _block_diag_mask(w, HGRN_HEAD)
    v16 = v.astype(BF16)
    o_inter = _mxu_nt((q * jnp.exp(bc)).astype(BF16), s_vk.astype(BF16))
    s_new = s_vk * jnp.exp(last) + jnp.where(bdh, _mxu_tn(v16, (k * jnp.exp(last - bc)).astype(BF16)), 0.0)

    lane2 = lax.broadcasted_iota(jnp.int32, (c, w), 1)
    row2 = lax.broadcasted_iota(jnp.int32, (c, w), 0)
    zpad = jnp.zeros((HGRN_HEAD - c, w), BF16)

    def expand(z):
        parts = []
        for h in range(w // HGRN_HEAD):
            parts += [jnp.where(lane2 // HGRN_HEAD == h, z, jnp.zeros((), BF16)), zpad]
        return jnp.concatenate(parts, axis=0)

    ones_bd = bdh.astype(BF16)
    t3 = lax.broadcasted_iota(jnp.int32, (SUB, SUB, w), 0)
    s3 = lax.broadcasted_iota(jnp.int32, (SUB, SUB, w), 1)
    l3 = lax.broadcasted_iota(jnp.int32, (SUB, SUB, w), 2) % HGRN_HEAD
    causal = (s3 >= t3) if rev else (s3 <= t3)
    a_rows = []
    for i in range(nsb):
        lo, hi = SUB * i, SUB * (i + 1)
        qi, ki, bi = q[lo:hi], k[lo:hi], bc[lo:hi]
        d3 = bi[:, None, :] - bi[None, :, :]
        x3 = jnp.where(causal, (qi[:, None, :] * ki[None, :, :]) * jnp.exp(jnp.minimum(d3, 0.0)), 0.0)
        r3 = _mxu(x3.reshape(SUB * SUB, w).astype(BF16), ones_bd).reshape(SUB, SUB, w)
        a_i = jnp.sum(jnp.where(l3 == s3 + lo, r3, 0.0), axis=1)
        if (not rev and i > 0) or (rev and i < nsb - 1):
            beta = bc[hi:hi + 1] if rev else bc[lo - 1:lo]
            earlier = (row2 >= hi) if rev else (row2 < lo)
            kp = jnp.where(earlier, k * jnp.exp(jnp.minimum(beta - bc, 0.0)), 0.0)
            a_i = a_i + _mxu_nt((qi * jnp.exp(bi - beta)).astype(BF16), expand(kp.astype(BF16)))
        a_rows.append(a_i)
    o = o_inter + _mxu(jnp.concatenate(a_rows, axis=0).astype(BF16), expand(v16))
    return o, s_new


def _hgrn_chunks_bounded(chains):
    c, w = chains[0][0].shape
    every = range(len(chains))
    revs = [ch[5] for ch in chains]
    bdh = _block_diag_mask(w, HGRN_HEAD)
    lane2 = lax.broadcasted_iota(jnp.int32, (c, w), 1)
    t_i = lax.broadcasted_iota(jnp.int32, (c, w), 0)
    s_i = lane2 % HGRN_HEAD
    incl = {False: s_i <= t_i, True: s_i >= t_i}
    zpad = jnp.zeros((HGRN_HEAD - c, w), BF16)

    def expand(z):
        parts = []
        for h in range(w // HGRN_HEAD):
            parts += [jnp.where(lane2 // HGRN_HEAD == h, z, jnp.zeros((), BF16)), zpad]
        return jnp.concatenate(parts, axis=0)

    bc = [_cumsum_rows(ch[3], ch[5]) for ch in chains]
    last = [bc[i][0:1] if revs[i] else bc[i][c - 1:c] for i in every]
    qt = [(chains[i][0] * jnp.exp(bc[i])).astype(BF16) for i in every]
    kt = [(chains[i][1] * jnp.exp(-bc[i])).astype(BF16) for i in every]
    kd = [(chains[i][1] * jnp.exp(last[i] - bc[i])).astype(BF16) for i in every]
    v16 = [chains[i][2].astype(BF16) for i in every]
    o_inter = [_mxu_nt(qt[i], chains[i][4].astype(BF16)) for i in every]
    a = [jnp.where(incl[revs[i]], _mxu_nt(qt[i], expand(kt[i])), 0.0).astype(BF16) for i in every]
    o = [o_inter[i] + _mxu(a[i], expand(v16[i])) for i in every]
    ds = [_mxu_tn(v16[i], kd[i]) for i in every]
    s_new = [chains[i][4] * jnp.exp(last[i]) + jnp.where(bdh, ds[i], 0.0) for i in every]
    return o, s_new


def _hgrn_scan_kernel(qf, ff, vf, qr, fr, vr, lg_ref, *rest, layer):
    n_cast = (len(rest) - 3) // 2
    of_o, or_o = rest[n_cast:n_cast + 2]
    s_ref = rest[-1]
    _run_cast_jobs(rest[:n_cast], rest[n_cast + 2:-1])

    @pl.when(pl.program_id(1) == 0)
    def _():
        s_ref[...] = jnp.zeros_like(s_ref)

    lg = lg_ref[...]
    e = jnp.exp(lg - jnp.max(lg, axis=0, keepdims=True))
    lb = jnp.sum(e[:layer + 1], axis=0) / jnp.sum(e, axis=0)
    ngs = qf.shape[1] // GROUP_W
    chains = []
    min_total = None
    for d, (q_ref, f_ref, v_ref) in enumerate(((qf, ff, vf), (qr, fr, vr))):
        fd = lb[d:d + 1, :] + (1.0 - lb[d:d + 1, :]) * _sigmoid(f_ref[...])
        qv = q_ref[...]
        qh, kh, vh, lf = qv * _sigmoid(qv), 1.0 - fd, v_ref[...], jnp.log(fd)
        total = jnp.min(jnp.sum(lf, axis=0, keepdims=True))
        min_total = total if min_total is None else jnp.minimum(min_total, total)
        for g in range(ngs):
            sl = slice(g * GROUP_W, (g + 1) * GROUP_W)
            chains.append((qh[:, sl], kh[:, sl], vh[:, sl], lf[:, sl], s_ref[d, g], d == 1))

    def emit(o, s_new):
        for d, o_ref in enumerate((of_o, or_o)):
            for g in range(ngs):
                o_ref[:, g * GROUP_W:(g + 1) * GROUP_W] = o[d * ngs + g]
                s_ref[d, g] = s_new[d * ngs + g]

    bounded = min_total >= -MAX_CHUNK_LOG_DECAY

    @pl.when(bounded)
    def _():
        emit(*_hgrn_chunks_bounded(chains))

    @pl.when(jnp.logical_not(bounded))
    def _():
        res = [_hgrn_chunk(*ch) for ch in chains]
        emit([r[0] for r in res], [r[1] for r in res])


def _hgrn_scan(p_h, lb_logits, lc, layer, cast_weights):
    n = p_h.shape[0]
    dh = lb_logits.shape[2]
    nb, ncb = n // CHUNK, lc // CHUNK
    bw = _pick(dh, HGRN_GROUPS_PER_STEP * GROUP_W, GROUP_W)
    ng = dh // bw
    blk = (CHUNK, bw)
    fwd = lambda sec: pl.BlockSpec(blk, lambda g, c: (c, sec * ng + g))
    rev = lambda sec: pl.BlockSpec(blk, lambda g, c: (_rev_chunk(c, ncb, nb), sec * ng + g))
    cast_in, cast_out, cast_shapes = _cast_jobs(cast_weights, ng, nb)
    return pl.pallas_call(
        functools.partial(_hgrn_scan_kernel, layer=layer),
        grid=(ng, nb),
        in_specs=[fwd(0), fwd(1), fwd(3), rev(0), rev(2), rev(3),
                  pl.BlockSpec((lb_logits.shape[0], 2, bw), lambda g, c: (0, 0, g))] + cast_in,
        out_specs=[pl.BlockSpec(blk, lambda g, c: (c, g)),
                   pl.BlockSpec(blk, lambda g, c: (_rev_chunk(c, ncb, nb), g))] + cast_out,
        out_shape=[jax.ShapeDtypeStruct((n, dh), F32)] * 2 + cast_shapes,
        scratch_shapes=[pltpu.VMEM((2, bw // GROUP_W, GROUP_W, GROUP_W), F32)],
        compiler_params=_params(("arbitrary", "arbitrary"), VMEM_LIMIT),
        name="hgrn_scan",
    )(p_h, p_h, p_h, p_h, p_h, p_h, lb_logits, *[w for w, _, _ in cast_weights])


def _mix_out_kernel(yf, yr, bonus, gate, of, orv, g_ref, lng, lnb, hng, hs_ref, hst_ref, u_o, *, dr, dh):
    hs, hst = hs_ref[...], hst_ref[...]

    def headmean(z):
        return _head_sum(z, hs, hst) * (1.0 / RWKV_HEAD)

    y = yf[...] + yr[...]
    yc = y - headmean(y)
    yn = yc * lax.rsqrt(headmean(yc * yc) + GN_EPS) * lng[...] + lnb[...]
    u_o[:, 0:dr] = ((yn + bonus[...]) * gate[...]).astype(u_o.dtype)
    o = of[...] + orv[...]
    g = g_ref[...]
    sg = g * _sigmoid(g)
    for h in range(dh // HGRN_HEAD):
        sl = slice(h * HGRN_HEAD, (h + 1) * HGRN_HEAD)
        oh = o[:, sl]
        on = oh * lax.rsqrt(jnp.mean(oh * oh, axis=-1, keepdims=True) + NORM_EPS) * hng[...]
        u_o[:, dr + h * HGRN_HEAD:dr + (h + 1) * HGRN_HEAD] = (on * sg[:, sl]).astype(u_o.dtype)


def _mix_out(yf, yr, bonus, gate, of, orv, p_h, ln_g, ln_b, hg, lc, l):
    dr, dh = yf.shape[1], of.shape[1]
    tb = _pick(int(np.gcd(lc, l)), 128, 8)
    off = lc // tb
    hs, hst = _head_indicators(dr)
    rows = lambda wd: pl.BlockSpec((tb, wd), lambda i: (i + off, 0))
    full = lambda a: pl.BlockSpec(a.shape, lambda i: (0,) * a.ndim)
    consts = [ln_g.reshape(1, dr), ln_b.reshape(1, dr), hg.reshape(1, HGRN_HEAD), hs, hst]
    return pl.pallas_call(
        functools.partial(_mix_out_kernel, dr=dr, dh=dh),
        grid=(l // tb,),
        in_specs=[rows(dr)] * 4 + [rows(dh)] * 2 + [pl.BlockSpec((tb, dh), lambda i: (i + off, 4))]
                 + [full(a) for a in consts],
        out_specs=pl.BlockSpec((tb, dr + dh), lambda i: (i, 0)),
        out_shape=jax.ShapeDtypeStruct((l, dr + dh), PROJ_DTYPE),
        compiler_params=_params(("arbitrary",), VMEM_LIMIT),
        name="mix_out",
    )(yf, yr, bonus, gate, of, orv, p_h, *consts)


def _res1_kernel(ux_ref, x_ref, mod_ref, gpost_ref, gpre_ref, x1_o, h2_o, *, d):
    ux = ux_ref[...]
    nrm = ux * lax.rsqrt(jnp.mean(ux * ux, axis=-1, keepdims=True) + NORM_EPS) * gpost_ref[...]
    x1 = x_ref[...] + mod_ref[0:1, 2 * d:3 * d] * nrm
    x1_o[...] = x1
    hn = x1 * lax.rsqrt(jnp.mean(x1 * x1, axis=-1, keepdims=True) + NORM_EPS) * gpre_ref[...]
    h2_o[...] = (hn * (1.0 + mod_ref[0:1, 4 * d:5 * d]) + mod_ref[0:1, 3 * d:4 * d]).astype(h2_o.dtype)


def _res1(ux, x2, mod, g_post, g_pre):
    l, d = x2.shape
    tb = _pick(l, 256, 8)
    rows = pl.BlockSpec((tb, d), lambda i: (i, 0))
    full = lambda a: pl.BlockSpec(a.shape, lambda i: (0,) * a.ndim)
    gp, gq = g_post.reshape(1, d), g_pre.reshape(1, d)
    return pl.pallas_call(
        functools.partial(_res1_kernel, d=d),
        grid=(l // tb,),
        in_specs=[rows, rows, full(mod), full(gp), full(gq)],
        out_specs=[rows, rows],
        out_shape=[jax.ShapeDtypeStruct((l, d), F32), jax.ShapeDtypeStruct((l, d), PROJ_DTYPE)],
        compiler_params=_params(("arbitrary",), VMEM_LIMIT),
        name="residual_mix",
    )(ux, x2, mod, gp, gq)


def _pad_cols(a, width):
    return jnp.pad(a, ((0, 0), (0, width - a.shape[1])))


def _pad_rows(a, height):
    return jnp.pad(a, ((0, height - a.shape[0]), (0, 0)))


def _round_up(n, m):
    return (n + m - 1) // m * m


def kernel(x, c, ctx, c_ctx, w_ada, b_ada, g_mix_pre, g_mix_post, g_ffn_pre, g_ffn_post, w_in, mu_shift, w0, w2, a0,
           a2, g2, k_k, k_a, r_k, ln_x_g, ln_x_b, hgrn_lb_logits, hgrn_norm_g, w_out, w_ff1, w_ff2):
    assert x.shape[0] == 1 and w_in.shape[0] == 1, "single batch, single layer"
    layer = 0
    x2, ctx2 = x[0], ctx[0]
    l, d = x2.shape
    lc = ctx2.shape[0]
    dr, dh = k_k.shape[1], hgrn_lb_logits.shape[2]
    rd_raw, ra_raw, rg_raw = w2.shape[2], a2.shape[2], g2.shape[1]
    rd, ra, rg = (_round_up(v, LANE) for v in (rd_raw, ra_raw, rg_raw))
    assert l % GRID_W == 0 and lc % CHUNK == 0 and dr % GROUP_W == 0 and dh % GROUP_W == 0

    cvec = jnp.concatenate([c, c_ctx[None, :], jnp.zeros((6, d), F32)], axis=0)
    mod = _ada(cvec, w_ada[layer], b_ada[layer])

    wi = w_in[layer]
    mu = mu_shift[layer][None, :]
    sel = (jnp.arange(mu.shape[1], dtype=jnp.int32) % 4)[None, :]
    o = 3 * dr
    cuts = [(0, o, o)]
    for raw, padded in ((rd_raw, rd), (rd_raw, rd), (ra_raw, ra), (ra_raw, ra), (rg_raw, rg)):
        cuts.append((o, o + raw, padded))
        o += raw
    rw_cols = o
    regroup = lambda a: jnp.concatenate([_pad_cols(a[:, s:e], wd) for s, e, wd in cuts], axis=1)
    w_r = regroup(wi[:, :rw_cols]).astype(PROJ_DTYPE)
    mu_p, sel_p = regroup(mu), regroup(sel)
    w2p = jnp.stack([_pad_rows(w2[layer, dd], rd) for dd in range(2)])
    a2p = jnp.stack([_pad_rows(a2[layer, dd], ra) for dd in range(2)])
    g2p = _pad_rows(g2[layer], rg)

    h = _prep(ctx2, x2, g_mix_pre[layer], mod)
    p_r = _mm(h, w_r, F32, tm_target=768, name="in_proj_rwkv")

    (r, v, a, lw0, lw1, k0, k1, b0, b1, bonus, gate) = _rwkv_features(
        p_r, sel_p, mu_p, w0[layer], w2p, a0[layer], a2p, g2p, k_k[layer], k_a[layer], r_k[layer].reshape(-1), lc, l)
    whole = lambda w: (w, 0, w.shape[1])
    yf, yr, w_h, w_up = _rwkv_scan(r, v, a, lw0, lw1, k0, k1, b0, b1, lc,
                                   [(wi, rw_cols, wi.shape[1] - rw_cols), whole(w_ff1[layer])])
    p_h = _mm(h, w_h, F32, tm_target=768, name="in_proj_hgrn")
    of, orv, w_down, w_o = _hgrn_scan(p_h, hgrn_lb_logits, lc, layer, [whole(w_ff2[layer]), whole(w_out[layer])])

    u = _mix_out(yf, yr, bonus, gate, of, orv, p_h, ln_x_g[layer], ln_x_b[layer], hgrn_norm_g[layer], lc, l)
    ux = _mm(u, w_o, F32, name="out_proj")
    x1, h2 = _res1(ux, x2, mod, g_mix_post[layer], g_ffn_pre[layer])
    act = _mm(h2, w_up, PROJ_DTYPE, act="relu2", name="ffn_up")
    out = _ffn_down_residual(act, w_down, x1, mod, g_ffn_post[layer])
    return out[None]
```

```python
import functools

import jax
import jax.numpy as jnp
import numpy as np
from jax import lax
from jax.experimental import pallas as pl
from jax.experimental.pallas import tpu as pltpu

F32 = jnp.float32
BF16 = jnp.bfloat16
PROJ_DTYPE = jnp.bfloat16

LANE = 128
GRID_W = 64
CHUNK = GRID_W
RWKV_HEAD = 64
HGRN_HEAD = 128
SUB = 16
GROUP_W = 256
RWKV_GROUPS_PER_STEP = 8
HGRN_GROUPS_PER_STEP = 8
MAX_CHUNK_LOG_DECAY = 60.0
NORM_EPS = 1e-6
GN_EPS = 64e-5
EXP_M05 = float(np.exp(-0.5))
VMEM_LIMIT = 56 * 1024 * 1024


def _pick(n, target, unit=LANE):
    best = None
    for m in range(unit, min(n, target) + 1, unit):
        if n % m == 0:
            best = m
    return best if best is not None else n


def _sigmoid(z):
    return 1.0 / (1.0 + jnp.exp(-z))


def _mxu(a, b):
    return jnp.dot(a, b, preferred_element_type=F32)


def _mxu_nt(a, b):
    return lax.dot_general(a, b, (((1,), (1,)), ((), ())), preferred_element_type=F32)


def _mxu_tn(a, b):
    return lax.dot_general(a, b, (((0,), (0,)), ((), ())), preferred_element_type=F32)


def _split3(z):
    hi = z.astype(BF16)
    rest = z - hi.astype(F32)
    mid = rest.astype(BF16)
    return hi, mid, (rest - mid.astype(F32)).astype(BF16)


def _select_sum(z, sel):
    hi, mid, _ = _split3(z)
    return _mxu(hi, sel) + _mxu(mid, sel)


def _head_sum(z, hs, hst):
    return _select_sum(_select_sum(z, hs), hst)


def _head_indicators(dr):
    hs = np.arange(dr)[:, None] // RWKV_HEAD == np.arange(LANE)[None, :]
    return jnp.asarray(hs, BF16), jnp.asarray(hs.T, BF16)


def _params(sem, vmem=None):
    return pltpu.CompilerParams(dimension_semantics=sem, vmem_limit_bytes=vmem)


def _ada_kernel(c_ref, w_ref, b_ref, o_ref):
    cv = c_ref[...]
    rows = cv.shape[0]
    hi, mid, lo = (t.astype(F32) for t in _split3(cv * _sigmoid(cv)))
    s3 = jnp.concatenate([hi, mid, lo], axis=0).astype(BF16)
    w = w_ref[...]
    w_hi = w.astype(BF16)
    w_lo = (w - w_hi.astype(F32)).astype(BF16)
    p = _mxu(s3, w_hi)
    q = _mxu(s3[:2 * rows], w_lo)
    o_ref[...] = (p[:rows] + p[rows:2 * rows] + p[2 * rows:] + q[:rows] + q[rows:]) + b_ref[...]


def _ada(cvec, w, b):
    rows, d = cvec.shape
    n = w.shape[1]
    tn = _pick(n, 512)
    return pl.pallas_call(
        _ada_kernel,
        grid=(n // tn,),
        in_specs=[pl.BlockSpec((rows, d), lambda j: (0, 0)),
                  pl.BlockSpec((d, tn), lambda j: (0, j)),
                  pl.BlockSpec((1, tn), lambda j: (0, j))],
        out_specs=pl.BlockSpec((rows, tn), lambda j: (0, j)),
        out_shape=jax.ShapeDtypeStruct((rows, n), F32),
        compiler_params=_params(("arbitrary",), VMEM_LIMIT),
        name="ada_mod",
    )(cvec, w, b.reshape(1, n))


def _prep_kernel(ctx_ref, x_ref, g_ref, mod_ref, o_ref, *, ncb, d):
    is_ctx = pl.program_id(0) < ncb
    rows = jnp.where(is_ctx, ctx_ref[...], x_ref[...])
    ms = jnp.mean(rows * rows, axis=-1, keepdims=True)
    hn = rows * lax.rsqrt(ms + NORM_EPS) * g_ref[...]
    shift = jnp.where(is_ctx, mod_ref[1:2, 0:d], mod_ref[0:1, 0:d])
    scale = jnp.where(is_ctx, mod_ref[1:2, d:2 * d], mod_ref[0:1, d:2 * d])
    o_ref[...] = (hn * (1.0 + scale) + shift).astype(o_ref.dtype)


def _prep(ctx2, x2, g, mod):
    lc, d = ctx2.shape
    l = x2.shape[0]
    tb = _pick(int(np.gcd(lc, l)), 256, 8)
    ncb = lc // tb
    return pl.pallas_call(
        functools.partial(_prep_kernel, ncb=ncb, d=d),
        grid=((lc + l) // tb,),
        in_specs=[pl.BlockSpec((tb, d), lambda i: (jnp.minimum(i, ncb - 1), 0)),
                  pl.BlockSpec((tb, d), lambda i: (jnp.maximum(i - ncb, 0), 0)),
                  pl.BlockSpec((1, d), lambda i: (0, 0)),
                  pl.BlockSpec(mod.shape, lambda i: (0, 0))],
        out_specs=pl.BlockSpec((tb, d), lambda i: (i, 0)),
        out_shape=jax.ShapeDtypeStruct((lc + l, d), PROJ_DTYPE),
        compiler_params=_params(("arbitrary",)),
        name="norm_modulate",
    )(ctx2, x2, g.reshape(1, d), mod)


def _mm_kernel(x_ref, w_ref, o_ref, *, act):
    acc = jnp.dot(x_ref[...], w_ref[...], preferred_element_type=F32)
    if act == "relu2":
        acc = jnp.square(jnp.maximum(acc, 0.0))
    o_ref[...] = acc.astype(o_ref.dtype)


def _mm(x, w, out_dtype, act=None, tm_target=1024, tn_target=1024, name="matmul"):
    m, k = x.shape
    n = w.shape[1]
    tm = _pick(m, tm_target)
    tn = _pick(n, tn_target)
    return pl.pallas_call(
        functools.partial(_mm_kernel, act=act),
        grid=(m // tm, n // tn),
        in_specs=[pl.BlockSpec((tm, k), lambda i, j: (i, 0)),
                  pl.BlockSpec((k, tn), lambda i, j: (0, j))],
        out_specs=pl.BlockSpec((tm, tn), lambda i, j: (i, j)),
        out_shape=jax.ShapeDtypeStruct((m, n), out_dtype),
        compiler_params=_params(("arbitrary", "arbitrary"), VMEM_LIMIT),
        name=name,
    )(x, w)


def _mm_wcast_kernel(x_ref, w_ref, o_ref, w_cast):
    @pl.when(pl.program_id(1) == 0)
    def _():
        w_cast[...] = w_ref[...].astype(w_cast.dtype)

    o_ref[...] = jnp.dot(x_ref[...], w_cast[...], preferred_element_type=F32).astype(o_ref.dtype)


def _mm_wcast(x, w, ncols, out_dtype, tm_target=1024, tn_target=1024, name="matmul_wcast"):
    m, k = x.shape
    tm, tn = _pick(m, tm_target), _pick(ncols, tn_target)
    return pl.pallas_call(
        _mm_wcast_kernel,
        grid=(ncols // tn, m // tm),
        in_specs=[pl.BlockSpec((tm, k), lambda j, i: (i, 0)),
                  pl.BlockSpec((k, tn), lambda j, i: (0, j))],
        out_specs=pl.BlockSpec((tm, tn), lambda j, i: (i, j)),
        out_shape=jax.ShapeDtypeStruct((m, ncols), out_dtype),
        scratch_shapes=[pltpu.VMEM((k, tn), PROJ_DTYPE)],
        compiler_params=_params(("arbitrary", "arbitrary"), VMEM_LIMIT),
        name=name,
    )(x, w)


def _ffn_down_kernel(a_ref, w_ref, x1_hbm, mod_ref, g_ref, o_ref, x1_buf, sem, *, d):
    i, kk = pl.program_id(0), pl.program_id(1)
    tm = x1_buf.shape[0]
    x1_copy = pltpu.make_async_copy(x1_hbm.at[pl.ds(pl.multiple_of(i * tm, tm), tm), :], x1_buf, sem)

    @pl.when(kk == 0)
    def _():
        x1_copy.start()
        o_ref[...] = jnp.zeros_like(o_ref)

    o_ref[...] += jnp.dot(a_ref[...], w_ref[...], preferred_element_type=F32)

    @pl.when(kk == pl.num_programs(1) - 1)
    def _():
        x1_copy.wait()
        gate = mod_ref[0:1, 5 * d:6 * d] * g_ref[...]
        rb = _pick(tm, 128, 8)
        for r0 in range(0, tm, rb):
            mv = o_ref[r0:r0 + rb, :]
            scale = lax.rsqrt(jnp.mean(mv * mv, axis=-1, keepdims=True) + NORM_EPS)
            o_ref[r0:r0 + rb, :] = x1_buf[r0:r0 + rb, :] + gate * (mv * scale)


def _ffn_down_residual(act, w, x1, mod, g_post, tm_target=512, tk_target=1024):
    m, k = act.shape
    d = w.shape[1]
    tm, tk = _pick(m, tm_target, 8), _pick(k, tk_target)
    full = lambda a: pl.BlockSpec(a.shape, lambda i, kk: (0,) * a.ndim)
    gp = g_post.reshape(1, d)
    return pl.pallas_call(
        functools.partial(_ffn_down_kernel, d=d),
        grid=(m // tm, k // tk),
        in_specs=[pl.BlockSpec((tm, tk), lambda i, kk: (i, kk)),
                  pl.BlockSpec((tk, d), lambda i, kk: (kk, 0)),
                  pl.BlockSpec(memory_space=pl.ANY), full(mod), full(gp)],
        out_specs=pl.BlockSpec((tm, d), lambda i, kk: (i, 0)),
        out_shape=jax.ShapeDtypeStruct((m, d), F32),
        scratch_shapes=[pltpu.VMEM((tm, d), F32), pltpu.SemaphoreType.DMA(())],
        compiler_params=_params(("arbitrary", "arbitrary"), VMEM_LIMIT),
        name="ffn_down",
    )(act, w, x1, mod, gp)


RING = 4


def _rwkv_feat_kernel(pm_hbm, pl_hbm, sel_ref, mu_ref, sell_ref, mul_ref, w0_ref, w2_ref, a0_ref, a2_ref, g2_ref,
                      kk_ref, ka_ref, rk_ref, hs_ref, hst_ref,
                      r_o, v_o, a_o, lw0_o, lw1_o, k0_o, k1_o, b0_o, b1_o, bonus_o, gate_o, ring_m, ring_l, sems,
                      *, ncb, nrows, dr, rd, ra):
    i = pl.program_id(0)
    nb = pl.num_programs(0)
    c = ring_m.shape[1]
    sources = ((pm_hbm, ring_m), (pl_hbm, ring_l))

    def fetch(blk):
        slot = blk % RING
        rows = pl.ds(pl.multiple_of(blk * c, c), c)
        return [pltpu.make_async_copy(src.at[rows, :], ring.at[slot], sems.at[n, slot])
                for n, (src, ring) in enumerate(sources)]

    def start(blk):
        for cp in fetch(blk):
            cp.start()

    def wait(blk):
        for cp in fetch(blk):
            cp.wait()

    @pl.when(i == 0)
    def _():
        start(0)
        start(1)
        start(2)
        wait(0)
        wait(1)

    @pl.when((i > 0) & (i + 1 < nb))
    def _():
        wait(i + 1)

    @pl.when((i > 0) & (i + 2 < nb))
    def _():
        start(i + 2)

    is_ctx = i < ncb
    xi = i - ncb
    s_cur, s_prv, s_nxt = i % RING, jnp.maximum(i - 1, 0) % RING, jnp.minimum(i + 1, nb - 1) % RING

    def shift_lerp(ring, sel, mu):
        cur, prv, nxt = ring[s_cur], ring[s_prv], ring[s_nxt]
        row = lax.broadcasted_iota(jnp.int32, cur.shape, 0)
        prev_last = jnp.where(is_ctx & (i > 0), prv[c - 1:c, :], 0.0)
        next_first = jnp.where(is_ctx & (i < ncb - 1), nxt[0:1, :], 0.0)
        before = jnp.where(row == 0, prev_last, pltpu.roll(cur, 1, 0))
        after = jnp.where(row == c - 1, next_first, pltpu.roll(cur, c - 1, 0))
        above = jnp.where(is_ctx, before, jnp.where(xi > 0, prv, 0.0))
        below = jnp.where(is_ctx, after, jnp.where(xi < nrows - 1, nxt, 0.0))
        shifted = jnp.where(sel == 0, before, jnp.where(sel == 1, after, jnp.where(sel == 2, above, below)))
        return cur + mu * (shifted - cur)

    m = shift_lerp(ring_m, sel_ref[...], mu_ref[...])
    ml = shift_lerp(ring_l, sell_ref[...], mul_ref[...])
    r, k, v = m[:, 0:dr], m[:, dr:2 * dr], m[:, 2 * dr:3 * dr]
    wl = (ml[:, 0:rd], ml[:, rd:2 * rd])
    al = (ml[:, 2 * rd:2 * rd + ra], ml[:, 2 * rd + ra:2 * rd + 2 * ra])
    gl = ml[:, 2 * rd + 2 * ra:]

    hs, hst = hs_ref[...], hst_ref[...]

    def headsum(z):
        return _head_sum(z, hs, hst)

    kkf = k * kk_ref[...]
    kk = kkf * lax.rsqrt(headsum(kkf * kkf) + 1e-12)
    r_o[...] = r
    v_o[...] = v
    a_o[...] = -kk
    for d, (lw_o, k_o, b_o) in enumerate(((lw0_o, k0_o, b0_o), (lw1_o, k1_o, b1_o))):
        wd = _mxu(jnp.tanh(wl[d]).astype(BF16), w2_ref[d]) + w0_ref[d:d + 1, :]
        lw_o[...] = -EXP_M05 * _sigmoid(wd)
        ad = _sigmoid(_mxu(al[d].astype(BF16), a2_ref[d]) + a0_ref[d:d + 1, :])
        k_o[...] = k * (1.0 + (ad - 1.0) * ka_ref[...])
        b_o[...] = kk * ad
    bonus_o[...] = headsum(r * k * rk_ref[...]) * v
    gate_o[...] = _mxu(_sigmoid(gl).astype(BF16), g2_ref[...])


def _rwkv_features(p_main, p_lora, sel, mu, sel_l, mu_l, w0, w2p, a0, a2p, g2p, k_k, k_a, r_k, lc, l):
    n = p_main.shape[0]
    dr = k_k.shape[0]
    rd, ra = w2p.shape[1], a2p.shape[1]
    nb, ncb = n // CHUNK, lc // CHUNK
    nheads = dr // RWKV_HEAD
    assert nheads <= LANE
    hs, hst = _head_indicators(dr)
    full = lambda a: pl.BlockSpec(a.shape, lambda i: (0,) * a.ndim)
    row1 = lambda a: a.reshape(1, -1)
    consts = [sel, mu, sel_l, mu_l, w0, w2p.astype(BF16), a0, a2p.astype(BF16), g2p.astype(BF16), row1(k_k),
              row1(k_a), row1(r_k), hs, hst]
    assert nb >= 3
    outs = pl.pallas_call(
        functools.partial(_rwkv_feat_kernel, ncb=ncb, nrows=l // CHUNK, dr=dr, rd=rd, ra=ra),
        grid=(nb,),
        in_specs=[pl.BlockSpec(memory_space=pl.ANY)] * 2 + [full(a) for a in consts],
        out_specs=[pl.BlockSpec((CHUNK, dr), lambda i: (i, 0))] * 11,
        out_shape=[jax.ShapeDtypeStruct((n, dr), F32)] * 11,
        scratch_shapes=[pltpu.VMEM((RING, CHUNK, p_main.shape[1]), F32), pltpu.VMEM((RING, CHUNK, p_lora.shape[1]), F32),
                        pltpu.SemaphoreType.DMA((2, RING))],
        compiler_params=_params(("arbitrary",), VMEM_LIMIT),
        name="rwkv_features",
    )(p_main, p_lora, *consts)
    return outs


def _block_diag_mask(n, w):
    r = lax.broadcasted_iota(jnp.int32, (n, n), 0)
    c = lax.broadcasted_iota(jnp.int32, (n, n), 1)
    return (r // w) == (c // w)


def _cumsum_rows(z, rev):
    c = z.shape[0]
    t = lax.broadcasted_iota(jnp.int32, (c, c), 0)
    s = lax.broadcasted_iota(jnp.int32, (c, c), 1)
    tri = ((s >= t) if rev else (s <= t)).astype(BF16)
    hi, mid, lo = _split3(z)
    return _mxu(tri, hi) + _mxu(tri, mid) + _mxu(tri, lo)


def _rwkv_chunks(chains):
    c, w = chains[0][0].shape
    nh = w // c
    every = range(len(chains))
    revs = [ch[7] for ch in chains]
    bd = _block_diag_mask(w, c)
    t_i = lax.broadcasted_iota(jnp.int32, (c, w), 0)
    s_i = lax.broadcasted_iota(jnp.int32, (c, w), 1) % c
    eye = (s_i == t_i).astype(F32)
    strict = {False: s_i < t_i, True: s_i > t_i}
    incl = {False: s_i <= t_i, True: s_i >= t_i}

    def expand(z):
        return jnp.where(bd, jnp.concatenate([z] * nh, axis=0), jnp.zeros((), z.dtype))

    def packed_mm(lhs, rhs):
        return _mxu(lhs.astype(BF16), expand(rhs.astype(BF16)))

    bc = [_cumsum_rows(ch[1], ch[7]) for ch in chains]
    ar, bk, v16 = [], [], []
    for i, (r, lw, k, v, a, b, _, _) in enumerate(chains):
        e_in, e_ex, e_ng = jnp.exp(bc[i]), jnp.exp(bc[i] - lw), jnp.exp(-bc[i])
        ar.append(jnp.concatenate([a * e_ex, r * e_in], axis=0).astype(BF16))
        bk.append(jnp.concatenate([b * e_ng, k * e_ng], axis=0).astype(BF16))
        v16.append(v.astype(BF16))
    gb = [_mxu_nt(ar[i], expand(bk[i][:c])) for i in every]
    gk = [_mxu_nt(ar[i], expand(bk[i][c:])) for i in every]
    l_ab = [jnp.where(strict[revs[i]], gb[i][:c], 0.0) for i in every]
    m_rb = [jnp.where(incl[revs[i]], gb[i][c:], 0.0) for i in every]
    lmk = [jnp.concatenate([jnp.where(strict[revs[i]], gk[i][:c], 0.0),
                            jnp.where(incl[revs[i]], gk[i][c:], 0.0)], axis=0) for i in every]
    t_m = [eye + l_ab[i] for i in every]
    p_m = [packed_mm(l_ab[i], l_ab[i]) for i in every]
    for _ in range(int(np.log2(c)) - 2):
        tp = [packed_mm(jnp.concatenate([t_m[i], p_m[i]], axis=0), p_m[i]) for i in every]
        t_m = [t_m[i] + tp[i][:c] for i in every]
        p_m = [tp[i][c:] for i in every]
    t_m = [t_m[i] + packed_mm(t_m[i], p_m[i]) for i in every]

    ars = [_mxu_nt(ar[i], chains[i][6].astype(BF16)) for i in every]
    lm = [_mxu(lmk[i].astype(BF16), expand(v16[i])) for i in every]
    u16 = [packed_mm(t_m[i], ars[i][:c] + lm[i][:c]).astype(BF16) for i in every]
    ys = [ars[i][c:] + _mxu(m_rb[i].astype(BF16), expand(u16[i])) + lm[i][c:] for i in every]
    ds = [_mxu_tn(jnp.concatenate([u16[i], v16[i]], axis=0), bk[i]) for i in every]
    s_new = []
    for i in every:
        last = bc[i][0:1] if revs[i] else bc[i][c - 1:c]
        s_new.append((chains[i][6] + jnp.where(bd, ds[i], 0.0)) * jnp.exp(last))
    return ys, s_new


def _cast_jobs(weights, ng, nb):
    in_specs, out_specs, shapes = [], [], []
    for w, col0, cols in weights:
        rows = w.shape[0]
        n = max(m for m in range(1, ng * nb + 1) if rows % m == 0 and (rows // m) % 16 == 0)
        slab = lambda g, c, n=n: jnp.minimum(g * nb + c, n - 1)
        in_specs.append(pl.BlockSpec((pl.Element(rows // n), pl.Element(cols)),
                                     lambda g, c, s=slab, h=rows // n, col0=col0: (s(g, c) * h, col0)))
        out_specs.append(pl.BlockSpec((rows // n, cols), lambda g, c, s=slab: (s(g, c), 0)))
        shapes.append(jax.ShapeDtypeStruct((rows, cols), PROJ_DTYPE))
    return in_specs, out_specs, shapes


def _run_cast_jobs(src_refs, dst_refs):
    for src, dst in zip(src_refs, dst_refs):
        dst[...] = src[...].astype(dst.dtype)


def _rwkv_scan_kernel(rf, lwf, kf, vf, af, bf, rr, lwr, kr, vr, ar, br, *rest):
    n_cast = (len(rest) - 3) // 2
    yf_o, yr_o = rest[n_cast:n_cast + 2]
    s_ref = rest[-1]
    _run_cast_jobs(rest[:n_cast], rest[n_cast + 2:-1])

    @pl.when(pl.program_id(1) == 0)
    def _():
        s_ref[...] = jnp.zeros_like(s_ref)

    ngs = rf.shape[1] // GROUP_W
    chains = []
    for d, refs in enumerate(((rf, lwf, kf, vf, af, bf), (rr, lwr, kr, vr, ar, br))):
        for g in range(ngs):
            sl = slice(g * GROUP_W, (g + 1) * GROUP_W)
            chains.append(tuple(ref[:, sl] for ref in refs) + (s_ref[d, g], d == 1))
    ys, s_new = _rwkv_chunks(chains)
    for d, o_ref in enumerate((yf_o, yr_o)):
        for g in range(ngs):
            o_ref[:, g * GROUP_W:(g + 1) * GROUP_W] = ys[d * ngs + g]
            s_ref[d, g] = s_new[d * ngs + g]


def _rev_chunk(c, ncb, nb):
    return jnp.where(c < ncb, ncb - 1 - c, nb - 1 - (c - ncb))


def _rwkv_scan(r, v, a, lw0, lw1, k0, k1, b0, b1, lc, cast_weights):
    n, dr = r.shape
    nb, ncb = n // CHUNK, lc // CHUNK
    bw = _pick(dr, RWKV_GROUPS_PER_STEP * GROUP_W, GROUP_W)
    fwd = pl.BlockSpec((CHUNK, bw), lambda g, c: (c, g))
    rev = pl.BlockSpec((CHUNK, bw), lambda g, c: (_rev_chunk(c, ncb, nb), g))
    cast_in, cast_out, cast_shapes = _cast_jobs(cast_weights, dr // bw, nb)
    return pl.pallas_call(
        _rwkv_scan_kernel,
        grid=(dr // bw, nb),
        in_specs=[fwd] * 6 + [rev] * 6 + cast_in,
        out_specs=[fwd, rev] + cast_out,
        out_shape=[jax.ShapeDtypeStruct((n, dr), F32)] * 2 + cast_shapes,
        scratch_shapes=[pltpu.VMEM((2, bw // GROUP_W, GROUP_W, GROUP_W), F32)],
        compiler_params=_params(("arbitrary", "arbitrary"), VMEM_LIMIT),
        name="rwkv_scan",
    )(r, lw0, k0, v, a, b0, r, lw1, k1, v, a, b1, *[w for w, _, _ in cast_weights])


def _hgrn_chunk(q, k, v, lf, s_vk, rev):
    c, w = q.shape
    nsb = c // SUB
    bc = _cumsum_rows(lf, rev)
    last = bc[0:1] if rev else bc[c - 1:c]
    bdh = _block_diag_mask(w, HGRN_HEAD)
    v16 = v.astype(BF16)
    o_inter = _mxu_nt((q * jnp.exp(bc)).astype(BF16), s_vk.astype(BF16))
    s_new = s_vk * jnp.exp(last) + jnp.where(bdh, _mxu_tn(v16, (k * jnp.exp(last - bc)).astype(BF16)), 0.0)

    lane2 = lax.broadcasted_iota(jnp.int32, (c, w), 1)
    row2 = lax.broadcasted_iota(jnp.int32, (c, w), 0)
    zpad = jnp.zeros((HGRN_HEAD - c, w), BF16)

    def expand(z):
        parts = []
        for h in range(w // HGRN_HEAD):
            parts += [jnp.where(lane2 // HGRN_HEAD == h, z, jnp.zeros((), BF16)), zpad]
        return jnp.concatenate(parts, axis=0)

    ones_bd = bdh.astype(BF16)
    t3 = lax.broadcasted_iota(jnp.int32, (SUB, SUB, w), 0)
    s3 = lax.broadcasted_iota(jnp.int32, (SUB, SUB, w), 1)
    l3 = lax.broadcasted_iota(jnp.int32, (SUB, SUB, w), 2) % HGRN_HEAD
    causal = (s3 >= t3) if rev else (s3 <= t3)
    a_rows = []
    for i in range(nsb):
        lo, hi = SUB * i, SUB * (i + 1)
        qi, ki, bi = q[lo:hi], k[lo:hi], bc[lo:hi]
        d3 = bi[:, None, :] - bi[None, :, :]
        x3 = jnp.where(causal, (qi[:, None, :] * ki[None, :, :]) * jnp.exp(jnp.minimum(d3, 0.0)), 0.0)
        r3 = _mxu(x3.reshape(SUB * SUB, w).astype(BF16), ones_bd).reshape(SUB, SUB, w)
        a_i = jnp.sum(jnp.where(l3 == s3 + lo, r3, 0.0), axis=1)
        if (not rev and i > 0) or (rev and i < nsb - 1):
            beta = bc[hi:hi + 1] if rev else bc[lo - 1:lo]
            earlier = (row2 >= hi) if rev else (row2 < lo)
            kp = jnp.where(earlier, k * jnp.exp(jnp.minimum(beta - bc, 0.0)), 0.0)
            a_i = a_i + _mxu_nt((qi * jnp.exp(bi - beta)).astype(BF16), expand(kp.astype(BF16)))
        a_rows.append(a_i)
    o = o_inter + _mxu(jnp.concatenate(a_rows, axis=0).astype(BF16), expand(v16))
    return o, s_new


def _hgrn_chunks_bounded(chains):
    c, w = chains[0][0].shape
    every = range(len(chains))
    revs = [ch[5] for ch in chains]
    bdh = _block_diag_mask(w, HGRN_HEAD)
    lane2 = lax.broadcasted_iota(jnp.int32, (c, w), 1)
    t_i = lax.broadcasted_iota(jnp.int32, (c, w), 0)
    s_i = lane2 % HGRN_HEAD
    incl = {False: s_i <= t_i, True: s_i >= t_i}
    zpad = jnp.zeros((HGRN_HEAD - c, w), BF16)

    def expand(z):
        parts = []
        for h in range(w // HGRN_HEAD):
            parts += [jnp.where(lane2 // HGRN_HEAD == h, z, jnp.zeros((), BF16)), zpad]
        return jnp.concatenate(parts, axis=0)

    bc = [_cumsum_rows(ch[3], ch[5]) for ch in chains]
    last = [bc[i][0:1] if revs[i] else bc[i][c - 1:c] for i in every]
    qt = [(chains[i][0] * jnp.exp(bc[i])).astype(BF16) for i in every]
    kt = [(chains[i][1] * jnp.exp(-bc[i])).astype(BF16) for i in every]
    kd = [(chains[i][1] * jnp.exp(last[i] - bc[i])).astype(BF16) for i in every]
    v16 = [chains[i][2].astype(BF16) for i in every]
    o_inter = [_mxu_nt(qt[i], chains[i][4].astype(BF16)) for i in every]
    a = [jnp.where(incl[revs[i]], _mxu_nt(qt[i], expand(kt[i])), 0.0).astype(BF16) for i in every]
    o = [o_inter[i] + _mxu(a[i], expand(v16[i])) for i in every]
    ds = [_mxu_tn(v16[i], kd[i]) for i in every]
    s_new = [chains[i][4] * jnp.exp(last[i]) + jnp.where(bdh, ds[i], 0.0) for i in every]
    return o, s_new


def _hgrn_scan_kernel(qf, ff, vf, qr, fr, vr, lg_ref, *rest, layer):
    n_cast = (len(rest) - 3) // 2
    of_o, or_o = rest[n_cast:n_cast + 2]
    s_ref = rest[-1]
    _run_cast_jobs(rest[:n_cast], rest[n_cast + 2:-1])

    @pl.when(pl.program_id(1) == 0)
    def _():
        s_ref[...] = jnp.zeros_like(s_ref)

    lg = lg_ref[...]
    e = jnp.exp(lg - jnp.max(lg, axis=0, keepdims=True))
    lb = jnp.sum(e[:layer + 1], axis=0) / jnp.sum(e, axis=0)
    ngs = qf.shape[1] // GROUP_W
    chains = []
    min_total = None
    for d, (q_ref, f_ref, v_ref) in enumerate(((qf, ff, vf), (qr, fr, vr))):
        fd = lb[d:d + 1, :] + (1.0 - lb[d:d + 1, :]) * _sigmoid(f_ref[...])
        qv = q_ref[...]
        qh, kh, vh, lf = qv * _sigmoid(qv), 1.0 - fd, v_ref[...], jnp.log(fd)
        total = jnp.min(jnp.sum(lf, axis=0, keepdims=True))
        min_total = total if min_total is None else jnp.minimum(min_total, total)
        for g in range(ngs):
            sl = slice(g * GROUP_W, (g + 1) * GROUP_W)
            chains.append((qh[:, sl], kh[:, sl], vh[:, sl], lf[:, sl], s_ref[d, g], d == 1))

    def emit(o, s_new):
        for d, o_ref in enumerate((of_o, or_o)):
            for g in range(ngs):
                o_ref[:, g * GROUP_W:(g + 1) * GROUP_W] = o[d * ngs + g]
                s_ref[d, g] = s_new[d * ngs + g]

    bounded = min_total >= -MAX_CHUNK_LOG_DECAY

    @pl.when(bounded)
    def _():
        emit(*_hgrn_chunks_bounded(chains))

    @pl.when(jnp.logical_not(bounded))
    def _():
        res = [_hgrn_chunk(*ch) for ch in chains]
        emit([r[0] for r in res], [r[1] for r in res])


def _hgrn_scan(p_h, lb_logits, lc, layer, cast_weights):
    n = p_h.shape[0]
    dh = lb_logits.shape[2]
    nb, ncb = n // CHUNK, lc // CHUNK
    bw = _pick(dh, HGRN_GROUPS_PER_STEP * GROUP_W, GROUP_W)
    ng = dh // bw
    blk = (CHUNK, bw)
    fwd = lambda sec: pl.BlockSpec(blk, lambda g, c: (c, sec * ng + g))
    rev = lambda sec: pl.BlockSpec(blk, lambda g, c: (_rev_chunk(c, ncb, nb), sec * ng + g))
    cast_in, cast_out, cast_shapes = _cast_jobs(cast_weights, ng, nb)
    return pl.pallas_call(
        functools.partial(_hgrn_scan_kernel, layer=layer),
        grid=(ng, nb),
        in_specs=[fwd(0), fwd(1), fwd(3), rev(0), rev(2), rev(3),
                  pl.BlockSpec((lb_logits.shape[0], 2, bw), lambda g, c: (0, 0, g))] + cast_in,
        out_specs=[pl.BlockSpec(blk, lambda g, c: (c, g)),
                   pl.BlockSpec(blk, lambda g, c: (_rev_chunk(c, ncb, nb), g))] + cast_out,
        out_shape=[jax.ShapeDtypeStruct((n, dh), F32)] * 2 + cast_shapes,
        scratch_shapes=[pltpu.VMEM((2, bw // GROUP_W, GROUP_W, GROUP_W), F32)],
        compiler_params=_params(("arbitrary", "arbitrary"), VMEM_LIMIT),
        name="hgrn_scan",
    )(p_h, p_h, p_h, p_h, p_h, p_h, lb_logits, *[w for w, _, _ in cast_weights])


def _mix_out_kernel(yf, yr, bonus, gate, of, orv, g_ref, lng, lnb, hng, hs_ref, hst_ref, u_o, *, dr, dh):
    hs, hst = hs_ref[...], hst_ref[...]

    def headmean(z):
        return _head_sum(z, hs, hst) * (1.0 / RWKV_HEAD)

    y = yf[...] + yr[...]
    yc = y - headmean(y)
    yn = yc * lax.rsqrt(headmean(yc * yc) + GN_EPS) * lng[...] + lnb[...]
    u_o[:, 0:dr] = ((yn + bonus[...]) * gate[...]).astype(u_o.dtype)
    o = of[...] + orv[...]
    g = g_ref[...]
    sg = g * _sigmoid(g)
    for h in range(dh // HGRN_HEAD):
        sl = slice(h * HGRN_HEAD, (h + 1) * HGRN_HEAD)
        oh = o[:, sl]
        on = oh * lax.rsqrt(jnp.mean(oh * oh, axis=-1, keepdims=True) + NORM_EPS) * hng[...]
        u_o[:, dr + h * HGRN_HEAD:dr + (h + 1) * HGRN_HEAD] = (on * sg[:, sl]).astype(u_o.dtype)


def _mix_out(yf, yr, bonus, gate, of, orv, p_h, ln_g, ln_b, hg, lc, l):
    dr, dh = yf.shape[1], of.shape[1]
    tb = _pick(int(np.gcd(lc, l)), 128, 8)
    off = lc // tb
    hs, hst = _head_indicators(dr)
    rows = lambda wd: pl.BlockSpec((tb, wd), lambda i: (i + off, 0))
    full = lambda a: pl.BlockSpec(a.shape, lambda i: (0,) * a.ndim)
    consts = [ln_g.reshape(1, dr), ln_b.reshape(1, dr), hg.reshape(1, HGRN_HEAD), hs, hst]
    return pl.pallas_call(
        functools.partial(_mix_out_kernel, dr=dr, dh=dh),
        grid=(l // tb,),
        in_specs=[rows(dr)] * 4 + [rows(dh)] * 2 + [pl.BlockSpec((tb, dh), lambda i: (i + off, 4))]
                 + [full(a) for a in consts],
        out_specs=pl.BlockSpec((tb, dr + dh), lambda i: (i, 0)),
        out_shape=jax.ShapeDtypeStruct((l, dr + dh), PROJ_DTYPE),
        compiler_params=_params(("arbitrary",), VMEM_LIMIT),
        name="mix_out",
    )(yf, yr, bonus, gate, of, orv, p_h, *consts)


def _res1_kernel(ux_ref, x_ref, mod_ref, gpost_ref, gpre_ref, x1_o, h2_o, *, d):
    ux = ux_ref[...]
    nrm = ux * lax.rsqrt(jnp.mean(ux * ux, axis=-1, keepdims=True) + NORM_EPS) * gpost_ref[...]
    x1 = x_ref[...] + mod_ref[0:1, 2 * d:3 * d] * nrm
    x1_o[...] = x1
    hn = x1 * lax.rsqrt(jnp.mean(x1 * x1, axis=-1, keepdims=True) + NORM_EPS) * gpre_ref[...]
    h2_o[...] = (hn * (1.0 + mod_ref[0:1, 4 * d:5 * d]) + mod_ref[0:1, 3 * d:4 * d]).astype(h2_o.dtype)


def _res1(ux, x2, mod, g_post, g_pre):
    l, d = x2.shape
    tb = _pick(l, 256, 8)
    rows = pl.BlockSpec((tb, d), lambda i: (i, 0))
    full = lambda a: pl.BlockSpec(a.shape, lambda i: (0,) * a.ndim)
    gp, gq = g_post.reshape(1, d), g_pre.reshape(1, d)
    return pl.pallas_call(
        functools.partial(_res1_kernel, d=d),
        grid=(l // tb,),
        in_specs=[rows, rows, full(mod), full(gp), full(gq)],
        out_specs=[rows, rows],
        out_shape=[jax.ShapeDtypeStruct((l, d), F32), jax.ShapeDtypeStruct((l, d), PROJ_DTYPE)],
        compiler_params=_params(("arbitrary",), VMEM_LIMIT),
        name="residual_mix",
    )(ux, x2, mod, gp, gq)


def _pad_cols(a, width):
    return jnp.pad(a, ((0, 0), (0, width - a.shape[1])))


def _pad_rows(a, height):
    return jnp.pad(a, ((0, height - a.shape[0]), (0, 0)))


def _round_up(n, m):
    return (n + m - 1) // m * m


def kernel(x, c, ctx, c_ctx, w_ada, b_ada, g_mix_pre, g_mix_post, g_ffn_pre, g_ffn_post, w_in, mu_shift, w0, w2, a0,
           a2, g2, k_k, k_a, r_k, ln_x_g, ln_x_b, hgrn_lb_logits, hgrn_norm_g, w_out, w_ff1, w_ff2):
    assert x.shape[0] == 1 and w_in.shape[0] == 1, "single batch, single layer"
    layer = 0
    x2, ctx2 = x[0], ctx[0]
    l, d = x2.shape
    lc = ctx2.shape[0]
    dr, dh = k_k.shape[1], hgrn_lb_logits.shape[2]
    rd_raw, ra_raw, rg_raw = w2.shape[2], a2.shape[2], g2.shape[1]
    rd, ra, rg = (_round_up(v, LANE) for v in (rd_raw, ra_raw, rg_raw))
    assert l % GRID_W == 0 and lc % CHUNK == 0 and dr % GROUP_W == 0 and dh % GROUP_W == 0

    cvec = jnp.concatenate([c, c_ctx[None, :], jnp.zeros((6, d), F32)], axis=0)
    mod = _ada(cvec, w_ada[layer], b_ada[layer])

    wi = w_in[layer]
    mu = mu_shift[layer][None, :]
    sel = (jnp.arange(mu.shape[1], dtype=jnp.int32) % 4)[None, :]
    o = 3 * dr
    cuts = []
    for raw, padded in ((rd_raw, rd), (rd_raw, rd), (ra_raw, ra), (ra_raw, ra), (rg_raw, rg)):
        cuts.append((o, o + raw, padded))
        o += raw
    rw_cols = o
    regroup = lambda a: jnp.concatenate([_pad_cols(a[:, s:e], wd) for s, e, wd in cuts], axis=1)
    w_lora = regroup(wi)
    w2p = jnp.stack([_pad_rows(w2[layer, dd], rd) for dd in range(2)])
    a2p = jnp.stack([_pad_rows(a2[layer, dd], ra) for dd in range(2)])
    g2p = _pad_rows(g2[layer], rg)

    h = _prep(ctx2, x2, g_mix_pre[layer], mod)
    p_main = _mm_wcast(h, wi, 3 * dr, F32, tm_target=768, tn_target=768, name="in_proj_rkv")
    p_lora = _mm_wcast(h, w_lora, w_lora.shape[1], F32, tm_target=768, name="in_proj_lora")

    (r, v, a, lw0, lw1, k0, k1, b0, b1, bonus, gate) = _rwkv_features(
        p_main, p_lora, sel[:, :3 * dr], mu[:, :3 * dr], regroup(sel), regroup(mu), w0[layer], w2p, a0[layer], a2p,
        g2p, k_k[layer], k_a[layer], r_k[layer].reshape(-1), lc, l)
    whole = lambda w: (w, 0, w.shape[1])
    yf, yr, w_h, w_up = _rwkv_scan(r, v, a, lw0, lw1, k0, k1, b0, b1, lc,
                                   [(wi, rw_cols, wi.shape[1] - rw_cols), whole(w_ff1[layer])])
    p_h = _mm(h, w_h, F32, tm_target=768, name="in_proj_hgrn")
    of, orv, w_down, w_o = _hgrn_scan(p_h, hgrn_lb_logits, lc, layer, [whole(w_ff2[layer]), whole(w_out[layer])])

    u = _mix_out(yf, yr, bonus, gate, of, orv, p_h, ln_x_g[layer], ln_x_b[layer], hgrn_norm_g[layer], lc, l)
    ux = _mm(u, w_o, F32, name="out_proj")
    x1, h2 = _res1(ux, x2, mod, g_mix_post[layer], g_ffn_pre[layer])
    act = _mm(h2, w_up, PROJ_DTYPE, act="relu2", name="ffn_up")
    out = _ffn_down_residual(act, w_down, x1, mod, g_ffn_post[layer])
    return out[None]
```

```python
import functools

import jax
import jax.numpy as jnp
import numpy as np
from jax import lax
from jax.experimental import pallas as pl
from jax.experimental.pallas import tpu as pltpu

F32 = jnp.float32
BF16 = jnp.bfloat16
PROJ_DTYPE = jnp.bfloat16

LANE = 128
GRID_W = 64
CHUNK = GRID_W
RWKV_HEAD = 64
HGRN_HEAD = 128
SUB = 16
GROUP_W = 256
RWKV_GROUPS_PER_STEP = 8
HGRN_GROUPS_PER_STEP = 8
MAX_CHUNK_LOG_DECAY = 60.0
NORM_EPS = 1e-6
GN_EPS = 64e-5
EXP_M05 = float(np.exp(-0.5))
VMEM_LIMIT = 56 * 1024 * 1024


def _pick(n, target, unit=LANE):
    best = None
    for m in range(unit, min(n, target) + 1, unit):
        if n % m == 0:
            best = m
    return best if best is not None else n


def _sigmoid(z):
    return 0.5 * jnp.tanh(0.5 * z) + 0.5


def _mxu(a, b):
    return jnp.dot(a, b, preferred_element_type=F32)


def _mxu_nt(a, b):
    return lax.dot_general(a, b, (((1,), (1,)), ((), ())), preferred_element_type=F32)


def _mxu_tn(a, b):
    return lax.dot_general(a, b, (((0,), (0,)), ((), ())), preferred_element_type=F32)


def _split3(z):
    hi = z.astype(BF16)
    rest = z - hi.astype(F32)
    mid = rest.astype(BF16)
    return hi, mid, (rest - mid.astype(F32)).astype(BF16)


def _select_sum(z, sel):
    hi, mid, _ = _split3(z)
    return _mxu(hi, sel) + _mxu(mid, sel)


def _head_sum(z, hs, hst):
    return _select_sum(_select_sum(z, hs), hst)


def _head_indicators(dr):
    hs = np.arange(dr)[:, None] // RWKV_HEAD == np.arange(LANE)[None, :]
    return jnp.asarray(hs, BF16), jnp.asarray(hs.T, BF16)


def _params(sem, vmem=None):
    return pltpu.CompilerParams(dimension_semantics=sem, vmem_limit_bytes=vmem)


def _ada_kernel(c_ref, w_ref, b_ref, o_ref):
    cv = c_ref[...]
    rows = cv.shape[0]
    hi, mid, lo = (t.astype(F32) for t in _split3(cv * _sigmoid(cv)))
    s3 = jnp.concatenate([hi, mid, lo], axis=0).astype(BF16)
    w = w_ref[...]
    w_hi = w.astype(BF16)
    w_lo = (w - w_hi.astype(F32)).astype(BF16)
    p = _mxu(s3, w_hi)
    q = _mxu(s3[:2 * rows], w_lo)
    o_ref[...] = (p[:rows] + p[rows:2 * rows] + p[2 * rows:] + q[:rows] + q[rows:]) + b_ref[...]


def _ada(cvec, w, b):
    rows, d = cvec.shape
    n = w.shape[1]
    tn = _pick(n, 512)
    return pl.pallas_call(
        _ada_kernel,
        grid=(n // tn,),
        in_specs=[pl.BlockSpec((rows, d), lambda j: (0, 0)),
                  pl.BlockSpec((d, tn), lambda j: (0, j)),
                  pl.BlockSpec((1, tn), lambda j: (0, j))],
        out_specs=pl.BlockSpec((rows, tn), lambda j: (0, j)),
        out_shape=jax.ShapeDtypeStruct((rows, n), F32),
        compiler_params=_params(("arbitrary",), VMEM_LIMIT),
        name="ada_mod",
    )(cvec, w, b.reshape(1, n))


def _prep_kernel(ctx_ref, x_ref, g_ref, mod_ref, o_ref, *, ncb, d):
    is_ctx = pl.program_id(0) < ncb
    rows = jnp.where(is_ctx, ctx_ref[...], x_ref[...])
    ms = jnp.mean(rows * rows, axis=-1, keepdims=True)
    hn = rows * lax.rsqrt(ms + NORM_EPS) * g_ref[...]
    shift = jnp.where(is_ctx, mod_ref[1:2, 0:d], mod_ref[0:1, 0:d])
    scale = jnp.where(is_ctx, mod_ref[1:2, d:2 * d], mod_ref[0:1, d:2 * d])
    o_ref[...] = (hn * (1.0 + scale) + shift).astype(o_ref.dtype)


def _prep(ctx2, x2, g, mod):
    lc, d = ctx2.shape
    l = x2.shape[0]
    tb = _pick(int(np.gcd(lc, l)), 256, 8)
    ncb = lc // tb
    return pl.pallas_call(
        functools.partial(_prep_kernel, ncb=ncb, d=d),
        grid=((lc + l) // tb,),
        in_specs=[pl.BlockSpec((tb, d), lambda i: (jnp.minimum(i, ncb - 1), 0)),
                  pl.BlockSpec((tb, d), lambda i: (jnp.maximum(i - ncb, 0), 0)),
                  pl.BlockSpec((1, d), lambda i: (0, 0)),
                  pl.BlockSpec(mod.shape, lambda i: (0, 0))],
        out_specs=pl.BlockSpec((tb, d), lambda i: (i, 0)),
        out_shape=jax.ShapeDtypeStruct((lc + l, d), PROJ_DTYPE),
        compiler_params=_params(("arbitrary",)),
        name="norm_modulate",
    )(ctx2, x2, g.reshape(1, d), mod)


def _mm_kernel(x_ref, w_ref, o_ref, *, act):
    acc = jnp.dot(x_ref[...], w_ref[...], preferred_element_type=F32)
    if act == "relu2":
        acc = jnp.square(jnp.maximum(acc, 0.0))
    o_ref[...] = acc.astype(o_ref.dtype)


def _mm(x, w, out_dtype, act=None, tm_target=1024, tn_target=1024, name="matmul"):
    m, k = x.shape
    n = w.shape[1]
    tm = _pick(m, tm_target)
    tn = _pick(n, tn_target)
    return pl.pallas_call(
        functools.partial(_mm_kernel, act=act),
        grid=(m // tm, n // tn),
        in_specs=[pl.BlockSpec((tm, k), lambda i, j: (i, 0)),
                  pl.BlockSpec((k, tn), lambda i, j: (0, j))],
        out_specs=pl.BlockSpec((tm, tn), lambda i, j: (i, j)),
        out_shape=jax.ShapeDtypeStruct((m, n), out_dtype),
        compiler_params=_params(("arbitrary", "arbitrary"), VMEM_LIMIT),
        name=name,
    )(x, w)


def _mm_wcast_kernel(x_ref, w_ref, o_ref, w_cast):
    @pl.when(pl.program_id(1) == 0)
    def _():
        w_cast[...] = w_ref[...].astype(w_cast.dtype)

    o_ref[...] = jnp.dot(x_ref[...], w_cast[...], preferred_element_type=F32).astype(o_ref.dtype)


def _mm_wcast(x, w, ncols, out_dtype, tm_target=1024, tn_target=1024, name="matmul_wcast"):
    m, k = x.shape
    tm, tn = _pick(m, tm_target), _pick(ncols, tn_target)
    return pl.pallas_call(
        _mm_wcast_kernel,
        grid=(ncols // tn, m // tm),
        in_specs=[pl.BlockSpec((tm, k), lambda j, i: (i, 0)),
                  pl.BlockSpec((k, tn), lambda j, i: (0, j))],
        out_specs=pl.BlockSpec((tm, tn), lambda j, i: (i, j)),
        out_shape=jax.ShapeDtypeStruct((m, ncols), out_dtype),
        scratch_shapes=[pltpu.VMEM((k, tn), PROJ_DTYPE)],
        compiler_params=_params(("arbitrary", "arbitrary"), VMEM_LIMIT),
        name=name,
    )(x, w)


def _res_proj_kernel(a_ref, w_ref, res_hbm, mod_ref, g_ref, *rest, d, gate_chunk, next_chunks):
    if next_chunks is None:
        o_ref, res_buf, sem = rest
    else:
        gnext_ref, o_ref, h_ref, res_buf, sem = rest
    i, kk = pl.program_id(0), pl.program_id(1)
    tm = res_buf.shape[0]
    res_copy = pltpu.make_async_copy(res_hbm.at[pl.ds(pl.multiple_of(i * tm, tm), tm), :], res_buf, sem)
    chunk = lambda n: mod_ref[0:1, n * d:(n + 1) * d]

    @pl.when(kk == 0)
    def _():
        res_copy.start()
        o_ref[...] = jnp.zeros_like(o_ref)

    o_ref[...] += jnp.dot(a_ref[...], w_ref[...], preferred_element_type=F32)

    @pl.when(kk == pl.num_programs(1) - 1)
    def _():
        res_copy.wait()
        gate = chunk(gate_chunk) * g_ref[...]
        rb = _pick(tm, 128, 8)
        for r0 in range(0, tm, rb):
            rows = slice(r0, r0 + rb)
            acc = o_ref[rows, :]
            scale = lax.rsqrt(jnp.mean(acc * acc, axis=-1, keepdims=True) + NORM_EPS)
            new = res_buf[rows, :] + gate * (acc * scale)
            o_ref[rows, :] = new
            if next_chunks is not None:
                hn = new * lax.rsqrt(jnp.mean(new * new, axis=-1, keepdims=True) + NORM_EPS) * gnext_ref[...]
                h_ref[rows, :] = (hn * (1.0 + chunk(next_chunks[1])) + chunk(next_chunks[0])).astype(h_ref.dtype)


def _residual_projection(a, w, res, mod, g_post, gate_chunk, g_next=None, next_chunks=None, tm_target=512,
                         tk_target=1024, name="residual_projection"):
    m, k = a.shape
    d = w.shape[1]
    tm, tk = _pick(m, tm_target, 8), _pick(k, tk_target)
    full = lambda z: pl.BlockSpec(z.shape, lambda i, kk: (0,) * z.ndim)
    rows = pl.BlockSpec((tm, d), lambda i, kk: (i, 0))
    consts = [mod, g_post.reshape(1, d)] + ([] if g_next is None else [g_next.reshape(1, d)])
    out_sds = jax.ShapeDtypeStruct((m, d), F32)
    return pl.pallas_call(
        functools.partial(_res_proj_kernel, d=d, gate_chunk=gate_chunk, next_chunks=next_chunks),
        grid=(m // tm, k // tk),
        in_specs=[pl.BlockSpec((tm, tk), lambda i, kk: (i, kk)),
                  pl.BlockSpec((tk, d), lambda i, kk: (kk, 0)),
                  pl.BlockSpec(memory_space=pl.ANY)] + [full(z) for z in consts],
        out_specs=rows if g_next is None else [rows, rows],
        out_shape=out_sds if g_next is None else [out_sds, jax.ShapeDtypeStruct((m, d), PROJ_DTYPE)],
        scratch_shapes=[pltpu.VMEM((tm, d), F32), pltpu.SemaphoreType.DMA(())],
        compiler_params=_params(("arbitrary", "arbitrary"), VMEM_LIMIT),
        name=name,
    )(a, w, res, *consts)


RING = 4


def _rwkv_feat_kernel(pm_hbm, pl_hbm, sel_ref, mu_ref, sell_ref, mul_ref, w0_ref, w2_ref, a0_ref, a2_ref, g2_ref,
                      kk_ref, ka_ref, rk_ref, hs_ref, hst_ref,
                      r_o, v_o, a_o, lw0_o, lw1_o, k0_o, k1_o, b0_o, b1_o, bonus_o, gate_o, ring_m, ring_l, sems,
                      *, ncb, nrows, dr, rd, ra):
    i = pl.program_id(0)
    nb = pl.num_programs(0)
    c = ring_m.shape[1]
    sources = ((pm_hbm, ring_m), (pl_hbm, ring_l))

    def fetch(blk):
        slot = blk % RING
        rows = pl.ds(pl.multiple_of(blk * c, c), c)
        return [pltpu.make_async_copy(src.at[rows, :], ring.at[slot], sems.at[n, slot])
                for n, (src, ring) in enumerate(sources)]

    def start(blk):
        for cp in fetch(blk):
            cp.start()

    def wait(blk):
        for cp in fetch(blk):
            cp.wait()

    @pl.when(i == 0)
    def _():
        start(0)
        start(1)
        start(2)
        wait(0)
        wait(1)

    @pl.when((i > 0) & (i + 1 < nb))
    def _():
        wait(i + 1)

    @pl.when((i > 0) & (i + 2 < nb))
    def _():
        start(i + 2)

    is_ctx = i < ncb
    xi = i - ncb
    s_cur, s_prv, s_nxt = i % RING, jnp.maximum(i - 1, 0) % RING, jnp.minimum(i + 1, nb - 1) % RING

    def shift_lerp(ring, sel, mu):
        cur, prv, nxt = ring[s_cur], ring[s_prv], ring[s_nxt]
        row = lax.broadcasted_iota(jnp.int32, cur.shape, 0)
        prev_last = jnp.where(is_ctx & (i > 0), prv[c - 1:c, :], 0.0)
        next_first = jnp.where(is_ctx & (i < ncb - 1), nxt[0:1, :], 0.0)
        before = jnp.where(row == 0, prev_last, pltpu.roll(cur, 1, 0))
        after = jnp.where(row == c - 1, next_first, pltpu.roll(cur, c - 1, 0))
        above = jnp.where(is_ctx, before, jnp.where(xi > 0, prv, 0.0))
        below = jnp.where(is_ctx, after, jnp.where(xi < nrows - 1, nxt, 0.0))
        shifted = jnp.where(sel == 0, before, jnp.where(sel == 1, after, jnp.where(sel == 2, above, below)))
        return cur + mu * (shifted - cur)

    m = shift_lerp(ring_m, sel_ref[...], mu_ref[...])
    ml = shift_lerp(ring_l, sell_ref[...], mul_ref[...])
    r, k, v = m[:, 0:dr], m[:, dr:2 * dr], m[:, 2 * dr:3 * dr]
    wl = (ml[:, 0:rd], ml[:, rd:2 * rd])
    al = (ml[:, 2 * rd:2 * rd + ra], ml[:, 2 * rd + ra:2 * rd + 2 * ra])
    gl = ml[:, 2 * rd + 2 * ra:]

    hs, hst = hs_ref[...], hst_ref[...]

    def headsum(z):
        return _head_sum(z, hs, hst)

    kkf = k * kk_ref[...]
    kk = kkf * lax.rsqrt(headsum(kkf * kkf) + 1e-12)
    r_o[...] = r
    v_o[...] = v
    a_o[...] = -kk
    for d, (lw_o, k_o, b_o) in enumerate(((lw0_o, k0_o, b0_o), (lw1_o, k1_o, b1_o))):
        wd = _mxu(jnp.tanh(wl[d]).astype(BF16), w2_ref[d]) + w0_ref[d:d + 1, :]
        lw_o[...] = -EXP_M05 * _sigmoid(wd)
        ad = _sigmoid(_mxu(al[d].astype(BF16), a2_ref[d]) + a0_ref[d:d + 1, :])
        k_o[...] = k * (1.0 + (ad - 1.0) * ka_ref[...])
        b_o[...] = kk * ad
    bonus_o[...] = headsum(r * k * rk_ref[...]) * v
    gate_o[...] = _mxu(_sigmoid(gl).astype(BF16), g2_ref[...])


def _rwkv_features(p_main, p_lora, sel, mu, sel_l, mu_l, w0, w2p, a0, a2p, g2p, k_k, k_a, r_k, lc, l):
    n = p_main.shape[0]
    dr = k_k.shape[0]
    rd, ra = w2p.shape[1], a2p.shape[1]
    nb, ncb = n // CHUNK, lc // CHUNK
    nheads = dr // RWKV_HEAD
    assert nheads <= LANE
    hs, hst = _head_indicators(dr)
    full = lambda a: pl.BlockSpec(a.shape, lambda i: (0,) * a.ndim)
    row1 = lambda a: a.reshape(1, -1)
    consts = [sel, mu, sel_l, mu_l, w0, w2p.astype(BF16), a0, a2p.astype(BF16), g2p.astype(BF16), row1(k_k),
              row1(k_a), row1(r_k), hs, hst]
    assert nb >= 3
    outs = pl.pallas_call(
        functools.partial(_rwkv_feat_kernel, ncb=ncb, nrows=l // CHUNK, dr=dr, rd=rd, ra=ra),
        grid=(nb,),
        in_specs=[pl.BlockSpec(memory_space=pl.ANY)] * 2 + [full(a) for a in consts],
        out_specs=[pl.BlockSpec((CHUNK, dr), lambda i: (i, 0))] * 11,
        out_shape=[jax.ShapeDtypeStruct((n, dr), F32)] * 11,
        scratch_shapes=[pltpu.VMEM((RING, CHUNK, p_main.shape[1]), F32), pltpu.VMEM((RING, CHUNK, p_lora.shape[1]), F32),
                        pltpu.SemaphoreType.DMA((2, RING))],
        compiler_params=_params(("arbitrary",), VMEM_LIMIT),
        name="rwkv_features",
    )(p_main, p_lora, *consts)
    return outs


def _block_diag_mask(n, w):
    r = lax.broadcasted_iota(jnp.int32, (n, n), 0)
    c = lax.broadcasted_iota(jnp.int32, (n, n), 1)
    return (r // w) == (c // w)


def _cumsum_rows(z, rev):
    c = z.shape[0]
    t = lax.broadcasted_iota(jnp.int32, (c, c), 0)
    s = lax.broadcasted_iota(jnp.int32, (c, c), 1)
    tri = ((s >= t) if rev else (s <= t)).astype(BF16)
    hi, mid, lo = _split3(z)
    return _mxu(tri, hi) + _mxu(tri, mid) + _mxu(tri, lo)


def _rwkv_chunks(chains):
    c, w = chains[0][0].shape
    nh = w // c
    every = range(len(chains))
    revs = [ch[7] for ch in chains]
    bd = _block_diag_mask(w, c)
    t_i = lax.broadcasted_iota(jnp.int32, (c, w), 0)
    s_i = lax.broadcasted_iota(jnp.int32, (c, w), 1) % c
    eye = (s_i == t_i).astype(F32)
    strict = {False: s_i < t_i, True: s_i > t_i}
    incl = {False: s_i <= t_i, True: s_i >= t_i}

    def expand(z):
        return jnp.where(bd, jnp.concatenate([z] * nh, axis=0), jnp.zeros((), z.dtype))

    def packed_mm(lhs, rhs):
        return _mxu(lhs.astype(BF16), expand(rhs.astype(BF16)))

    bc = [_cumsum_rows(ch[1], ch[7]) for ch in chains]
    ar, bk, v16 = [], [], []
    for i, (r, lw, k, v, a, b, _, _) in enumerate(chains):
        e_in, e_ex, e_ng = jnp.exp(bc[i]), jnp.exp(bc[i] - lw), jnp.exp(-bc[i])
        ar.append(jnp.concatenate([a * e_ex, r * e_in], axis=0).astype(BF16))
        bk.append(jnp.concatenate([b * e_ng, k * e_ng], axis=0).astype(BF16))
        v16.append(v.astype(BF16))
    gb = [_mxu_nt(ar[i], expand(bk[i][:c])) for i in every]
    gk = [_mxu_nt(ar[i], expand(bk[i][c:])) for i in every]
    l_ab = [jnp.where(strict[revs[i]], gb[i][:c], 0.0) for i in every]
    m_rb = [jnp.where(incl[revs[i]], gb[i][c:], 0.0) for i in every]
    lmk = [jnp.concatenate([jnp.where(strict[revs[i]], gk[i][:c], 0.0),
                            jnp.where(incl[revs[i]], gk[i][c:], 0.0)], axis=0) for i in every]
    t_m = [eye + l_ab[i] for i in every]
    p_m = [packed_mm(l_ab[i], l_ab[i]) for i in every]
    for _ in range(int(np.log2(c)) - 2):
        tp = [packed_mm(jnp.concatenate([t_m[i], p_m[i]], axis=0), p_m[i]) for i in every]
        t_m = [t_m[i] + tp[i][:c] for i in every]
        p_m = [tp[i][c:] for i in every]
    t_m = [t_m[i] + packed_mm(t_m[i], p_m[i]) for i in every]

    ars = [_mxu_nt(ar[i], chains[i][6].astype(BF16)) for i in every]
    lm = [_mxu(lmk[i].astype(BF16), expand(v16[i])) for i in every]
    u16 = [packed_mm(t_m[i], ars[i][:c] + lm[i][:c]).astype(BF16) for i in every]
    ys = [ars[i][c:] + _mxu(m_rb[i].astype(BF16), expand(u16[i])) + lm[i][c:] for i in every]
    ds = [_mxu_tn(jnp.concatenate([u16[i], v16[i]], axis=0), bk[i]) for i in every]
    s_new = []
    for i in every:
        last = bc[i][0:1] if revs[i] else bc[i][c - 1:c]
        s_new.append((chains[i][6] + jnp.where(bd, ds[i], 0.0)) * jnp.exp(last))
    return ys, s_new


def _cast_jobs(weights, ng, nb):
    in_specs, out_specs, shapes = [], [], []
    for w, col0, cols in weights:
        rows = w.shape[0]
        n = max(m for m in range(1, ng * nb + 1) if rows % m == 0 and (rows // m) % 16 == 0)
        slab = lambda g, c, n=n: jnp.minimum(g * nb + c, n - 1)
        in_specs.append(pl.BlockSpec((pl.Element(rows // n), pl.Element(cols)),
                                     lambda g, c, s=slab, h=rows // n, col0=col0: (s(g, c) * h, col0)))
        out_specs.append(pl.BlockSpec((rows // n, cols), lambda g, c, s=slab: (s(g, c), 0)))
        shapes.append(jax.ShapeDtypeStruct((rows, cols), PROJ_DTYPE))
    return in_specs, out_specs, shapes


def _run_cast_jobs(src_refs, dst_refs):
    for src, dst in zip(src_refs, dst_refs):
        dst[...] = src[...].astype(dst.dtype)


def _rwkv_scan_kernel(rf, lwf, kf, vf, af, bf, rr, lwr, kr, vr, ar, br, *rest):
    n_cast = (len(rest) - 3) // 2
    yf_o, yr_o = rest[n_cast:n_cast + 2]
    s_ref = rest[-1]
    _run_cast_jobs(rest[:n_cast], rest[n_cast + 2:-1])

    @pl.when(pl.program_id(1) == 0)
    def _():
        s_ref[...] = jnp.zeros_like(s_ref)

    ngs = rf.shape[1] // GROUP_W
    chains = []
    for d, refs in enumerate(((rf, lwf, kf, vf, af, bf), (rr, lwr, kr, vr, ar, br))):
        for g in range(ngs):
            sl = slice(g * GROUP_W, (g + 1) * GROUP_W)
            chains.append(tuple(ref[:, sl] for ref in refs) + (s_ref[d, g], d == 1))
    ys, s_new = _rwkv_chunks(chains)
    for d, o_ref in enumerate((yf_o, yr_o)):
        for g in range(ngs):
            o_ref[:, g * GROUP_W:(g + 1) * GROUP_W] = ys[d * ngs + g]
            s_ref[d, g] = s_new[d * ngs + g]


def _rev_chunk(c, ncb, nb):
    return jnp.where(c < ncb, ncb - 1 - c, nb - 1 - (c - ncb))


def _rwkv_scan(r, v, a, lw0, lw1, k0, k1, b0, b1, lc, cast_weights):
    n, dr = r.shape
    nb, ncb = n // CHUNK, lc // CHUNK
    bw = _pick(dr, RWKV_GROUPS_PER_STEP * GROUP_W, GROUP_W)
    fwd = pl.BlockSpec((CHUNK, bw), lambda g, c: (c, g))
    rev = pl.BlockSpec((CHUNK, bw), lambda g, c: (_rev_chunk(c, ncb, nb), g))
    cast_in, cast_out, cast_shapes = _cast_jobs(cast_weights, dr // bw, nb)
    return pl.pallas_call(
        _rwkv_scan_kernel,
        grid=(dr // bw, nb),
        in_specs=[fwd] * 6 + [rev] * 6 + cast_in,
        out_specs=[fwd, rev] + cast_out,
        out_shape=[jax.ShapeDtypeStruct((n, dr), F32)] * 2 + cast_shapes,
        scratch_shapes=[pltpu.VMEM((2, bw // GROUP_W, GROUP_W, GROUP_W), F32)],
        compiler_params=_params(("arbitrary", "arbitrary"), VMEM_LIMIT),
        name="rwkv_scan",
    )(r, lw0, k0, v, a, b0, r, lw1, k1, v, a, b1, *[w for w, _, _ in cast_weights])


def _hgrn_chunk(q, k, v, lf, s_vk, rev):
    c, w = q.shape
    nsb = c // SUB
    bc = _cumsum_rows(lf, rev)
    last = bc[0:1] if rev else bc[c - 1:c]
    bdh = _block_diag_mask(w, HGRN_HEAD)
    v16 = v.astype(BF16)
    o_inter = _mxu_nt((q * jnp.exp(bc)).astype(BF16), s_vk.astype(BF16))
    s_new = s_vk * jnp.exp(last) + jnp.where(bdh, _mxu_tn(v16, (k * jnp.exp(last - bc)).astype(BF16)), 0.0)

    lane2 = lax.broadcasted_iota(jnp.int32, (c, w), 1)
    row2 = lax.broadcasted_iota(jnp.int32, (c, w), 0)
    zpad = jnp.zeros((HGRN_HEAD - c, w), BF16)

    def expand(z):
        parts = []
        for h in range(w // HGRN_HEAD):
            parts += [jnp.where(lane2 // HGRN_HEAD == h, z, jnp.zeros((), BF16)), zpad]
        return jnp.concatenate(parts, axis=0)

    ones_bd = bdh.astype(BF16)
    t3 = lax.broadcasted_iota(jnp.int32, (SUB, SUB, w), 0)
    s3 = lax.broadcasted_iota(jnp.int32, (SUB, SUB, w), 1)
    l3 = lax.broadcasted_iota(jnp.int32, (SUB, SUB, w), 2) % HGRN_HEAD
    causal = (s3 >= t3) if rev else (s3 <= t3)
    a_rows = []
    for i in range(nsb):
        lo, hi = SUB * i, SUB * (i + 1)
        qi, ki, bi = q[lo:hi], k[lo:hi], bc[lo:hi]
        d3 = bi[:, None, :] - bi[None, :, :]
        x3 = jnp.where(causal, (qi[:, None, :] * ki[None, :, :]) * jnp.exp(jnp.minimum(d3, 0.0)), 0.0)
        r3 = _mxu(x3.reshape(SUB * SUB, w).astype(BF16), ones_bd).reshape(SUB, SUB, w)
        a_i = jnp.sum(jnp.where(l3 == s3 + lo, r3, 0.0), axis=1)
        if (not rev and i > 0) or (rev and i < nsb - 1):
            beta = bc[hi:hi + 1] if rev else bc[lo - 1:lo]
            earlier = (row2 >= hi) if rev else (row2 < lo)
            kp = jnp.where(earlier, k * jnp.exp(jnp.minimum(beta - bc, 0.0)), 0.0)
            a_i = a_i + _mxu_nt((qi * jnp.exp(bi - beta)).astype(BF16), expand(kp.astype(BF16)))
        a_rows.append(a_i)
    o = o_inter + _mxu(jnp.concatenate(a_rows, axis=0).astype(BF16), expand(v16))
    return o, s_new


def _hgrn_chunks_bounded(chains):
    c, w = chains[0][0].shape
    every = range(len(chains))
    revs = [ch[5] for ch in chains]
    bdh = _block_diag_mask(w, HGRN_HEAD)
    lane2 = lax.broadcasted_iota(jnp.int32, (c, w), 1)
    t_i = lax.broadcasted_iota(jnp.int32, (c, w), 0)
    s_i = lane2 % HGRN_HEAD
    incl = {False: s_i <= t_i, True: s_i >= t_i}
    zpad = jnp.zeros((HGRN_HEAD - c, w), BF16)

    def expand(z):
        parts = []
        for h in range(w // HGRN_HEAD):
            parts += [jnp.where(lane2 // HGRN_HEAD == h, z, jnp.zeros((), BF16)), zpad]
        return jnp.concatenate(parts, axis=0)

    bc = [_cumsum_rows(ch[3], ch[5]) for ch in chains]
    last = [bc[i][0:1] if revs[i] else bc[i][c - 1:c] for i in every]
    qt = [(chains[i][0] * jnp.exp(bc[i])).astype(BF16) for i in every]
    kt = [(chains[i][1] * jnp.exp(-bc[i])).astype(BF16) for i in every]
    kd = [(chains[i][1] * jnp.exp(last[i] - bc[i])).astype(BF16) for i in every]
    v16 = [chains[i][2].astype(BF16) for i in every]
    o_inter = [_mxu_nt(qt[i], chains[i][4].astype(BF16)) for i in every]
    a = [jnp.where(incl[revs[i]], _mxu_nt(qt[i], expand(kt[i])), 0.0).astype(BF16) for i in every]
    o = [o_inter[i] + _mxu(a[i], expand(v16[i])) for i in every]
    ds = [_mxu_tn(v16[i], kd[i]) for i in every]
    s_new = [chains[i][4] * jnp.exp(last[i]) + jnp.where(bdh, ds[i], 0.0) for i in every]
    return o, s_new


def _hgrn_scan_kernel(qf, ff, vf, qr, fr, vr, lg_ref, *rest, layer):
    n_cast = (len(rest) - 3) // 2
    of_o, or_o = rest[n_cast:n_cast + 2]
    s_ref = rest[-1]
    _run_cast_jobs(rest[:n_cast], rest[n_cast + 2:-1])

    @pl.when(pl.program_id(1) == 0)
    def _():
        s_ref[...] = jnp.zeros_like(s_ref)

    lg = lg_ref[...]
    e = jnp.exp(lg - jnp.max(lg, axis=0, keepdims=True))
    lb = jnp.sum(e[:layer + 1], axis=0) / jnp.sum(e, axis=0)
    ngs = qf.shape[1] // GROUP_W
    chains = []
    min_total = None
    for d, (q_ref, f_ref, v_ref) in enumerate(((qf, ff, vf), (qr, fr, vr))):
        fd = lb[d:d + 1, :] + (1.0 - lb[d:d + 1, :]) * _sigmoid(f_ref[...])
        qv = q_ref[...]
        qh, kh, vh, lf = qv * _sigmoid(qv), 1.0 - fd, v_ref[...], jnp.log(fd)
        total = jnp.min(jnp.sum(lf, axis=0, keepdims=True))
        min_total = total if min_total is None else jnp.minimum(min_total, total)
        for g in range(ngs):
            sl = slice(g * GROUP_W, (g + 1) * GROUP_W)
            chains.append((qh[:, sl], kh[:, sl], vh[:, sl], lf[:, sl], s_ref[d, g], d == 1))

    def emit(o, s_new):
        for d, o_ref in enumerate((of_o, or_o)):
            for g in range(ngs):
                o_ref[:, g * GROUP_W:(g + 1) * GROUP_W] = o[d * ngs + g]
                s_ref[d, g] = s_new[d * ngs + g]

    bounded = min_total >= -MAX_CHUNK_LOG_DECAY

    @pl.when(bounded)
    def _():
        emit(*_hgrn_chunks_bounded(chains))

    @pl.when(jnp.logical_not(bounded))
    def _():
        res = [_hgrn_chunk(*ch) for ch in chains]
        emit([r[0] for r in res], [r[1] for r in res])


def _hgrn_scan(p_h, lb_logits, lc, layer, cast_weights):
    n = p_h.shape[0]
    dh = lb_logits.shape[2]
    nb, ncb = n // CHUNK, lc // CHUNK
    bw = _pick(dh, HGRN_GROUPS_PER_STEP * GROUP_W, GROUP_W)
    ng = dh // bw
    blk = (CHUNK, bw)
    fwd = lambda sec: pl.BlockSpec(blk, lambda g, c: (c, sec * ng + g))
    rev = lambda sec: pl.BlockSpec(blk, lambda g, c: (_rev_chunk(c, ncb, nb), sec * ng + g))
    cast_in, cast_out, cast_shapes = _cast_jobs(cast_weights, ng, nb)
    return pl.pallas_call(
        functools.partial(_hgrn_scan_kernel, layer=layer),
        grid=(ng, nb),
        in_specs=[fwd(0), fwd(1), fwd(3), rev(0), rev(2), rev(3),
                  pl.BlockSpec((lb_logits.shape[0], 2, bw), lambda g, c: (0, 0, g))] + cast_in,
        out_specs=[pl.BlockSpec(blk, lambda g, c: (c, g)),
                   pl.BlockSpec(blk, lambda g, c: (_rev_chunk(c, ncb, nb), g))] + cast_out,
        out_shape=[jax.ShapeDtypeStruct((n, dh), F32)] * 2 + cast_shapes,
        scratch_shapes=[pltpu.VMEM((2, bw // GROUP_W, GROUP_W, GROUP_W), F32)],
        compiler_params=_params(("arbitrary", "arbitrary"), VMEM_LIMIT),
        name="hgrn_scan",
    )(p_h, p_h, p_h, p_h, p_h, p_h, lb_logits, *[w for w, _, _ in cast_weights])


def _mix_out_kernel(yf, yr, bonus, gate, of, orv, g_ref, lng, lnb, hng, hs_ref, hst_ref, u_o, *, dr, dh):
    hs, hst = hs_ref[...], hst_ref[...]

    def headmean(z):
        return _head_sum(z, hs, hst) * (1.0 / RWKV_HEAD)

    y = yf[...] + yr[...]
    yc = y - headmean(y)
    yn = yc * lax.rsqrt(headmean(yc * yc) + GN_EPS) * lng[...] + lnb[...]
    u_o[:, 0:dr] = ((yn + bonus[...]) * gate[...]).astype(u_o.dtype)
    o = of[...] + orv[...]
    g = g_ref[...]
    sg = g * _sigmoid(g)
    for h in range(dh // HGRN_HEAD):
        sl = slice(h * HGRN_HEAD, (h + 1) * HGRN_HEAD)
        oh = o[:, sl]
        on = oh * lax.rsqrt(jnp.mean(oh * oh, axis=-1, keepdims=True) + NORM_EPS) * hng[...]
        u_o[:, dr + h * HGRN_HEAD:dr + (h + 1) * HGRN_HEAD] = (on * sg[:, sl]).astype(u_o.dtype)


def _mix_out(yf, yr, bonus, gate, of, orv, p_h, ln_g, ln_b, hg, lc, l):
    dr, dh = yf.shape[1], of.shape[1]
    tb = _pick(int(np.gcd(lc, l)), 128, 8)
    off = lc // tb
    hs, hst = _head_indicators(dr)
    rows = lambda wd: pl.BlockSpec((tb, wd), lambda i: (i + off, 0))
    full = lambda a: pl.BlockSpec(a.shape, lambda i: (0,) * a.ndim)
    consts = [ln_g.reshape(1, dr), ln_b.reshape(1, dr), hg.reshape(1, HGRN_HEAD), hs, hst]
    return pl.pallas_call(
        functools.partial(_mix_out_kernel, dr=dr, dh=dh),
        grid=(l // tb,),
        in_specs=[rows(dr)] * 4 + [rows(dh)] * 2 + [pl.BlockSpec((tb, dh), lambda i: (i + off, 4))]
                 + [full(a) for a in consts],
        out_specs=pl.BlockSpec((tb, dr + dh), lambda i: (i, 0)),
        out_shape=jax.ShapeDtypeStruct((l, dr + dh), PROJ_DTYPE),
        compiler_params=_params(("arbitrary",), VMEM_LIMIT),
        name="mix_out",
    )(yf, yr, bonus, gate, of, orv, p_h, *consts)


def _pad_cols(a, width):
    return jnp.pad(a, ((0, 0), (0, width - a.shape[1])))


def _pad_rows(a, height):
    return jnp.pad(a, ((0, height - a.shape[0]), (0, 0)))


def _round_up(n, m):
    return (n + m - 1) // m * m


def kernel(x, c, ctx, c_ctx, w_ada, b_ada, g_mix_pre, g_mix_post, g_ffn_pre, g_ffn_post, w_in, mu_shift, w0, w2, a0,
           a2, g2, k_k, k_a, r_k, ln_x_g, ln_x_b, hgrn_lb_logits, hgrn_norm_g, w_out, w_ff1, w_ff2):
    assert x.shape[0] == 1 and w_in.shape[0] == 1, "single batch, single layer"
    layer = 0
    x2, ctx2 = x[0], ctx[0]
    l, d = x2.shape
    lc = ctx2.shape[0]
    dr, dh = k_k.shape[1], hgrn_lb_logits.shape[2]
    rd_raw, ra_raw, rg_raw = w2.shape[2], a2.shape[2], g2.shape[1]
    rd, ra, rg = (_round_up(v, LANE) for v in (rd_raw, ra_raw, rg_raw))
    assert l % GRID_W == 0 and lc % CHUNK == 0 and dr % GROUP_W == 0 and dh % GROUP_W == 0

    cvec = jnp.concatenate([c, c_ctx[None, :], jnp.zeros((6, d), F32)], axis=0)
    mod = _ada(cvec, w_ada[layer], b_ada[layer])

    wi = w_in[layer]
    mu = mu_shift[layer][None, :]
    sel = (jnp.arange(mu.shape[1], dtype=jnp.int32) % 4)[None, :]
    o = 3 * dr
    cuts = []
    for raw, padded in ((rd_raw, rd), (rd_raw, rd), (ra_raw, ra), (ra_raw, ra), (rg_raw, rg)):
        cuts.append((o, o + raw, padded))
        o += raw
    rw_cols = o
    regroup = lambda a: jnp.concatenate([_pad_cols(a[:, s:e], wd) for s, e, wd in cuts], axis=1)
    w_lora = regroup(wi)
    w2p = jnp.stack([_pad_rows(w2[layer, dd], rd) for dd in range(2)])
    a2p = jnp.stack([_pad_rows(a2[layer, dd], ra) for dd in range(2)])
    g2p = _pad_rows(g2[layer], rg)

    h = _prep(ctx2, x2, g_mix_pre[layer], mod)
    p_main = _mm_wcast(h, wi, 3 * dr, F32, tm_target=768, tn_target=768, name="in_proj_rkv")
    p_lora = _mm_wcast(h, w_lora, w_lora.shape[1], F32, tm_target=768, name="in_proj_lora")

    (r, v, a, lw0, lw1, k0, k1, b0, b1, bonus, gate) = _rwkv_features(
        p_main, p_lora, sel[:, :3 * dr], mu[:, :3 * dr], regroup(sel), regroup(mu), w0[layer], w2p, a0[layer], a2p,
        g2p, k_k[layer], k_a[layer], r_k[layer].reshape(-1), lc, l)
    whole = lambda w: (w, 0, w.shape[1])
    yf, yr, w_h, w_up = _rwkv_scan(r, v, a, lw0, lw1, k0, k1, b0, b1, lc,
                                   [(wi, rw_cols, wi.shape[1] - rw_cols), whole(w_ff1[layer])])
    p_h = _mm(h, w_h, F32, tm_target=768, name="in_proj_hgrn")
    of, orv, w_down, w_o = _hgrn_scan(p_h, hgrn_lb_logits, lc, layer, [whole(w_ff2[layer]), whole(w_out[layer])])

    u = _mix_out(yf, yr, bonus, gate, of, orv, p_h, ln_x_g[layer], ln_x_b[layer], hgrn_norm_g[layer], lc, l)
    x1, h2 = _residual_projection(u, w_o, x2, mod, g_mix_post[layer], 2, g_ffn_pre[layer], (3, 4), tk_target=512,
                                  name="out_proj")
    act = _mm(h2, w_up, PROJ_DTYPE, act="relu2", name="ffn_up")
    out = _residual_projection(act, w_down, x1, mod, g_ffn_post[layer], 5, name="ffn_down")
    return out[None]
```

```python
import functools

import jax
import jax.numpy as jnp
import numpy as np
from jax import lax
from jax.experimental import pallas as pl
from jax.experimental.pallas import tpu as pltpu

F32 = jnp.float32
BF16 = jnp.bfloat16
PROJ_DTYPE = jnp.bfloat16

LANE = 128
GRID_W = 64
CHUNK = GRID_W
RWKV_HEAD = 64
HGRN_HEAD = 128
SUB = 16
GROUP_W = 256
RWKV_GROUPS_PER_STEP = 8
HGRN_GROUPS_PER_STEP = 8
MAX_CHUNK_LOG_DECAY = 60.0
NORM_EPS = 1e-6
GN_EPS = 64e-5
EXP_M05 = float(np.exp(-0.5))
VMEM_LIMIT = 56 * 1024 * 1024


def _pick(n, target, unit=LANE):
    best = None
    for m in range(unit, min(n, target) + 1, unit):
        if n % m == 0:
            best = m
    return best if best is not None else n


def _sigmoid(z):
    return 0.5 * jnp.tanh(0.5 * z) + 0.5


def _mxu(a, b):
    return jnp.dot(a, b, preferred_element_type=F32)


def _mxu_nt(a, b):
    return lax.dot_general(a, b, (((1,), (1,)), ((), ())), preferred_element_type=F32)


def _mxu_tn(a, b):
    return lax.dot_general(a, b, (((0,), (0,)), ((), ())), preferred_element_type=F32)


def _split3(z):
    hi = z.astype(BF16)
    rest = z - hi.astype(F32)
    mid = rest.astype(BF16)
    return hi, mid, (rest - mid.astype(F32)).astype(BF16)


def _select_sum(z, sel):
    hi, mid, _ = _split3(z)
    return _mxu(hi, sel) + _mxu(mid, sel)


def _head_sum(z, hs, hst):
    return _select_sum(_select_sum(z, hs), hst)


def _head_indicators(dr):
    hs = np.arange(dr)[:, None] // RWKV_HEAD == np.arange(LANE)[None, :]
    return jnp.asarray(hs, BF16), jnp.asarray(hs.T, BF16)


def _params(sem, vmem=None):
    return pltpu.CompilerParams(dimension_semantics=sem, vmem_limit_bytes=vmem)


def _ada_kernel(c_ref, w_ref, b_ref, o_ref):
    cv = c_ref[...]
    rows = cv.shape[0]
    hi, mid, lo = (t.astype(F32) for t in _split3(cv * _sigmoid(cv)))
    s3 = jnp.concatenate([hi, mid, lo], axis=0).astype(BF16)
    w = w_ref[...]
    w_hi = w.astype(BF16)
    w_lo = (w - w_hi.astype(F32)).astype(BF16)
    p = _mxu(s3, w_hi)
    q = _mxu(s3[:2 * rows], w_lo)
    o_ref[...] = (p[:rows] + p[rows:2 * rows] + p[2 * rows:] + q[:rows] + q[rows:]) + b_ref[...]


def _ada(cvec, w, b):
    rows, d = cvec.shape
    n = w.shape[1]
    tn = _pick(n, 512)
    return pl.pallas_call(
        _ada_kernel,
        grid=(n // tn,),
        in_specs=[pl.BlockSpec((rows, d), lambda j: (0, 0)),
                  pl.BlockSpec((d, tn), lambda j: (0, j)),
                  pl.BlockSpec((1, tn), lambda j: (0, j))],
        out_specs=pl.BlockSpec((rows, tn), lambda j: (0, j)),
        out_shape=jax.ShapeDtypeStruct((rows, n), F32),
        compiler_params=_params(("arbitrary",), VMEM_LIMIT),
        name="ada_mod",
    )(cvec, w, b.reshape(1, n))


def _prep_kernel(ctx_ref, x_ref, g_ref, mod_ref, o_ref, *, ncb, d):
    is_ctx = pl.program_id(0) < ncb
    rows = jnp.where(is_ctx, ctx_ref[...], x_ref[...])
    ms = jnp.mean(rows * rows, axis=-1, keepdims=True)
    hn = rows * lax.rsqrt(ms + NORM_EPS) * g_ref[...]
    shift = jnp.where(is_ctx, mod_ref[1:2, 0:d], mod_ref[0:1, 0:d])
    scale = jnp.where(is_ctx, mod_ref[1:2, d:2 * d], mod_ref[0:1, d:2 * d])
    o_ref[...] = (hn * (1.0 + scale) + shift).astype(o_ref.dtype)


def _prep(ctx2, x2, g, mod):
    lc, d = ctx2.shape
    l = x2.shape[0]
    tb = _pick(int(np.gcd(lc, l)), 256, 8)
    ncb = lc // tb
    return pl.pallas_call(
        functools.partial(_prep_kernel, ncb=ncb, d=d),
        grid=((lc + l) // tb,),
        in_specs=[pl.BlockSpec((tb, d), lambda i: (jnp.minimum(i, ncb - 1), 0)),
                  pl.BlockSpec((tb, d), lambda i: (jnp.maximum(i - ncb, 0), 0)),
                  pl.BlockSpec((1, d), lambda i: (0, 0)),
                  pl.BlockSpec(mod.shape, lambda i: (0, 0))],
        out_specs=pl.BlockSpec((tb, d), lambda i: (i, 0)),
        out_shape=jax.ShapeDtypeStruct((lc + l, d), PROJ_DTYPE),
        compiler_params=_params(("arbitrary",)),
        name="norm_modulate",
    )(ctx2, x2, g.reshape(1, d), mod)


def _mm_kernel(x_ref, w_ref, o_ref, *, act):
    acc = jnp.dot(x_ref[...], w_ref[...], preferred_element_type=F32)
    if act == "relu2":
        acc = jnp.square(jnp.maximum(acc, 0.0))
    o_ref[...] = acc.astype(o_ref.dtype)


def _mm(x, w, out_dtype, act=None, tm_target=1024, tn_target=1024, name="matmul"):
    m, k = x.shape
    n = w.shape[1]
    tm = _pick(m, tm_target)
    tn = _pick(n, tn_target)
    return pl.pallas_call(
        functools.partial(_mm_kernel, act=act),
        grid=(m // tm, n // tn),
        in_specs=[pl.BlockSpec((tm, k), lambda i, j: (i, 0)),
                  pl.BlockSpec((k, tn), lambda i, j: (0, j))],
        out_specs=pl.BlockSpec((tm, tn), lambda i, j: (i, j)),
        out_shape=jax.ShapeDtypeStruct((m, n), out_dtype),
        compiler_params=_params(("arbitrary", "arbitrary"), VMEM_LIMIT),
        name=name,
    )(x, w)


def _mm_wcast_kernel(x_ref, w_ref, o_ref, w_cast):
    @pl.when(pl.program_id(1) == 0)
    def _():
        w_cast[...] = w_ref[...].astype(w_cast.dtype)

    o_ref[...] = jnp.dot(x_ref[...], w_cast[...], preferred_element_type=F32).astype(o_ref.dtype)


def _mm_wcast(x, w, ncols, out_dtype, tm_target=1024, tn_target=1024, name="matmul_wcast"):
    m, k = x.shape
    tm, tn = _pick(m, tm_target), _pick(ncols, tn_target)
    return pl.pallas_call(
        _mm_wcast_kernel,
        grid=(ncols // tn, m // tm),
        in_specs=[pl.BlockSpec((tm, k), lambda j, i: (i, 0)),
                  pl.BlockSpec((k, tn), lambda j, i: (0, j))],
        out_specs=pl.BlockSpec((tm, tn), lambda j, i: (i, j)),
        out_shape=jax.ShapeDtypeStruct((m, ncols), out_dtype),
        scratch_shapes=[pltpu.VMEM((k, tn), PROJ_DTYPE)],
        compiler_params=_params(("arbitrary", "arbitrary"), VMEM_LIMIT),
        name=name,
    )(x, w)


def _res_proj_kernel(a_ref, w_ref, res_hbm, mod_ref, g_ref, o_ref, res_buf, sem, *, d, gate_chunk):
    i, kk = pl.program_id(0), pl.program_id(1)
    tm = res_buf.shape[0]
    res_copy = pltpu.make_async_copy(res_hbm.at[pl.ds(pl.multiple_of(i * tm, tm), tm), :], res_buf, sem)

    @pl.when(kk == 0)
    def _():
        res_copy.start()
        o_ref[...] = jnp.zeros_like(o_ref)

    o_ref[...] += jnp.dot(a_ref[...], w_ref[...], preferred_element_type=F32)

    @pl.when(kk == pl.num_programs(1) - 1)
    def _():
        res_copy.wait()
        gate = mod_ref[0:1, gate_chunk * d:(gate_chunk + 1) * d] * g_ref[...]
        rb = _pick(tm, 128, 8)
        for r0 in range(0, tm, rb):
            rows = slice(r0, r0 + rb)
            acc = o_ref[rows, :]
            scale = lax.rsqrt(jnp.mean(acc * acc, axis=-1, keepdims=True) + NORM_EPS)
            o_ref[rows, :] = res_buf[rows, :] + gate * (acc * scale)


def _residual_projection(a, w, res, mod, g_post, gate_chunk, tm_target=512, tk_target=1024, name="residual_projection"):
    m, k = a.shape
    d = w.shape[1]
    tm, tk = _pick(m, tm_target, 8), _pick(k, tk_target)
    full = lambda z: pl.BlockSpec(z.shape, lambda i, kk: (0,) * z.ndim)
    gp = g_post.reshape(1, d)
    return pl.pallas_call(
        functools.partial(_res_proj_kernel, d=d, gate_chunk=gate_chunk),
        grid=(m // tm, k // tk),
        in_specs=[pl.BlockSpec((tm, tk), lambda i, kk: (i, kk)),
                  pl.BlockSpec((tk, d), lambda i, kk: (kk, 0)),
                  pl.BlockSpec(memory_space=pl.ANY), full(mod), full(gp)],
        out_specs=pl.BlockSpec((tm, d), lambda i, kk: (i, 0)),
        out_shape=jax.ShapeDtypeStruct((m, d), F32),
        scratch_shapes=[pltpu.VMEM((tm, d), F32), pltpu.SemaphoreType.DMA(())],
        compiler_params=_params(("arbitrary", "arbitrary"), VMEM_LIMIT),
        name=name,
    )(a, w, res, mod, gp)


RING = 4


def _rwkv_feat_kernel(pm_hbm, pl_hbm, sel_ref, mu_ref, sell_ref, mul_ref, w0_ref, w2_ref, a0_ref, a2_ref, g2_ref,
                      kk_ref, ka_ref, rk_ref, hs_ref, hst_ref,
                      r_o, v_o, a_o, lw0_o, lw1_o, k0_o, k1_o, b0_o, b1_o, bonus_o, gate_o, ring_m, ring_l, sems,
                      *, ncb, nrows, dr, rd, ra):
    i = pl.program_id(0)
    nb = pl.num_programs(0)
    c = ring_m.shape[1]
    sources = ((pm_hbm, ring_m), (pl_hbm, ring_l))

    def fetch(blk):
        slot = blk % RING
        rows = pl.ds(pl.multiple_of(blk * c, c), c)
        return [pltpu.make_async_copy(src.at[rows, :], ring.at[slot], sems.at[n, slot])
                for n, (src, ring) in enumerate(sources)]

    def start(blk):
        for cp in fetch(blk):
            cp.start()

    def wait(blk):
        for cp in fetch(blk):
            cp.wait()

    @pl.when(i == 0)
    def _():
        start(0)
        start(1)
        start(2)
        wait(0)
        wait(1)

    @pl.when((i > 0) & (i + 1 < nb))
    def _():
        wait(i + 1)

    @pl.when((i > 0) & (i + 2 < nb))
    def _():
        start(i + 2)

    is_ctx = i < ncb
    xi = i - ncb
    s_cur, s_prv, s_nxt = i % RING, jnp.maximum(i - 1, 0) % RING, jnp.minimum(i + 1, nb - 1) % RING

    def shift_lerp(ring, sel, mu):
        cur, prv, nxt = ring[s_cur], ring[s_prv], ring[s_nxt]
        row = lax.broadcasted_iota(jnp.int32, cur.shape, 0)
        prev_last = jnp.where(is_ctx & (i > 0), prv[c - 1:c, :], 0.0)
        next_first = jnp.where(is_ctx & (i < ncb - 1), nxt[0:1, :], 0.0)
        before = jnp.where(row == 0, prev_last, pltpu.roll(cur, 1, 0))
        after = jnp.where(row == c - 1, next_first, pltpu.roll(cur, c - 1, 0))
        above = jnp.where(is_ctx, before, jnp.where(xi > 0, prv, 0.0))
        below = jnp.where(is_ctx, after, jnp.where(xi < nrows - 1, nxt, 0.0))
        shifted = jnp.where(sel == 0, before, jnp.where(sel == 1, after, jnp.where(sel == 2, above, below)))
        return cur + mu * (shifted - cur)

    m = shift_lerp(ring_m, sel_ref[...], mu_ref[...])
    ml = shift_lerp(ring_l, sell_ref[...], mul_ref[...])
    r, k, v = m[:, 0:dr], m[:, dr:2 * dr], m[:, 2 * dr:3 * dr]
    wl = (ml[:, 0:rd], ml[:, rd:2 * rd])
    al = (ml[:, 2 * rd:2 * rd + ra], ml[:, 2 * rd + ra:2 * rd + 2 * ra])
    gl = ml[:, 2 * rd + 2 * ra:]

    hs, hst = hs_ref[...], hst_ref[...]

    def headsum(z):
        return _head_sum(z, hs, hst)

    kkf = k * kk_ref[...]
    kk = kkf * lax.rsqrt(headsum(kkf * kkf) + 1e-12)
    r_o[...] = r
    v_o[...] = v
    a_o[...] = -kk
    for d, (lw_o, k_o, b_o) in enumerate(((lw0_o, k0_o, b0_o), (lw1_o, k1_o, b1_o))):
        wd = _mxu(jnp.tanh(wl[d]).astype(BF16), w2_ref[d]) + w0_ref[d:d + 1, :]
        lw_o[...] = -EXP_M05 * _sigmoid(wd)
        ad = _sigmoid(_mxu(al[d].astype(BF16), a2_ref[d]) + a0_ref[d:d + 1, :])
        k_o[...] = k * (1.0 + (ad - 1.0) * ka_ref[...])
        b_o[...] = kk * ad
    bonus_o[...] = (headsum(r * k * rk_ref[...]) * v).astype(bonus_o.dtype)
    gate_o[...] = _mxu(_sigmoid(gl).astype(BF16), g2_ref[...]).astype(gate_o.dtype)


def _rwkv_features(p_main, p_lora, sel, mu, sel_l, mu_l, w0, w2p, a0, a2p, g2p, k_k, k_a, r_k, lc, l):
    n = p_main.shape[0]
    dr = k_k.shape[0]
    rd, ra = w2p.shape[1], a2p.shape[1]
    nb, ncb = n // CHUNK, lc // CHUNK
    nheads = dr // RWKV_HEAD
    assert nheads <= LANE
    hs, hst = _head_indicators(dr)
    full = lambda a: pl.BlockSpec(a.shape, lambda i: (0,) * a.ndim)
    row1 = lambda a: a.reshape(1, -1)
    consts = [sel, mu, sel_l, mu_l, w0, w2p.astype(BF16), a0, a2p.astype(BF16), g2p.astype(BF16), row1(k_k),
              row1(k_a), row1(r_k), hs, hst]
    assert nb >= 3
    outs = pl.pallas_call(
        functools.partial(_rwkv_feat_kernel, ncb=ncb, nrows=l // CHUNK, dr=dr, rd=rd, ra=ra),
        grid=(nb,),
        in_specs=[pl.BlockSpec(memory_space=pl.ANY)] * 2 + [full(a) for a in consts],
        out_specs=[pl.BlockSpec((CHUNK, dr), lambda i: (i, 0))] * 11,
        out_shape=[jax.ShapeDtypeStruct((n, dr), F32)] * 9 + [jax.ShapeDtypeStruct((n, dr), PROJ_DTYPE)] * 2,
        scratch_shapes=[pltpu.VMEM((RING, CHUNK, p_main.shape[1]), F32), pltpu.VMEM((RING, CHUNK, p_lora.shape[1]), F32),
                        pltpu.SemaphoreType.DMA((2, RING))],
        compiler_params=_params(("arbitrary",), VMEM_LIMIT),
        name="rwkv_features",
    )(p_main, p_lora, *consts)
    return outs


def _block_diag_mask(n, w):
    r = lax.broadcasted_iota(jnp.int32, (n, n), 0)
    c = lax.broadcasted_iota(jnp.int32, (n, n), 1)
    return (r // w) == (c // w)


def _cumsum_rows(z, rev):
    c = z.shape[0]
    t = lax.broadcasted_iota(jnp.int32, (c, c), 0)
    s = lax.broadcasted_iota(jnp.int32, (c, c), 1)
    tri = ((s >= t) if rev else (s <= t)).astype(BF16)
    hi, mid, lo = _split3(z)
    return _mxu(tri, hi) + _mxu(tri, mid) + _mxu(tri, lo)


def _rwkv_chunks(chains):
    c, w = chains[0][0].shape
    nh = w // c
    every = range(len(chains))
    revs = [ch[7] for ch in chains]
    bd = _block_diag_mask(w, c)
    t_i = lax.broadcasted_iota(jnp.int32, (c, w), 0)
    s_i = lax.broadcasted_iota(jnp.int32, (c, w), 1) % c
    eye = (s_i == t_i).astype(F32)
    strict = {False: s_i < t_i, True: s_i > t_i}
    incl = {False: s_i <= t_i, True: s_i >= t_i}

    def expand(z):
        return jnp.where(bd, jnp.concatenate([z] * nh, axis=0), jnp.zeros((), z.dtype))

    def packed_mm(lhs, rhs):
        return _mxu(lhs.astype(BF16), expand(rhs.astype(BF16)))

    bc = [_cumsum_rows(ch[1], ch[7]) for ch in chains]
    ar, bk, v16 = [], [], []
    for i, (r, lw, k, v, a, b, _, _) in enumerate(chains):
        e_in, e_ex, e_ng = jnp.exp(bc[i]), jnp.exp(bc[i] - lw), jnp.exp(-bc[i])
        ar.append(jnp.concatenate([a * e_ex, r * e_in], axis=0).astype(BF16))
        bk.append(jnp.concatenate([b * e_ng, k * e_ng], axis=0).astype(BF16))
        v16.append(v.astype(BF16))
    gb = [_mxu_nt(ar[i], expand(bk[i][:c])) for i in every]
    gk = [_mxu_nt(ar[i], expand(bk[i][c:])) for i in every]
    l_ab = [jnp.where(strict[revs[i]], gb[i][:c], 0.0) for i in every]
    m_rb = [jnp.where(incl[revs[i]], gb[i][c:], 0.0) for i in every]
    lmk = [jnp.concatenate([jnp.where(strict[revs[i]], gk[i][:c], 0.0),
                            jnp.where(incl[revs[i]], gk[i][c:], 0.0)], axis=0) for i in every]
    t_m = [eye + l_ab[i] for i in every]
    p_m = [packed_mm(l_ab[i], l_ab[i]) for i in every]
    for _ in range(int(np.log2(c)) - 2):
        tp = [packed_mm(jnp.concatenate([t_m[i], p_m[i]], axis=0), p_m[i]) for i in every]
        t_m = [t_m[i] + tp[i][:c] for i in every]
        p_m = [tp[i][c:] for i in every]
    t_m = [t_m[i] + packed_mm(t_m[i], p_m[i]) for i in every]

    ars = [_mxu_nt(ar[i], chains[i][6].astype(BF16)) for i in every]
    lm = [_mxu(lmk[i].astype(BF16), expand(v16[i])) for i in every]
    u16 = [packed_mm(t_m[i], ars[i][:c] + lm[i][:c]).astype(BF16) for i in every]
    ys = [ars[i][c:] + _mxu(m_rb[i].astype(BF16), expand(u16[i])) + lm[i][c:] for i in every]
    ds = [_mxu_tn(jnp.concatenate([u16[i], v16[i]], axis=0), bk[i]) for i in every]
    s_new = []
    for i in every:
        last = bc[i][0:1] if revs[i] else bc[i][c - 1:c]
        s_new.append((chains[i][6] + jnp.where(bd, ds[i], 0.0)) * jnp.exp(last))
    return ys, s_new


def _cast_jobs(weights, ng, nb):
    in_specs, out_specs, shapes = [], [], []
    for w, col0, cols in weights:
        rows = w.shape[0]
        n = max(m for m in range(1, ng * nb + 1) if rows % m == 0 and (rows // m) % 16 == 0)
        slab = lambda g, c, n=n: jnp.minimum(g * nb + c, n - 1)
        in_specs.append(pl.BlockSpec((pl.Element(rows // n), pl.Element(cols)),
                                     lambda g, c, s=slab, h=rows // n, col0=col0: (s(g, c) * h, col0)))
        out_specs.append(pl.BlockSpec((rows // n, cols), lambda g, c, s=slab: (s(g, c), 0)))
        shapes.append(jax.ShapeDtypeStruct((rows, cols), PROJ_DTYPE))
    return in_specs, out_specs, shapes


def _run_cast_jobs(src_refs, dst_refs):
    for src, dst in zip(src_refs, dst_refs):
        dst[...] = src[...].astype(dst.dtype)


def _rwkv_scan_kernel(rf, lwf, kf, vf, af, bf, rr, lwr, kr, vr, ar, br, *rest):
    n_cast = (len(rest) - 3) // 2
    yf_o, yr_o = rest[n_cast:n_cast + 2]
    s_ref = rest[-1]
    _run_cast_jobs(rest[:n_cast], rest[n_cast + 2:-1])

    @pl.when(pl.program_id(1) == 0)
    def _():
        s_ref[...] = jnp.zeros_like(s_ref)

    ngs = rf.shape[1] // GROUP_W
    chains = []
    for d, refs in enumerate(((rf, lwf, kf, vf, af, bf), (rr, lwr, kr, vr, ar, br))):
        for g in range(ngs):
            sl = slice(g * GROUP_W, (g + 1) * GROUP_W)
            chains.append(tuple(ref[:, sl] for ref in refs) + (s_ref[d, g], d == 1))
    ys, s_new = _rwkv_chunks(chains)
    for d, o_ref in enumerate((yf_o, yr_o)):
        for g in range(ngs):
            o_ref[:, g * GROUP_W:(g + 1) * GROUP_W] = ys[d * ngs + g]
            s_ref[d, g] = s_new[d * ngs + g]


def _rev_chunk(c, ncb, nb):
    return jnp.where(c < ncb, ncb - 1 - c, nb - 1 - (c - ncb))


def _rwkv_scan(r, v, a, lw0, lw1, k0, k1, b0, b1, lc, cast_weights):
    n, dr = r.shape
    nb, ncb = n // CHUNK, lc // CHUNK
    bw = _pick(dr, RWKV_GROUPS_PER_STEP * GROUP_W, GROUP_W)
    fwd = pl.BlockSpec((CHUNK, bw), lambda g, c: (c, g))
    rev = pl.BlockSpec((CHUNK, bw), lambda g, c: (_rev_chunk(c, ncb, nb), g))
    cast_in, cast_out, cast_shapes = _cast_jobs(cast_weights, dr // bw, nb)
    return pl.pallas_call(
        _rwkv_scan_kernel,
        grid=(dr // bw, nb),
        in_specs=[fwd] * 6 + [rev] * 6 + cast_in,
        out_specs=[fwd, rev] + cast_out,
        out_shape=[jax.ShapeDtypeStruct((n, dr), F32)] * 2 + cast_shapes,
        scratch_shapes=[pltpu.VMEM((2, bw // GROUP_W, GROUP_W, GROUP_W), F32)],
        compiler_params=_params(("arbitrary", "arbitrary"), VMEM_LIMIT),
        name="rwkv_scan",
    )(r, lw0, k0, v, a, b0, r, lw1, k1, v, a, b1, *[w for w, _, _ in cast_weights])


def _hgrn_chunk(q, k, v, lf, s_vk, rev):
    c, w = q.shape
    nsb = c // SUB
    bc = _cumsum_rows(lf, rev)
    last = bc[0:1] if rev else bc[c - 1:c]
    bdh = _block_diag_mask(w, HGRN_HEAD)
    v16 = v.astype(BF16)
    o_inter = _mxu_nt((q * jnp.exp(bc)).astype(BF16), s_vk.astype(BF16))
    s_new = s_vk * jnp.exp(last) + jnp.where(bdh, _mxu_tn(v16, (k * jnp.exp(last - bc)).astype(BF16)), 0.0)

    lane2 = lax.broadcasted_iota(jnp.int32, (c, w), 1)
    row2 = lax.broadcasted_iota(jnp.int32, (c, w), 0)
    zpad = jnp.zeros((HGRN_HEAD - c, w), BF16)

    def expand(z):
        parts = []
        for h in range(w // HGRN_HEAD):
            parts += [jnp.where(lane2 // HGRN_HEAD == h, z, jnp.zeros((), BF16)), zpad]
        return jnp.concatenate(parts, axis=0)

    ones_bd = bdh.astype(BF16)
    t3 = lax.broadcasted_iota(jnp.int32, (SUB, SUB, w), 0)
    s3 = lax.broadcasted_iota(jnp.int32, (SUB, SUB, w), 1)
    l3 = lax.broadcasted_iota(jnp.int32, (SUB, SUB, w), 2) % HGRN_HEAD
    causal = (s3 >= t3) if rev else (s3 <= t3)
    a_rows = []
    for i in range(nsb):
        lo, hi = SUB * i, SUB * (i + 1)
        qi, ki, bi = q[lo:hi], k[lo:hi], bc[lo:hi]
        d3 = bi[:, None, :] - bi[None, :, :]
        x3 = jnp.where(causal, (qi[:, None, :] * ki[None, :, :]) * jnp.exp(jnp.minimum(d3, 0.0)), 0.0)
        r3 = _mxu(x3.reshape(SUB * SUB, w).astype(BF16), ones_bd).reshape(SUB, SUB, w)
        a_i = jnp.sum(jnp.where(l3 == s3 + lo, r3, 0.0), axis=1)
        if (not rev and i > 0) or (rev and i < nsb - 1):
            beta = bc[hi:hi + 1] if rev else bc[lo - 1:lo]
            earlier = (row2 >= hi) if rev else (row2 < lo)
            kp = jnp.where(earlier, k * jnp.exp(jnp.minimum(beta - bc, 0.0)), 0.0)
            a_i = a_i + _mxu_nt((qi * jnp.exp(bi - beta)).astype(BF16), expand(kp.astype(BF16)))
        a_rows.append(a_i)
    o = o_inter + _mxu(jnp.concatenate(a_rows, axis=0).astype(BF16), expand(v16))
    return o, s_new


def _hgrn_chunks_bounded(chains):
    c, w = chains[0][0].shape
    every = range(len(chains))
    revs = [ch[5] for ch in chains]
    bdh = _block_diag_mask(w, HGRN_HEAD)
    lane2 = lax.broadcasted_iota(jnp.int32, (c, w), 1)
    t_i = lax.broadcasted_iota(jnp.int32, (c, w), 0)
    s_i = lane2 % HGRN_HEAD
    incl = {False: s_i <= t_i, True: s_i >= t_i}
    zpad = jnp.zeros((HGRN_HEAD - c, w), BF16)

    def expand(z):
        parts = []
        for h in range(w // HGRN_HEAD):
            parts += [jnp.where(lane2 // HGRN_HEAD == h, z, jnp.zeros((), BF16)), zpad]
        return jnp.concatenate(parts, axis=0)

    bc = [_cumsum_rows(ch[3], ch[5]) for ch in chains]
    last = [bc[i][0:1] if revs[i] else bc[i][c - 1:c] for i in every]
    qt = [(chains[i][0] * jnp.exp(bc[i])).astype(BF16) for i in every]
    kt = [(chains[i][1] * jnp.exp(-bc[i])).astype(BF16) for i in every]
    kd = [(chains[i][1] * jnp.exp(last[i] - bc[i])).astype(BF16) for i in every]
    v16 = [chains[i][2].astype(BF16) for i in every]
    o_inter = [_mxu_nt(qt[i], chains[i][4].astype(BF16)) for i in every]
    a = [jnp.where(incl[revs[i]], _mxu_nt(qt[i], expand(kt[i])), 0.0).astype(BF16) for i in every]
    o = [o_inter[i] + _mxu(a[i], expand(v16[i])) for i in every]
    ds = [_mxu_tn(v16[i], kd[i]) for i in every]
    s_new = [chains[i][4] * jnp.exp(last[i]) + jnp.where(bdh, ds[i], 0.0) for i in every]
    return o, s_new


def _hgrn_scan_kernel(qf, ff, vf, qr, fr, vr, lg_ref, *rest, layer):
    n_cast = (len(rest) - 3) // 2
    of_o, or_o = rest[n_cast:n_cast + 2]
    s_ref = rest[-1]
    _run_cast_jobs(rest[:n_cast], rest[n_cast + 2:-1])

    @pl.when(pl.program_id(1) == 0)
    def _():
        s_ref[...] = jnp.zeros_like(s_ref)

    lg = lg_ref[...]
    e = jnp.exp(lg - jnp.max(lg, axis=0, keepdims=True))
    lb = jnp.sum(e[:layer + 1], axis=0) / jnp.sum(e, axis=0)
    ngs = qf.shape[1] // GROUP_W
    chains = []
    min_total = None
    for d, (q_ref, f_ref, v_ref) in enumerate(((qf, ff, vf), (qr, fr, vr))):
        fd = lb[d:d + 1, :] + (1.0 - lb[d:d + 1, :]) * _sigmoid(f_ref[...])
        qv = q_ref[...]
        qh, kh, vh, lf = qv * _sigmoid(qv), 1.0 - fd, v_ref[...], jnp.log(fd)
        total = jnp.min(jnp.sum(lf, axis=0, keepdims=True))
        min_total = total if min_total is None else jnp.minimum(min_total, total)
        for g in range(ngs):
            sl = slice(g * GROUP_W, (g + 1) * GROUP_W)
            chains.append((qh[:, sl], kh[:, sl], vh[:, sl], lf[:, sl], s_ref[d, g], d == 1))

    def emit(o, s_new):
        for d, o_ref in enumerate((of_o, or_o)):
            for g in range(ngs):
                o_ref[:, g * GROUP_W:(g + 1) * GROUP_W] = o[d * ngs + g]
                s_ref[d, g] = s_new[d * ngs + g]

    bounded = min_total >= -MAX_CHUNK_LOG_DECAY

    @pl.when(bounded)
    def _():
        emit(*_hgrn_chunks_bounded(chains))

    @pl.when(jnp.logical_not(bounded))
    def _():
        res = [_hgrn_chunk(*ch) for ch in chains]
        emit([r[0] for r in res], [r[1] for r in res])


def _hgrn_scan(p_h, lb_logits, lc, layer, cast_weights):
    n = p_h.shape[0]
    dh = lb_logits.shape[2]
    nb, ncb = n // CHUNK, lc // CHUNK
    bw = _pick(dh, HGRN_GROUPS_PER_STEP * GROUP_W, GROUP_W)
    ng = dh // bw
    blk = (CHUNK, bw)
    fwd = lambda sec: pl.BlockSpec(blk, lambda g, c: (c, sec * ng + g))
    rev = lambda sec: pl.BlockSpec(blk, lambda g, c: (_rev_chunk(c, ncb, nb), sec * ng + g))
    cast_in, cast_out, cast_shapes = _cast_jobs(cast_weights, ng, nb)
    return pl.pallas_call(
        functools.partial(_hgrn_scan_kernel, layer=layer),
        grid=(ng, nb),
        in_specs=[fwd(0), fwd(1), fwd(3), rev(0), rev(2), rev(3),
                  pl.BlockSpec((lb_logits.shape[0], 2, bw), lambda g, c: (0, 0, g))] + cast_in,
        out_specs=[pl.BlockSpec(blk, lambda g, c: (c, g)),
                   pl.BlockSpec(blk, lambda g, c: (_rev_chunk(c, ncb, nb), g))] + cast_out,
        out_shape=[jax.ShapeDtypeStruct((n, dh), F32)] * 2 + cast_shapes,
        scratch_shapes=[pltpu.VMEM((2, bw // GROUP_W, GROUP_W, GROUP_W), F32)],
        compiler_params=_params(("arbitrary", "arbitrary"), VMEM_LIMIT),
        name="hgrn_scan",
    )(p_h, p_h, p_h, p_h, p_h, p_h, lb_logits, *[w for w, _, _ in cast_weights])


def _mix_out_kernel(yf, yr, bonus, gate, of, orv, g_ref, lng, lnb, hng, hs_ref, hst_ref, u_o, *, dr, dh):
    hs, hst = hs_ref[...], hst_ref[...]

    def headmean(z):
        return _head_sum(z, hs, hst) * (1.0 / RWKV_HEAD)

    y = yf[...] + yr[...]
    yc = y - headmean(y)
    yn = yc * lax.rsqrt(headmean(yc * yc) + GN_EPS) * lng[...] + lnb[...]
    u_o[:, 0:dr] = ((yn + bonus[...].astype(F32)) * gate[...].astype(F32)).astype(u_o.dtype)
    o = of[...] + orv[...]
    g = g_ref[...]
    sg = g * _sigmoid(g)
    for h in range(dh // HGRN_HEAD):
        sl = slice(h * HGRN_HEAD, (h + 1) * HGRN_HEAD)
        oh = o[:, sl]
        on = oh * lax.rsqrt(jnp.mean(oh * oh, axis=-1, keepdims=True) + NORM_EPS) * hng[...]
        u_o[:, dr + h * HGRN_HEAD:dr + (h + 1) * HGRN_HEAD] = (on * sg[:, sl]).astype(u_o.dtype)


def _mix_out(yf, yr, bonus, gate, of, orv, p_h, ln_g, ln_b, hg, lc, l):
    dr, dh = yf.shape[1], of.shape[1]
    tb = _pick(int(np.gcd(lc, l)), 128, 8)
    off = lc // tb
    hs, hst = _head_indicators(dr)
    rows = lambda wd: pl.BlockSpec((tb, wd), lambda i: (i + off, 0))
    full = lambda a: pl.BlockSpec(a.shape, lambda i: (0,) * a.ndim)
    consts = [ln_g.reshape(1, dr), ln_b.reshape(1, dr), hg.reshape(1, HGRN_HEAD), hs, hst]
    return pl.pallas_call(
        functools.partial(_mix_out_kernel, dr=dr, dh=dh),
        grid=(l // tb,),
        in_specs=[rows(dr)] * 4 + [rows(dh)] * 2 + [pl.BlockSpec((tb, dh), lambda i: (i + off, 4))]
                 + [full(a) for a in consts],
        out_specs=pl.BlockSpec((tb, dr + dh), lambda i: (i, 0)),
        out_shape=jax.ShapeDtypeStruct((l, dr + dh), PROJ_DTYPE),
        compiler_params=_params(("arbitrary",), VMEM_LIMIT),
        name="mix_out",
    )(yf, yr, bonus, gate, of, orv, p_h, *consts)


def _res_mix_kernel(ux_ref, x_ref, mod_ref, gpost_ref, gpre_ref, x1_o, h2_o, *, d):
    ux = ux_ref[...]
    nrm = ux * lax.rsqrt(jnp.mean(ux * ux, axis=-1, keepdims=True) + NORM_EPS) * gpost_ref[...]
    x1 = x_ref[...] + mod_ref[0:1, 2 * d:3 * d] * nrm
    x1_o[...] = x1
    hn = x1 * lax.rsqrt(jnp.mean(x1 * x1, axis=-1, keepdims=True) + NORM_EPS) * gpre_ref[...]
    h2_o[...] = (hn * (1.0 + mod_ref[0:1, 4 * d:5 * d]) + mod_ref[0:1, 3 * d:4 * d]).astype(h2_o.dtype)


def _res_mix(ux, x2, mod, g_post, g_pre):
    l, d = x2.shape
    tb = _pick(l, 256, 8)
    rows = pl.BlockSpec((tb, d), lambda i: (i, 0))
    full = lambda a: pl.BlockSpec(a.shape, lambda i: (0,) * a.ndim)
    gp, gq = g_post.reshape(1, d), g_pre.reshape(1, d)
    return pl.pallas_call(
        functools.partial(_res_mix_kernel, d=d),
        grid=(l // tb,),
        in_specs=[rows, rows, full(mod), full(gp), full(gq)],
        out_specs=[rows, rows],
        out_shape=[jax.ShapeDtypeStruct((l, d), F32), jax.ShapeDtypeStruct((l, d), PROJ_DTYPE)],
        compiler_params=_params(("arbitrary",), VMEM_LIMIT),
        name="residual_mix",
    )(ux, x2, mod, gp, gq)


def _pad_cols(a, width):
    return jnp.pad(a, ((0, 0), (0, width - a.shape[1])))


def _pad_rows(a, height):
    return jnp.pad(a, ((0, height - a.shape[0]), (0, 0)))


def _round_up(n, m):
    return (n + m - 1) // m * m


def kernel(x, c, ctx, c_ctx, w_ada, b_ada, g_mix_pre, g_mix_post, g_ffn_pre, g_ffn_post, w_in, mu_shift, w0, w2, a0,
           a2, g2, k_k, k_a, r_k, ln_x_g, ln_x_b, hgrn_lb_logits, hgrn_norm_g, w_out, w_ff1, w_ff2):
    assert x.shape[0] == 1 and w_in.shape[0] == 1, "single batch, single layer"
    layer = 0
    x2, ctx2 = x[0], ctx[0]
    l, d = x2.shape
    lc = ctx2.shape[0]
    dr, dh = k_k.shape[1], hgrn_lb_logits.shape[2]
    rd_raw, ra_raw, rg_raw = w2.shape[2], a2.shape[2], g2.shape[1]
    rd, ra, rg = (_round_up(v, LANE) for v in (rd_raw, ra_raw, rg_raw))
    assert l % GRID_W == 0 and lc % CHUNK == 0 and dr % GROUP_W == 0 and dh % GROUP_W == 0

    cvec = jnp.concatenate([c, c_ctx[None, :], jnp.zeros((6, d), F32)], axis=0)
    mod = _ada(cvec, w_ada[layer], b_ada[layer])

    wi = w_in[layer]
    mu = mu_shift[layer][None, :]
    sel = (jnp.arange(mu.shape[1], dtype=jnp.int32) % 4)[None, :]
    o = 3 * dr
    cuts = []
    for raw, padded in ((rd_raw, rd), (rd_raw, rd), (ra_raw, ra), (ra_raw, ra), (rg_raw, rg)):
        cuts.append((o, o + raw, padded))
        o += raw
    rw_cols = o
    regroup = lambda a: jnp.concatenate([_pad_cols(a[:, s:e], wd) for s, e, wd in cuts], axis=1)
    w_lora = regroup(wi)
    w2p = jnp.stack([_pad_rows(w2[layer, dd], rd) for dd in range(2)])
    a2p = jnp.stack([_pad_rows(a2[layer, dd], ra) for dd in range(2)])
    g2p = _pad_rows(g2[layer], rg)

    h = _prep(ctx2, x2, g_mix_pre[layer], mod)
    p_main = _mm_wcast(h, wi, 3 * dr, F32, tm_target=768, tn_target=768, name="in_proj_rkv")
    p_lora = _mm_wcast(h, w_lora, w_lora.shape[1], F32, tm_target=768, name="in_proj_lora")

    (r, v, a, lw0, lw1, k0, k1, b0, b1, bonus, gate) = _rwkv_features(
        p_main, p_lora, sel[:, :3 * dr], mu[:, :3 * dr], regroup(sel), regroup(mu), w0[layer], w2p, a0[layer], a2p,
        g2p, k_k[layer], k_a[layer], r_k[layer].reshape(-1), lc, l)
    whole = lambda w: (w, 0, w.shape[1])
    yf, yr, w_h, w_up = _rwkv_scan(r, v, a, lw0, lw1, k0, k1, b0, b1, lc,
                                   [(wi, rw_cols, wi.shape[1] - rw_cols), whole(w_ff1[layer])])
    p_h = _mm(h, w_h, F32, tm_target=768, name="in_proj_hgrn")
    of, orv, w_down, w_o = _hgrn_scan(p_h, hgrn_lb_logits, lc, layer, [whole(w_ff2[layer]), whole(w_out[layer])])

    u = _mix_out(yf, yr, bonus, gate, of, orv, p_h, ln_x_g[layer], ln_x_b[layer], hgrn_norm_g[layer], lc, l)
    ux = _mm(u, w_o, F32, name="out_proj")
    x1, h2 = _res_mix(ux, x2, mod, g_mix_post[layer], g_ffn_pre[layer])
    act = _mm(h2, w_up, PROJ_DTYPE, act="relu2", name="ffn_up")
    out = _residual_projection(act, w_down, x1, mod, g_ffn_post[layer], 5, name="ffn_down")
    return out[None]
```

```python
import functools

import jax
import jax.numpy as jnp
import numpy as np
from jax import lax
from jax.experimental import pallas as pl
from jax.experimental.pallas import tpu as pltpu

F32 = jnp.float32
BF16 = jnp.bfloat16
PROJ_DTYPE = jnp.bfloat16

LANE = 128
GRID_W = 64
CHUNK = GRID_W
RWKV_HEAD = 64
HGRN_HEAD = 128
SUB = 16
GROUP_W = 256
RWKV_GROUPS_PER_STEP = 8
HGRN_GROUPS_PER_STEP = 8
MAX_CHUNK_LOG_DECAY = 60.0
NORM_EPS = 1e-6
GN_EPS = 64e-5
EXP_M05 = float(np.exp(-0.5))
VMEM_LIMIT = 56 * 1024 * 1024

MM_TILE = 1024
IN_PROJ_TILE = 768
SWEEP_ROWS = 512
ROW_BLOCK = 256
EPILOGUE_ROWS = 128
ADA_COLS = 512


def _pick(n, target, unit=LANE):
    best = None
    for m in range(unit, min(n, target) + 1, unit):
        if n % m == 0:
            best = m
    return best if best is not None else n


def _sigmoid(z):
    return 0.5 * jnp.tanh(0.5 * z) + 0.5


def _mxu(a, b):
    return jnp.dot(a, b, preferred_element_type=F32)


def _mxu_nt(a, b):
    return lax.dot_general(a, b, (((1,), (1,)), ((), ())), preferred_element_type=F32)


def _mxu_tn(a, b):
    return lax.dot_general(a, b, (((0,), (0,)), ((), ())), preferred_element_type=F32)


def _split3(z):
    hi = z.astype(BF16)
    rest = z - hi.astype(F32)
    mid = rest.astype(BF16)
    return hi, mid, (rest - mid.astype(F32)).astype(BF16)


def _select_sum(z, sel):
    hi, mid, _ = _split3(z)
    return _mxu(hi, sel) + _mxu(mid, sel)


def _head_sum(z, hs, hst):
    return _select_sum(_select_sum(z, hs), hst)


def _head_indicators(dr):
    hs = np.arange(dr)[:, None] // RWKV_HEAD == np.arange(LANE)[None, :]
    return jnp.asarray(hs, BF16), jnp.asarray(hs.T, BF16)


def _params(sem, vmem=None):
    return pltpu.CompilerParams(dimension_semantics=sem, vmem_limit_bytes=vmem)


def _ada_kernel(c_ref, w_ref, b_ref, o_ref):
    cv = c_ref[...]
    rows = cv.shape[0]
    hi, mid, lo = (t.astype(F32) for t in _split3(cv * _sigmoid(cv)))
    s3 = jnp.concatenate([hi, mid, lo], axis=0).astype(BF16)
    w = w_ref[...]
    w_hi = w.astype(BF16)
    w_lo = (w - w_hi.astype(F32)).astype(BF16)
    p = _mxu(s3, w_hi)
    q = _mxu(s3[:2 * rows], w_lo)
    o_ref[...] = (p[:rows] + p[rows:2 * rows] + p[2 * rows:] + q[:rows] + q[rows:]) + b_ref[...]


def _ada(cvec, w, b):
    rows, d = cvec.shape
    n = w.shape[1]
    tn = _pick(n, ADA_COLS)
    return pl.pallas_call(
        _ada_kernel,
        grid=(n // tn,),
        in_specs=[pl.BlockSpec((rows, d), lambda j: (0, 0)),
                  pl.BlockSpec((d, tn), lambda j: (0, j)),
                  pl.BlockSpec((1, tn), lambda j: (0, j))],
        out_specs=pl.BlockSpec((rows, tn), lambda j: (0, j)),
        out_shape=jax.ShapeDtypeStruct((rows, n), F32),
        compiler_params=_params(("arbitrary",), VMEM_LIMIT),
        name="ada_mod",
    )(cvec, w, b.reshape(1, n))


def _prep_kernel(ctx_ref, x_ref, g_ref, mod_ref, o_ref, *, ncb, d):
    is_ctx = pl.program_id(0) < ncb
    rows = jnp.where(is_ctx, ctx_ref[...], x_ref[...])
    ms = jnp.mean(rows * rows, axis=-1, keepdims=True)
    hn = rows * lax.rsqrt(ms + NORM_EPS) * g_ref[...]
    shift = jnp.where(is_ctx, mod_ref[1:2, 0:d], mod_ref[0:1, 0:d])
    scale = jnp.where(is_ctx, mod_ref[1:2, d:2 * d], mod_ref[0:1, d:2 * d])
    o_ref[...] = (hn * (1.0 + scale) + shift).astype(o_ref.dtype)


def _prep(ctx2, x2, g, mod):
    lc, d = ctx2.shape
    l = x2.shape[0]
    tb = _pick(int(np.gcd(lc, l)), ROW_BLOCK, 8)
    ncb = lc // tb
    return pl.pallas_call(
        functools.partial(_prep_kernel, ncb=ncb, d=d),
        grid=((lc + l) // tb,),
        in_specs=[pl.BlockSpec((tb, d), lambda i: (jnp.minimum(i, ncb - 1), 0)),
                  pl.BlockSpec((tb, d), lambda i: (jnp.maximum(i - ncb, 0), 0)),
                  pl.BlockSpec((1, d), lambda i: (0, 0)),
                  pl.BlockSpec(mod.shape, lambda i: (0, 0))],
        out_specs=pl.BlockSpec((tb, d), lambda i: (i, 0)),
        out_shape=jax.ShapeDtypeStruct((lc + l, d), PROJ_DTYPE),
        compiler_params=_params(("arbitrary",)),
        name="norm_modulate",
    )(ctx2, x2, g.reshape(1, d), mod)


def _mm_kernel(x_ref, w_ref, o_ref, *, act):
    acc = jnp.dot(x_ref[...], w_ref[...], preferred_element_type=F32)
    if act == "relu2":
        acc = jnp.square(jnp.maximum(acc, 0.0))
    o_ref[...] = acc.astype(o_ref.dtype)


def _mm(x, w, out_dtype, act=None, tm_target=MM_TILE, tn_target=MM_TILE, name="matmul"):
    m, k = x.shape
    n = w.shape[1]
    tm = _pick(m, tm_target)
    tn = _pick(n, tn_target)
    return pl.pallas_call(
        functools.partial(_mm_kernel, act=act),
        grid=(m // tm, n // tn),
        in_specs=[pl.BlockSpec((tm, k), lambda i, j: (i, 0)),
                  pl.BlockSpec((k, tn), lambda i, j: (0, j))],
        out_specs=pl.BlockSpec((tm, tn), lambda i, j: (i, j)),
        out_shape=jax.ShapeDtypeStruct((m, n), out_dtype),
        compiler_params=_params(("arbitrary", "arbitrary"), VMEM_LIMIT),
        name=name,
    )(x, w)


def _mm_wcast_kernel(x_ref, w_ref, o_ref, w_cast):
    @pl.when(pl.program_id(1) == 0)
    def _():
        w_cast[...] = w_ref[...].astype(w_cast.dtype)

    o_ref[...] = jnp.dot(x_ref[...], w_cast[...], preferred_element_type=F32).astype(o_ref.dtype)


def _mm_wcast(x, w, ncols, out_dtype, tm_target=MM_TILE, tn_target=MM_TILE, name="matmul_wcast"):
    m, k = x.shape
    tm, tn = _pick(m, tm_target), _pick(ncols, tn_target)
    return pl.pallas_call(
        _mm_wcast_kernel,
        grid=(ncols // tn, m // tm),
        in_specs=[pl.BlockSpec((tm, k), lambda j, i: (i, 0)),
                  pl.BlockSpec((k, tn), lambda j, i: (0, j))],
        out_specs=pl.BlockSpec((tm, tn), lambda j, i: (i, j)),
        out_shape=jax.ShapeDtypeStruct((m, ncols), out_dtype),
        scratch_shapes=[pltpu.VMEM((k, tn), PROJ_DTYPE)],
        compiler_params=_params(("arbitrary", "arbitrary"), VMEM_LIMIT),
        name=name,
    )(x, w)


def _res_proj_kernel(a_ref, w_ref, res_hbm, mod_ref, g_ref, o_ref, res_buf, sem, *, d, gate_chunk):
    i, kk = pl.program_id(0), pl.program_id(1)
    tm = res_buf.shape[0]
    res_copy = pltpu.make_async_copy(res_hbm.at[pl.ds(pl.multiple_of(i * tm, tm), tm), :], res_buf, sem)

    @pl.when(kk == 0)
    def _():
        res_copy.start()
        o_ref[...] = jnp.zeros_like(o_ref)

    o_ref[...] += jnp.dot(a_ref[...], w_ref[...], preferred_element_type=F32)

    @pl.when(kk == pl.num_programs(1) - 1)
    def _():
        res_copy.wait()
        gate = mod_ref[0:1, gate_chunk * d:(gate_chunk + 1) * d] * g_ref[...]
        rb = _pick(tm, EPILOGUE_ROWS, 8)
        for r0 in range(0, tm, rb):
            rows = slice(r0, r0 + rb)
            acc = o_ref[rows, :]
            scale = lax.rsqrt(jnp.mean(acc * acc, axis=-1, keepdims=True) + NORM_EPS)
            o_ref[rows, :] = res_buf[rows, :] + gate * (acc * scale)


def _residual_projection(a, w, res, mod, g_post, gate_chunk, tm_target=SWEEP_ROWS, tk_target=MM_TILE,
                         name="residual_projection"):
    m, k = a.shape
    d = w.shape[1]
    tm, tk = _pick(m, tm_target, 8), _pick(k, tk_target)
    full = lambda z: pl.BlockSpec(z.shape, lambda i, kk: (0,) * z.ndim)
    gp = g_post.reshape(1, d)
    return pl.pallas_call(
        functools.partial(_res_proj_kernel, d=d, gate_chunk=gate_chunk),
        grid=(m // tm, k // tk),
        in_specs=[pl.BlockSpec((tm, tk), lambda i, kk: (i, kk)),
                  pl.BlockSpec((tk, d), lambda i, kk: (kk, 0)),
                  pl.BlockSpec(memory_space=pl.ANY), full(mod), full(gp)],
        out_specs=pl.BlockSpec((tm, d), lambda i, kk: (i, 0)),
        out_shape=jax.ShapeDtypeStruct((m, d), F32),
        scratch_shapes=[pltpu.VMEM((tm, d), F32), pltpu.SemaphoreType.DMA(())],
        compiler_params=_params(("arbitrary", "arbitrary"), VMEM_LIMIT),
        name=name,
    )(a, w, res, mod, gp)


RING = 4


def _rwkv_feat_kernel(pm_hbm, pl_hbm, sel_ref, mu_ref, sell_ref, mul_ref, w0_ref, w2_ref, a0_ref, a2_ref, g2_ref,
                      kk_ref, ka_ref, rk_ref, hs_ref, hst_ref,
                      r_o, v_o, a_o, lw0_o, lw1_o, k0_o, k1_o, b0_o, b1_o, bonus_o, gate_o, ring_m, ring_l, sems,
                      *, ncb, nrows, dr, rd, ra):
    i = pl.program_id(0)
    nb = pl.num_programs(0)
    c = ring_m.shape[1]
    sources = ((pm_hbm, ring_m), (pl_hbm, ring_l))

    def fetch(blk):
        slot = blk % RING
        rows = pl.ds(pl.multiple_of(blk * c, c), c)
        return [pltpu.make_async_copy(src.at[rows, :], ring.at[slot], sems.at[n, slot])
                for n, (src, ring) in enumerate(sources)]

    def start(blk):
        for cp in fetch(blk):
            cp.start()

    def wait(blk):
        for cp in fetch(blk):
            cp.wait()

    @pl.when(i == 0)
    def _():
        start(0)
        start(1)
        start(2)
        wait(0)
        wait(1)

    @pl.when((i > 0) & (i + 1 < nb))
    def _():
        wait(i + 1)

    @pl.when((i > 0) & (i + 2 < nb))
    def _():
        start(i + 2)

    is_ctx = i < ncb
    xi = i - ncb
    s_cur, s_prv, s_nxt = i % RING, jnp.maximum(i - 1, 0) % RING, jnp.minimum(i + 1, nb - 1) % RING

    def shift_lerp(ring, sel, mu):
        cur, prv, nxt = ring[s_cur], ring[s_prv], ring[s_nxt]
        row = lax.broadcasted_iota(jnp.int32, cur.shape, 0)
        prev_last = jnp.where(is_ctx & (i > 0), prv[c - 1:c, :], 0.0)
        next_first = jnp.where(is_ctx & (i < ncb - 1), nxt[0:1, :], 0.0)
        before = jnp.where(row == 0, prev_last, pltpu.roll(cur, 1, 0))
        after = jnp.where(row == c - 1, next_first, pltpu.roll(cur, c - 1, 0))
        above = jnp.where(is_ctx, before, jnp.where(xi > 0, prv, 0.0))
        below = jnp.where(is_ctx, after, jnp.where(xi < nrows - 1, nxt, 0.0))
        shifted = jnp.where(sel == 0, before, jnp.where(sel == 1, after, jnp.where(sel == 2, above, below)))
        return cur + mu * (shifted - cur)

    m = shift_lerp(ring_m, sel_ref[...], mu_ref[...])
    ml = shift_lerp(ring_l, sell_ref[...], mul_ref[...])
    r, k, v = m[:, 0:dr], m[:, dr:2 * dr], m[:, 2 * dr:3 * dr]
    wl = (ml[:, 0:rd], ml[:, rd:2 * rd])
    al = (ml[:, 2 * rd:2 * rd + ra], ml[:, 2 * rd + ra:2 * rd + 2 * ra])
    gl = ml[:, 2 * rd + 2 * ra:]

    hs, hst = hs_ref[...], hst_ref[...]

    def headsum(z):
        return _head_sum(z, hs, hst)

    kkf = k * kk_ref[...]
    kk = kkf * lax.rsqrt(headsum(kkf * kkf) + 1e-12)
    r_o[...] = r
    v_o[...] = v
    a_o[...] = -kk
    for d, (lw_o, k_o, b_o) in enumerate(((lw0_o, k0_o, b0_o), (lw1_o, k1_o, b1_o))):
        wd = _mxu(jnp.tanh(wl[d]).astype(BF16), w2_ref[d]) + w0_ref[d:d + 1, :]
        lw_o[...] = -EXP_M05 * _sigmoid(wd)
        ad = _sigmoid(_mxu(al[d].astype(BF16), a2_ref[d]) + a0_ref[d:d + 1, :])
        k_o[...] = k * (1.0 + (ad - 1.0) * ka_ref[...])
        b_o[...] = kk * ad
    bonus_o[...] = (headsum(r * k * rk_ref[...]) * v).astype(bonus_o.dtype)
    gate_o[...] = _mxu(_sigmoid(gl).astype(BF16), g2_ref[...]).astype(gate_o.dtype)


def _rwkv_features(p_main, p_lora, sel, mu, sel_l, mu_l, w0, w2p, a0, a2p, g2p, k_k, k_a, r_k, lc, l):
    n = p_main.shape[0]
    dr = k_k.shape[0]
    rd, ra = w2p.shape[1], a2p.shape[1]
    nb, ncb = n // CHUNK, lc // CHUNK
    nheads = dr // RWKV_HEAD
    assert nheads <= LANE
    hs, hst = _head_indicators(dr)
    full = lambda a: pl.BlockSpec(a.shape, lambda i: (0,) * a.ndim)
    row1 = lambda a: a.reshape(1, -1)
    consts = [sel, mu, sel_l, mu_l, w0, w2p.astype(BF16), a0, a2p.astype(BF16), g2p.astype(BF16), row1(k_k),
              row1(k_a), row1(r_k), hs, hst]
    assert nb >= 3
    outs = pl.pallas_call(
        functools.partial(_rwkv_feat_kernel, ncb=ncb, nrows=l // CHUNK, dr=dr, rd=rd, ra=ra),
        grid=(nb,),
        in_specs=[pl.BlockSpec(memory_space=pl.ANY)] * 2 + [full(a) for a in consts],
        out_specs=[pl.BlockSpec((CHUNK, dr), lambda i: (i, 0))] * 11,
        out_shape=[jax.ShapeDtypeStruct((n, dr), F32)] * 9 + [jax.ShapeDtypeStruct((n, dr), PROJ_DTYPE)] * 2,
        scratch_shapes=[pltpu.VMEM((RING, CHUNK, p_main.shape[1]), F32), pltpu.VMEM((RING, CHUNK, p_lora.shape[1]), F32),
                        pltpu.SemaphoreType.DMA((2, RING))],
        compiler_params=_params(("arbitrary",), VMEM_LIMIT),
        name="rwkv_features",
    )(p_main, p_lora, *consts)
    return outs


def _block_diag_mask(n, w):
    r = lax.broadcasted_iota(jnp.int32, (n, n), 0)
    c = lax.broadcasted_iota(jnp.int32, (n, n), 1)
    return (r // w) == (c // w)


def _cumsum_rows(z, rev):
    c = z.shape[0]
    t = lax.broadcasted_iota(jnp.int32, (c, c), 0)
    s = lax.broadcasted_iota(jnp.int32, (c, c), 1)
    tri = ((s >= t) if rev else (s <= t)).astype(BF16)
    hi, mid, lo = _split3(z)
    return _mxu(tri, hi) + _mxu(tri, mid) + _mxu(tri, lo)


def _rwkv_chunks(chains):
    c, w = chains[0][0].shape
    nh = w // c
    every = range(len(chains))
    revs = [ch[7] for ch in chains]
    bd = _block_diag_mask(w, c)
    t_i = lax.broadcasted_iota(jnp.int32, (c, w), 0)
    s_i = lax.broadcasted_iota(jnp.int32, (c, w), 1) % c
    eye = (s_i == t_i).astype(F32)
    strict = {False: s_i < t_i, True: s_i > t_i}
    incl = {False: s_i <= t_i, True: s_i >= t_i}

    def expand(z):
        return jnp.where(bd, jnp.concatenate([z] * nh, axis=0), jnp.zeros((), z.dtype))

    def packed_mm(lhs, rhs):
        return _mxu(lhs.astype(BF16), expand(rhs.astype(BF16)))

    bc = [_cumsum_rows(ch[1], ch[7]) for ch in chains]
    ar, bk, v16 = [], [], []
    for i, (r, lw, k, v, a, b, _, _) in enumerate(chains):
        e_in, e_ex, e_ng = jnp.exp(bc[i]), jnp.exp(bc[i] - lw), jnp.exp(-bc[i])
        ar.append(jnp.concatenate([a * e_ex, r * e_in], axis=0).astype(BF16))
        bk.append(jnp.concatenate([b * e_ng, k * e_ng], axis=0).astype(BF16))
        v16.append(v.astype(BF16))
    gb = [_mxu_nt(ar[i], expand(bk[i][:c])) for i in every]
    gk = [_mxu_nt(ar[i], expand(bk[i][c:])) for i in every]
    l_ab = [jnp.where(strict[revs[i]], gb[i][:c], 0.0) for i in every]
    m_rb = [jnp.where(incl[revs[i]], gb[i][c:], 0.0) for i in every]
    lmk = [jnp.concatenate([jnp.where(strict[revs[i]], gk[i][:c], 0.0),
                            jnp.where(incl[revs[i]], gk[i][c:], 0.0)], axis=0) for i in every]
    t_m = [eye + l_ab[i] for i in every]
    p_m = [packed_mm(l_ab[i], l_ab[i]) for i in every]
    for _ in range(int(np.log2(c)) - 2):
        tp = [packed_mm(jnp.concatenate([t_m[i], p_m[i]], axis=0), p_m[i]) for i in every]
        t_m = [t_m[i] + tp[i][:c] for i in every]
        p_m = [tp[i][c:] for i in every]
    t_m = [t_m[i] + packed_mm(t_m[i], p_m[i]) for i in every]

    ars = [_mxu_nt(ar[i], chains[i][6].astype(BF16)) for i in every]
    lm = [_mxu(lmk[i].astype(BF16), expand(v16[i])) for i in every]
    u16 = [packed_mm(t_m[i], ars[i][:c] + lm[i][:c]).astype(BF16) for i in every]
    ys = [ars[i][c:] + _mxu(m_rb[i].astype(BF16), expand(u16[i])) + lm[i][c:] for i in every]
    ds = [_mxu_tn(jnp.concatenate([u16[i], v16[i]], axis=0), bk[i]) for i in every]
    s_new = []
    for i in every:
        last = bc[i][0:1] if revs[i] else bc[i][c - 1:c]
        s_new.append((chains[i][6] + jnp.where(bd, ds[i], 0.0)) * jnp.exp(last))
    return ys, s_new


def _cast_jobs(weights, ng, nb):
    in_specs, out_specs, shapes = [], [], []
    for w, col0, cols in weights:
        rows = w.shape[0]
        n = max(m for m in range(1, ng * nb + 1) if rows % m == 0 and (rows // m) % 16 == 0)
        slab = lambda g, c, n=n: jnp.minimum(g * nb + c, n - 1)
        in_specs.append(pl.BlockSpec((pl.Element(rows // n), pl.Element(cols)),
                                     lambda g, c, s=slab, h=rows // n, col0=col0: (s(g, c) * h, col0)))
        out_specs.append(pl.BlockSpec((rows // n, cols), lambda g, c, s=slab: (s(g, c), 0)))
        shapes.append(jax.ShapeDtypeStruct((rows, cols), PROJ_DTYPE))
    return in_specs, out_specs, shapes


def _run_cast_jobs(src_refs, dst_refs):
    for src, dst in zip(src_refs, dst_refs):
        dst[...] = src[...].astype(dst.dtype)


def _rwkv_scan_kernel(rf, lwf, kf, vf, af, bf, rr, lwr, kr, vr, ar, br, *rest):
    n_cast = (len(rest) - 3) // 2
    yf_o, yr_o = rest[n_cast:n_cast + 2]
    s_ref = rest[-1]

    @pl.when(pl.program_id(1) == 0)
    def _():
        s_ref[...] = jnp.zeros_like(s_ref)

    ngs = rf.shape[1] // GROUP_W
    chains = []
    for d, refs in enumerate(((rf, lwf, kf, vf, af, bf), (rr, lwr, kr, vr, ar, br))):
        for g in range(ngs):
            sl = slice(g * GROUP_W, (g + 1) * GROUP_W)
            chains.append(tuple(ref[:, sl] for ref in refs) + (s_ref[d, g], d == 1))
    ys, s_new = _rwkv_chunks(chains)
    _run_cast_jobs(rest[:n_cast], rest[n_cast + 2:-1])
    for d, o_ref in enumerate((yf_o, yr_o)):
        for g in range(ngs):
            o_ref[:, g * GROUP_W:(g + 1) * GROUP_W] = ys[d * ngs + g]
            s_ref[d, g] = s_new[d * ngs + g]


def _rev_chunk(c, ncb, nb):
    return jnp.where(c < ncb, ncb - 1 - c, nb - 1 - (c - ncb))


def _rwkv_scan(r, v, a, lw0, lw1, k0, k1, b0, b1, lc, cast_weights):
    n, dr = r.shape
    nb, ncb = n // CHUNK, lc // CHUNK
    bw = _pick(dr, RWKV_GROUPS_PER_STEP * GROUP_W, GROUP_W)
    fwd = pl.BlockSpec((CHUNK, bw), lambda g, c: (c, g))
    rev = pl.BlockSpec((CHUNK, bw), lambda g, c: (_rev_chunk(c, ncb, nb), g))
    cast_in, cast_out, cast_shapes = _cast_jobs(cast_weights, dr // bw, nb)
    return pl.pallas_call(
        _rwkv_scan_kernel,
        grid=(dr // bw, nb),
        in_specs=[fwd] * 6 + [rev] * 6 + cast_in,
        out_specs=[fwd, rev] + cast_out,
        out_shape=[jax.ShapeDtypeStruct((n, dr), F32)] * 2 + cast_shapes,
        scratch_shapes=[pltpu.VMEM((2, bw // GROUP_W, GROUP_W, GROUP_W), F32)],
        compiler_params=_params(("arbitrary", "arbitrary"), VMEM_LIMIT),
        name="rwkv_scan",
    )(r, lw0, k0, v, a, b0, r, lw1, k1, v, a, b1, *[w for w, _, _ in cast_weights])


def _hgrn_chunk(q, k, v, lf, s_vk, rev):
    c, w = q.shape
    nsb = c // SUB
    bc = _cumsum_rows(lf, rev)
    last = bc[0:1] if rev else bc[c - 1:c]
    bdh = _block_diag_mask(w, HGRN_HEAD)
    v16 = v.astype(BF16)
    o_inter = _mxu_nt((q * jnp.exp(bc)).astype(BF16), s_vk.astype(BF16))
    s_new = s_vk * jnp.exp(last) + jnp.where(bdh, _mxu_tn(v16, (k * jnp.exp(last - bc)).astype(BF16)), 0.0)

    lane2 = lax.broadcasted_iota(jnp.int32, (c, w), 1)
    row2 = lax.broadcasted_iota(jnp.int32, (c, w), 0)
    zpad = jnp.zeros((HGRN_HEAD - c, w), BF16)

    def expand(z):
        parts = []
        for h in range(w // HGRN_HEAD):
            parts += [jnp.where(lane2 // HGRN_HEAD == h, z, jnp.zeros((), BF16)), zpad]
        return jnp.concatenate(parts, axis=0)

    ones_bd = bdh.astype(BF16)
    t3 = lax.broadcasted_iota(jnp.int32, (SUB, SUB, w), 0)
    s3 = lax.broadcasted_iota(jnp.int32, (SUB, SUB, w), 1)
    l3 = lax.broadcasted_iota(jnp.int32, (SUB, SUB, w), 2) % HGRN_HEAD
    causal = (s3 >= t3) if rev else (s3 <= t3)
    a_rows = []
    for i in range(nsb):
        lo, hi = SUB * i, SUB * (i + 1)
        qi, ki, bi = q[lo:hi], k[lo:hi], bc[lo:hi]
        d3 = bi[:, None, :] - bi[None, :, :]
        x3 = jnp.where(causal, (qi[:, None, :] * ki[None, :, :]) * jnp.exp(jnp.minimum(d3, 0.0)), 0.0)
        r3 = _mxu(x3.reshape(SUB * SUB, w).astype(BF16), ones_bd).reshape(SUB, SUB, w)
        a_i = jnp.sum(jnp.where(l3 == s3 + lo, r3, 0.0), axis=1)
        if (not rev and i > 0) or (rev and i < nsb - 1):
            beta = bc[hi:hi + 1] if rev else bc[lo - 1:lo]
            earlier = (row2 >= hi) if rev else (row2 < lo)
            kp = jnp.where(earlier, k * jnp.exp(jnp.minimum(beta - bc, 0.0)), 0.0)
            a_i = a_i + _mxu_nt((qi * jnp.exp(bi - beta)).astype(BF16), expand(kp.astype(BF16)))
        a_rows.append(a_i)
    o = o_inter + _mxu(jnp.concatenate(a_rows, axis=0).astype(BF16), expand(v16))
    return o, s_new


def _hgrn_chunks_bounded(chains):
    c, w = chains[0][0].shape
    every = range(len(chains))
    revs = [ch[5] for ch in chains]
    bdh = _block_diag_mask(w, HGRN_HEAD)
    lane2 = lax.broadcasted_iota(jnp.int32, (c, w), 1)
    t_i = lax.broadcasted_iota(jnp.int32, (c, w), 0)
    s_i = lane2 % HGRN_HEAD
    incl = {False: s_i <= t_i, True: s_i >= t_i}
    zpad = jnp.zeros((HGRN_HEAD - c, w), BF16)

    def expand(z):
        parts = []
        for h in range(w // HGRN_HEAD):
            parts += [jnp.where(lane2 // HGRN_HEAD == h, z, jnp.zeros((), BF16)), zpad]
        return jnp.concatenate(parts, axis=0)

    bc = [_cumsum_rows(ch[3], ch[5]) for ch in chains]
    last = [bc[i][0:1] if revs[i] else bc[i][c - 1:c] for i in every]
    qt = [(chains[i][0] * jnp.exp(bc[i])).astype(BF16) for i in every]
    kt = [(chains[i][1] * jnp.exp(-bc[i])).astype(BF16) for i in every]
    kd = [(chains[i][1] * jnp.exp(last[i] - bc[i])).astype(BF16) for i in every]
    v16 = [chains[i][2].astype(BF16) for i in every]
    o_inter = [_mxu_nt(qt[i], chains[i][4].astype(BF16)) for i in every]
    a = [jnp.where(incl[revs[i]], _mxu_nt(qt[i], expand(kt[i])), 0.0).astype(BF16) for i in every]
    o = [o_inter[i] + _mxu(a[i], expand(v16[i])) for i in every]
    ds = [_mxu_tn(v16[i], kd[i]) for i in every]
    s_new = [chains[i][4] * jnp.exp(last[i]) + jnp.where(bdh, ds[i], 0.0) for i in every]
    return o, s_new


def _hgrn_scan_kernel(qf, ff, vf, qr, fr, vr, lg_ref, *rest, layer):
    n_cast = (len(rest) - 3) // 2
    of_o, or_o = rest[n_cast:n_cast + 2]
    s_ref = rest[-1]

    @pl.when(pl.program_id(1) == 0)
    def _():
        s_ref[...] = jnp.zeros_like(s_ref)

    lg = lg_ref[...]
    e = jnp.exp(lg - jnp.max(lg, axis=0, keepdims=True))
    lb = jnp.sum(e[:layer + 1], axis=0) / jnp.sum(e, axis=0)
    ngs = qf.shape[1] // GROUP_W
    chains = []
    min_total = None
    for d, (q_ref, f_ref, v_ref) in enumerate(((qf, ff, vf), (qr, fr, vr))):
        fd = lb[d:d + 1, :] + (1.0 - lb[d:d + 1, :]) * _sigmoid(f_ref[...])
        qv = q_ref[...]
        qh, kh, vh, lf = qv * _sigmoid(qv), 1.0 - fd, v_ref[...], jnp.log(fd)
        total = jnp.min(jnp.sum(lf, axis=0, keepdims=True))
        min_total = total if min_total is None else jnp.minimum(min_total, total)
        for g in range(ngs):
            sl = slice(g * GROUP_W, (g + 1) * GROUP_W)
            chains.append((qh[:, sl], kh[:, sl], vh[:, sl], lf[:, sl], s_ref[d, g], d == 1))

    def emit(o, s_new):
        for d, o_ref in enumerate((of_o, or_o)):
            for g in range(ngs):
                o_ref[:, g * GROUP_W:(g + 1) * GROUP_W] = o[d * ngs + g]
                s_ref[d, g] = s_new[d * ngs + g]
        _run_cast_jobs(rest[:n_cast], rest[n_cast + 2:-1])

    bounded = min_total >= -MAX_CHUNK_LOG_DECAY

    @pl.when(bounded)
    def _():
        emit(*_hgrn_chunks_bounded(chains))

    @pl.when(jnp.logical_not(bounded))
    def _():
        res = [_hgrn_chunk(*ch) for ch in chains]
        emit([r[0] for r in res], [r[1] for r in res])


def _hgrn_scan(p_h, lb_logits, lc, layer, cast_weights):
    n = p_h.shape[0]
    dh = lb_logits.shape[2]
    nb, ncb = n // CHUNK, lc // CHUNK
    bw = _pick(dh, HGRN_GROUPS_PER_STEP * GROUP_W, GROUP_W)
    ng = dh // bw
    blk = (CHUNK, bw)
    fwd = lambda sec: pl.BlockSpec(blk, lambda g, c: (c, sec * ng + g))
    rev = lambda sec: pl.BlockSpec(blk, lambda g, c: (_rev_chunk(c, ncb, nb), sec * ng + g))
    cast_in, cast_out, cast_shapes = _cast_jobs(cast_weights, ng, nb)
    return pl.pallas_call(
        functools.partial(_hgrn_scan_kernel, layer=layer),
        grid=(ng, nb),
        in_specs=[fwd(0), fwd(1), fwd(3), rev(0), rev(2), rev(3),
                  pl.BlockSpec((lb_logits.shape[0], 2, bw), lambda g, c: (0, 0, g))] + cast_in,
        out_specs=[pl.BlockSpec(blk, lambda g, c: (c, g)),
                   pl.BlockSpec(blk, lambda g, c: (_rev_chunk(c, ncb, nb), g))] + cast_out,
        out_shape=[jax.ShapeDtypeStruct((n, dh), F32)] * 2 + cast_shapes,
        scratch_shapes=[pltpu.VMEM((2, bw // GROUP_W, GROUP_W, GROUP_W), F32)],
        compiler_params=_params(("arbitrary", "arbitrary"), VMEM_LIMIT),
        name="hgrn_scan",
    )(p_h, p_h, p_h, p_h, p_h, p_h, lb_logits, *[w for w, _, _ in cast_weights])


def _mix_out_kernel(yf, yr, bonus, gate, of, orv, g_ref, lng, lnb, hng, hs_ref, hst_ref, u_o, *, dr, dh):
    hs, hst = hs_ref[...], hst_ref[...]

    def headmean(z):
        return _head_sum(z, hs, hst) * (1.0 / RWKV_HEAD)

    y = yf[...] + yr[...]
    yc = y - headmean(y)
    yn = yc * lax.rsqrt(headmean(yc * yc) + GN_EPS) * lng[...] + lnb[...]
    u_o[:, 0:dr] = ((yn + bonus[...].astype(F32)) * gate[...].astype(F32)).astype(u_o.dtype)
    o = of[...] + orv[...]
    g = g_ref[...]
    sg = g * _sigmoid(g)
    for h in range(dh // HGRN_HEAD):
        sl = slice(h * HGRN_HEAD, (h + 1) * HGRN_HEAD)
        oh = o[:, sl]
        on = oh * lax.rsqrt(jnp.mean(oh * oh, axis=-1, keepdims=True) + NORM_EPS) * hng[...]
        u_o[:, dr + h * HGRN_HEAD:dr + (h + 1) * HGRN_HEAD] = (on * sg[:, sl]).astype(u_o.dtype)


def _mix_out(yf, yr, bonus, gate, of, orv, p_h, ln_g, ln_b, hg, lc, l):
    dr, dh = yf.shape[1], of.shape[1]
    tb = _pick(int(np.gcd(lc, l)), EPILOGUE_ROWS, 8)
    off = lc // tb
    hs, hst = _head_indicators(dr)
    rows = lambda wd: pl.BlockSpec((tb, wd), lambda i: (i + off, 0))
    full = lambda a: pl.BlockSpec(a.shape, lambda i: (0,) * a.ndim)
    consts = [ln_g.reshape(1, dr), ln_b.reshape(1, dr), hg.reshape(1, HGRN_HEAD), hs, hst]
    return pl.pallas_call(
        functools.partial(_mix_out_kernel, dr=dr, dh=dh),
        grid=(l // tb,),
        in_specs=[rows(dr)] * 4 + [rows(dh)] * 2 + [pl.BlockSpec((tb, dh), lambda i: (i + off, 4))]
                 + [full(a) for a in consts],
        out_specs=pl.BlockSpec((tb, dr + dh), lambda i: (i, 0)),
        out_shape=jax.ShapeDtypeStruct((l, dr + dh), PROJ_DTYPE),
        compiler_params=_params(("arbitrary",), VMEM_LIMIT),
        name="mix_out",
    )(yf, yr, bonus, gate, of, orv, p_h, *consts)


def _res_mix_kernel(ux_ref, x_ref, mod_ref, gpost_ref, gpre_ref, x1_o, h2_o, *, d):
    ux = ux_ref[...]
    nrm = ux * lax.rsqrt(jnp.mean(ux * ux, axis=-1, keepdims=True) + NORM_EPS) * gpost_ref[...]
    x1 = x_ref[...] + mod_ref[0:1, 2 * d:3 * d] * nrm
    x1_o[...] = x1
    hn = x1 * lax.rsqrt(jnp.mean(x1 * x1, axis=-1, keepdims=True) + NORM_EPS) * gpre_ref[...]
    h2_o[...] = (hn * (1.0 + mod_ref[0:1, 4 * d:5 * d]) + mod_ref[0:1, 3 * d:4 * d]).astype(h2_o.dtype)


def _res_mix(ux, x2, mod, g_post, g_pre):
    l, d = x2.shape
    tb = _pick(l, ROW_BLOCK, 8)
    rows = pl.BlockSpec((tb, d), lambda i: (i, 0))
    full = lambda a: pl.BlockSpec(a.shape, lambda i: (0,) * a.ndim)
    gp, gq = g_post.reshape(1, d), g_pre.reshape(1, d)
    return pl.pallas_call(
        functools.partial(_res_mix_kernel, d=d),
        grid=(l // tb,),
        in_specs=[rows, rows, full(mod), full(gp), full(gq)],
        out_specs=[rows, rows],
        out_shape=[jax.ShapeDtypeStruct((l, d), F32), jax.ShapeDtypeStruct((l, d), PROJ_DTYPE)],
        compiler_params=_params(("arbitrary",), VMEM_LIMIT),
        name="residual_mix",
    )(ux, x2, mod, gp, gq)


def _pad_cols(a, width):
    return jnp.pad(a, ((0, 0), (0, width - a.shape[1])))


def _pad_rows(a, height):
    return jnp.pad(a, ((0, height - a.shape[0]), (0, 0)))


def _round_up(n, m):
    return (n + m - 1) // m * m


def kernel(x, c, ctx, c_ctx, w_ada, b_ada, g_mix_pre, g_mix_post, g_ffn_pre, g_ffn_post, w_in, mu_shift, w0, w2, a0,
           a2, g2, k_k, k_a, r_k, ln_x_g, ln_x_b, hgrn_lb_logits, hgrn_norm_g, w_out, w_ff1, w_ff2):
    assert x.shape[0] == 1 and w_in.shape[0] == 1, "single batch, single layer"
    layer = 0
    x2, ctx2 = x[0], ctx[0]
    l, d = x2.shape
    lc = ctx2.shape[0]
    dr, dh = k_k.shape[1], hgrn_lb_logits.shape[2]
    rd_raw, ra_raw, rg_raw = w2.shape[2], a2.shape[2], g2.shape[1]
    rd, ra, rg = (_round_up(v, LANE) for v in (rd_raw, ra_raw, rg_raw))
    assert l % GRID_W == 0 and lc % CHUNK == 0 and dr % GROUP_W == 0 and dh % GROUP_W == 0

    cvec = jnp.concatenate([c, c_ctx[None, :], jnp.zeros((6, d), F32)], axis=0)
    mod = _ada(cvec, w_ada[layer], b_ada[layer])

    wi = w_in[layer]
    mu = mu_shift[layer][None, :]
    sel = (jnp.arange(mu.shape[1], dtype=jnp.int32) % 4)[None, :]
    o = 3 * dr
    cuts = []
    for raw, padded in ((rd_raw, rd), (rd_raw, rd), (ra_raw, ra), (ra_raw, ra), (rg_raw, rg)):
        cuts.append((o, o + raw, padded))
        o += raw
    rw_cols = o
    regroup = lambda a: jnp.concatenate([_pad_cols(a[:, s:e], wd) for s, e, wd in cuts], axis=1)
    w_lora = regroup(wi)
    w2p = jnp.stack([_pad_rows(w2[layer, dd], rd) for dd in range(2)])
    a2p = jnp.stack([_pad_rows(a2[layer, dd], ra) for dd in range(2)])
    g2p = _pad_rows(g2[layer], rg)

    h = _prep(ctx2, x2, g_mix_pre[layer], mod)
    p_main = _mm_wcast(h, wi, 3 * dr, F32, tm_target=IN_PROJ_TILE, tn_target=IN_PROJ_TILE, name="in_proj_rkv")
    p_lora = _mm_wcast(h, w_lora, w_lora.shape[1], F32, tm_target=IN_PROJ_TILE, name="in_proj_lora")

    (r, v, a, lw0, lw1, k0, k1, b0, b1, bonus, gate) = _rwkv_features(
        p_main, p_lora, sel[:, :3 * dr], mu[:, :3 * dr], regroup(sel), regroup(mu), w0[layer], w2p, a0[layer], a2p,
        g2p, k_k[layer], k_a[layer], r_k[layer].reshape(-1), lc, l)
    whole = lambda w: (w, 0, w.shape[1])
    yf, yr, w_h, w_up = _rwkv_scan(r, v, a, lw0, lw1, k0, k1, b0, b1, lc,
                                   [(wi, rw_cols, wi.shape[1] - rw_cols), whole(w_ff1[layer])])
    p_h = _mm(h, w_h, F32, tm_target=IN_PROJ_TILE, name="in_proj_hgrn")
    of, orv, w_down, w_o = _hgrn_scan(p_h, hgrn_lb_logits, lc, layer, [whole(w_ff2[layer]), whole(w_out[layer])])

    u = _mix_out(yf, yr, bonus, gate, of, orv, p_h, ln_x_g[layer], ln_x_b[layer], hgrn_norm_g[layer], lc, l)
    ux = _mm(u, w_o, F32, name="out_proj")
    x1, h2 = _res_mix(ux, x2, mod, g_mix_post[layer], g_ffn_pre[layer])
    act = _mm(h2, w_up, PROJ_DTYPE, act="relu2", name="ffn_up")
    out = _residual_projection(act, w_down, x1, mod, g_ffn_post[layer], 5, name="ffn_down")
    return out[None]
```

```python
import functools

import jax
import jax.numpy as jnp
import numpy as np
from jax import lax
from jax.experimental import pallas as pl
from jax.experimental.pallas import tpu as pltpu

F32 = jnp.float32
BF16 = jnp.bfloat16
PROJ_DTYPE = jnp.bfloat16

LANE = 128
GRID_W = 64
CHUNK = GRID_W
RWKV_HEAD = 64
HGRN_HEAD = 128
SUB = 16
GROUP_W = 256
RWKV_GROUPS_PER_STEP = 8
HGRN_GROUPS_PER_STEP = 8
MAX_CHUNK_LOG_DECAY = 60.0
NORM_EPS = 1e-6
GN_EPS = 64e-5
EXP_M05 = float(np.exp(-0.5))
VMEM_LIMIT = 56 * 1024 * 1024

MM_TILE = 1024
IN_PROJ_TILE = 768
SWEEP_ROWS = 512
ROW_BLOCK = 256
EPILOGUE_ROWS = 128
ADA_COLS = 512


def _pick(n, target, unit=LANE):
    best = None
    for m in range(unit, min(n, target) + 1, unit):
        if n % m == 0:
            best = m
    return best if best is not None else n


def _sigmoid(z):
    return 0.5 * jnp.tanh(0.5 * z) + 0.5


def _mxu(a, b):
    return jnp.dot(a, b, preferred_element_type=F32)


def _mxu_nt(a, b):
    return lax.dot_general(a, b, (((1,), (1,)), ((), ())), preferred_element_type=F32)


def _mxu_tn(a, b):
    return lax.dot_general(a, b, (((0,), (0,)), ((), ())), preferred_element_type=F32)


def _split3(z):
    hi = z.astype(BF16)
    rest = z - hi.astype(F32)
    mid = rest.astype(BF16)
    return hi, mid, (rest - mid.astype(F32)).astype(BF16)


def _select_sum(z, sel):
    hi, mid, _ = _split3(z)
    return _mxu(hi, sel) + _mxu(mid, sel)


def _head_sum(z, hs, hst):
    return _select_sum(_select_sum(z, hs), hst)


def _head_indicators(dr):
    hs = np.arange(dr)[:, None] // RWKV_HEAD == np.arange(LANE)[None, :]
    return jnp.asarray(hs, BF16), jnp.asarray(hs.T, BF16)


def _params(sem, vmem=None):
    return pltpu.CompilerParams(dimension_semantics=sem, vmem_limit_bytes=vmem)


def _ada_kernel(c_ref, w_ref, b_ref, o_ref):
    cv = c_ref[...]
    rows = cv.shape[0]
    hi, mid, lo = (t.astype(F32) for t in _split3(cv * _sigmoid(cv)))
    s3 = jnp.concatenate([hi, mid, lo], axis=0).astype(BF16)
    w = w_ref[...]
    w_hi = w.astype(BF16)
    w_lo = (w - w_hi.astype(F32)).astype(BF16)
    p = _mxu(s3, w_hi)
    q = _mxu(s3[:2 * rows], w_lo)
    o_ref[...] = (p[:rows] + p[rows:2 * rows] + p[2 * rows:] + q[:rows] + q[rows:]) + b_ref[...]


def _ada(cvec, w, b):
    rows, d = cvec.shape
    n = w.shape[1]
    tn = _pick(n, ADA_COLS)
    return pl.pallas_call(
        _ada_kernel,
        grid=(n // tn,),
        in_specs=[pl.BlockSpec((rows, d), lambda j: (0, 0)),
                  pl.BlockSpec((d, tn), lambda j: (0, j)),
                  pl.BlockSpec((1, tn), lambda j: (0, j))],
        out_specs=pl.BlockSpec((rows, tn), lambda j: (0, j)),
        out_shape=jax.ShapeDtypeStruct((rows, n), F32),
        compiler_params=_params(("arbitrary",), VMEM_LIMIT),
        name="ada_mod",
    )(cvec, w, b.reshape(1, n))


def _prep_kernel(ctx_ref, x_ref, g_ref, mod_ref, o_ref, *, ncb, d):
    is_ctx = pl.program_id(0) < ncb
    rows = jnp.where(is_ctx, ctx_ref[...], x_ref[...])
    ms = jnp.mean(rows * rows, axis=-1, keepdims=True)
    hn = rows * lax.rsqrt(ms + NORM_EPS) * g_ref[...]
    shift = jnp.where(is_ctx, mod_ref[1:2, 0:d], mod_ref[0:1, 0:d])
    scale = jnp.where(is_ctx, mod_ref[1:2, d:2 * d], mod_ref[0:1, d:2 * d])
    o_ref[...] = (hn * (1.0 + scale) + shift).astype(o_ref.dtype)


def _prep(ctx2, x2, g, mod):
    lc, d = ctx2.shape
    l = x2.shape[0]
    tb = _pick(int(np.gcd(lc, l)), ROW_BLOCK, 8)
    ncb = lc // tb
    return pl.pallas_call(
        functools.partial(_prep_kernel, ncb=ncb, d=d),
        grid=((lc + l) // tb,),
        in_specs=[pl.BlockSpec((tb, d), lambda i: (jnp.minimum(i, ncb - 1), 0)),
                  pl.BlockSpec((tb, d), lambda i: (jnp.maximum(i - ncb, 0), 0)),
                  pl.BlockSpec((1, d), lambda i: (0, 0)),
                  pl.BlockSpec(mod.shape, lambda i: (0, 0))],
        out_specs=pl.BlockSpec((tb, d), lambda i: (i, 0)),
        out_shape=jax.ShapeDtypeStruct((lc + l, d), PROJ_DTYPE),
        compiler_params=_params(("arbitrary",)),
        name="norm_modulate",
    )(ctx2, x2, g.reshape(1, d), mod)


def _mm_kernel(x_ref, w_ref, o_ref, *, act):
    acc = jnp.dot(x_ref[...], w_ref[...], preferred_element_type=F32)
    if act == "relu2":
        acc = jnp.square(jnp.maximum(acc, 0.0))
    o_ref[...] = acc.astype(o_ref.dtype)


def _mm(x, w, out_dtype, act=None, tm_target=MM_TILE, tn_target=MM_TILE, name="matmul"):
    m, k = x.shape
    n = w.shape[1]
    tm = _pick(m, tm_target)
    tn = _pick(n, tn_target)
    return pl.pallas_call(
        functools.partial(_mm_kernel, act=act),
        grid=(m // tm, n // tn),
        in_specs=[pl.BlockSpec((tm, k), lambda i, j: (i, 0)),
                  pl.BlockSpec((k, tn), lambda i, j: (0, j))],
        out_specs=pl.BlockSpec((tm, tn), lambda i, j: (i, j)),
        out_shape=jax.ShapeDtypeStruct((m, n), out_dtype),
        compiler_params=_params(("arbitrary", "arbitrary"), VMEM_LIMIT),
        name=name,
    )(x, w)


def _mm_wcast_kernel(x_ref, w_ref, o_ref, w_cast):
    @pl.when(pl.program_id(1) == 0)
    def _():
        w_cast[...] = w_ref[...].astype(w_cast.dtype)

    o_ref[...] = jnp.dot(x_ref[...], w_cast[...], preferred_element_type=F32).astype(o_ref.dtype)


def _mm_wcast(x, w, ncols, out_dtype, tm_target=MM_TILE, tn_target=MM_TILE, name="matmul_wcast"):
    m, k = x.shape
    tm, tn = _pick(m, tm_target), _pick(ncols, tn_target)
    return pl.pallas_call(
        _mm_wcast_kernel,
        grid=(ncols // tn, m // tm),
        in_specs=[pl.BlockSpec((tm, k), lambda j, i: (i, 0)),
                  pl.BlockSpec((k, tn), lambda j, i: (0, j))],
        out_specs=pl.BlockSpec((tm, tn), lambda j, i: (i, j)),
        out_shape=jax.ShapeDtypeStruct((m, ncols), out_dtype),
        scratch_shapes=[pltpu.VMEM((k, tn), PROJ_DTYPE)],
        compiler_params=_params(("arbitrary", "arbitrary"), VMEM_LIMIT),
        name=name,
    )(x, w)


def _res_proj_kernel(a_ref, w_ref, res_hbm, mod_ref, g_ref, o_ref, res_buf, sem, *, d, gate_chunk):
    i, kk = pl.program_id(0), pl.program_id(1)
    tm = res_buf.shape[0]
    res_copy = pltpu.make_async_copy(res_hbm.at[pl.ds(pl.multiple_of(i * tm, tm), tm), :], res_buf, sem)

    @pl.when(kk == 0)
    def _():
        res_copy.start()
        o_ref[...] = jnp.zeros_like(o_ref)

    o_ref[...] += jnp.dot(a_ref[...], w_ref[...], preferred_element_type=F32)

    @pl.when(kk == pl.num_programs(1) - 1)
    def _():
        res_copy.wait()
        gate = mod_ref[0:1, gate_chunk * d:(gate_chunk + 1) * d] * g_ref[...]
        rb = _pick(tm, EPILOGUE_ROWS, 8)
        for r0 in range(0, tm, rb):
            rows = slice(r0, r0 + rb)
            acc = o_ref[rows, :]
            scale = lax.rsqrt(jnp.mean(acc * acc, axis=-1, keepdims=True) + NORM_EPS)
            o_ref[rows, :] = res_buf[rows, :] + gate * (acc * scale)


def _residual_projection(a, w, res, mod, g_post, gate_chunk, tm_target=SWEEP_ROWS, tk_target=MM_TILE,
                         name="residual_projection"):
    m, k = a.shape
    d = w.shape[1]
    tm, tk = _pick(m, tm_target, 8), _pick(k, tk_target)
    full = lambda z: pl.BlockSpec(z.shape, lambda i, kk: (0,) * z.ndim)
    gp = g_post.reshape(1, d)
    return pl.pallas_call(
        functools.partial(_res_proj_kernel, d=d, gate_chunk=gate_chunk),
        grid=(m // tm, k // tk),
        in_specs=[pl.BlockSpec((tm, tk), lambda i, kk: (i, kk)),
                  pl.BlockSpec((tk, d), lambda i, kk: (kk, 0)),
                  pl.BlockSpec(memory_space=pl.ANY), full(mod), full(gp)],
        out_specs=pl.BlockSpec((tm, d), lambda i, kk: (i, 0)),
        out_shape=jax.ShapeDtypeStruct((m, d), F32),
        scratch_shapes=[pltpu.VMEM((tm, d), F32), pltpu.SemaphoreType.DMA(())],
        compiler_params=_params(("arbitrary", "arbitrary"), VMEM_LIMIT),
        name=name,
    )(a, w, res, mod, gp)


RING = 4


def _rwkv_feat_kernel(pm_hbm, pl_hbm, sel_ref, mu_ref, sell_ref, mul_ref, w0_ref, w2_ref, a0_ref, a2_ref, g2_ref,
                      kk_ref, ka_ref, rk_ref, hs_ref, hst_ref,
                      r_o, v_o, a_o, lw0_o, lw1_o, k0_o, k1_o, b0_o, b1_o, bonus_o, gate_o, ring_m, ring_l, sems,
                      *, ncb, nrows, dr, rd, ra):
    i = pl.program_id(0)
    nb = pl.num_programs(0)
    c = ring_m.shape[1]
    sources = ((pm_hbm, ring_m), (pl_hbm, ring_l))

    def fetch(blk):
        slot = blk % RING
        rows = pl.ds(pl.multiple_of(blk * c, c), c)
        return [pltpu.make_async_copy(src.at[rows, :], ring.at[slot], sems.at[n, slot])
                for n, (src, ring) in enumerate(sources)]

    def start(blk):
        for cp in fetch(blk):
            cp.start()

    def wait(blk):
        for cp in fetch(blk):
            cp.wait()

    @pl.when(i == 0)
    def _():
        start(0)
        start(1)
        start(2)
        wait(0)
        wait(1)

    @pl.when((i > 0) & (i + 1 < nb))
    def _():
        wait(i + 1)

    @pl.when((i > 0) & (i + 2 < nb))
    def _():
        start(i + 2)

    is_ctx = i < ncb
    xi = i - ncb
    s_cur, s_prv, s_nxt = i % RING, jnp.maximum(i - 1, 0) % RING, jnp.minimum(i + 1, nb - 1) % RING

    def shift_lerp(ring, sel, mu):
        cur, prv, nxt = ring[s_cur], ring[s_prv], ring[s_nxt]
        row = lax.broadcasted_iota(jnp.int32, cur.shape, 0)
        prev_last = jnp.where(is_ctx & (i > 0), prv[c - 1:c, :], 0.0)
        next_first = jnp.where(is_ctx & (i < ncb - 1), nxt[0:1, :], 0.0)
        before = jnp.where(row == 0, prev_last, pltpu.roll(cur, 1, 0))
        after = jnp.where(row == c - 1, next_first, pltpu.roll(cur, c - 1, 0))
        above = jnp.where(is_ctx, before, jnp.where(xi > 0, prv, 0.0))
        below = jnp.where(is_ctx, after, jnp.where(xi < nrows - 1, nxt, 0.0))
        shifted = jnp.where(sel == 0, before, jnp.where(sel == 1, after, jnp.where(sel == 2, above, below)))
        return cur + mu * (shifted - cur)

    m = shift_lerp(ring_m, sel_ref[...], mu_ref[...])
    ml = shift_lerp(ring_l, sell_ref[...], mul_ref[...])
    r, k, v = m[:, 0:dr], m[:, dr:2 * dr], m[:, 2 * dr:3 * dr]
    wl = (ml[:, 0:rd], ml[:, rd:2 * rd])
    al = (ml[:, 2 * rd:2 * rd + ra], ml[:, 2 * rd + ra:2 * rd + 2 * ra])
    gl = ml[:, 2 * rd + 2 * ra:]

    hs, hst = hs_ref[...], hst_ref[...]

    def headsum(z):
        return _head_sum(z, hs, hst)

    kkf = k * kk_ref[...]
    kk = kkf * lax.rsqrt(headsum(kkf * kkf) + 1e-12)
    r_o[...] = r
    v_o[...] = v
    a_o[...] = -kk
    for d, (lw_o, k_o, b_o) in enumerate(((lw0_o, k0_o, b0_o), (lw1_o, k1_o, b1_o))):
        wd = _mxu(jnp.tanh(wl[d]).astype(BF16), w2_ref[d]) + w0_ref[d:d + 1, :]
        lw_o[...] = -EXP_M05 * _sigmoid(wd)
        ad = _sigmoid(_mxu(al[d].astype(BF16), a2_ref[d]) + a0_ref[d:d + 1, :])
        k_o[...] = k * (1.0 + (ad - 1.0) * ka_ref[...])
        b_o[...] = kk * ad
    bonus_o[...] = (headsum(r * k * rk_ref[...]) * v).astype(bonus_o.dtype)
    gate_o[...] = _mxu(_sigmoid(gl).astype(BF16), g2_ref[...]).astype(gate_o.dtype)


def _rwkv_features(p_main, p_lora, sel, mu, sel_l, mu_l, w0, w2p, a0, a2p, g2p, k_k, k_a, r_k, lc, l):
    n = p_main.shape[0]
    dr = k_k.shape[0]
    rd, ra = w2p.shape[1], a2p.shape[1]
    nb, ncb = n // CHUNK, lc // CHUNK
    nheads = dr // RWKV_HEAD
    assert nheads <= LANE
    hs, hst = _head_indicators(dr)
    full = lambda a: pl.BlockSpec(a.shape, lambda i: (0,) * a.ndim)
    row1 = lambda a: a.reshape(1, -1)
    consts = [sel, mu, sel_l, mu_l, w0, w2p.astype(BF16), a0, a2p.astype(BF16), g2p.astype(BF16), row1(k_k),
              row1(k_a), row1(r_k), hs, hst]
    assert nb >= 3
    outs = pl.pallas_call(
        functools.partial(_rwkv_feat_kernel, ncb=ncb, nrows=l // CHUNK, dr=dr, rd=rd, ra=ra),
        grid=(nb,),
        in_specs=[pl.BlockSpec(memory_space=pl.ANY)] * 2 + [full(a) for a in consts],
        out_specs=[pl.BlockSpec((CHUNK, dr), lambda i: (i, 0))] * 11,
        out_shape=[jax.ShapeDtypeStruct((n, dr), F32)] * 9 + [jax.ShapeDtypeStruct((n, dr), PROJ_DTYPE)] * 2,
        scratch_shapes=[pltpu.VMEM((RING, CHUNK, p_main.shape[1]), F32), pltpu.VMEM((RING, CHUNK, p_lora.shape[1]), F32),
                        pltpu.SemaphoreType.DMA((2, RING))],
        compiler_params=_params(("arbitrary",), VMEM_LIMIT),
        name="rwkv_features",
    )(p_main, p_lora, *consts)
    return outs


def _block_diag_mask(n, w):
    r = lax.broadcasted_iota(jnp.int32, (n, n), 0)
    c = lax.broadcasted_iota(jnp.int32, (n, n), 1)
    return (r // w) == (c // w)


def _cumsum_rows(z, rev):
    c = z.shape[0]
    t = lax.broadcasted_iota(jnp.int32, (c, c), 0)
    s = lax.broadcasted_iota(jnp.int32, (c, c), 1)
    tri = ((s >= t) if rev else (s <= t)).astype(BF16)
    hi, mid, lo = _split3(z)
    return _mxu(tri, hi) + _mxu(tri, mid) + _mxu(tri, lo)


def _rwkv_chunks(chains):
    c, w = chains[0][0].shape
    nh = w // c
    every = range(len(chains))
    revs = [ch[7] for ch in chains]
    bd = _block_diag_mask(w, c)
    t_i = lax.broadcasted_iota(jnp.int32, (c, w), 0)
    s_i = lax.broadcasted_iota(jnp.int32, (c, w), 1) % c
    eye = (s_i == t_i).astype(F32)
    strict = {False: s_i < t_i, True: s_i > t_i}
    incl = {False: s_i <= t_i, True: s_i >= t_i}

    def expand(z):
        return jnp.where(bd, jnp.concatenate([z] * nh, axis=0), jnp.zeros((), z.dtype))

    def packed_mm(lhs, rhs):
        return _mxu(lhs.astype(BF16), expand(rhs.astype(BF16)))

    bc = [_cumsum_rows(ch[1], ch[7]) for ch in chains]
    ar, bk, v16 = [], [], []
    for i, (r, lw, k, v, a, b, _, _) in enumerate(chains):
        e_in, e_ex, e_ng = jnp.exp(bc[i]), jnp.exp(bc[i] - lw), jnp.exp(-bc[i])
        ar.append(jnp.concatenate([a * e_ex, r * e_in], axis=0).astype(BF16))
        bk.append(jnp.concatenate([b * e_ng, k * e_ng], axis=0).astype(BF16))
        v16.append(v.astype(BF16))
    gb = [_mxu_nt(ar[i], expand(bk[i][:c])) for i in every]
    gk = [_mxu_nt(ar[i], expand(bk[i][c:])) for i in every]
    l_ab = [jnp.where(strict[revs[i]], gb[i][:c], 0.0) for i in every]
    m_rb = [jnp.where(incl[revs[i]], gb[i][c:], 0.0) for i in every]
    lmk = [jnp.concatenate([jnp.where(strict[revs[i]], gk[i][:c], 0.0),
                            jnp.where(incl[revs[i]], gk[i][c:], 0.0)], axis=0) for i in every]
    t_m = [eye + l_ab[i] for i in every]
    p_m = [packed_mm(l_ab[i], l_ab[i]) for i in every]
    for _ in range(int(np.log2(c)) - 2):
        tp = [packed_mm(jnp.concatenate([t_m[i], p_m[i]], axis=0), p_m[i]) for i in every]
        t_m = [t_m[i] + tp[i][:c] for i in every]
        p_m = [tp[i][c:] for i in every]
    t_m = [t_m[i] + packed_mm(t_m[i], p_m[i]) for i in every]

    ars = [_mxu_nt(ar[i], chains[i][6].astype(BF16)) for i in every]
    lm = [_mxu(lmk[i].astype(BF16), expand(v16[i])) for i in every]
    u16 = [packed_mm(t_m[i], ars[i][:c] + lm[i][:c]).astype(BF16) for i in every]
    ys = [ars[i][c:] + _mxu(m_rb[i].astype(BF16), expand(u16[i])) + lm[i][c:] for i in every]
    ds = [_mxu_tn(jnp.concatenate([u16[i], v16[i]], axis=0), bk[i]) for i in every]
    s_new = []
    for i in every:
        last = bc[i][0:1] if revs[i] else bc[i][c - 1:c]
        s_new.append((chains[i][6] + jnp.where(bd, ds[i], 0.0)) * jnp.exp(last))
    return ys, s_new


def _cast_jobs(weights, ng, nb):
    in_specs, out_specs, shapes = [], [], []
    for w, col0, cols in weights:
        rows = w.shape[0]
        n = max(m for m in range(1, ng * nb + 1) if rows % m == 0 and (rows // m) % 16 == 0)
        slab = lambda g, c, n=n: jnp.minimum(g * nb + c, n - 1)
        in_specs.append(pl.BlockSpec((pl.Element(rows // n), pl.Element(cols)),
                                     lambda g, c, s=slab, h=rows // n, col0=col0: (s(g, c) * h, col0)))
        out_specs.append(pl.BlockSpec((rows // n, cols), lambda g, c, s=slab: (s(g, c), 0)))
        shapes.append(jax.ShapeDtypeStruct((rows, cols), PROJ_DTYPE))
    return in_specs, out_specs, shapes


def _run_cast_jobs(src_refs, dst_refs):
    for src, dst in zip(src_refs, dst_refs):
        dst[...] = src[...].astype(dst.dtype)


def _rwkv_scan_kernel(rf, lwf, kf, vf, af, bf, rr, lwr, kr, vr, ar, br, *rest):
    n_cast = (len(rest) - 3) // 2
    yf_o, yr_o = rest[n_cast:n_cast + 2]
    s_ref = rest[-1]

    @pl.when(pl.program_id(1) == 0)
    def _():
        s_ref[...] = jnp.zeros_like(s_ref)

    ngs = rf.shape[1] // GROUP_W
    chains = []
    for d, refs in enumerate(((rf, lwf, kf, vf, af, bf), (rr, lwr, kr, vr, ar, br))):
        for g in range(ngs):
            sl = slice(g * GROUP_W, (g + 1) * GROUP_W)
            chains.append(tuple(ref[:, sl] for ref in refs) + (s_ref[d, g], d == 1))
    ys, s_new = _rwkv_chunks(chains)
    _run_cast_jobs(rest[:n_cast], rest[n_cast + 2:-1])
    for d, o_ref in enumerate((yf_o, yr_o)):
        for g in range(ngs):
            o_ref[:, g * GROUP_W:(g + 1) * GROUP_W] = ys[d * ngs + g]
            s_ref[d, g] = s_new[d * ngs + g]


def _rev_chunk(c, ncb, nb):
    return jnp.where(c < ncb, ncb - 1 - c, nb - 1 - (c - ncb))


def _rwkv_scan(r, v, a, lw0, lw1, k0, k1, b0, b1, lc, cast_weights):
    n, dr = r.shape
    nb, ncb = n // CHUNK, lc // CHUNK
    bw = _pick(dr, RWKV_GROUPS_PER_STEP * GROUP_W, GROUP_W)
    fwd = pl.BlockSpec((CHUNK, bw), lambda g, c: (c, g))
    rev = pl.BlockSpec((CHUNK, bw), lambda g, c: (_rev_chunk(c, ncb, nb), g))
    cast_in, cast_out, cast_shapes = _cast_jobs(cast_weights, dr // bw, nb)
    return pl.pallas_call(
        _rwkv_scan_kernel,
        grid=(dr // bw, nb),
        in_specs=[fwd] * 6 + [rev] * 6 + cast_in,
        out_specs=[fwd, rev] + cast_out,
        out_shape=[jax.ShapeDtypeStruct((n, dr), F32)] * 2 + cast_shapes,
        scratch_shapes=[pltpu.VMEM((2, bw // GROUP_W, GROUP_W, GROUP_W), F32)],
        compiler_params=_params(("arbitrary", "arbitrary"), VMEM_LIMIT),
        name="rwkv_scan",
    )(r, lw0, k0, v, a, b0, r, lw1, k1, v, a, b1, *[w for w, _, _ in cast_weights])


def _hgrn_chunk(q, k, v, lf, s_vk, rev):
    c, w = q.shape
    nsb = c // SUB
    bc = _cumsum_rows(lf, rev)
    last = bc[0:1] if rev else bc[c - 1:c]
    bdh = _block_diag_mask(w, HGRN_HEAD)
    v16 = v.astype(BF16)
    o_inter = _mxu_nt((q * jnp.exp(bc)).astype(BF16), s_vk.astype(BF16))
    s_new = s_vk * jnp.exp(last) + jnp.where(bdh, _mxu_tn(v16, (k * jnp.exp(last - bc)).astype(BF16)), 0.0)

    lane2 = lax.broadcasted_iota(jnp.int32, (c, w), 1)
    row2 = lax.broadcasted_iota(jnp.int32, (c, w), 0)
    zpad = jnp.zeros((HGRN_HEAD - c, w), BF16)

    def expand(z):
        parts = []
        for h in range(w // HGRN_HEAD):
            parts += [jnp.where(lane2 // HGRN_HEAD == h, z, jnp.zeros((), BF16)), zpad]
        return jnp.concatenate(parts, axis=0)

    ones_bd = bdh.astype(BF16)
    t3 = lax.broadcasted_iota(jnp.int32, (SUB, SUB, w), 0)
    s3 = lax.broadcasted_iota(jnp.int32, (SUB, SUB, w), 1)
    l3 = lax.broadcasted_iota(jnp.int32, (SUB, SUB, w), 2) % HGRN_HEAD
    causal = (s3 >= t3) if rev else (s3 <= t3)
    a_rows = []
    for i in range(nsb):
        lo, hi = SUB * i, SUB * (i + 1)
        qi, ki, bi = q[lo:hi], k[lo:hi], bc[lo:hi]
        d3 = bi[:, None, :] - bi[None, :, :]
        x3 = jnp.where(causal, (qi[:, None, :] * ki[None, :, :]) * jnp.exp(jnp.minimum(d3, 0.0)), 0.0)
        r3 = _mxu(x3.reshape(SUB * SUB, w).astype(BF16), ones_bd).reshape(SUB, SUB, w)
        a_i = jnp.sum(jnp.where(l3 == s3 + lo, r3, 0.0), axis=1)
        if (not rev and i > 0) or (rev and i < nsb - 1):
            beta = bc[hi:hi + 1] if rev else bc[lo - 1:lo]
            earlier = (row2 >= hi) if rev else (row2 < lo)
            kp = jnp.where(earlier, k * jnp.exp(jnp.minimum(beta - bc, 0.0)), 0.0)
            a_i = a_i + _mxu_nt((qi * jnp.exp(bi - beta)).astype(BF16), expand(kp.astype(BF16)))
        a_rows.append(a_i)
    o = o_inter + _mxu(jnp.concatenate(a_rows, axis=0).astype(BF16), expand(v16))
    return o, s_new


def _hgrn_chunks_bounded(chains):
    c, w = chains[0][0].shape
    every = range(len(chains))
    revs = [ch[5] for ch in chains]
    bdh = _block_diag_mask(w, HGRN_HEAD)
    lane2 = lax.broadcasted_iota(jnp.int32, (c, w), 1)
    t_i = lax.broadcasted_iota(jnp.int32, (c, w), 0)
    s_i = lane2 % HGRN_HEAD
    incl = {False: s_i <= t_i, True: s_i >= t_i}
    zpad = jnp.zeros((HGRN_HEAD - c, w), BF16)

    def expand(z):
        parts = []
        for h in range(w // HGRN_HEAD):
            parts += [jnp.where(lane2 // HGRN_HEAD == h, z, jnp.zeros((), BF16)), zpad]
        return jnp.concatenate(parts, axis=0)

    bc = [_cumsum_rows(ch[3], ch[5]) for ch in chains]
    last = [bc[i][0:1] if revs[i] else bc[i][c - 1:c] for i in every]
    qt = [(chains[i][0] * jnp.exp(bc[i])).astype(BF16) for i in every]
    kt = [(chains[i][1] * jnp.exp(-bc[i])).astype(BF16) for i in every]
    kd = [(chains[i][1] * jnp.exp(last[i] - bc[i])).astype(BF16) for i in every]
    v16 = [chains[i][2].astype(BF16) for i in every]
    o_inter = [_mxu_nt(qt[i], chains[i][4].astype(BF16)) for i in every]
    a = [jnp.where(incl[revs[i]], _mxu_nt(qt[i], expand(kt[i])), 0.0).astype(BF16) for i in every]
    o = [o_inter[i] + _mxu(a[i], expand(v16[i])) for i in every]
    ds = [_mxu_tn(v16[i], kd[i]) for i in every]
    s_new = [chains[i][4] * jnp.exp(last[i]) + jnp.where(bdh, ds[i], 0.0) for i in every]
    return o, s_new


def _hgrn_scan_kernel(qf, ff, vf, qr, fr, vr, lg_ref, *rest, layer):
    n_cast = (len(rest) - 3) // 2
    of_o, or_o = rest[n_cast:n_cast + 2]
    s_ref = rest[-1]

    @pl.when(pl.program_id(1) == 0)
    def _():
        s_ref[...] = jnp.zeros_like(s_ref)

    lg = lg_ref[...]
    e = jnp.exp(lg - jnp.max(lg, axis=0, keepdims=True))
    lb = jnp.sum(e[:layer + 1], axis=0) / jnp.sum(e, axis=0)
    ngs = qf.shape[1] // GROUP_W
    chunk = qf.shape[0]

    def chains():
        out = []
        for d, (q_ref, f_ref, v_ref) in enumerate(((qf, ff, vf), (qr, fr, vr))):
            fd = lb[d:d + 1, :] + (1.0 - lb[d:d + 1, :]) * _sigmoid(f_ref[...])
            qv = q_ref[...]
            qh, kh, vh, lf = qv * _sigmoid(qv), 1.0 - fd, v_ref[...], jnp.log(fd)
            for g in range(ngs):
                sl = slice(g * GROUP_W, (g + 1) * GROUP_W)
                out.append((qh[:, sl], kh[:, sl], vh[:, sl], lf[:, sl], s_ref[d, g], d == 1))
        return out

    def emit(o, s_new):
        for d, o_ref in enumerate((of_o, or_o)):
            for g in range(ngs):
                o_ref[:, g * GROUP_W:(g + 1) * GROUP_W] = o[d * ngs + g]
                s_ref[d, g] = s_new[d * ngs + g]
        _run_cast_jobs(rest[:n_cast], rest[n_cast + 2:-1])

    bounded = chunk * jnp.min(jnp.log(lb)) >= -MAX_CHUNK_LOG_DECAY

    @pl.when(bounded)
    def _():
        emit(*_hgrn_chunks_bounded(chains()))

    @pl.when(jnp.logical_not(bounded))
    def _():
        res = [_hgrn_chunk(*ch) for ch in chains()]
        emit([r[0] for r in res], [r[1] for r in res])


def _hgrn_scan(p_h, lb_logits, lc, layer, cast_weights):
    n = p_h.shape[0]
    dh = lb_logits.shape[2]
    nb, ncb = n // CHUNK, lc // CHUNK
    bw = _pick(dh, HGRN_GROUPS_PER_STEP * GROUP_W, GROUP_W)
    ng = dh // bw
    blk = (CHUNK, bw)
    fwd = lambda sec: pl.BlockSpec(blk, lambda g, c: (c, sec * ng + g))
    rev = lambda sec: pl.BlockSpec(blk, lambda g, c: (_rev_chunk(c, ncb, nb), sec * ng + g))
    cast_in, cast_out, cast_shapes = _cast_jobs(cast_weights, ng, nb)
    return pl.pallas_call(
        functools.partial(_hgrn_scan_kernel, layer=layer),
        grid=(ng, nb),
        in_specs=[fwd(0), fwd(1), fwd(3), rev(0), rev(2), rev(3),
                  pl.BlockSpec((lb_logits.shape[0], 2, bw), lambda g, c: (0, 0, g))] + cast_in,
        out_specs=[pl.BlockSpec(blk, lambda g, c: (c, g)),
                   pl.BlockSpec(blk, lambda g, c: (_rev_chunk(c, ncb, nb), g))] + cast_out,
        out_shape=[jax.ShapeDtypeStruct((n, dh), F32)] * 2 + cast_shapes,
        scratch_shapes=[pltpu.VMEM((2, bw // GROUP_W, GROUP_W, GROUP_W), F32)],
        compiler_params=_params(("arbitrary", "arbitrary"), VMEM_LIMIT),
        name="hgrn_scan",
    )(p_h, p_h, p_h, p_h, p_h, p_h, lb_logits, *[w for w, _, _ in cast_weights])


def _mix_out_kernel(yf, yr, bonus, gate, of, orv, g_ref, lng, lnb, hng, hs_ref, hst_ref, u_o, *, dr, dh):
    hs, hst = hs_ref[...], hst_ref[...]

    def headmean(z):
        return _head_sum(z, hs, hst) * (1.0 / RWKV_HEAD)

    y = yf[...] + yr[...]
    yc = y - headmean(y)
    yn = yc * lax.rsqrt(headmean(yc * yc) + GN_EPS) * lng[...] + lnb[...]
    u_o[:, 0:dr] = ((yn + bonus[...].astype(F32)) * gate[...].astype(F32)).astype(u_o.dtype)
    o = of[...] + orv[...]
    g = g_ref[...]
    sg = g * _sigmoid(g)
    for h in range(dh // HGRN_HEAD):
        sl = slice(h * HGRN_HEAD, (h + 1) * HGRN_HEAD)
        oh = o[:, sl]
        on = oh * lax.rsqrt(jnp.mean(oh * oh, axis=-1, keepdims=True) + NORM_EPS) * hng[...]
        u_o[:, dr + h * HGRN_HEAD:dr + (h + 1) * HGRN_HEAD] = (on * sg[:, sl]).astype(u_o.dtype)


def _mix_out(yf, yr, bonus, gate, of, orv, p_h, ln_g, ln_b, hg, lc, l):
    dr, dh = yf.shape[1], of.shape[1]
    tb = _pick(int(np.gcd(lc, l)), EPILOGUE_ROWS, 8)
    off = lc // tb
    hs, hst = _head_indicators(dr)
    rows = lambda wd: pl.BlockSpec((tb, wd), lambda i: (i + off, 0))
    full = lambda a: pl.BlockSpec(a.shape, lambda i: (0,) * a.ndim)
    consts = [ln_g.reshape(1, dr), ln_b.reshape(1, dr), hg.reshape(1, HGRN_HEAD), hs, hst]
    return pl.pallas_call(
        functools.partial(_mix_out_kernel, dr=dr, dh=dh),
        grid=(l // tb,),
        in_specs=[rows(dr)] * 4 + [rows(dh)] * 2 + [pl.BlockSpec((tb, dh), lambda i: (i + off, 4))]
                 + [full(a) for a in consts],
        out_specs=pl.BlockSpec((tb, dr + dh), lambda i: (i, 0)),
        out_shape=jax.ShapeDtypeStruct((l, dr + dh), PROJ_DTYPE),
        compiler_params=_params(("arbitrary",), VMEM_LIMIT),
        name="mix_out",
    )(yf, yr, bonus, gate, of, orv, p_h, *consts)


def _res_mix_kernel(ux_ref, x_ref, mod_ref, gpost_ref, gpre_ref, x1_o, h2_o, *, d):
    ux = ux_ref[...]
    nrm = ux * lax.rsqrt(jnp.mean(ux * ux, axis=-1, keepdims=True) + NORM_EPS) * gpost_ref[...]
    x1 = x_ref[...] + mod_ref[0:1, 2 * d:3 * d] * nrm
    x1_o[...] = x1
    hn = x1 * lax.rsqrt(jnp.mean(x1 * x1, axis=-1, keepdims=True) + NORM_EPS) * gpre_ref[...]
    h2_o[...] = (hn * (1.0 + mod_ref[0:1, 4 * d:5 * d]) + mod_ref[0:1, 3 * d:4 * d]).astype(h2_o.dtype)


def _res_mix(ux, x2, mod, g_post, g_pre):
    l, d = x2.shape
    tb = _pick(l, ROW_BLOCK, 8)
    rows = pl.BlockSpec((tb, d), lambda i: (i, 0))
    full = lambda a: pl.BlockSpec(a.shape, lambda i: (0,) * a.ndim)
    gp, gq = g_post.reshape(1, d), g_pre.reshape(1, d)
    return pl.pallas_call(
        functools.partial(_res_mix_kernel, d=d),
        grid=(l // tb,),
        in_specs=[rows, rows, full(mod), full(gp), full(gq)],
        out_specs=[rows, rows],
        out_shape=[jax.ShapeDtypeStruct((l, d), F32), jax.ShapeDtypeStruct((l, d), PROJ_DTYPE)],
        compiler_params=_params(("arbitrary",), VMEM_LIMIT),
        name="residual_mix",
    )(ux, x2, mod, gp, gq)


def _pad_cols(a, width):
    return jnp.pad(a, ((0, 0), (0, width - a.shape[1])))


def _pad_rows(a, height):
    return jnp.pad(a, ((0, height - a.shape[0]), (0, 0)))


def _round_up(n, m):
    return (n + m - 1) // m * m


def kernel(x, c, ctx, c_ctx, w_ada, b_ada, g_mix_pre, g_mix_post, g_ffn_pre, g_ffn_post, w_in, mu_shift, w0, w2, a0,
           a2, g2, k_k, k_a, r_k, ln_x_g, ln_x_b, hgrn_lb_logits, hgrn_norm_g, w_out, w_ff1, w_ff2):
    assert x.shape[0] == 1 and w_in.shape[0] == 1, "single batch, single layer"
    layer = 0
    x2, ctx2 = x[0], ctx[0]
    l, d = x2.shape
    lc = ctx2.shape[0]
    dr, dh = k_k.shape[1], hgrn_lb_logits.shape[2]
    rd_raw, ra_raw, rg_raw = w2.shape[2], a2.shape[2], g2.shape[1]
    rd, ra, rg = (_round_up(v, LANE) for v in (rd_raw, ra_raw, rg_raw))
    assert l % GRID_W == 0 and lc % CHUNK == 0 and dr % GROUP_W == 0 and dh % GROUP_W == 0

    cvec = jnp.concatenate([c, c_ctx[None, :], jnp.zeros((6, d), F32)], axis=0)
    mod = _ada(cvec, w_ada[layer], b_ada[layer])

    wi = w_in[layer]
    mu = mu_shift[layer][None, :]
    sel = (jnp.arange(mu.shape[1], dtype=jnp.int32) % 4)[None, :]
    o = 3 * dr
    cuts = []
    for raw, padded in ((rd_raw, rd), (rd_raw, rd), (ra_raw, ra), (ra_raw, ra), (rg_raw, rg)):
        cuts.append((o, o + raw, padded))
        o += raw
    rw_cols = o
    regroup = lambda a: jnp.concatenate([_pad_cols(a[:, s:e], wd) for s, e, wd in cuts], axis=1)
    w_lora = regroup(wi)
    w2p = jnp.stack([_pad_rows(w2[layer, dd], rd) for dd in range(2)])
    a2p = jnp.stack([_pad_rows(a2[layer, dd], ra) for dd in range(2)])
    g2p = _pad_rows(g2[layer], rg)

    h = _prep(ctx2, x2, g_mix_pre[layer], mod)
    p_main = _mm_wcast(h, wi, 3 * dr, F32, tm_target=IN_PROJ_TILE, tn_target=IN_PROJ_TILE, name="in_proj_rkv")
    p_lora = _mm_wcast(h, w_lora, w_lora.shape[1], F32, tm_target=IN_PROJ_TILE, name="in_proj_lora")

    (r, v, a, lw0, lw1, k0, k1, b0, b1, bonus, gate) = _rwkv_features(
        p_main, p_lora, sel[:, :3 * dr], mu[:, :3 * dr], regroup(sel), regroup(mu), w0[layer], w2p, a0[layer], a2p,
        g2p, k_k[layer], k_a[layer], r_k[layer].reshape(-1), lc, l)
    whole = lambda w: (w, 0, w.shape[1])
    yf, yr, w_h, w_up = _rwkv_scan(r, v, a, lw0, lw1, k0, k1, b0, b1, lc,
                                   [(wi, rw_cols, wi.shape[1] - rw_cols), whole(w_ff1[layer])])
    p_h = _mm(h, w_h, F32, tm_target=IN_PROJ_TILE, name="in_proj_hgrn")
    of, orv, w_down, w_o = _hgrn_scan(p_h, hgrn_lb_logits, lc, layer, [whole(w_ff2[layer]), whole(w_out[layer])])

    u = _mix_out(yf, yr, bonus, gate, of, orv, p_h, ln_x_g[layer], ln_x_b[layer], hgrn_norm_g[layer], lc, l)
    ux = _mm(u, w_o, F32, name="out_proj")
    x1, h2 = _res_mix(ux, x2, mod, g_mix_post[layer], g_ffn_pre[layer])
    act = _mm(h2, w_up, PROJ_DTYPE, act="relu2", name="ffn_up")
    out = _residual_projection(act, w_down, x1, mod, g_ffn_post[layer], 5, name="ffn_down")
    return out[None]
```

```python
import functools

import jax
import jax.numpy as jnp
import numpy as np
from jax import lax
from jax.experimental import pallas as pl
from jax.experimental.pallas import tpu as pltpu

F32 = jnp.float32
BF16 = jnp.bfloat16
PROJ_DTYPE = jnp.bfloat16

LANE = 128
GRID_W = 64
CHUNK = GRID_W
RWKV_HEAD = 64
HGRN_HEAD = 128
SUB = 16
GROUP_W = 256
RWKV_GROUPS_PER_STEP = 8
HGRN_GROUPS_PER_STEP = 8
MAX_CHUNK_LOG_DECAY = 80.0
NORM_EPS = 1e-6
GN_EPS = 64e-5
EXP_M05 = float(np.exp(-0.5))
VMEM_LIMIT = 56 * 1024 * 1024

MM_TILE = 1024
IN_PROJ_TILE = 768
SWEEP_ROWS = 512
ROW_BLOCK = 256
EPILOGUE_ROWS = 128
ADA_COLS = 512


def _pick(n, target, unit=LANE):
    best = None
    for m in range(unit, min(n, target) + 1, unit):
        if n % m == 0:
            best = m
    return best if best is not None else n


def _sigmoid(z):
    return 0.5 * jnp.tanh(0.5 * z) + 0.5


def _mxu(a, b):
    return jnp.dot(a, b, preferred_element_type=F32)


def _mxu_nt(a, b):
    return lax.dot_general(a, b, (((1,), (1,)), ((), ())), preferred_element_type=F32)


def _mxu_tn(a, b):
    return lax.dot_general(a, b, (((0,), (0,)), ((), ())), preferred_element_type=F32)


def _split3(z):
    hi = z.astype(BF16)
    rest = z - hi.astype(F32)
    mid = rest.astype(BF16)
    return hi, mid, (rest - mid.astype(F32)).astype(BF16)


def _select_sum(z, sel):
    hi, mid, _ = _split3(z)
    return _mxu(hi, sel) + _mxu(mid, sel)


def _head_sum(z, hs, hst):
    return _select_sum(_select_sum(z, hs), hst)


def _head_indicators(dr):
    hs = np.arange(dr)[:, None] // RWKV_HEAD == np.arange(LANE)[None, :]
    return jnp.asarray(hs, BF16), jnp.asarray(hs.T, BF16)


def _params(sem, vmem=None):
    return pltpu.CompilerParams(dimension_semantics=sem, vmem_limit_bytes=vmem)


def _ada_kernel(c_ref, w_ref, b_ref, o_ref):
    cv = c_ref[...]
    rows = cv.shape[0]
    hi, mid, lo = (t.astype(F32) for t in _split3(cv * _sigmoid(cv)))
    s3 = jnp.concatenate([hi, mid, lo], axis=0).astype(BF16)
    w = w_ref[...]
    w_hi = w.astype(BF16)
    w_lo = (w - w_hi.astype(F32)).astype(BF16)
    p = _mxu(s3, w_hi)
    q = _mxu(s3[:2 * rows], w_lo)
    o_ref[...] = (p[:rows] + p[rows:2 * rows] + p[2 * rows:] + q[:rows] + q[rows:]) + b_ref[...]


def _ada(cvec, w, b):
    rows, d = cvec.shape
    n = w.shape[1]
    tn = _pick(n, ADA_COLS)
    return pl.pallas_call(
        _ada_kernel,
        grid=(n // tn,),
        in_specs=[pl.BlockSpec((rows, d), lambda j: (0, 0)),
                  pl.BlockSpec((d, tn), lambda j: (0, j)),
                  pl.BlockSpec((1, tn), lambda j: (0, j))],
        out_specs=pl.BlockSpec((rows, tn), lambda j: (0, j)),
        out_shape=jax.ShapeDtypeStruct((rows, n), F32),
        compiler_params=_params(("arbitrary",), VMEM_LIMIT),
        name="ada_mod",
    )(cvec, w, b.reshape(1, n))


def _prep_kernel(ctx_ref, x_ref, g_ref, mod_ref, o_ref, *, ncb, d):
    is_ctx = pl.program_id(0) < ncb
    rows = jnp.where(is_ctx, ctx_ref[...], x_ref[...])
    ms = jnp.mean(rows * rows, axis=-1, keepdims=True)
    hn = rows * lax.rsqrt(ms + NORM_EPS) * g_ref[...]
    shift = jnp.where(is_ctx, mod_ref[1:2, 0:d], mod_ref[0:1, 0:d])
    scale = jnp.where(is_ctx, mod_ref[1:2, d:2 * d], mod_ref[0:1, d:2 * d])
    o_ref[...] = (hn * (1.0 + scale) + shift).astype(o_ref.dtype)


def _prep(ctx2, x2, g, mod):
    lc, d = ctx2.shape
    l = x2.shape[0]
    tb = _pick(int(np.gcd(lc, l)), ROW_BLOCK, 8)
    ncb = lc // tb
    return pl.pallas_call(
        functools.partial(_prep_kernel, ncb=ncb, d=d),
        grid=((lc + l) // tb,),
        in_specs=[pl.BlockSpec((tb, d), lambda i: (jnp.minimum(i, ncb - 1), 0)),
                  pl.BlockSpec((tb, d), lambda i: (jnp.maximum(i - ncb, 0), 0)),
                  pl.BlockSpec((1, d), lambda i: (0, 0)),
                  pl.BlockSpec(mod.shape, lambda i: (0, 0))],
        out_specs=pl.BlockSpec((tb, d), lambda i: (i, 0)),
        out_shape=jax.ShapeDtypeStruct((lc + l, d), PROJ_DTYPE),
        compiler_params=_params(("arbitrary",)),
        name="norm_modulate",
    )(ctx2, x2, g.reshape(1, d), mod)


def _mm_kernel(x_ref, w_ref, o_ref, *, act):
    acc = jnp.dot(x_ref[...], w_ref[...], preferred_element_type=F32)
    if act == "relu2":
        acc = jnp.square(jnp.maximum(acc, 0.0))
    o_ref[...] = acc.astype(o_ref.dtype)


def _mm(x, w, out_dtype, act=None, tm_target=MM_TILE, tn_target=MM_TILE, name="matmul"):
    m, k = x.shape
    n = w.shape[1]
    tm = _pick(m, tm_target)
    tn = _pick(n, tn_target)
    return pl.pallas_call(
        functools.partial(_mm_kernel, act=act),
        grid=(m // tm, n // tn),
        in_specs=[pl.BlockSpec((tm, k), lambda i, j: (i, 0)),
                  pl.BlockSpec((k, tn), lambda i, j: (0, j))],
        out_specs=pl.BlockSpec((tm, tn), lambda i, j: (i, j)),
        out_shape=jax.ShapeDtypeStruct((m, n), out_dtype),
        compiler_params=_params(("arbitrary", "arbitrary"), VMEM_LIMIT),
        name=name,
    )(x, w)


def _mm_wcast_kernel(x_ref, w_ref, o_ref, w_cast):
    @pl.when(pl.program_id(1) == 0)
    def _():
        w_cast[...] = w_ref[...].astype(w_cast.dtype)

    o_ref[...] = jnp.dot(x_ref[...], w_cast[...], preferred_element_type=F32).astype(o_ref.dtype)


def _mm_wcast(x, w, ncols, out_dtype, tm_target=MM_TILE, tn_target=MM_TILE, name="matmul_wcast"):
    m, k = x.shape
    tm, tn = _pick(m, tm_target), _pick(ncols, tn_target)
    return pl.pallas_call(
        _mm_wcast_kernel,
        grid=(ncols // tn, m // tm),
        in_specs=[pl.BlockSpec((tm, k), lambda j, i: (i, 0)),
                  pl.BlockSpec((k, tn), lambda j, i: (0, j))],
        out_specs=pl.BlockSpec((tm, tn), lambda j, i: (i, j)),
        out_shape=jax.ShapeDtypeStruct((m, ncols), out_dtype),
        scratch_shapes=[pltpu.VMEM((k, tn), PROJ_DTYPE)],
        compiler_params=_params(("arbitrary", "arbitrary"), VMEM_LIMIT),
        name=name,
    )(x, w)


def _res_proj_kernel(a_ref, w_ref, res_hbm, mod_ref, g_ref, o_ref, res_buf, sem, *, d, gate_chunk):
    i, kk = pl.program_id(0), pl.program_id(1)
    tm = res_buf.shape[0]
    res_copy = pltpu.make_async_copy(res_hbm.at[pl.ds(pl.multiple_of(i * tm, tm), tm), :], res_buf, sem)

    @pl.when(kk == 0)
    def _():
        res_copy.start()
        o_ref[...] = jnp.zeros_like(o_ref)

    o_ref[...] += jnp.dot(a_ref[...], w_ref[...], preferred_element_type=F32)

    @pl.when(kk == pl.num_programs(1) - 1)
    def _():
        res_copy.wait()
        gate = mod_ref[0:1, gate_chunk * d:(gate_chunk + 1) * d] * g_ref[...]
        rb = _pick(tm, EPILOGUE_ROWS, 8)
        for r0 in range(0, tm, rb):
            rows = slice(r0, r0 + rb)
            acc = o_ref[rows, :]
            scale = lax.rsqrt(jnp.mean(acc * acc, axis=-1, keepdims=True) + NORM_EPS)
            o_ref[rows, :] = res_buf[rows, :] + gate * (acc * scale)


def _residual_projection(a, w, res, mod, g_post, gate_chunk, tm_target=SWEEP_ROWS, tk_target=MM_TILE,
                         name="residual_projection"):
    m, k = a.shape
    d = w.shape[1]
    tm, tk = _pick(m, tm_target, 8), _pick(k, tk_target)
    full = lambda z: pl.BlockSpec(z.shape, lambda i, kk: (0,) * z.ndim)
    gp = g_post.reshape(1, d)
    return pl.pallas_call(
        functools.partial(_res_proj_kernel, d=d, gate_chunk=gate_chunk),
        grid=(m // tm, k // tk),
        in_specs=[pl.BlockSpec((tm, tk), lambda i, kk: (i, kk)),
                  pl.BlockSpec((tk, d), lambda i, kk: (kk, 0)),
                  pl.BlockSpec(memory_space=pl.ANY), full(mod), full(gp)],
        out_specs=pl.BlockSpec((tm, d), lambda i, kk: (i, 0)),
        out_shape=jax.ShapeDtypeStruct((m, d), F32),
        scratch_shapes=[pltpu.VMEM((tm, d), F32), pltpu.SemaphoreType.DMA(())],
        compiler_params=_params(("arbitrary", "arbitrary"), VMEM_LIMIT),
        name=name,
    )(a, w, res, mod, gp)


RING = 4


def _rwkv_feat_kernel(pm_hbm, pl_hbm, sel_ref, mu_ref, sell_ref, mul_ref, w0_ref, w2_ref, a0_ref, a2_ref, g2_ref,
                      kk_ref, ka_ref, rk_ref, hs_ref, hst_ref,
                      r_o, v_o, a_o, lw0_o, lw1_o, k0_o, k1_o, b0_o, b1_o, bonus_o, gate_o, ring_m, ring_l, sems,
                      *, ncb, nrows, dr, rd, ra):
    i = pl.program_id(0)
    nb = pl.num_programs(0)
    c = ring_m.shape[1]
    sources = ((pm_hbm, ring_m), (pl_hbm, ring_l))

    def fetch(blk):
        slot = blk % RING
        rows = pl.ds(pl.multiple_of(blk * c, c), c)
        return [pltpu.make_async_copy(src.at[rows, :], ring.at[slot], sems.at[n, slot])
                for n, (src, ring) in enumerate(sources)]

    def start(blk):
        for cp in fetch(blk):
            cp.start()

    def wait(blk):
        for cp in fetch(blk):
            cp.wait()

    @pl.when(i == 0)
    def _():
        start(0)
        start(1)
        start(2)
        wait(0)
        wait(1)

    @pl.when((i > 0) & (i + 1 < nb))
    def _():
        wait(i + 1)

    @pl.when((i > 0) & (i + 2 < nb))
    def _():
        start(i + 2)

    is_ctx = i < ncb
    xi = i - ncb
    s_cur, s_prv, s_nxt = i % RING, jnp.maximum(i - 1, 0) % RING, jnp.minimum(i + 1, nb - 1) % RING

    def shift_lerp(ring, sel, mu):
        cur, prv, nxt = ring[s_cur], ring[s_prv], ring[s_nxt]
        row = lax.broadcasted_iota(jnp.int32, cur.shape, 0)
        prev_last = jnp.where(is_ctx & (i > 0), prv[c - 1:c, :], 0.0)
        next_first = jnp.where(is_ctx & (i < ncb - 1), nxt[0:1, :], 0.0)
        before = jnp.where(row == 0, prev_last, pltpu.roll(cur, 1, 0))
        after = jnp.where(row == c - 1, next_first, pltpu.roll(cur, c - 1, 0))
        above = jnp.where(is_ctx, before, jnp.where(xi > 0, prv, 0.0))
        below = jnp.where(is_ctx, after, jnp.where(xi < nrows - 1, nxt, 0.0))
        shifted = jnp.where(sel == 0, before, jnp.where(sel == 1, after, jnp.where(sel == 2, above, below)))
        return cur + mu * (shifted - cur)

    m = shift_lerp(ring_m, sel_ref[...], mu_ref[...])
    ml = shift_lerp(ring_l, sell_ref[...], mul_ref[...])
    r, k, v = m[:, 0:dr], m[:, dr:2 * dr], m[:, 2 * dr:3 * dr]
    wl = (ml[:, 0:rd], ml[:, rd:2 * rd])
    al = (ml[:, 2 * rd:2 * rd + ra], ml[:, 2 * rd + ra:2 * rd + 2 * ra])
    gl = ml[:, 2 * rd + 2 * ra:]

    hs, hst = hs_ref[...], hst_ref[...]

    def headsum(z):
        return _head_sum(z, hs, hst)

    kkf = k * kk_ref[...]
    kk = kkf * lax.rsqrt(headsum(kkf * kkf) + 1e-12)
    r_o[...] = r
    v_o[...] = v
    a_o[...] = -kk
    for d, (lw_o, k_o, b_o) in enumerate(((lw0_o, k0_o, b0_o), (lw1_o, k1_o, b1_o))):
        wd = _mxu(jnp.tanh(wl[d]).astype(BF16), w2_ref[d]) + w0_ref[d:d + 1, :]
        lw_o[...] = -EXP_M05 * _sigmoid(wd)
        ad = _sigmoid(_mxu(al[d].astype(BF16), a2_ref[d]) + a0_ref[d:d + 1, :])
        k_o[...] = k * (1.0 + (ad - 1.0) * ka_ref[...])
        b_o[...] = kk * ad
    bonus_o[...] = (headsum(r * k * rk_ref[...]) * v).astype(bonus_o.dtype)
    gate_o[...] = _mxu(_sigmoid(gl).astype(BF16), g2_ref[...]).astype(gate_o.dtype)


def _rwkv_features(p_main, p_lora, sel, mu, sel_l, mu_l, w0, w2p, a0, a2p, g2p, k_k, k_a, r_k, lc, l):
    n = p_main.shape[0]
    dr = k_k.shape[0]
    rd, ra = w2p.shape[1], a2p.shape[1]
    nb, ncb = n // CHUNK, lc // CHUNK
    nheads = dr // RWKV_HEAD
    assert nheads <= LANE
    hs, hst = _head_indicators(dr)
    full = lambda a: pl.BlockSpec(a.shape, lambda i: (0,) * a.ndim)
    row1 = lambda a: a.reshape(1, -1)
    consts = [sel, mu, sel_l, mu_l, w0, w2p.astype(BF16), a0, a2p.astype(BF16), g2p.astype(BF16), row1(k_k),
              row1(k_a), row1(r_k), hs, hst]
    assert nb >= 3
    outs = pl.pallas_call(
        functools.partial(_rwkv_feat_kernel, ncb=ncb, nrows=l // CHUNK, dr=dr, rd=rd, ra=ra),
        grid=(nb,),
        in_specs=[pl.BlockSpec(memory_space=pl.ANY)] * 2 + [full(a) for a in consts],
        out_specs=[pl.BlockSpec((CHUNK, dr), lambda i: (i, 0))] * 11,
        out_shape=[jax.ShapeDtypeStruct((n, dr), F32)] * 9 + [jax.ShapeDtypeStruct((n, dr), PROJ_DTYPE)] * 2,
        scratch_shapes=[pltpu.VMEM((RING, CHUNK, p_main.shape[1]), F32), pltpu.VMEM((RING, CHUNK, p_lora.shape[1]), F32),
                        pltpu.SemaphoreType.DMA((2, RING))],
        compiler_params=_params(("arbitrary",), VMEM_LIMIT),
        name="rwkv_features",
    )(p_main, p_lora, *consts)
    return outs


def _block_diag_mask(n, w):
    r = lax.broadcasted_iota(jnp.int32, (n, n), 0)
    c = lax.broadcasted_iota(jnp.int32, (n, n), 1)
    return (r // w) == (c // w)


def _cumsum_rows(z, rev):
    c = z.shape[0]
    t = lax.broadcasted_iota(jnp.int32, (c, c), 0)
    s = lax.broadcasted_iota(jnp.int32, (c, c), 1)
    tri = ((s >= t) if rev else (s <= t)).astype(BF16)
    hi, mid, lo = _split3(z)
    return _mxu(tri, hi) + _mxu(tri, mid) + _mxu(tri, lo)


def _rwkv_chunks(chains):
    c, w = chains[0][0].shape
    nh = w // c
    every = range(len(chains))
    revs = [ch[7] for ch in chains]
    bd = _block_diag_mask(w, c)
    t_i = lax.broadcasted_iota(jnp.int32, (c, w), 0)
    s_i = lax.broadcasted_iota(jnp.int32, (c, w), 1) % c
    eye = (s_i == t_i).astype(F32)
    strict = {False: s_i < t_i, True: s_i > t_i}
    incl = {False: s_i <= t_i, True: s_i >= t_i}

    def expand(z):
        return jnp.where(bd, jnp.concatenate([z] * nh, axis=0), jnp.zeros((), z.dtype))

    def packed_mm(lhs, rhs):
        return _mxu(lhs.astype(BF16), expand(rhs.astype(BF16)))

    bc = [_cumsum_rows(ch[1], ch[7]) for ch in chains]
    ar, bk, v16 = [], [], []
    for i, (r, lw, k, v, a, b, _, _) in enumerate(chains):
        e_in, e_ex, e_ng = jnp.exp(bc[i]), jnp.exp(bc[i] - lw), jnp.exp(-bc[i])
        ar.append(jnp.concatenate([a * e_ex, r * e_in], axis=0).astype(BF16))
        bk.append(jnp.concatenate([b * e_ng, k * e_ng], axis=0).astype(BF16))
        v16.append(v.astype(BF16))
    gb = [_mxu_nt(ar[i], expand(bk[i][:c])) for i in every]
    gk = [_mxu_nt(ar[i], expand(bk[i][c:])) for i in every]
    l_ab = [jnp.where(strict[revs[i]], gb[i][:c], 0.0) for i in every]
    m_rb = [jnp.where(incl[revs[i]], gb[i][c:], 0.0) for i in every]
    lmk = [jnp.concatenate([jnp.where(strict[revs[i]], gk[i][:c], 0.0),
                            jnp.where(incl[revs[i]], gk[i][c:], 0.0)], axis=0) for i in every]
    t_m = [eye + l_ab[i] for i in every]
    p_m = [packed_mm(l_ab[i], l_ab[i]) for i in every]
    for _ in range(int(np.log2(c)) - 2):
        tp = [packed_mm(jnp.concatenate([t_m[i], p_m[i]], axis=0), p_m[i]) for i in every]
        t_m = [t_m[i] + tp[i][:c] for i in every]
        p_m = [tp[i][c:] for i in every]
    t_m = [t_m[i] + packed_mm(t_m[i], p_m[i]) for i in every]

    ars = [_mxu_nt(ar[i], chains[i][6].astype(BF16)) for i in every]
    lm = [_mxu(lmk[i].astype(BF16), expand(v16[i])) for i in every]
    u16 = [packed_mm(t_m[i], ars[i][:c] + lm[i][:c]).astype(BF16) for i in every]
    ys = [ars[i][c:] + _mxu(m_rb[i].astype(BF16), expand(u16[i])) + lm[i][c:] for i in every]
    ds = [_mxu_tn(jnp.concatenate([u16[i], v16[i]], axis=0), bk[i]) for i in every]
    s_new = []
    for i in every:
        last = bc[i][0:1] if revs[i] else bc[i][c - 1:c]
        s_new.append((chains[i][6] + jnp.where(bd, ds[i], 0.0)) * jnp.exp(last))
    return ys, s_new


def _cast_jobs(weights, ng, nb):
    in_specs, out_specs, shapes = [], [], []
    for w, col0, cols in weights:
        rows = w.shape[0]
        n = max(m for m in range(1, ng * nb + 1) if rows % m == 0 and (rows // m) % 16 == 0)
        slab = lambda g, c, n=n: jnp.minimum(g * nb + c, n - 1)
        in_specs.append(pl.BlockSpec((pl.Element(rows // n), pl.Element(cols)),
                                     lambda g, c, s=slab, h=rows // n, col0=col0: (s(g, c) * h, col0)))
        out_specs.append(pl.BlockSpec((rows // n, cols), lambda g, c, s=slab: (s(g, c), 0)))
        shapes.append(jax.ShapeDtypeStruct((rows, cols), PROJ_DTYPE))
    return in_specs, out_specs, shapes


def _run_cast_jobs(src_refs, dst_refs):
    for src, dst in zip(src_refs, dst_refs):
        dst[...] = src[...].astype(dst.dtype)


def _rwkv_scan_kernel(rf, lwf, kf, vf, af, bf, rr, lwr, kr, vr, ar, br, *rest):
    n_cast = (len(rest) - 3) // 2
    yf_o, yr_o = rest[n_cast:n_cast + 2]
    s_ref = rest[-1]

    @pl.when(pl.program_id(1) == 0)
    def _():
        s_ref[...] = jnp.zeros_like(s_ref)

    ngs = rf.shape[1] // GROUP_W
    chains = []
    for d, refs in enumerate(((rf, lwf, kf, vf, af, bf), (rr, lwr, kr, vr, ar, br))):
        for g in range(ngs):
            sl = slice(g * GROUP_W, (g + 1) * GROUP_W)
            chains.append(tuple(ref[:, sl] for ref in refs) + (s_ref[d, g], d == 1))
    ys, s_new = _rwkv_chunks(chains)
    _run_cast_jobs(rest[:n_cast], rest[n_cast + 2:-1])
    for d, o_ref in enumerate((yf_o, yr_o)):
        for g in range(ngs):
            o_ref[:, g * GROUP_W:(g + 1) * GROUP_W] = ys[d * ngs + g]
            s_ref[d, g] = s_new[d * ngs + g]


def _rev_chunk(c, ncb, nb):
    return jnp.where(c < ncb, ncb - 1 - c, nb - 1 - (c - ncb))


def _rwkv_scan(r, v, a, lw0, lw1, k0, k1, b0, b1, lc, cast_weights):
    n, dr = r.shape
    nb, ncb = n // CHUNK, lc // CHUNK
    bw = _pick(dr, RWKV_GROUPS_PER_STEP * GROUP_W, GROUP_W)
    fwd = pl.BlockSpec((CHUNK, bw), lambda g, c: (c, g))
    rev = pl.BlockSpec((CHUNK, bw), lambda g, c: (_rev_chunk(c, ncb, nb), g))
    cast_in, cast_out, cast_shapes = _cast_jobs(cast_weights, dr // bw, nb)
    return pl.pallas_call(
        _rwkv_scan_kernel,
        grid=(dr // bw, nb),
        in_specs=[fwd] * 6 + [rev] * 6 + cast_in,
        out_specs=[fwd, rev] + cast_out,
        out_shape=[jax.ShapeDtypeStruct((n, dr), F32)] * 2 + cast_shapes,
        scratch_shapes=[pltpu.VMEM((2, bw // GROUP_W, GROUP_W, GROUP_W), F32)],
        compiler_params=_params(("arbitrary", "arbitrary"), VMEM_LIMIT),
        name="rwkv_scan",
    )(r, lw0, k0, v, a, b0, r, lw1, k1, v, a, b1, *[w for w, _, _ in cast_weights])


def _hgrn_chunk(q, k, v, lf, s_vk, rev):
    c, w = q.shape
    nsb = c // SUB
    bc = _cumsum_rows(lf, rev)
    last = bc[0:1] if rev else bc[c - 1:c]
    bdh = _block_diag_mask(w, HGRN_HEAD)
    v16 = v.astype(BF16)
    o_inter = _mxu_nt((q * jnp.exp(bc)).astype(BF16), s_vk.astype(BF16))
    s_new = s_vk * jnp.exp(last) + jnp.where(bdh, _mxu_tn(v16, (k * jnp.exp(last - bc)).astype(BF16)), 0.0)

    lane2 = lax.broadcasted_iota(jnp.int32, (c, w), 1)
    row2 = lax.broadcasted_iota(jnp.int32, (c, w), 0)
    zpad = jnp.zeros((HGRN_HEAD - c, w), BF16)

    def expand(z):
        parts = []
        for h in range(w // HGRN_HEAD):
            parts += [jnp.where(lane2 // HGRN_HEAD == h, z, jnp.zeros((), BF16)), zpad]
        return jnp.concatenate(parts, axis=0)

    ones_bd = bdh.astype(BF16)
    t3 = lax.broadcasted_iota(jnp.int32, (SUB, SUB, w), 0)
    s3 = lax.broadcasted_iota(jnp.int32, (SUB, SUB, w), 1)
    l3 = lax.broadcasted_iota(jnp.int32, (SUB, SUB, w), 2) % HGRN_HEAD
    causal = (s3 >= t3) if rev else (s3 <= t3)
    a_rows = []
    for i in range(nsb):
        lo, hi = SUB * i, SUB * (i + 1)
        qi, ki, bi = q[lo:hi], k[lo:hi], bc[lo:hi]
        d3 = bi[:, None, :] - bi[None, :, :]
        x3 = jnp.where(causal, (qi[:, None, :] * ki[None, :, :]) * jnp.exp(jnp.minimum(d3, 0.0)), 0.0)
        r3 = _mxu(x3.reshape(SUB * SUB, w).astype(BF16), ones_bd).reshape(SUB, SUB, w)
        a_i = jnp.sum(jnp.where(l3 == s3 + lo, r3, 0.0), axis=1)
        if (not rev and i > 0) or (rev and i < nsb - 1):
            beta = bc[hi:hi + 1] if rev else bc[lo - 1:lo]
            earlier = (row2 >= hi) if rev else (row2 < lo)
            kp = jnp.where(earlier, k * jnp.exp(jnp.minimum(beta - bc, 0.0)), 0.0)
            a_i = a_i + _mxu_nt((qi * jnp.exp(bi - beta)).astype(BF16), expand(kp.astype(BF16)))
        a_rows.append(a_i)
    o = o_inter + _mxu(jnp.concatenate(a_rows, axis=0).astype(BF16), expand(v16))
    return o, s_new


def _hgrn_chunks_bounded(chains):
    c, w = chains[0][0].shape
    every = range(len(chains))
    revs = [ch[5] for ch in chains]
    bdh = _block_diag_mask(w, HGRN_HEAD)
    lane2 = lax.broadcasted_iota(jnp.int32, (c, w), 1)
    t_i = lax.broadcasted_iota(jnp.int32, (c, w), 0)
    s_i = lane2 % HGRN_HEAD
    incl = {False: s_i <= t_i, True: s_i >= t_i}
    zpad = jnp.zeros((HGRN_HEAD - c, w), BF16)

    def expand(z):
        parts = []
        for h in range(w // HGRN_HEAD):
            parts += [jnp.where(lane2 // HGRN_HEAD == h, z, jnp.zeros((), BF16)), zpad]
        return jnp.concatenate(parts, axis=0)

    bc = [_cumsum_rows(ch[3], ch[5]) for ch in chains]
    last = [bc[i][0:1] if revs[i] else bc[i][c - 1:c] for i in every]
    qt = [(chains[i][0] * jnp.exp(bc[i])).astype(BF16) for i in every]
    kt = [(chains[i][1] * jnp.exp(-bc[i])).astype(BF16) for i in every]
    kd = [(chains[i][1] * jnp.exp(last[i] - bc[i])).astype(BF16) for i in every]
    v16 = [chains[i][2].astype(BF16) for i in every]
    o_inter = [_mxu_nt(qt[i], chains[i][4].astype(BF16)) for i in every]
    a = [jnp.where(incl[revs[i]], _mxu_nt(qt[i], expand(kt[i])), 0.0).astype(BF16) for i in every]
    o = [o_inter[i] + _mxu(a[i], expand(v16[i])) for i in every]
    ds = [_mxu_tn(v16[i], kd[i]) for i in every]
    s_new = [chains[i][4] * jnp.exp(last[i]) + jnp.where(bdh, ds[i], 0.0) for i in every]
    return o, s_new


def _hgrn_scan_kernel(qf, ff, vf, qr, fr, vr, lg_ref, *rest, layer):
    n_cast = (len(rest) - 3) // 2
    of_o, or_o = rest[n_cast:n_cast + 2]
    s_ref = rest[-1]

    @pl.when(pl.program_id(1) == 0)
    def _():
        s_ref[...] = jnp.zeros_like(s_ref)

    lg = lg_ref[...]
    e = jnp.exp(lg - jnp.max(lg, axis=0, keepdims=True))
    lb = jnp.sum(e[:layer + 1], axis=0) / jnp.sum(e, axis=0)
    ngs = qf.shape[1] // GROUP_W
    chunk = qf.shape[0]

    def chains():
        out = []
        for d, (q_ref, f_ref, v_ref) in enumerate(((qf, ff, vf), (qr, fr, vr))):
            fd = lb[d:d + 1, :] + (1.0 - lb[d:d + 1, :]) * _sigmoid(f_ref[...])
            qv = q_ref[...]
            qh, kh, vh, lf = qv * _sigmoid(qv), 1.0 - fd, v_ref[...], jnp.log(fd)
            for g in range(ngs):
                sl = slice(g * GROUP_W, (g + 1) * GROUP_W)
                out.append((qh[:, sl], kh[:, sl], vh[:, sl], lf[:, sl], s_ref[d, g], d == 1))
        return out

    def emit(o, s_new):
        for d, o_ref in enumerate((of_o, or_o)):
            for g in range(ngs):
                o_ref[:, g * GROUP_W:(g + 1) * GROUP_W] = o[d * ngs + g]
                s_ref[d, g] = s_new[d * ngs + g]
        _run_cast_jobs(rest[:n_cast], rest[n_cast + 2:-1])

    bounded = chunk * jnp.min(jnp.log(lb)) >= -MAX_CHUNK_LOG_DECAY

    @pl.when(bounded)
    def _():
        emit(*_hgrn_chunks_bounded(chains()))

    @pl.when(jnp.logical_not(bounded))
    def _():
        res = [_hgrn_chunk(*ch) for ch in chains()]
        emit([r[0] for r in res], [r[1] for r in res])


def _hgrn_scan(p_h, lb_logits, lc, layer, cast_weights):
    n = p_h.shape[0]
    dh = lb_logits.shape[2]
    nb, ncb = n // CHUNK, lc // CHUNK
    bw = _pick(dh, HGRN_GROUPS_PER_STEP * GROUP_W, GROUP_W)
    ng = dh // bw
    blk = (CHUNK, bw)
    fwd = lambda sec: pl.BlockSpec(blk, lambda g, c: (c, sec * ng + g))
    rev = lambda sec: pl.BlockSpec(blk, lambda g, c: (_rev_chunk(c, ncb, nb), sec * ng + g))
    cast_in, cast_out, cast_shapes = _cast_jobs(cast_weights, ng, nb)
    return pl.pallas_call(
        functools.partial(_hgrn_scan_kernel, layer=layer),
        grid=(ng, nb),
        in_specs=[fwd(0), fwd(1), fwd(3), rev(0), rev(2), rev(3),
                  pl.BlockSpec((lb_logits.shape[0], 2, bw), lambda g, c: (0, 0, g))] + cast_in,
        out_specs=[pl.BlockSpec(blk, lambda g, c: (c, g)),
                   pl.BlockSpec(blk, lambda g, c: (_rev_chunk(c, ncb, nb), g))] + cast_out,
        out_shape=[jax.ShapeDtypeStruct((n, dh), F32)] * 2 + cast_shapes,
        scratch_shapes=[pltpu.VMEM((2, bw // GROUP_W, GROUP_W, GROUP_W), F32)],
        compiler_params=_params(("arbitrary", "arbitrary"), VMEM_LIMIT),
        name="hgrn_scan",
    )(p_h, p_h, p_h, p_h, p_h, p_h, lb_logits, *[w for w, _, _ in cast_weights])


def _mix_out_kernel(yf, yr, bonus, gate, of, orv, g_ref, lng, lnb, hng, hs_ref, hst_ref, u_o, *, dr, dh):
    hs, hst = hs_ref[...], hst_ref[...]

    def headmean(z):
        return _head_sum(z, hs, hst) * (1.0 / RWKV_HEAD)

    y = yf[...] + yr[...]
    yc = y - headmean(y)
    yn = yc * lax.rsqrt(headmean(yc * yc) + GN_EPS) * lng[...] + lnb[...]
    u_o[:, 0:dr] = ((yn + bonus[...].astype(F32)) * gate[...].astype(F32)).astype(u_o.dtype)
    o = of[...] + orv[...]
    g = g_ref[...]
    sg = g * _sigmoid(g)
    for h in range(dh // HGRN_HEAD):
        sl = slice(h * HGRN_HEAD, (h + 1) * HGRN_HEAD)
        oh = o[:, sl]
        on = oh * lax.rsqrt(jnp.mean(oh * oh, axis=-1, keepdims=True) + NORM_EPS) * hng[...]
        u_o[:, dr + h * HGRN_HEAD:dr + (h + 1) * HGRN_HEAD] = (on * sg[:, sl]).astype(u_o.dtype)


def _mix_out(yf, yr, bonus, gate, of, orv, p_h, ln_g, ln_b, hg, lc, l):
    dr, dh = yf.shape[1], of.shape[1]
    tb = _pick(int(np.gcd(lc, l)), EPILOGUE_ROWS, 8)
    off = lc // tb
    hs, hst = _head_indicators(dr)
    rows = lambda wd: pl.BlockSpec((tb, wd), lambda i: (i + off, 0))
    full = lambda a: pl.BlockSpec(a.shape, lambda i: (0,) * a.ndim)
    consts = [ln_g.reshape(1, dr), ln_b.reshape(1, dr), hg.reshape(1, HGRN_HEAD), hs, hst]
    return pl.pallas_call(
        functools.partial(_mix_out_kernel, dr=dr, dh=dh),
        grid=(l // tb,),
        in_specs=[rows(dr)] * 4 + [rows(dh)] * 2 + [pl.BlockSpec((tb, dh), lambda i: (i + off, 4))]
                 + [full(a) for a in consts],
        out_specs=pl.BlockSpec((tb, dr + dh), lambda i: (i, 0)),
        out_shape=jax.ShapeDtypeStruct((l, dr + dh), PROJ_DTYPE),
        compiler_params=_params(("arbitrary",), VMEM_LIMIT),
        name="mix_out",
    )(yf, yr, bonus, gate, of, orv, p_h, *consts)


def _res_mix_kernel(ux_ref, x_ref, mod_ref, gpost_ref, gpre_ref, x1_o, h2_o, *, d):
    ux = ux_ref[...]
    nrm = ux * lax.rsqrt(jnp.mean(ux * ux, axis=-1, keepdims=True) + NORM_EPS) * gpost_ref[...]
    x1 = x_ref[...] + mod_ref[0:1, 2 * d:3 * d] * nrm
    x1_o[...] = x1
    hn = x1 * lax.rsqrt(jnp.mean(x1 * x1, axis=-1, keepdims=True) + NORM_EPS) * gpre_ref[...]
    h2_o[...] = (hn * (1.0 + mod_ref[0:1, 4 * d:5 * d]) + mod_ref[0:1, 3 * d:4 * d]).astype(h2_o.dtype)


def _res_mix(ux, x2, mod, g_post, g_pre):
    l, d = x2.shape
    tb = _pick(l, ROW_BLOCK, 8)
    rows = pl.BlockSpec((tb, d), lambda i: (i, 0))
    full = lambda a: pl.BlockSpec(a.shape, lambda i: (0,) * a.ndim)
    gp, gq = g_post.reshape(1, d), g_pre.reshape(1, d)
    return pl.pallas_call(
        functools.partial(_res_mix_kernel, d=d),
        grid=(l // tb,),
        in_specs=[rows, rows, full(mod), full(gp), full(gq)],
        out_specs=[rows, rows],
        out_shape=[jax.ShapeDtypeStruct((l, d), F32), jax.ShapeDtypeStruct((l, d), PROJ_DTYPE)],
        compiler_params=_params(("arbitrary",), VMEM_LIMIT),
        name="residual_mix",
    )(ux, x2, mod, gp, gq)


def _pad_cols(a, width):
    return jnp.pad(a, ((0, 0), (0, width - a.shape[1])))


def _pad_rows(a, height):
    return jnp.pad(a, ((0, height - a.shape[0]), (0, 0)))


def _round_up(n, m):
    return (n + m - 1) // m * m


def kernel(x, c, ctx, c_ctx, w_ada, b_ada, g_mix_pre, g_mix_post, g_ffn_pre, g_ffn_post, w_in, mu_shift, w0, w2, a0,
           a2, g2, k_k, k_a, r_k, ln_x_g, ln_x_b, hgrn_lb_logits, hgrn_norm_g, w_out, w_ff1, w_ff2):
    assert x.shape[0] == 1 and w_in.shape[0] == 1, "single batch, single layer"
    layer = 0
    x2, ctx2 = x[0], ctx[0]
    l, d = x2.shape
    lc = ctx2.shape[0]
    dr, dh = k_k.shape[1], hgrn_lb_logits.shape[2]
    rd_raw, ra_raw, rg_raw = w2.shape[2], a2.shape[2], g2.shape[1]
    rd, ra, rg = (_round_up(v, LANE) for v in (rd_raw, ra_raw, rg_raw))
    assert l % GRID_W == 0 and lc % CHUNK == 0 and dr % GROUP_W == 0 and dh % GROUP_W == 0

    cvec = jnp.concatenate([c, c_ctx[None, :], jnp.zeros((6, d), F32)], axis=0)
    mod = _ada(cvec, w_ada[layer], b_ada[layer])

    wi = w_in[layer]
    mu = mu_shift[layer][None, :]
    sel = (jnp.arange(mu.shape[1], dtype=jnp.int32) % 4)[None, :]
    o = 3 * dr
    cuts = []
    for raw, padded in ((rd_raw, rd), (rd_raw, rd), (ra_raw, ra), (ra_raw, ra), (rg_raw, rg)):
        cuts.append((o, o + raw, padded))
        o += raw
    rw_cols = o
    regroup = lambda a: jnp.concatenate([_pad_cols(a[:, s:e], wd) for s, e, wd in cuts], axis=1)
    w_lora = regroup(wi)
    w2p = jnp.stack([_pad_rows(w2[layer, dd], rd) for dd in range(2)])
    a2p = jnp.stack([_pad_rows(a2[layer, dd], ra) for dd in range(2)])
    g2p = _pad_rows(g2[layer], rg)

    h = _prep(ctx2, x2, g_mix_pre[layer], mod)
    p_main = _mm_wcast(h, wi, 3 * dr, F32, tm_target=IN_PROJ_TILE, tn_target=IN_PROJ_TILE, name="in_proj_rkv")
    p_lora = _mm_wcast(h, w_lora, w_lora.shape[1], F32, tm_target=IN_PROJ_TILE, name="in_proj_lora")

    (r, v, a, lw0, lw1, k0, k1, b0, b1, bonus, gate) = _rwkv_features(
        p_main, p_lora, sel[:, :3 * dr], mu[:, :3 * dr], regroup(sel), regroup(mu), w0[layer], w2p, a0[layer], a2p,
        g2p, k_k[layer], k_a[layer], r_k[layer].reshape(-1), lc, l)
    whole = lambda w: (w, 0, w.shape[1])
    yf, yr, w_h, w_up = _rwkv_scan(r, v, a, lw0, lw1, k0, k1, b0, b1, lc,
                                   [(wi, rw_cols, wi.shape[1] - rw_cols), whole(w_ff1[layer])])
    p_h = _mm(h, w_h, F32, tm_target=IN_PROJ_TILE, name="in_proj_hgrn")
    of, orv, w_down, w_o = _hgrn_scan(p_h, hgrn_lb_logits, lc, layer, [whole(w_ff2[layer]), whole(w_out[layer])])

    u = _mix_out(yf, yr, bonus, gate, of, orv, p_h, ln_x_g[layer], ln_x_b[layer], hgrn_norm_g[layer], lc, l)
    ux = _mm(u, w_o, F32, name="out_proj")
    x1, h2 = _res_mix(ux, x2, mod, g_mix_post[layer], g_ffn_pre[layer])
    act = _mm(h2, w_up, PROJ_DTYPE, act="relu2", name="ffn_up")
    out = _residual_projection(act, w_down, x1, mod, g_ffn_post[layer], 5, name="ffn_down")
    return out[None]
```

```python
import functools

import jax
import jax.numpy as jnp
import numpy as np
from jax import lax
from jax.experimental import pallas as pl
from jax.experimental.pallas import tpu as pltpu

F32 = jnp.float32
BF16 = jnp.bfloat16
PROJ_DTYPE = jnp.bfloat16

LANE = 128
GRID_W = 64
CHUNK = GRID_W
RWKV_HEAD = 64
HGRN_HEAD = 128
SUB = 16
GROUP_W = 256
RWKV_GROUPS_PER_STEP = 8
HGRN_GROUPS_PER_STEP = 8
MAX_CHUNK_LOG_DECAY = 80.0
NORM_EPS = 1e-6
GN_EPS = 64e-5
EXP_M05 = float(np.exp(-0.5))
VMEM_LIMIT = 56 * 1024 * 1024

MM_TILE = 1024
IN_PROJ_TILE = 768
SWEEP_ROWS = 512
ROW_BLOCK = 256
EPILOGUE_ROWS = 128
ADA_COLS = 512


def _pick(n, target, unit=LANE):
    best = None
    for m in range(unit, min(n, target) + 1, unit):
        if n % m == 0:
            best = m
    return best if best is not None else n


def _sigmoid(z):
    return 0.5 * jnp.tanh(0.5 * z) + 0.5


def _mxu(a, b):
    return jnp.dot(a, b, preferred_element_type=F32)


def _mxu_nt(a, b):
    return lax.dot_general(a, b, (((1,), (1,)), ((), ())), preferred_element_type=F32)


def _mxu_tn(a, b):
    return lax.dot_general(a, b, (((0,), (0,)), ((), ())), preferred_element_type=F32)


def _split3(z):
    hi = z.astype(BF16)
    rest = z - hi.astype(F32)
    mid = rest.astype(BF16)
    return hi, mid, (rest - mid.astype(F32)).astype(BF16)


def _select_sum(z, sel):
    hi, mid, _ = _split3(z)
    return _mxu(hi, sel) + _mxu(mid, sel)


def _head_sum(z, hs, hst):
    return _select_sum(_select_sum(z, hs), hst)


def _head_indicators(dr):
    hs = np.arange(dr)[:, None] // RWKV_HEAD == np.arange(LANE)[None, :]
    return jnp.asarray(hs, BF16), jnp.asarray(hs.T, BF16)


def _params(sem, vmem=None):
    return pltpu.CompilerParams(dimension_semantics=sem, vmem_limit_bytes=vmem)


def _ada_kernel(c_ref, w_ref, b_ref, o_ref):
    cv = c_ref[...]
    rows = cv.shape[0]
    hi, mid, lo = (t.astype(F32) for t in _split3(cv * _sigmoid(cv)))
    s3 = jnp.concatenate([hi, mid, lo], axis=0).astype(BF16)
    w = w_ref[...]
    w_hi = w.astype(BF16)
    w_lo = (w - w_hi.astype(F32)).astype(BF16)
    p = _mxu(s3, w_hi)
    q = _mxu(s3[:2 * rows], w_lo)
    o_ref[...] = (p[:rows] + p[rows:2 * rows] + p[2 * rows:] + q[:rows] + q[rows:]) + b_ref[...]


def _ada(cvec, w, b):
    rows, d = cvec.shape
    n = w.shape[1]
    tn = _pick(n, ADA_COLS)
    return pl.pallas_call(
        _ada_kernel,
        grid=(n // tn,),
        in_specs=[pl.BlockSpec((rows, d), lambda j: (0, 0)),
                  pl.BlockSpec((d, tn), lambda j: (0, j)),
                  pl.BlockSpec((1, tn), lambda j: (0, j))],
        out_specs=pl.BlockSpec((rows, tn), lambda j: (0, j)),
        out_shape=jax.ShapeDtypeStruct((rows, n), F32),
        compiler_params=_params(("arbitrary",), VMEM_LIMIT),
        name="ada_mod",
    )(cvec, w, b.reshape(1, n))


def _prep_kernel(ctx_ref, x_ref, g_ref, mod_ref, o_ref, *, ncb, d):
    is_ctx = pl.program_id(0) < ncb
    rows = jnp.where(is_ctx, ctx_ref[...], x_ref[...])
    ms = jnp.mean(rows * rows, axis=-1, keepdims=True)
    hn = rows * lax.rsqrt(ms + NORM_EPS) * g_ref[...]
    shift = jnp.where(is_ctx, mod_ref[1:2, 0:d], mod_ref[0:1, 0:d])
    scale = jnp.where(is_ctx, mod_ref[1:2, d:2 * d], mod_ref[0:1, d:2 * d])
    o_ref[...] = (hn * (1.0 + scale) + shift).astype(o_ref.dtype)


def _prep(ctx2, x2, g, mod):
    lc, d = ctx2.shape
    l = x2.shape[0]
    tb = _pick(int(np.gcd(lc, l)), ROW_BLOCK, 8)
    ncb = lc // tb
    return pl.pallas_call(
        functools.partial(_prep_kernel, ncb=ncb, d=d),
        grid=((lc + l) // tb,),
        in_specs=[pl.BlockSpec((tb, d), lambda i: (jnp.minimum(i, ncb - 1), 0)),
                  pl.BlockSpec((tb, d), lambda i: (jnp.maximum(i - ncb, 0), 0)),
                  pl.BlockSpec((1, d), lambda i: (0, 0)),
                  pl.BlockSpec(mod.shape, lambda i: (0, 0))],
        out_specs=pl.BlockSpec((tb, d), lambda i: (i, 0)),
        out_shape=jax.ShapeDtypeStruct((lc + l, d), PROJ_DTYPE),
        compiler_params=_params(("arbitrary",)),
        name="norm_modulate",
    )(ctx2, x2, g.reshape(1, d), mod)


def _mm_kernel(x_ref, w_ref, o_ref, *, act):
    acc = jnp.dot(x_ref[...], w_ref[...], preferred_element_type=F32)
    if act == "relu2":
        acc = jnp.square(jnp.maximum(acc, 0.0))
    o_ref[...] = acc.astype(o_ref.dtype)


def _mm(x, w, out_dtype, act=None, tm_target=MM_TILE, tn_target=MM_TILE, name="matmul"):
    m, k = x.shape
    n = w.shape[1]
    tm = _pick(m, tm_target)
    tn = _pick(n, tn_target)
    return pl.pallas_call(
        functools.partial(_mm_kernel, act=act),
        grid=(m // tm, n // tn),
        in_specs=[pl.BlockSpec((tm, k), lambda i, j: (i, 0)),
                  pl.BlockSpec((k, tn), lambda i, j: (0, j))],
        out_specs=pl.BlockSpec((tm, tn), lambda i, j: (i, j)),
        out_shape=jax.ShapeDtypeStruct((m, n), out_dtype),
        compiler_params=_params(("arbitrary", "arbitrary"), VMEM_LIMIT),
        name=name,
    )(x, w)


def _mm_wcast_kernel(x_ref, w_ref, o_ref, w_cast):
    @pl.when(pl.program_id(1) == 0)
    def _():
        w_cast[...] = w_ref[...].astype(w_cast.dtype)

    o_ref[...] = jnp.dot(x_ref[...], w_cast[...], preferred_element_type=F32).astype(o_ref.dtype)


def _mm_wcast(x, w, ncols, out_dtype, tm_target=MM_TILE, tn_target=MM_TILE, name="matmul_wcast"):
    m, k = x.shape
    tm, tn = _pick(m, tm_target), _pick(ncols, tn_target)
    return pl.pallas_call(
        _mm_wcast_kernel,
        grid=(ncols // tn, m // tm),
        in_specs=[pl.BlockSpec((tm, k), lambda j, i: (i, 0)),
                  pl.BlockSpec((k, tn), lambda j, i: (0, j))],
        out_specs=pl.BlockSpec((tm, tn), lambda j, i: (i, j)),
        out_shape=jax.ShapeDtypeStruct((m, ncols), out_dtype),
        scratch_shapes=[pltpu.VMEM((k, tn), PROJ_DTYPE)],
        compiler_params=_params(("arbitrary", "arbitrary"), VMEM_LIMIT),
        name=name,
    )(x, w)


def _res_proj_kernel(a_ref, w_ref, res_hbm, mod_ref, g_ref, o_ref, res_buf, sem, *, d, gate_chunk):
    i, kk = pl.program_id(0), pl.program_id(1)
    tm = res_buf.shape[0]
    res_copy = pltpu.make_async_copy(res_hbm.at[pl.ds(pl.multiple_of(i * tm, tm), tm), :], res_buf, sem)

    @pl.when(kk == 0)
    def _():
        res_copy.start()
        o_ref[...] = jnp.zeros_like(o_ref)

    o_ref[...] += jnp.dot(a_ref[...], w_ref[...], preferred_element_type=F32)

    @pl.when(kk == pl.num_programs(1) - 1)
    def _():
        res_copy.wait()
        gate = mod_ref[0:1, gate_chunk * d:(gate_chunk + 1) * d] * g_ref[...]
        rb = _pick(tm, EPILOGUE_ROWS, 8)
        for r0 in range(0, tm, rb):
            rows = slice(r0, r0 + rb)
            acc = o_ref[rows, :]
            scale = lax.rsqrt(jnp.mean(acc * acc, axis=-1, keepdims=True) + NORM_EPS)
            o_ref[rows, :] = res_buf[rows, :] + gate * (acc * scale)


def _residual_projection(a, w, res, mod, g_post, gate_chunk, tm_target=SWEEP_ROWS, tk_target=MM_TILE,
                         name="residual_projection"):
    m, k = a.shape
    d = w.shape[1]
    tm, tk = _pick(m, tm_target, 8), _pick(k, tk_target)
    full = lambda z: pl.BlockSpec(z.shape, lambda i, kk: (0,) * z.ndim)
    gp = g_post.reshape(1, d)
    return pl.pallas_call(
        functools.partial(_res_proj_kernel, d=d, gate_chunk=gate_chunk),
        grid=(m // tm, k // tk),
        in_specs=[pl.BlockSpec((tm, tk), lambda i, kk: (i, kk)),
                  pl.BlockSpec((tk, d), lambda i, kk: (kk, 0)),
                  pl.BlockSpec(memory_space=pl.ANY), full(mod), full(gp)],
        out_specs=pl.BlockSpec((tm, d), lambda i, kk: (i, 0)),
        out_shape=jax.ShapeDtypeStruct((m, d), F32),
        scratch_shapes=[pltpu.VMEM((tm, d), F32), pltpu.SemaphoreType.DMA(())],
        compiler_params=_params(("arbitrary", "arbitrary"), VMEM_LIMIT),
        name=name,
    )(a, w, res, mod, gp)


RING = 4


def _rwkv_feat_kernel(pm_hbm, pl_hbm, sel_ref, mu_ref, sell_ref, mul_ref, w0_ref, w2_ref, a0_ref, a2_ref, g2_ref,
                      kk_ref, ka_ref, rk_ref, hs_ref, hst_ref,
                      shared_o, perdir_o, bonus_o, gate_o, ring_m, ring_l, sems,
                      *, ncb, nrows, dr, rd, ra):
    i = pl.program_id(0)
    nb = pl.num_programs(0)
    c = ring_m.shape[1]
    sources = ((pm_hbm, ring_m), (pl_hbm, ring_l))

    def fetch(blk):
        slot = blk % RING
        rows = pl.ds(pl.multiple_of(blk * c, c), c)
        return [pltpu.make_async_copy(src.at[rows, :], ring.at[slot], sems.at[n, slot])
                for n, (src, ring) in enumerate(sources)]

    def start(blk):
        for cp in fetch(blk):
            cp.start()

    def wait(blk):
        for cp in fetch(blk):
            cp.wait()

    @pl.when(i == 0)
    def _():
        start(0)
        start(1)
        start(2)
        wait(0)
        wait(1)

    @pl.when((i > 0) & (i + 1 < nb))
    def _():
        wait(i + 1)

    @pl.when((i > 0) & (i + 2 < nb))
    def _():
        start(i + 2)

    is_ctx = i < ncb
    xi = i - ncb
    s_cur, s_prv, s_nxt = i % RING, jnp.maximum(i - 1, 0) % RING, jnp.minimum(i + 1, nb - 1) % RING

    def shift_lerp(ring, sel, mu):
        cur, prv, nxt = ring[s_cur], ring[s_prv], ring[s_nxt]
        row = lax.broadcasted_iota(jnp.int32, cur.shape, 0)
        prev_last = jnp.where(is_ctx & (i > 0), prv[c - 1:c, :], 0.0)
        next_first = jnp.where(is_ctx & (i < ncb - 1), nxt[0:1, :], 0.0)
        before = jnp.where(row == 0, prev_last, pltpu.roll(cur, 1, 0))
        after = jnp.where(row == c - 1, next_first, pltpu.roll(cur, c - 1, 0))
        above = jnp.where(is_ctx, before, jnp.where(xi > 0, prv, 0.0))
        below = jnp.where(is_ctx, after, jnp.where(xi < nrows - 1, nxt, 0.0))
        shifted = jnp.where(sel == 0, before, jnp.where(sel == 1, after, jnp.where(sel == 2, above, below)))
        return cur + mu * (shifted - cur)

    m = shift_lerp(ring_m, sel_ref[...], mu_ref[...])
    ml = shift_lerp(ring_l, sell_ref[...], mul_ref[...])
    r, k, v = m[:, 0:dr], m[:, dr:2 * dr], m[:, 2 * dr:3 * dr]
    wl = (ml[:, 0:rd], ml[:, rd:2 * rd])
    al = (ml[:, 2 * rd:2 * rd + ra], ml[:, 2 * rd + ra:2 * rd + 2 * ra])
    gl = ml[:, 2 * rd + 2 * ra:]

    hs, hst = hs_ref[...], hst_ref[...]

    def headsum(z):
        return _head_sum(z, hs, hst)

    kkf = k * kk_ref[...]
    kk = kkf * lax.rsqrt(headsum(kkf * kkf) + 1e-12)
    shared_o[0] = r
    shared_o[1] = v
    shared_o[2] = -kk
    for d in range(2):
        wd = _mxu(jnp.tanh(wl[d]).astype(BF16), w2_ref[d]) + w0_ref[d:d + 1, :]
        perdir_o[d, 0] = -EXP_M05 * _sigmoid(wd)
        ad = _sigmoid(_mxu(al[d].astype(BF16), a2_ref[d]) + a0_ref[d:d + 1, :])
        perdir_o[d, 1] = k * (1.0 + (ad - 1.0) * ka_ref[...])
        perdir_o[d, 2] = kk * ad
    bonus_o[...] = (headsum(r * k * rk_ref[...]) * v).astype(bonus_o.dtype)
    gate_o[...] = _mxu(_sigmoid(gl).astype(BF16), g2_ref[...]).astype(gate_o.dtype)


def _rwkv_features(p_main, p_lora, sel, mu, sel_l, mu_l, w0, w2p, a0, a2p, g2p, k_k, k_a, r_k, lc, l):
    n = p_main.shape[0]
    dr = k_k.shape[0]
    rd, ra = w2p.shape[1], a2p.shape[1]
    nb, ncb = n // CHUNK, lc // CHUNK
    nheads = dr // RWKV_HEAD
    assert nheads <= LANE
    hs, hst = _head_indicators(dr)
    full = lambda a: pl.BlockSpec(a.shape, lambda i: (0,) * a.ndim)
    row1 = lambda a: a.reshape(1, -1)
    consts = [sel, mu, sel_l, mu_l, w0, w2p.astype(BF16), a0, a2p.astype(BF16), g2p.astype(BF16), row1(k_k),
              row1(k_a), row1(r_k), hs, hst]
    assert nb >= 3
    outs = pl.pallas_call(
        functools.partial(_rwkv_feat_kernel, ncb=ncb, nrows=l // CHUNK, dr=dr, rd=rd, ra=ra),
        grid=(nb,),
        in_specs=[pl.BlockSpec(memory_space=pl.ANY)] * 2 + [full(a) for a in consts],
        out_specs=[pl.BlockSpec((3, CHUNK, dr), lambda i: (0, i, 0)),
                   pl.BlockSpec((2, 3, CHUNK, dr), lambda i: (0, 0, i, 0))]
                  + [pl.BlockSpec((CHUNK, dr), lambda i: (i, 0))] * 2,
        out_shape=[jax.ShapeDtypeStruct((3, n, dr), F32), jax.ShapeDtypeStruct((2, 3, n, dr), F32)]
                  + [jax.ShapeDtypeStruct((n, dr), PROJ_DTYPE)] * 2,
        scratch_shapes=[pltpu.VMEM((RING, CHUNK, p_main.shape[1]), F32), pltpu.VMEM((RING, CHUNK, p_lora.shape[1]), F32),
                        pltpu.SemaphoreType.DMA((2, RING))],
        compiler_params=_params(("arbitrary",), VMEM_LIMIT),
        name="rwkv_features",
    )(p_main, p_lora, *consts)
    return outs


def _block_diag_mask(n, w):
    r = lax.broadcasted_iota(jnp.int32, (n, n), 0)
    c = lax.broadcasted_iota(jnp.int32, (n, n), 1)
    return (r // w) == (c // w)


def _cumsum_rows(z, rev):
    c = z.shape[0]
    t = lax.broadcasted_iota(jnp.int32, (c, c), 0)
    s = lax.broadcasted_iota(jnp.int32, (c, c), 1)
    tri = ((s >= t) if rev else (s <= t)).astype(BF16)
    hi, mid, lo = _split3(z)
    return _mxu(tri, hi) + _mxu(tri, mid) + _mxu(tri, lo)


def _rwkv_chunks(chains):
    c, w = chains[0][0].shape
    nh = w // c
    every = range(len(chains))
    revs = [ch[7] for ch in chains]
    bd = _block_diag_mask(w, c)
    t_i = lax.broadcasted_iota(jnp.int32, (c, w), 0)
    s_i = lax.broadcasted_iota(jnp.int32, (c, w), 1) % c
    eye = (s_i == t_i).astype(F32)
    strict = {False: s_i < t_i, True: s_i > t_i}
    incl = {False: s_i <= t_i, True: s_i >= t_i}

    def expand(z):
        return jnp.where(bd, jnp.concatenate([z] * nh, axis=0), jnp.zeros((), z.dtype))

    def packed_mm(lhs, rhs):
        return _mxu(lhs.astype(BF16), expand(rhs.astype(BF16)))

    bc = [_cumsum_rows(ch[1], ch[7]) for ch in chains]
    ar, bk, v16 = [], [], []
    for i, (r, lw, k, v, a, b, _, _) in enumerate(chains):
        e_in, e_ex, e_ng = jnp.exp(bc[i]), jnp.exp(bc[i] - lw), jnp.exp(-bc[i])
        ar.append(jnp.concatenate([a * e_ex, r * e_in], axis=0).astype(BF16))
        bk.append(jnp.concatenate([b * e_ng, k * e_ng], axis=0).astype(BF16))
        v16.append(v.astype(BF16))
    gb = [_mxu_nt(ar[i], expand(bk[i][:c])) for i in every]
    gk = [_mxu_nt(ar[i], expand(bk[i][c:])) for i in every]
    l_ab = [jnp.where(strict[revs[i]], gb[i][:c], 0.0) for i in every]
    m_rb = [jnp.where(incl[revs[i]], gb[i][c:], 0.0) for i in every]
    lmk = [jnp.concatenate([jnp.where(strict[revs[i]], gk[i][:c], 0.0),
                            jnp.where(incl[revs[i]], gk[i][c:], 0.0)], axis=0) for i in every]
    t_m = [eye + l_ab[i] for i in every]
    p_m = [packed_mm(l_ab[i], l_ab[i]) for i in every]
    for _ in range(int(np.log2(c)) - 2):
        tp = [packed_mm(jnp.concatenate([t_m[i], p_m[i]], axis=0), p_m[i]) for i in every]
        t_m = [t_m[i] + tp[i][:c] for i in every]
        p_m = [tp[i][c:] for i in every]
    t_m = [t_m[i] + packed_mm(t_m[i], p_m[i]) for i in every]

    ars = [_mxu_nt(ar[i], chains[i][6].astype(BF16)) for i in every]
    lm = [_mxu(lmk[i].astype(BF16), expand(v16[i])) for i in every]
    u16 = [packed_mm(t_m[i], ars[i][:c] + lm[i][:c]).astype(BF16) for i in every]
    ys = [ars[i][c:] + _mxu(m_rb[i].astype(BF16), expand(u16[i])) + lm[i][c:] for i in every]
    ds = [_mxu_tn(jnp.concatenate([u16[i], v16[i]], axis=0), bk[i]) for i in every]
    s_new = []
    for i in every:
        last = bc[i][0:1] if revs[i] else bc[i][c - 1:c]
        s_new.append((chains[i][6] + jnp.where(bd, ds[i], 0.0)) * jnp.exp(last))
    return ys, s_new


def _cast_jobs(weights, ng, nb):
    in_specs, out_specs, shapes = [], [], []
    for w, col0, cols in weights:
        rows = w.shape[0]
        n = max(m for m in range(1, ng * nb + 1) if rows % m == 0 and (rows // m) % 16 == 0)
        slab = lambda g, c, n=n: jnp.minimum(g * nb + c, n - 1)
        in_specs.append(pl.BlockSpec((pl.Element(rows // n), pl.Element(cols)),
                                     lambda g, c, s=slab, h=rows // n, col0=col0: (s(g, c) * h, col0)))
        out_specs.append(pl.BlockSpec((rows // n, cols), lambda g, c, s=slab: (s(g, c), 0)))
        shapes.append(jax.ShapeDtypeStruct((rows, cols), PROJ_DTYPE))
    return in_specs, out_specs, shapes


def _run_cast_jobs(src_refs, dst_refs):
    for src, dst in zip(src_refs, dst_refs):
        dst[...] = src[...].astype(dst.dtype)


def _rwkv_scan_kernel(shared_f, perdir_f, shared_r, perdir_r, *rest):
    n_cast = (len(rest) - 3) // 2
    yf_o, yr_o = rest[n_cast:n_cast + 2]
    s_ref = rest[-1]

    @pl.when(pl.program_id(1) == 0)
    def _():
        s_ref[...] = jnp.zeros_like(s_ref)

    ngs = shared_f.shape[2] // GROUP_W
    chains = []
    for d, (sh, pd) in enumerate(((shared_f, perdir_f), (shared_r, perdir_r))):
        for g in range(ngs):
            sl = slice(g * GROUP_W, (g + 1) * GROUP_W)
            r, v, a = (sh[j, :, sl] for j in range(3))
            lw, k, b = (pd[j, :, sl] for j in range(3))
            chains.append((r, lw, k, v, a, b, s_ref[d, g], d == 1))
    ys, s_new = _rwkv_chunks(chains)
    _run_cast_jobs(rest[:n_cast], rest[n_cast + 2:-1])
    for d, o_ref in enumerate((yf_o, yr_o)):
        for g in range(ngs):
            o_ref[:, g * GROUP_W:(g + 1) * GROUP_W] = ys[d * ngs + g]
            s_ref[d, g] = s_new[d * ngs + g]


def _rev_chunk(c, ncb, nb):
    return jnp.where(c < ncb, ncb - 1 - c, nb - 1 - (c - ncb))


def _rwkv_scan(shared, perdir, lc, cast_weights):
    _, n, dr = shared.shape
    nb, ncb = n // CHUNK, lc // CHUNK
    bw = _pick(dr, RWKV_GROUPS_PER_STEP * GROUP_W, GROUP_W)
    rc = lambda c: _rev_chunk(c, ncb, nb)
    fwd = pl.BlockSpec((CHUNK, bw), lambda g, c: (c, g))
    rev = pl.BlockSpec((CHUNK, bw), lambda g, c: (rc(c), g))
    in_specs = [pl.BlockSpec((3, CHUNK, bw), lambda g, c: (0, c, g)),
                pl.BlockSpec((None, 3, CHUNK, bw), lambda g, c: (0, 0, c, g)),
                pl.BlockSpec((3, CHUNK, bw), lambda g, c: (0, rc(c), g)),
                pl.BlockSpec((None, 3, CHUNK, bw), lambda g, c: (1, 0, rc(c), g))]
    cast_in, cast_out, cast_shapes = _cast_jobs(cast_weights, dr // bw, nb)
    return pl.pallas_call(
        _rwkv_scan_kernel,
        grid=(dr // bw, nb),
        in_specs=in_specs + cast_in,
        out_specs=[fwd, rev] + cast_out,
        out_shape=[jax.ShapeDtypeStruct((n, dr), F32)] * 2 + cast_shapes,
        scratch_shapes=[pltpu.VMEM((2, bw // GROUP_W, GROUP_W, GROUP_W), F32)],
        compiler_params=_params(("arbitrary", "arbitrary"), VMEM_LIMIT),
        name="rwkv_scan",
    )(shared, perdir, shared, perdir, *[w for w, _, _ in cast_weights])


def _hgrn_chunk(q, k, v, lf, s_vk, rev):
    c, w = q.shape
    nsb = c // SUB
    bc = _cumsum_rows(lf, rev)
    last = bc[0:1] if rev else bc[c - 1:c]
    bdh = _block_diag_mask(w, HGRN_HEAD)
    v16 = v.astype(BF16)
    o_inter = _mxu_nt((q * jnp.exp(bc)).astype(BF16), s_vk.astype(BF16))
    s_new = s_vk * jnp.exp(last) + jnp.where(bdh, _mxu_tn(v16, (k * jnp.exp(last - bc)).astype(BF16)), 0.0)

    lane2 = lax.broadcasted_iota(jnp.int32, (c, w), 1)
    row2 = lax.broadcasted_iota(jnp.int32, (c, w), 0)
    zpad = jnp.zeros((HGRN_HEAD - c, w), BF16)

    def expand(z):
        parts = []
        for h in range(w // HGRN_HEAD):
            parts += [jnp.where(lane2 // HGRN_HEAD == h, z, jnp.zeros((), BF16)), zpad]
        return jnp.concatenate(parts, axis=0)

    ones_bd = bdh.astype(BF16)
    t3 = lax.broadcasted_iota(jnp.int32, (SUB, SUB, w), 0)
    s3 = lax.broadcasted_iota(jnp.int32, (SUB, SUB, w), 1)
    l3 = lax.broadcasted_iota(jnp.int32, (SUB, SUB, w), 2) % HGRN_HEAD
    causal = (s3 >= t3) if rev else (s3 <= t3)
    a_rows = []
    for i in range(nsb):
        lo, hi = SUB * i, SUB * (i + 1)
        qi, ki, bi = q[lo:hi], k[lo:hi], bc[lo:hi]
        d3 = bi[:, None, :] - bi[None, :, :]
        x3 = jnp.where(causal, (qi[:, None, :] * ki[None, :, :]) * jnp.exp(jnp.minimum(d3, 0.0)), 0.0)
        r3 = _mxu(x3.reshape(SUB * SUB, w).astype(BF16), ones_bd).reshape(SUB, SUB, w)
        a_i = jnp.sum(jnp.where(l3 == s3 + lo, r3, 0.0), axis=1)
        if (not rev and i > 0) or (rev and i < nsb - 1):
            beta = bc[hi:hi + 1] if rev else bc[lo - 1:lo]
            earlier = (row2 >= hi) if rev else (row2 < lo)
            kp = jnp.where(earlier, k * jnp.exp(jnp.minimum(beta - bc, 0.0)), 0.0)
            a_i = a_i + _mxu_nt((qi * jnp.exp(bi - beta)).astype(BF16), expand(kp.astype(BF16)))
        a_rows.append(a_i)
    o = o_inter + _mxu(jnp.concatenate(a_rows, axis=0).astype(BF16), expand(v16))
    return o, s_new


def _hgrn_chunks_bounded(chains):
    c, w = chains[0][0].shape
    every = range(len(chains))
    revs = [ch[5] for ch in chains]
    bdh = _block_diag_mask(w, HGRN_HEAD)
    lane2 = lax.broadcasted_iota(jnp.int32, (c, w), 1)
    t_i = lax.broadcasted_iota(jnp.int32, (c, w), 0)
    s_i = lane2 % HGRN_HEAD
    incl = {False: s_i <= t_i, True: s_i >= t_i}
    zpad = jnp.zeros((HGRN_HEAD - c, w), BF16)

    def expand(z):
        parts = []
        for h in range(w // HGRN_HEAD):
            parts += [jnp.where(lane2 // HGRN_HEAD == h, z, jnp.zeros((), BF16)), zpad]
        return jnp.concatenate(parts, axis=0)

    bc = [_cumsum_rows(ch[3], ch[5]) for ch in chains]
    last = [bc[i][0:1] if revs[i] else bc[i][c - 1:c] for i in every]
    qt = [(chains[i][0] * jnp.exp(bc[i])).astype(BF16) for i in every]
    kt = [(chains[i][1] * jnp.exp(-bc[i])).astype(BF16) for i in every]
    kd = [(chains[i][1] * jnp.exp(last[i] - bc[i])).astype(BF16) for i in every]
    v16 = [chains[i][2].astype(BF16) for i in every]
    o_inter = [_mxu_nt(qt[i], chains[i][4].astype(BF16)) for i in every]
    a = [jnp.where(incl[revs[i]], _mxu_nt(qt[i], expand(kt[i])), 0.0).astype(BF16) for i in every]
    o = [o_inter[i] + _mxu(a[i], expand(v16[i])) for i in every]
    ds = [_mxu_tn(v16[i], kd[i]) for i in every]
    s_new = [chains[i][4] * jnp.exp(last[i]) + jnp.where(bdh, ds[i], 0.0) for i in every]
    return o, s_new


def _hgrn_scan_kernel(qf, ff, vf, qr, fr, vr, lg_ref, *rest, layer):
    n_cast = (len(rest) - 3) // 2
    of_o, or_o = rest[n_cast:n_cast + 2]
    s_ref = rest[-1]

    @pl.when(pl.program_id(1) == 0)
    def _():
        s_ref[...] = jnp.zeros_like(s_ref)

    lg = lg_ref[...]
    e = jnp.exp(lg - jnp.max(lg, axis=0, keepdims=True))
    lb = jnp.sum(e[:layer + 1], axis=0) / jnp.sum(e, axis=0)
    ngs = qf.shape[1] // GROUP_W
    chunk = qf.shape[0]

    def chains():
        out = []
        for d, (q_ref, f_ref, v_ref) in enumerate(((qf, ff, vf), (qr, fr, vr))):
            fd = lb[d:d + 1, :] + (1.0 - lb[d:d + 1, :]) * _sigmoid(f_ref[...])
            qv = q_ref[...]
            qh, kh, vh, lf = qv * _sigmoid(qv), 1.0 - fd, v_ref[...], jnp.log(fd)
            for g in range(ngs):
                sl = slice(g * GROUP_W, (g + 1) * GROUP_W)
                out.append((qh[:, sl], kh[:, sl], vh[:, sl], lf[:, sl], s_ref[d, g], d == 1))
        return out

    def emit(o, s_new):
        for d, o_ref in enumerate((of_o, or_o)):
            for g in range(ngs):
                o_ref[:, g * GROUP_W:(g + 1) * GROUP_W] = o[d * ngs + g]
                s_ref[d, g] = s_new[d * ngs + g]
        _run_cast_jobs(rest[:n_cast], rest[n_cast + 2:-1])

    bounded = chunk * jnp.min(jnp.log(lb)) >= -MAX_CHUNK_LOG_DECAY

    @pl.when(bounded)
    def _():
        emit(*_hgrn_chunks_bounded(chains()))

    @pl.when(jnp.logical_not(bounded))
    def _():
        res = [_hgrn_chunk(*ch) for ch in chains()]
        emit([r[0] for r in res], [r[1] for r in res])


def _hgrn_scan(p_h, lb_logits, lc, layer, cast_weights):
    n = p_h.shape[0]
    dh = lb_logits.shape[2]
    nb, ncb = n // CHUNK, lc // CHUNK
    bw = _pick(dh, HGRN_GROUPS_PER_STEP * GROUP_W, GROUP_W)
    ng = dh // bw
    blk = (CHUNK, bw)
    fwd = lambda sec: pl.BlockSpec(blk, lambda g, c: (c, sec * ng + g))
    rev = lambda sec: pl.BlockSpec(blk, lambda g, c: (_rev_chunk(c, ncb, nb), sec * ng + g))
    cast_in, cast_out, cast_shapes = _cast_jobs(cast_weights, ng, nb)
    return pl.pallas_call(
        functools.partial(_hgrn_scan_kernel, layer=layer),
        grid=(ng, nb),
        in_specs=[fwd(0), fwd(1), fwd(3), rev(0), rev(2), rev(3),
                  pl.BlockSpec((lb_logits.shape[0], 2, bw), lambda g, c: (0, 0, g))] + cast_in,
        out_specs=[pl.BlockSpec(blk, lambda g, c: (c, g)),
                   pl.BlockSpec(blk, lambda g, c: (_rev_chunk(c, ncb, nb), g))] + cast_out,
        out_shape=[jax.ShapeDtypeStruct((n, dh), F32)] * 2 + cast_shapes,
        scratch_shapes=[pltpu.VMEM((2, bw // GROUP_W, GROUP_W, GROUP_W), F32)],
        compiler_params=_params(("arbitrary", "arbitrary"), VMEM_LIMIT),
        name="hgrn_scan",
    )(p_h, p_h, p_h, p_h, p_h, p_h, lb_logits, *[w for w, _, _ in cast_weights])


def _mix_out_kernel(yf, yr, bonus, gate, of, orv, g_ref, lng, lnb, hng, hs_ref, hst_ref, u_o, *, dr, dh):
    hs, hst = hs_ref[...], hst_ref[...]

    def headmean(z):
        return _head_sum(z, hs, hst) * (1.0 / RWKV_HEAD)

    y = yf[...] + yr[...]
    yc = y - headmean(y)
    yn = yc * lax.rsqrt(headmean(yc * yc) + GN_EPS) * lng[...] + lnb[...]
    u_o[:, 0:dr] = ((yn + bonus[...].astype(F32)) * gate[...].astype(F32)).astype(u_o.dtype)
    o = of[...] + orv[...]
    g = g_ref[...]
    sg = g * _sigmoid(g)
    for h in range(dh // HGRN_HEAD):
        sl = slice(h * HGRN_HEAD, (h + 1) * HGRN_HEAD)
        oh = o[:, sl]
        on = oh * lax.rsqrt(jnp.mean(oh * oh, axis=-1, keepdims=True) + NORM_EPS) * hng[...]
        u_o[:, dr + h * HGRN_HEAD:dr + (h + 1) * HGRN_HEAD] = (on * sg[:, sl]).astype(u_o.dtype)


def _mix_out(yf, yr, bonus, gate, of, orv, p_h, ln_g, ln_b, hg, lc, l):
    dr, dh = yf.shape[1], of.shape[1]
    tb = _pick(int(np.gcd(lc, l)), EPILOGUE_ROWS, 8)
    off = lc // tb
    hs, hst = _head_indicators(dr)
    rows = lambda wd: pl.BlockSpec((tb, wd), lambda i: (i + off, 0))
    full = lambda a: pl.BlockSpec(a.shape, lambda i: (0,) * a.ndim)
    consts = [ln_g.reshape(1, dr), ln_b.reshape(1, dr), hg.reshape(1, HGRN_HEAD), hs, hst]
    return pl.pallas_call(
        functools.partial(_mix_out_kernel, dr=dr, dh=dh),
        grid=(l // tb,),
        in_specs=[rows(dr)] * 4 + [rows(dh)] * 2 + [pl.BlockSpec((tb, dh), lambda i: (i + off, 4))]
                 + [full(a) for a in consts],
        out_specs=pl.BlockSpec((tb, dr + dh), lambda i: (i, 0)),
        out_shape=jax.ShapeDtypeStruct((l, dr + dh), PROJ_DTYPE),
        compiler_params=_params(("arbitrary",), VMEM_LIMIT),
        name="mix_out",
    )(yf, yr, bonus, gate, of, orv, p_h, *consts)


def _res_mix_kernel(ux_ref, x_ref, mod_ref, gpost_ref, gpre_ref, x1_o, h2_o, *, d):
    ux = ux_ref[...]
    nrm = ux * lax.rsqrt(jnp.mean(ux * ux, axis=-1, keepdims=True) + NORM_EPS) * gpost_ref[...]
    x1 = x_ref[...] + mod_ref[0:1, 2 * d:3 * d] * nrm
    x1_o[...] = x1
    hn = x1 * lax.rsqrt(jnp.mean(x1 * x1, axis=-1, keepdims=True) + NORM_EPS) * gpre_ref[...]
    h2_o[...] = (hn * (1.0 + mod_ref[0:1, 4 * d:5 * d]) + mod_ref[0:1, 3 * d:4 * d]).astype(h2_o.dtype)


def _res_mix(ux, x2, mod, g_post, g_pre):
    l, d = x2.shape
    tb = _pick(l, ROW_BLOCK, 8)
    rows = pl.BlockSpec((tb, d), lambda i: (i, 0))
    full = lambda a: pl.BlockSpec(a.shape, lambda i: (0,) * a.ndim)
    gp, gq = g_post.reshape(1, d), g_pre.reshape(1, d)
    return pl.pallas_call(
        functools.partial(_res_mix_kernel, d=d),
        grid=(l // tb,),
        in_specs=[rows, rows, full(mod), full(gp), full(gq)],
        out_specs=[rows, rows],
        out_shape=[jax.ShapeDtypeStruct((l, d), F32), jax.ShapeDtypeStruct((l, d), PROJ_DTYPE)],
        compiler_params=_params(("arbitrary",), VMEM_LIMIT),
        name="residual_mix",
    )(ux, x2, mod, gp, gq)


def _pad_cols(a, width):
    return jnp.pad(a, ((0, 0), (0, width - a.shape[1])))


def _pad_rows(a, height):
    return jnp.pad(a, ((0, height - a.shape[0]), (0, 0)))


def _round_up(n, m):
    return (n + m - 1) // m * m


def kernel(x, c, ctx, c_ctx, w_ada, b_ada, g_mix_pre, g_mix_post, g_ffn_pre, g_ffn_post, w_in, mu_shift, w0, w2, a0,
           a2, g2, k_k, k_a, r_k, ln_x_g, ln_x_b, hgrn_lb_logits, hgrn_norm_g, w_out, w_ff1, w_ff2):
    assert x.shape[0] == 1 and w_in.shape[0] == 1, "single batch, single layer"
    layer = 0
    x2, ctx2 = x[0], ctx[0]
    l, d = x2.shape
    lc = ctx2.shape[0]
    dr, dh = k_k.shape[1], hgrn_lb_logits.shape[2]
    rd_raw, ra_raw, rg_raw = w2.shape[2], a2.shape[2], g2.shape[1]
    rd, ra, rg = (_round_up(v, LANE) for v in (rd_raw, ra_raw, rg_raw))
    assert l % GRID_W == 0 and lc % CHUNK == 0 and dr % GROUP_W == 0 and dh % GROUP_W == 0

    cvec = jnp.concatenate([c, c_ctx[None, :], jnp.zeros((6, d), F32)], axis=0)
    mod = _ada(cvec, w_ada[layer], b_ada[layer])

    wi = w_in[layer]
    mu = mu_shift[layer][None, :]
    sel = (jnp.arange(mu.shape[1], dtype=jnp.int32) % 4)[None, :]
    o = 3 * dr
    cuts = []
    for raw, padded in ((rd_raw, rd), (rd_raw, rd), (ra_raw, ra), (ra_raw, ra), (rg_raw, rg)):
        cuts.append((o, o + raw, padded))
        o += raw
    rw_cols = o
    regroup = lambda a: jnp.concatenate([_pad_cols(a[:, s:e], wd) for s, e, wd in cuts], axis=1)
    w_lora = regroup(wi)
    w2p = jnp.stack([_pad_rows(w2[layer, dd], rd) for dd in range(2)])
    a2p = jnp.stack([_pad_rows(a2[layer, dd], ra) for dd in range(2)])
    g2p = _pad_rows(g2[layer], rg)

    h = _prep(ctx2, x2, g_mix_pre[layer], mod)
    p_main = _mm_wcast(h, wi, 3 * dr, F32, tm_target=IN_PROJ_TILE, tn_target=IN_PROJ_TILE, name="in_proj_rkv")
    p_lora = _mm_wcast(h, w_lora, w_lora.shape[1], F32, tm_target=IN_PROJ_TILE, name="in_proj_lora")

    shared, perdir, bonus, gate = _rwkv_features(
        p_main, p_lora, sel[:, :3 * dr], mu[:, :3 * dr], regroup(sel), regroup(mu), w0[layer], w2p, a0[layer], a2p,
        g2p, k_k[layer], k_a[layer], r_k[layer].reshape(-1), lc, l)
    whole = lambda w: (w, 0, w.shape[1])
    yf, yr, w_h, w_up = _rwkv_scan(shared, perdir, lc, [(wi, rw_cols, wi.shape[1] - rw_cols), whole(w_ff1[layer])])
    p_h = _mm(h, w_h, F32, tm_target=IN_PROJ_TILE, name="in_proj_hgrn")
    of, orv, w_down, w_o = _hgrn_scan(p_h, hgrn_lb_logits, lc, layer, [whole(w_ff2[layer]), whole(w_out[layer])])

    u = _mix_out(yf, yr, bonus, gate, of, orv, p_h, ln_x_g[layer], ln_x_b[layer], hgrn_norm_g[layer], lc, l)
    ux = _mm(u, w_o, F32, name="out_proj")
    x1, h2 = _res_mix(ux, x2, mod, g_mix_post[layer], g_ffn_pre[layer])
    act = _mm(h2, w_up, PROJ_DTYPE, act="relu2", name="ffn_up")
    out = _residual_projection(act, w_down, x1, mod, g_ffn_post[layer], 5, name="ffn_down")
    return out[None]
```

```python
import functools

import jax
import jax.numpy as jnp
import numpy as np
from jax import lax
from jax.experimental import pallas as pl
from jax.experimental.pallas import tpu as pltpu

F32 = jnp.float32
BF16 = jnp.bfloat16
PROJ_DTYPE = jnp.bfloat16

LANE = 128
GRID_W = 64
CHUNK = GRID_W
RWKV_HEAD = 64
HGRN_HEAD = 128
SUB = 16
GROUP_W = 256
RWKV_GROUPS_PER_STEP = 8
HGRN_GROUPS_PER_STEP = 8
MAX_CHUNK_LOG_DECAY = 80.0
NORM_EPS = 1e-6
GN_EPS = 64e-5
EXP_M05 = float(np.exp(-0.5))
VMEM_LIMIT = 56 * 1024 * 1024

MM_TILE = 1024
IN_PROJ_TILE = 768
SWEEP_ROWS = 512
ROW_BLOCK = 256
EPILOGUE_ROWS = 128
ADA_COLS = 512


def _pick(n, target, unit=LANE):
    best = None
    for m in range(unit, min(n, target) + 1, unit):
        if n % m == 0:
            best = m
    return best if best is not None else n


def _sigmoid(z):
    return 0.5 * jnp.tanh(0.5 * z) + 0.5


def _mxu(a, b):
    return jnp.dot(a, b, preferred_element_type=F32)


def _mxu_nt(a, b):
    return lax.dot_general(a, b, (((1,), (1,)), ((), ())), preferred_element_type=F32)


def _mxu_tn(a, b):
    return lax.dot_general(a, b, (((0,), (0,)), ((), ())), preferred_element_type=F32)


def _split3(z):
    hi = z.astype(BF16)
    rest = z - hi.astype(F32)
    mid = rest.astype(BF16)
    return hi, mid, (rest - mid.astype(F32)).astype(BF16)


def _select_sum(z, sel):
    hi, mid, _ = _split3(z)
    return _mxu(hi, sel) + _mxu(mid, sel)


def _head_sum(z, hs, hst):
    return _select_sum(_select_sum(z, hs), hst)


def _head_indicators(dr):
    hs = np.arange(dr)[:, None] // RWKV_HEAD == np.arange(LANE)[None, :]
    return jnp.asarray(hs, BF16), jnp.asarray(hs.T, BF16)


def _params(sem, vmem=None):
    return pltpu.CompilerParams(dimension_semantics=sem, vmem_limit_bytes=vmem)


def _ada_kernel(c_ref, w_ref, b_ref, o_ref):
    cv = c_ref[...]
    rows = cv.shape[0]
    hi, mid, lo = (t.astype(F32) for t in _split3(cv * _sigmoid(cv)))
    s3 = jnp.concatenate([hi, mid, lo], axis=0).astype(BF16)
    w = w_ref[...]
    w_hi = w.astype(BF16)
    w_lo = (w - w_hi.astype(F32)).astype(BF16)
    p = _mxu(s3, w_hi)
    q = _mxu(s3[:2 * rows], w_lo)
    o_ref[...] = (p[:rows] + p[rows:2 * rows] + p[2 * rows:] + q[:rows] + q[rows:]) + b_ref[...]


def _ada(cvec, w, b):
    rows, d = cvec.shape
    n = w.shape[1]
    tn = _pick(n, ADA_COLS)
    return pl.pallas_call(
        _ada_kernel,
        grid=(n // tn,),
        in_specs=[pl.BlockSpec((rows, d), lambda j: (0, 0)),
                  pl.BlockSpec((d, tn), lambda j: (0, j)),
                  pl.BlockSpec((1, tn), lambda j: (0, j))],
        out_specs=pl.BlockSpec((rows, tn), lambda j: (0, j)),
        out_shape=jax.ShapeDtypeStruct((rows, n), F32),
        compiler_params=_params(("arbitrary",), VMEM_LIMIT),
        name="ada_mod",
    )(cvec, w, b.reshape(1, n))


def _prep_kernel(ctx_ref, x_ref, g_ref, mod_ref, o_ref, *, ncb, d):
    is_ctx = pl.program_id(0) < ncb
    rows = jnp.where(is_ctx, ctx_ref[...], x_ref[...])
    ms = jnp.mean(rows * rows, axis=-1, keepdims=True)
    hn = rows * lax.rsqrt(ms + NORM_EPS) * g_ref[...]
    shift = jnp.where(is_ctx, mod_ref[1:2, 0:d], mod_ref[0:1, 0:d])
    scale = jnp.where(is_ctx, mod_ref[1:2, d:2 * d], mod_ref[0:1, d:2 * d])
    o_ref[...] = (hn * (1.0 + scale) + shift).astype(o_ref.dtype)


def _prep(ctx2, x2, g, mod):
    lc, d = ctx2.shape
    l = x2.shape[0]
    tb = _pick(int(np.gcd(lc, l)), ROW_BLOCK, 8)
    ncb = lc // tb
    return pl.pallas_call(
        functools.partial(_prep_kernel, ncb=ncb, d=d),
        grid=((lc + l) // tb,),
        in_specs=[pl.BlockSpec((tb, d), lambda i: (jnp.minimum(i, ncb - 1), 0)),
                  pl.BlockSpec((tb, d), lambda i: (jnp.maximum(i - ncb, 0), 0)),
                  pl.BlockSpec((1, d), lambda i: (0, 0)),
                  pl.BlockSpec(mod.shape, lambda i: (0, 0))],
        out_specs=pl.BlockSpec((tb, d), lambda i: (i, 0)),
        out_shape=jax.ShapeDtypeStruct((lc + l, d), PROJ_DTYPE),
        compiler_params=_params(("arbitrary",)),
        name="norm_modulate",
    )(ctx2, x2, g.reshape(1, d), mod)


def _mm_kernel(x_ref, w_ref, o_ref, *, act):
    acc = jnp.dot(x_ref[...], w_ref[...], preferred_element_type=F32)
    if act == "relu2":
        acc = jnp.square(jnp.maximum(acc, 0.0))
    o_ref[...] = acc.astype(o_ref.dtype)


def _mm(x, w, out_dtype, act=None, tm_target=MM_TILE, tn_target=MM_TILE, name="matmul"):
    m, k = x.shape
    n = w.shape[1]
    tm = _pick(m, tm_target)
    tn = _pick(n, tn_target)
    return pl.pallas_call(
        functools.partial(_mm_kernel, act=act),
        grid=(m // tm, n // tn),
        in_specs=[pl.BlockSpec((tm, k), lambda i, j: (i, 0)),
                  pl.BlockSpec((k, tn), lambda i, j: (0, j))],
        out_specs=pl.BlockSpec((tm, tn), lambda i, j: (i, j)),
        out_shape=jax.ShapeDtypeStruct((m, n), out_dtype),
        compiler_params=_params(("arbitrary", "arbitrary"), VMEM_LIMIT),
        name=name,
    )(x, w)


def _mm_wcast_kernel(x_ref, w_ref, o_ref, w_cast):
    @pl.when(pl.program_id(1) == 0)
    def _():
        w_cast[...] = w_ref[...].astype(w_cast.dtype)

    o_ref[...] = jnp.dot(x_ref[...], w_cast[...], preferred_element_type=F32).astype(o_ref.dtype)


def _mm_wcast(x, w, ncols, out_dtype, tm_target=MM_TILE, tn_target=MM_TILE, name="matmul_wcast"):
    m, k = x.shape
    tm, tn = _pick(m, tm_target), _pick(ncols, tn_target)
    return pl.pallas_call(
        _mm_wcast_kernel,
        grid=(ncols // tn, m // tm),
        in_specs=[pl.BlockSpec((tm, k), lambda j, i: (i, 0)),
                  pl.BlockSpec((k, tn), lambda j, i: (0, j))],
        out_specs=pl.BlockSpec((tm, tn), lambda j, i: (i, j)),
        out_shape=jax.ShapeDtypeStruct((m, ncols), out_dtype),
        scratch_shapes=[pltpu.VMEM((k, tn), PROJ_DTYPE)],
        compiler_params=_params(("arbitrary", "arbitrary"), VMEM_LIMIT),
        name=name,
    )(x, w)


def _res_proj_kernel(a_ref, w_ref, res_hbm, mod_ref, g_ref, o_ref, res_buf, sem, *, d, gate_chunk):
    i, kk = pl.program_id(0), pl.program_id(1)
    tm = res_buf.shape[0]
    res_copy = pltpu.make_async_copy(res_hbm.at[pl.ds(pl.multiple_of(i * tm, tm), tm), :], res_buf, sem)

    @pl.when(kk == 0)
    def _():
        res_copy.start()
        o_ref[...] = jnp.zeros_like(o_ref)

    o_ref[...] += jnp.dot(a_ref[...], w_ref[...], preferred_element_type=F32)

    @pl.when(kk == pl.num_programs(1) - 1)
    def _():
        res_copy.wait()
        gate = mod_ref[0:1, gate_chunk * d:(gate_chunk + 1) * d] * g_ref[...]
        rb = _pick(tm, EPILOGUE_ROWS, 8)
        for r0 in range(0, tm, rb):
            rows = slice(r0, r0 + rb)
            acc = o_ref[rows, :]
            scale = lax.rsqrt(jnp.mean(acc * acc, axis=-1, keepdims=True) + NORM_EPS)
            o_ref[rows, :] = res_buf[rows, :] + gate * (acc * scale)


def _residual_projection(a, w, res, mod, g_post, gate_chunk, tm_target=SWEEP_ROWS, tk_target=MM_TILE,
                         name="residual_projection"):
    m, k = a.shape
    d = w.shape[1]
    tm, tk = _pick(m, tm_target, 8), _pick(k, tk_target)
    full = lambda z: pl.BlockSpec(z.shape, lambda i, kk: (0,) * z.ndim)
    gp = g_post.reshape(1, d)
    return pl.pallas_call(
        functools.partial(_res_proj_kernel, d=d, gate_chunk=gate_chunk),
        grid=(m // tm, k // tk),
        in_specs=[pl.BlockSpec((tm, tk), lambda i, kk: (i, kk)),
                  pl.BlockSpec((tk, d), lambda i, kk: (kk, 0)),
                  pl.BlockSpec(memory_space=pl.ANY), full(mod), full(gp)],
        out_specs=pl.BlockSpec((tm, d), lambda i, kk: (i, 0)),
        out_shape=jax.ShapeDtypeStruct((m, d), F32),
        scratch_shapes=[pltpu.VMEM((tm, d), F32), pltpu.SemaphoreType.DMA(())],
        compiler_params=_params(("arbitrary", "arbitrary"), VMEM_LIMIT),
        name=name,
    )(a, w, res, mod, gp)


RING = 4


def _rwkv_feat_kernel(pm_hbm, pl_hbm, sel_ref, mu_ref, sell_ref, mul_ref, w0_ref, w2_ref, a0_ref, a2_ref, g2_ref,
                      kk_ref, ka_ref, rk_ref, hs_ref, hst_ref,
                      shared_o, perdir_o, bonus_o, gate_o, ring_m, ring_l, sems,
                      *, ncb, nrows, dr, rd, ra):
    i = pl.program_id(0)
    nb = pl.num_programs(0)
    c = ring_m.shape[1]
    sources = ((pm_hbm, ring_m), (pl_hbm, ring_l))

    def fetch(blk):
        slot = blk % RING
        rows = pl.ds(pl.multiple_of(blk * c, c), c)
        return [pltpu.make_async_copy(src.at[rows, :], ring.at[slot], sems.at[n, slot])
                for n, (src, ring) in enumerate(sources)]

    def start(blk):
        for cp in fetch(blk):
            cp.start()

    def wait(blk):
        for cp in fetch(blk):
            cp.wait()

    @pl.when(i == 0)
    def _():
        start(0)
        start(1)
        start(2)
        wait(0)
        wait(1)

    @pl.when((i > 0) & (i + 1 < nb))
    def _():
        wait(i + 1)

    @pl.when((i > 0) & (i + 2 < nb))
    def _():
        start(i + 2)

    is_ctx = i < ncb
    xi = i - ncb
    s_cur, s_prv, s_nxt = i % RING, jnp.maximum(i - 1, 0) % RING, jnp.minimum(i + 1, nb - 1) % RING

    def shift_lerp(ring, sel, mu, interior):
        cur, prv, nxt = ring[s_cur], ring[s_prv], ring[s_nxt]
        row = lax.broadcasted_iota(jnp.int32, cur.shape, 0)
        if interior:
            before = jnp.where(row == 0, 0.0, pltpu.roll(cur, 1, 0))
            after = jnp.where(row == c - 1, 0.0, pltpu.roll(cur, c - 1, 0))
            above, below = prv, nxt
        else:
            prev_last = jnp.where(is_ctx & (i > 0), prv[c - 1:c, :], 0.0)
            next_first = jnp.where(is_ctx & (i < ncb - 1), nxt[0:1, :], 0.0)
            before = jnp.where(row == 0, prev_last, pltpu.roll(cur, 1, 0))
            after = jnp.where(row == c - 1, next_first, pltpu.roll(cur, c - 1, 0))
            above = jnp.where(is_ctx, before, jnp.where(xi > 0, prv, 0.0))
            below = jnp.where(is_ctx, after, jnp.where(xi < nrows - 1, nxt, 0.0))
        shifted = jnp.where(sel == 0, before, jnp.where(sel == 1, after, jnp.where(sel == 2, above, below)))
        return cur + mu * (shifted - cur)

    def emit_block(interior):
        m = shift_lerp(ring_m, sel_ref[...], mu_ref[...], interior)
        ml = shift_lerp(ring_l, sell_ref[...], mul_ref[...], interior)
        r, k, v = m[:, 0:dr], m[:, dr:2 * dr], m[:, 2 * dr:3 * dr]
        wl = (ml[:, 0:rd], ml[:, rd:2 * rd])
        al = (ml[:, 2 * rd:2 * rd + ra], ml[:, 2 * rd + ra:2 * rd + 2 * ra])
        gl = ml[:, 2 * rd + 2 * ra:]

        hs, hst = hs_ref[...], hst_ref[...]

        def headsum(z):
            return _head_sum(z, hs, hst)

        kkf = k * kk_ref[...]
        kk = kkf * lax.rsqrt(headsum(kkf * kkf) + 1e-12)
        shared_o[0] = r
        shared_o[1] = v
        shared_o[2] = -kk
        for d in range(2):
            wd = _mxu(jnp.tanh(wl[d]).astype(BF16), w2_ref[d]) + w0_ref[d:d + 1, :]
            perdir_o[d, 0] = -EXP_M05 * _sigmoid(wd)
            ad = _sigmoid(_mxu(al[d].astype(BF16), a2_ref[d]) + a0_ref[d:d + 1, :])
            perdir_o[d, 1] = k * (1.0 + (ad - 1.0) * ka_ref[...])
            perdir_o[d, 2] = kk * ad
        bonus_o[...] = (headsum(r * k * rk_ref[...]) * v).astype(bonus_o.dtype)
        gate_o[...] = _mxu(_sigmoid(gl).astype(BF16), g2_ref[...]).astype(gate_o.dtype)

    interior = (xi > 0) & (xi < nrows - 1)

    @pl.when(interior)
    def _():
        emit_block(True)

    @pl.when(jnp.logical_not(interior))
    def _():
        emit_block(False)


def _rwkv_features(p_main, p_lora, sel, mu, sel_l, mu_l, w0, w2p, a0, a2p, g2p, k_k, k_a, r_k, lc, l):
    n = p_main.shape[0]
    dr = k_k.shape[0]
    rd, ra = w2p.shape[1], a2p.shape[1]
    nb, ncb = n // CHUNK, lc // CHUNK
    nheads = dr // RWKV_HEAD
    assert nheads <= LANE
    hs, hst = _head_indicators(dr)
    full = lambda a: pl.BlockSpec(a.shape, lambda i: (0,) * a.ndim)
    row1 = lambda a: a.reshape(1, -1)
    consts = [sel, mu, sel_l, mu_l, w0, w2p.astype(BF16), a0, a2p.astype(BF16), g2p.astype(BF16), row1(k_k),
              row1(k_a), row1(r_k), hs, hst]
    assert nb >= 3
    outs = pl.pallas_call(
        functools.partial(_rwkv_feat_kernel, ncb=ncb, nrows=l // CHUNK, dr=dr, rd=rd, ra=ra),
        grid=(nb,),
        in_specs=[pl.BlockSpec(memory_space=pl.ANY)] * 2 + [full(a) for a in consts],
        out_specs=[pl.BlockSpec((3, CHUNK, dr), lambda i: (0, i, 0)),
                   pl.BlockSpec((2, 3, CHUNK, dr), lambda i: (0, 0, i, 0))]
                  + [pl.BlockSpec((CHUNK, dr), lambda i: (i, 0))] * 2,
        out_shape=[jax.ShapeDtypeStruct((3, n, dr), F32), jax.ShapeDtypeStruct((2, 3, n, dr), F32)]
                  + [jax.ShapeDtypeStruct((n, dr), PROJ_DTYPE)] * 2,
        scratch_shapes=[pltpu.VMEM((RING, CHUNK, p_main.shape[1]), F32), pltpu.VMEM((RING, CHUNK, p_lora.shape[1]), F32),
                        pltpu.SemaphoreType.DMA((2, RING))],
        compiler_params=_params(("arbitrary",), VMEM_LIMIT),
        name="rwkv_features",
    )(p_main, p_lora, *consts)
    return outs


def _block_diag_mask(n, w):
    r = lax.broadcasted_iota(jnp.int32, (n, n), 0)
    c = lax.broadcasted_iota(jnp.int32, (n, n), 1)
    return (r // w) == (c // w)


def _cumsum_rows(z, rev):
    c = z.shape[0]
    t = lax.broadcasted_iota(jnp.int32, (c, c), 0)
    s = lax.broadcasted_iota(jnp.int32, (c, c), 1)
    tri = ((s >= t) if rev else (s <= t)).astype(BF16)
    hi, mid, lo = _split3(z)
    return _mxu(tri, hi) + _mxu(tri, mid) + _mxu(tri, lo)


def _rwkv_chunks(chains):
    c, w = chains[0][0].shape
    nh = w // c
    every = range(len(chains))
    revs = [ch[7] for ch in chains]
    bd = _block_diag_mask(w, c)
    t_i = lax.broadcasted_iota(jnp.int32, (c, w), 0)
    s_i = lax.broadcasted_iota(jnp.int32, (c, w), 1) % c
    eye = (s_i == t_i).astype(F32)
    strict = {False: s_i < t_i, True: s_i > t_i}
    incl = {False: s_i <= t_i, True: s_i >= t_i}

    def expand(z):
        return jnp.where(bd, jnp.concatenate([z] * nh, axis=0), jnp.zeros((), z.dtype))

    def packed_mm(lhs, rhs):
        return _mxu(lhs.astype(BF16), expand(rhs.astype(BF16)))

    bc = [_cumsum_rows(ch[1], ch[7]) for ch in chains]
    ar, bk, v16 = [], [], []
    for i, (r, lw, k, v, a, b, _, _) in enumerate(chains):
        e_in, e_ex, e_ng = jnp.exp(bc[i]), jnp.exp(bc[i] - lw), jnp.exp(-bc[i])
        ar.append(jnp.concatenate([a * e_ex, r * e_in], axis=0).astype(BF16))
        bk.append(jnp.concatenate([b * e_ng, k * e_ng], axis=0).astype(BF16))
        v16.append(v.astype(BF16))
    gb = [_mxu_nt(ar[i], expand(bk[i][:c])) for i in every]
    gk = [_mxu_nt(ar[i], expand(bk[i][c:])) for i in every]
    l_ab = [jnp.where(strict[revs[i]], gb[i][:c], 0.0) for i in every]
    m_rb = [jnp.where(incl[revs[i]], gb[i][c:], 0.0) for i in every]
    lmk = [jnp.concatenate([jnp.where(strict[revs[i]], gk[i][:c], 0.0),
                            jnp.where(incl[revs[i]], gk[i][c:], 0.0)], axis=0) for i in every]
    t_m = [eye + l_ab[i] for i in every]
    p_m = [packed_mm(l_ab[i], l_ab[i]) for i in every]
    for _ in range(int(np.log2(c)) - 2):
        tp = [packed_mm(jnp.concatenate([t_m[i], p_m[i]], axis=0), p_m[i]) for i in every]
        t_m = [t_m[i] + tp[i][:c] for i in every]
        p_m = [tp[i][c:] for i in every]
    t_m = [t_m[i] + packed_mm(t_m[i], p_m[i]) for i in every]

    ars = [_mxu_nt(ar[i], chains[i][6].astype(BF16)) for i in every]
    lm = [_mxu(lmk[i].astype(BF16), expand(v16[i])) for i in every]
    u16 = [packed_mm(t_m[i], ars[i][:c] + lm[i][:c]).astype(BF16) for i in every]
    ys = [ars[i][c:] + _mxu(m_rb[i].astype(BF16), expand(u16[i])) + lm[i][c:] for i in every]
    ds = [_mxu_tn(jnp.concatenate([u16[i], v16[i]], axis=0), bk[i]) for i in every]
    s_new = []
    for i in every:
        last = bc[i][0:1] if revs[i] else bc[i][c - 1:c]
        s_new.append((chains[i][6] + jnp.where(bd, ds[i], 0.0)) * jnp.exp(last))
    return ys, s_new


def _cast_jobs(weights, ng, nb):
    in_specs, out_specs, shapes = [], [], []
    for w, col0, cols in weights:
        rows = w.shape[0]
        n = max(m for m in range(1, ng * nb + 1) if rows % m == 0 and (rows // m) % 16 == 0)
        slab = lambda g, c, n=n: jnp.minimum(g * nb + c, n - 1)
        in_specs.append(pl.BlockSpec((pl.Element(rows // n), pl.Element(cols)),
                                     lambda g, c, s=slab, h=rows // n, col0=col0: (s(g, c) * h, col0)))
        out_specs.append(pl.BlockSpec((rows // n, cols), lambda g, c, s=slab: (s(g, c), 0)))
        shapes.append(jax.ShapeDtypeStruct((rows, cols), PROJ_DTYPE))
    return in_specs, out_specs, shapes


def _run_cast_jobs(src_refs, dst_refs):
    for src, dst in zip(src_refs, dst_refs):
        dst[...] = src[...].astype(dst.dtype)


def _rwkv_scan_kernel(shared_f, perdir_f, shared_r, perdir_r, *rest):
    n_cast = (len(rest) - 3) // 2
    yf_o, yr_o = rest[n_cast:n_cast + 2]
    s_ref = rest[-1]

    @pl.when(pl.program_id(1) == 0)
    def _():
        s_ref[...] = jnp.zeros_like(s_ref)

    ngs = shared_f.shape[2] // GROUP_W
    chains = []
    for d, (sh, pd) in enumerate(((shared_f, perdir_f), (shared_r, perdir_r))):
        for g in range(ngs):
            sl = slice(g * GROUP_W, (g + 1) * GROUP_W)
            r, v, a = (sh[j, :, sl] for j in range(3))
            lw, k, b = (pd[j, :, sl] for j in range(3))
            chains.append((r, lw, k, v, a, b, s_ref[d, g], d == 1))
    ys, s_new = _rwkv_chunks(chains)
    _run_cast_jobs(rest[:n_cast], rest[n_cast + 2:-1])
    for d, o_ref in enumerate((yf_o, yr_o)):
        for g in range(ngs):
            o_ref[:, g * GROUP_W:(g + 1) * GROUP_W] = ys[d * ngs + g]
            s_ref[d, g] = s_new[d * ngs + g]


def _rev_chunk(c, ncb, nb):
    return jnp.where(c < ncb, ncb - 1 - c, nb - 1 - (c - ncb))


def _rwkv_scan(shared, perdir, lc, cast_weights):
    _, n, dr = shared.shape
    nb, ncb = n // CHUNK, lc // CHUNK
    bw = _pick(dr, RWKV_GROUPS_PER_STEP * GROUP_W, GROUP_W)
    rc = lambda c: _rev_chunk(c, ncb, nb)
    fwd = pl.BlockSpec((CHUNK, bw), lambda g, c: (c, g))
    rev = pl.BlockSpec((CHUNK, bw), lambda g, c: (rc(c), g))
    in_specs = [pl.BlockSpec((3, CHUNK, bw), lambda g, c: (0, c, g)),
                pl.BlockSpec((None, 3, CHUNK, bw), lambda g, c: (0, 0, c, g)),
                pl.BlockSpec((3, CHUNK, bw), lambda g, c: (0, rc(c), g)),
                pl.BlockSpec((None, 3, CHUNK, bw), lambda g, c: (1, 0, rc(c), g))]
    cast_in, cast_out, cast_shapes = _cast_jobs(cast_weights, dr // bw, nb)
    return pl.pallas_call(
        _rwkv_scan_kernel,
        grid=(dr // bw, nb),
        in_specs=in_specs + cast_in,
        out_specs=[fwd, rev] + cast_out,
        out_shape=[jax.ShapeDtypeStruct((n, dr), F32)] * 2 + cast_shapes,
        scratch_shapes=[pltpu.VMEM((2, bw // GROUP_W, GROUP_W, GROUP_W), F32)],
        compiler_params=_params(("arbitrary", "arbitrary"), VMEM_LIMIT),
        name="rwkv_scan",
    )(shared, perdir, shared, perdir, *[w for w, _, _ in cast_weights])


def _hgrn_chunk(q, k, v, lf, s_vk, rev):
    c, w = q.shape
    nsb = c // SUB
    bc = _cumsum_rows(lf, rev)
    last = bc[0:1] if rev else bc[c - 1:c]
    bdh = _block_diag_mask(w, HGRN_HEAD)
    v16 = v.astype(BF16)
    o_inter = _mxu_nt((q * jnp.exp(bc)).astype(BF16), s_vk.astype(BF16))
    s_new = s_vk * jnp.exp(last) + jnp.where(bdh, _mxu_tn(v16, (k * jnp.exp(last - bc)).astype(BF16)), 0.0)

    lane2 = lax.broadcasted_iota(jnp.int32, (c, w), 1)
    row2 = lax.broadcasted_iota(jnp.int32, (c, w), 0)
    zpad = jnp.zeros((HGRN_HEAD - c, w), BF16)

    def expand(z):
        parts = []
        for h in range(w // HGRN_HEAD):
            parts += [jnp.where(lane2 // HGRN_HEAD == h, z, jnp.zeros((), BF16)), zpad]
        return jnp.concatenate(parts, axis=0)

    ones_bd = bdh.astype(BF16)
    t3 = lax.broadcasted_iota(jnp.int32, (SUB, SUB, w), 0)
    s3 = lax.broadcasted_iota(jnp.int32, (SUB, SUB, w), 1)
    l3 = lax.broadcasted_iota(jnp.int32, (SUB, SUB, w), 2) % HGRN_HEAD
    causal = (s3 >= t3) if rev else (s3 <= t3)
    a_rows = []
    for i in range(nsb):
        lo, hi = SUB * i, SUB * (i + 1)
        qi, ki, bi = q[lo:hi], k[lo:hi], bc[lo:hi]
        d3 = bi[:, None, :] - bi[None, :, :]
        x3 = jnp.where(causal, (qi[:, None, :] * ki[None, :, :]) * jnp.exp(jnp.minimum(d3, 0.0)), 0.0)
        r3 = _mxu(x3.reshape(SUB * SUB, w).astype(BF16), ones_bd).reshape(SUB, SUB, w)
        a_i = jnp.sum(jnp.where(l3 == s3 + lo, r3, 0.0), axis=1)
        if (not rev and i > 0) or (rev and i < nsb - 1):
            beta = bc[hi:hi + 1] if rev else bc[lo - 1:lo]
            earlier = (row2 >= hi) if rev else (row2 < lo)
            kp = jnp.where(earlier, k * jnp.exp(jnp.minimum(beta - bc, 0.0)), 0.0)
            a_i = a_i + _mxu_nt((qi * jnp.exp(bi - beta)).astype(BF16), expand(kp.astype(BF16)))
        a_rows.append(a_i)
    o = o_inter + _mxu(jnp.concatenate(a_rows, axis=0).astype(BF16), expand(v16))
    return o, s_new


def _hgrn_chunks_bounded(chains):
    c, w = chains[0][0].shape
    every = range(len(chains))
    revs = [ch[5] for ch in chains]
    bdh = _block_diag_mask(w, HGRN_HEAD)
    lane2 = lax.broadcasted_iota(jnp.int32, (c, w), 1)
    t_i = lax.broadcasted_iota(jnp.int32, (c, w), 0)
    s_i = lane2 % HGRN_HEAD
    incl = {False: s_i <= t_i, True: s_i >= t_i}
    zpad = jnp.zeros((HGRN_HEAD - c, w), BF16)

    def expand(z):
        parts = []
        for h in range(w // HGRN_HEAD):
            parts += [jnp.where(lane2 // HGRN_HEAD == h, z, jnp.zeros((), BF16)), zpad]
        return jnp.concatenate(parts, axis=0)

    bc = [_cumsum_rows(ch[3], ch[5]) for ch in chains]
    last = [bc[i][0:1] if revs[i] else bc[i][c - 1:c] for i in every]
    qt = [(chains[i][0] * jnp.exp(bc[i])).astype(BF16) for i in every]
    kt = [(chains[i][1] * jnp.exp(-bc[i])).astype(BF16) for i in every]
    kd = [(chains[i][1] * jnp.exp(last[i] - bc[i])).astype(BF16) for i in every]
    v16 = [chains[i][2].astype(BF16) for i in every]
    o_inter = [_mxu_nt(qt[i], chains[i][4].astype(BF16)) for i in every]
    a = [jnp.where(incl[revs[i]], _mxu_nt(qt[i], expand(kt[i])), 0.0).astype(BF16) for i in every]
    o = [o_inter[i] + _mxu(a[i], expand(v16[i])) for i in every]
    ds = [_mxu_tn(v16[i], kd[i]) for i in every]
    s_new = [chains[i][4] * jnp.exp(last[i]) + jnp.where(bdh, ds[i], 0.0) for i in every]
    return o, s_new


def _hgrn_scan_kernel(qf, ff, vf, qr, fr, vr, lg_ref, *rest, layer):
    n_cast = (len(rest) - 3) // 2
    of_o, or_o = rest[n_cast:n_cast + 2]
    s_ref = rest[-1]

    @pl.when(pl.program_id(1) == 0)
    def _():
        s_ref[...] = jnp.zeros_like(s_ref)

    lg = lg_ref[...]
    e = jnp.exp(lg - jnp.max(lg, axis=0, keepdims=True))
    lb = jnp.sum(e[:layer + 1], axis=0) / jnp.sum(e, axis=0)
    ngs = qf.shape[1] // GROUP_W
    chunk = qf.shape[0]

    def chains():
        out = []
        for d, (q_ref, f_ref, v_ref) in enumerate(((qf, ff, vf), (qr, fr, vr))):
            fd = lb[d:d + 1, :] + (1.0 - lb[d:d + 1, :]) * _sigmoid(f_ref[...])
            qv = q_ref[...]
            qh, kh, vh, lf = qv * _sigmoid(qv), 1.0 - fd, v_ref[...], jnp.log(fd)
            for g in range(ngs):
                sl = slice(g * GROUP_W, (g + 1) * GROUP_W)
                out.append((qh[:, sl], kh[:, sl], vh[:, sl], lf[:, sl], s_ref[d, g], d == 1))
        return out

    def emit(o, s_new):
        for d, o_ref in enumerate((of_o, or_o)):
            for g in range(ngs):
                o_ref[:, g * GROUP_W:(g + 1) * GROUP_W] = o[d * ngs + g]
                s_ref[d, g] = s_new[d * ngs + g]
        _run_cast_jobs(rest[:n_cast], rest[n_cast + 2:-1])

    bounded = chunk * jnp.min(jnp.log(lb)) >= -MAX_CHUNK_LOG_DECAY

    @pl.when(bounded)
    def _():
        emit(*_hgrn_chunks_bounded(chains()))

    @pl.when(jnp.logical_not(bounded))
    def _():
        res = [_hgrn_chunk(*ch) for ch in chains()]
        emit([r[0] for r in res], [r[1] for r in res])


def _hgrn_scan(p_h, lb_logits, lc, layer, cast_weights):
    n = p_h.shape[0]
    dh = lb_logits.shape[2]
    nb, ncb = n // CHUNK, lc // CHUNK
    bw = _pick(dh, HGRN_GROUPS_PER_STEP * GROUP_W, GROUP_W)
    ng = dh // bw
    blk = (CHUNK, bw)
    fwd = lambda sec: pl.BlockSpec(blk, lambda g, c: (c, sec * ng + g))
    rev = lambda sec: pl.BlockSpec(blk, lambda g, c: (_rev_chunk(c, ncb, nb), sec * ng + g))
    cast_in, cast_out, cast_shapes = _cast_jobs(cast_weights, ng, nb)
    return pl.pallas_call(
        functools.partial(_hgrn_scan_kernel, layer=layer),
        grid=(ng, nb),
        in_specs=[fwd(0), fwd(1), fwd(3), rev(0), rev(2), rev(3),
                  pl.BlockSpec((lb_logits.shape[0], 2, bw), lambda g, c: (0, 0, g))] + cast_in,
        out_specs=[pl.BlockSpec(blk, lambda g, c: (c, g)),
                   pl.BlockSpec(blk, lambda g, c: (_rev_chunk(c, ncb, nb), g))] + cast_out,
        out_shape=[jax.ShapeDtypeStruct((n, dh), F32)] * 2 + cast_shapes,
        scratch_shapes=[pltpu.VMEM((2, bw // GROUP_W, GROUP_W, GROUP_W), F32)],
        compiler_params=_params(("arbitrary", "arbitrary"), VMEM_LIMIT),
        name="hgrn_scan",
    )(p_h, p_h, p_h, p_h, p_h, p_h, lb_logits, *[w for w, _, _ in cast_weights])


def _mix_out_kernel(yf, yr, bonus, gate, of, orv, g_ref, lng, lnb, hng, hs_ref, hst_ref, u_o, *, dr, dh):
    hs, hst = hs_ref[...], hst_ref[...]

    def headmean(z):
        return _head_sum(z, hs, hst) * (1.0 / RWKV_HEAD)

    y = yf[...] + yr[...]
    yc = y - headmean(y)
    yn = yc * lax.rsqrt(headmean(yc * yc) + GN_EPS) * lng[...] + lnb[...]
    u_o[:, 0:dr] = ((yn + bonus[...].astype(F32)) * gate[...].astype(F32)).astype(u_o.dtype)
    o = of[...] + orv[...]
    g = g_ref[...]
    sg = g * _sigmoid(g)
    for h in range(dh // HGRN_HEAD):
        sl = slice(h * HGRN_HEAD, (h + 1) * HGRN_HEAD)
        oh = o[:, sl]
        on = oh * lax.rsqrt(jnp.mean(oh * oh, axis=-1, keepdims=True) + NORM_EPS) * hng[...]
        u_o[:, dr + h * HGRN_HEAD:dr + (h + 1) * HGRN_HEAD] = (on * sg[:, sl]).astype(u_o.dtype)


def _mix_out(yf, yr, bonus, gate, of, orv, p_h, ln_g, ln_b, hg, lc, l):
    dr, dh = yf.shape[1], of.shape[1]
    tb = _pick(int(np.gcd(lc, l)), EPILOGUE_ROWS, 8)
    off = lc // tb
    hs, hst = _head_indicators(dr)
    rows = lambda wd: pl.BlockSpec((tb, wd), lambda i: (i + off, 0))
    full = lambda a: pl.BlockSpec(a.shape, lambda i: (0,) * a.ndim)
    consts = [ln_g.reshape(1, dr), ln_b.reshape(1, dr), hg.reshape(1, HGRN_HEAD), hs, hst]
    return pl.pallas_call(
        functools.partial(_mix_out_kernel, dr=dr, dh=dh),
        grid=(l // tb,),
        in_specs=[rows(dr)] * 4 + [rows(dh)] * 2 + [pl.BlockSpec((tb, dh), lambda i: (i + off, 4))]
                 + [full(a) for a in consts],
        out_specs=pl.BlockSpec((tb, dr + dh), lambda i: (i, 0)),
        out_shape=jax.ShapeDtypeStruct((l, dr + dh), PROJ_DTYPE),
        compiler_params=_params(("arbitrary",), VMEM_LIMIT),
        name="mix_out",
    )(yf, yr, bonus, gate, of, orv, p_h, *consts)


def _res_mix_kernel(ux_ref, x_ref, mod_ref, gpost_ref, gpre_ref, x1_o, h2_o, *, d):
    ux = ux_ref[...]
    nrm = ux * lax.rsqrt(jnp.mean(ux * ux, axis=-1, keepdims=True) + NORM_EPS) * gpost_ref[...]
    x1 = x_ref[...] + mod_ref[0:1, 2 * d:3 * d] * nrm
    x1_o[...] = x1
    hn = x1 * lax.rsqrt(jnp.mean(x1 * x1, axis=-1, keepdims=True) + NORM_EPS) * gpre_ref[...]
    h2_o[...] = (hn * (1.0 + mod_ref[0:1, 4 * d:5 * d]) + mod_ref[0:1, 3 * d:4 * d]).astype(h2_o.dtype)


def _res_mix(ux, x2, mod, g_post, g_pre):
    l, d = x2.shape
    tb = _pick(l, ROW_BLOCK, 8)
    rows = pl.BlockSpec((tb, d), lambda i: (i, 0))
    full = lambda a: pl.BlockSpec(a.shape, lambda i: (0,) * a.ndim)
    gp, gq = g_post.reshape(1, d), g_pre.reshape(1, d)
    return pl.pallas_call(
        functools.partial(_res_mix_kernel, d=d),
        grid=(l // tb,),
        in_specs=[rows, rows, full(mod), full(gp), full(gq)],
        out_specs=[rows, rows],
        out_shape=[jax.ShapeDtypeStruct((l, d), F32), jax.ShapeDtypeStruct((l, d), PROJ_DTYPE)],
        compiler_params=_params(("arbitrary",), VMEM_LIMIT),
        name="residual_mix",
    )(ux, x2, mod, gp, gq)


def _pad_cols(a, width):
    return jnp.pad(a, ((0, 0), (0, width - a.shape[1])))


def _pad_rows(a, height):
    return jnp.pad(a, ((0, height - a.shape[0]), (0, 0)))


def _round_up(n, m):
    return (n + m - 1) // m * m


def kernel(x, c, ctx, c_ctx, w_ada, b_ada, g_mix_pre, g_mix_post, g_ffn_pre, g_ffn_post, w_in, mu_shift, w0, w2, a0,
           a2, g2, k_k, k_a, r_k, ln_x_g, ln_x_b, hgrn_lb_logits, hgrn_norm_g, w_out, w_ff1, w_ff2):
    assert x.shape[0] == 1 and w_in.shape[0] == 1, "single batch, single layer"
    layer = 0
    x2, ctx2 = x[0], ctx[0]
    l, d = x2.shape
    lc = ctx2.shape[0]
    dr, dh = k_k.shape[1], hgrn_lb_logits.shape[2]
    rd_raw, ra_raw, rg_raw = w2.shape[2], a2.shape[2], g2.shape[1]
    rd, ra, rg = (_round_up(v, LANE) for v in (rd_raw, ra_raw, rg_raw))
    assert l % GRID_W == 0 and lc % CHUNK == 0 and dr % GROUP_W == 0 and dh % GROUP_W == 0

    cvec = jnp.concatenate([c, c_ctx[None, :], jnp.zeros((6, d), F32)], axis=0)
    mod = _ada(cvec, w_ada[layer], b_ada[layer])

    wi = w_in[layer]
    mu = mu_shift[layer][None, :]
    sel = (jnp.arange(mu.shape[1], dtype=jnp.int32) % 4)[None, :]
    o = 3 * dr
    cuts = []
    for raw, padded in ((rd_raw, rd), (rd_raw, rd), (ra_raw, ra), (ra_raw, ra), (rg_raw, rg)):
        cuts.append((o, o + raw, padded))
        o += raw
    rw_cols = o
    regroup = lambda a: jnp.concatenate([_pad_cols(a[:, s:e], wd) for s, e, wd in cuts], axis=1)
    w_lora = regroup(wi)
    w2p = jnp.stack([_pad_rows(w2[layer, dd], rd) for dd in range(2)])
    a2p = jnp.stack([_pad_rows(a2[layer, dd], ra) for dd in range(2)])
    g2p = _pad_rows(g2[layer], rg)

    h = _prep(ctx2, x2, g_mix_pre[layer], mod)
    p_main = _mm_wcast(h, wi, 3 * dr, F32, tm_target=IN_PROJ_TILE, tn_target=IN_PROJ_TILE, name="in_proj_rkv")
    p_lora = _mm_wcast(h, w_lora, w_lora.shape[1], F32, tm_target=IN_PROJ_TILE, name="in_proj_lora")

    shared, perdir, bonus, gate = _rwkv_features(
        p_main, p_lora, sel[:, :3 * dr], mu[:, :3 * dr], regroup(sel), regroup(mu), w0[layer], w2p, a0[layer], a2p,
        g2p, k_k[layer], k_a[layer], r_k[layer].reshape(-1), lc, l)
    whole = lambda w: (w, 0, w.shape[1])
    yf, yr, w_h, w_up = _rwkv_scan(shared, perdir, lc, [(wi, rw_cols, wi.shape[1] - rw_cols), whole(w_ff1[layer])])
    p_h = _mm(h, w_h, F32, tm_target=IN_PROJ_TILE, name="in_proj_hgrn")
    of, orv, w_down, w_o = _hgrn_scan(p_h, hgrn_lb_logits, lc, layer, [whole(w_ff2[layer]), whole(w_out[layer])])

    u = _mix_out(yf, yr, bonus, gate, of, orv, p_h, ln_x_g[layer], ln_x_b[layer], hgrn_norm_g[layer], lc, l)
    ux = _mm(u, w_o, F32, name="out_proj")
    x1, h2 = _res_mix(ux, x2, mod, g_mix_post[layer], g_ffn_pre[layer])
    act = _mm(h2, w_up, PROJ_DTYPE, act="relu2", name="ffn_up")
    out = _residual_projection(act, w_down, x1, mod, g_ffn_post[layer], 5, name="ffn_down")
    return out[None]
```

```python
import functools

import jax
import jax.numpy as jnp
import numpy as np
from jax import lax
from jax.experimental import pallas as pl
from jax.experimental.pallas import tpu as pltpu

F32 = jnp.float32
BF16 = jnp.bfloat16
PROJ_DTYPE = jnp.bfloat16

LANE = 128
GRID_W = 64
CHUNK = GRID_W
RWKV_HEAD = 64
HGRN_HEAD = 128
SUB = 16
GROUP_W = 256
RWKV_GROUPS_PER_STEP = 8
HGRN_GROUPS_PER_STEP = 8
MAX_CHUNK_LOG_DECAY = 80.0
NORM_EPS = 1e-6
GN_EPS = 64e-5
EXP_M05 = float(np.exp(-0.5))
VMEM_LIMIT = 56 * 1024 * 1024

MM_TILE = 1024
IN_PROJ_TILE = 768
SWEEP_ROWS = 512
ROW_BLOCK = 256
EPILOGUE_ROWS = 128
ADA_COLS = 512


def _pick(n, target, unit=LANE):
    best = None
    for m in range(unit, min(n, target) + 1, unit):
        if n % m == 0:
            best = m
    return best if best is not None else n


def _sigmoid(z):
    return 0.5 * jnp.tanh(0.5 * z) + 0.5


def _mxu(a, b):
    return jnp.dot(a, b, preferred_element_type=F32)


def _mxu_nt(a, b):
    return lax.dot_general(a, b, (((1,), (1,)), ((), ())), preferred_element_type=F32)


def _mxu_tn(a, b):
    return lax.dot_general(a, b, (((0,), (0,)), ((), ())), preferred_element_type=F32)


def _split3(z):
    hi = z.astype(BF16)
    rest = z - hi.astype(F32)
    mid = rest.astype(BF16)
    return hi, mid, (rest - mid.astype(F32)).astype(BF16)


def _select_sum(z, sel):
    hi, mid, _ = _split3(z)
    return _mxu(hi, sel) + _mxu(mid, sel)


def _head_sum(z, hs, hst):
    return _select_sum(_select_sum(z, hs), hst)


def _head_indicators(dr):
    hs = np.arange(dr)[:, None] // RWKV_HEAD == np.arange(LANE)[None, :]
    return jnp.asarray(hs, BF16), jnp.asarray(hs.T, BF16)


def _params(sem, vmem=None):
    return pltpu.CompilerParams(dimension_semantics=sem, vmem_limit_bytes=vmem)


def _ada_kernel(c_ref, w_ref, b_ref, o_ref):
    cv = c_ref[...]
    rows = cv.shape[0]
    hi, mid, lo = (t.astype(F32) for t in _split3(cv * _sigmoid(cv)))
    s3 = jnp.concatenate([hi, mid, lo], axis=0).astype(BF16)
    w = w_ref[...]
    w_hi = w.astype(BF16)
    w_lo = (w - w_hi.astype(F32)).astype(BF16)
    p = _mxu(s3, w_hi)
    q = _mxu(s3[:2 * rows], w_lo)
    o_ref[...] = (p[:rows] + p[rows:2 * rows] + p[2 * rows:] + q[:rows] + q[rows:]) + b_ref[...]


def _ada(cvec, w, b):
    rows, d = cvec.shape
    n = w.shape[1]
    tn = _pick(n, ADA_COLS)
    return pl.pallas_call(
        _ada_kernel,
        grid=(n // tn,),
        in_specs=[pl.BlockSpec((rows, d), lambda j: (0, 0)),
                  pl.BlockSpec((d, tn), lambda j: (0, j)),
                  pl.BlockSpec((1, tn), lambda j: (0, j))],
        out_specs=pl.BlockSpec((rows, tn), lambda j: (0, j)),
        out_shape=jax.ShapeDtypeStruct((rows, n), F32),
        compiler_params=_params(("arbitrary",), VMEM_LIMIT),
        name="ada_mod",
    )(cvec, w, b.reshape(1, n))


def _prep_kernel(ctx_ref, x_ref, g_ref, mod_ref, o_ref, *, ncb, d):
    is_ctx = pl.program_id(0) < ncb
    rows = jnp.where(is_ctx, ctx_ref[...], x_ref[...])
    ms = jnp.mean(rows * rows, axis=-1, keepdims=True)
    hn = rows * lax.rsqrt(ms + NORM_EPS) * g_ref[...]
    shift = jnp.where(is_ctx, mod_ref[1:2, 0:d], mod_ref[0:1, 0:d])
    scale = jnp.where(is_ctx, mod_ref[1:2, d:2 * d], mod_ref[0:1, d:2 * d])
    o_ref[...] = (hn * (1.0 + scale) + shift).astype(o_ref.dtype)


def _prep(ctx2, x2, g, mod):
    lc, d = ctx2.shape
    l = x2.shape[0]
    tb = _pick(int(np.gcd(lc, l)), ROW_BLOCK, 8)
    ncb = lc // tb
    return pl.pallas_call(
        functools.partial(_prep_kernel, ncb=ncb, d=d),
        grid=((lc + l) // tb,),
        in_specs=[pl.BlockSpec((tb, d), lambda i: (jnp.minimum(i, ncb - 1), 0)),
                  pl.BlockSpec((tb, d), lambda i: (jnp.maximum(i - ncb, 0), 0)),
                  pl.BlockSpec((1, d), lambda i: (0, 0)),
                  pl.BlockSpec(mod.shape, lambda i: (0, 0))],
        out_specs=pl.BlockSpec((tb, d), lambda i: (i, 0)),
        out_shape=jax.ShapeDtypeStruct((lc + l, d), PROJ_DTYPE),
        compiler_params=_params(("arbitrary",)),
        name="norm_modulate",
    )(ctx2, x2, g.reshape(1, d), mod)


def _mm_kernel(x_ref, w_ref, o_ref, *, act):
    acc = jnp.dot(x_ref[...], w_ref[...], preferred_element_type=F32)
    if act == "relu2":
        acc = jnp.square(jnp.maximum(acc, 0.0))
    o_ref[...] = acc.astype(o_ref.dtype)


def _mm(x, w, out_dtype, act=None, tm_target=MM_TILE, tn_target=MM_TILE, name="matmul"):
    m, k = x.shape
    n = w.shape[1]
    tm = _pick(m, tm_target)
    tn = _pick(n, tn_target)
    return pl.pallas_call(
        functools.partial(_mm_kernel, act=act),
        grid=(m // tm, n // tn),
        in_specs=[pl.BlockSpec((tm, k), lambda i, j: (i, 0)),
                  pl.BlockSpec((k, tn), lambda i, j: (0, j))],
        out_specs=pl.BlockSpec((tm, tn), lambda i, j: (i, j)),
        out_shape=jax.ShapeDtypeStruct((m, n), out_dtype),
        compiler_params=_params(("arbitrary", "arbitrary"), VMEM_LIMIT),
        name=name,
    )(x, w)


def _mm_wcast_kernel(x_ref, w_ref, o_ref, w_cast):
    @pl.when(pl.program_id(1) == 0)
    def _():
        w_cast[...] = w_ref[...].astype(w_cast.dtype)

    o_ref[...] = jnp.dot(x_ref[...], w_cast[...], preferred_element_type=F32).astype(o_ref.dtype)


def _mm_wcast(x, w, ncols, out_dtype, tm_target=MM_TILE, tn_target=MM_TILE, name="matmul_wcast"):
    m, k = x.shape
    tm, tn = _pick(m, tm_target), _pick(ncols, tn_target)
    return pl.pallas_call(
        _mm_wcast_kernel,
        grid=(ncols // tn, m // tm),
        in_specs=[pl.BlockSpec((tm, k), lambda j, i: (i, 0)),
                  pl.BlockSpec((k, tn), lambda j, i: (0, j))],
        out_specs=pl.BlockSpec((tm, tn), lambda j, i: (i, j)),
        out_shape=jax.ShapeDtypeStruct((m, ncols), out_dtype),
        scratch_shapes=[pltpu.VMEM((k, tn), PROJ_DTYPE)],
        compiler_params=_params(("arbitrary", "arbitrary"), VMEM_LIMIT),
        name=name,
    )(x, w)


def _res_proj_kernel(a_ref, w_ref, res_hbm, mod_ref, g_ref, o_ref, res_buf, sem, *, d, gate_chunk):
    i, kk = pl.program_id(0), pl.program_id(1)
    tm = res_buf.shape[0]
    res_copy = pltpu.make_async_copy(res_hbm.at[pl.ds(pl.multiple_of(i * tm, tm), tm), :], res_buf, sem)

    @pl.when(kk == 0)
    def _():
        res_copy.start()
        o_ref[...] = jnp.zeros_like(o_ref)

    o_ref[...] += jnp.dot(a_ref[...], w_ref[...], preferred_element_type=F32)

    @pl.when(kk == pl.num_programs(1) - 1)
    def _():
        res_copy.wait()
        gate = mod_ref[0:1, gate_chunk * d:(gate_chunk + 1) * d] * g_ref[...]
        rb = _pick(tm, EPILOGUE_ROWS, 8)
        for r0 in range(0, tm, rb):
            rows = slice(r0, r0 + rb)
            acc = o_ref[rows, :]
            scale = lax.rsqrt(jnp.mean(acc * acc, axis=-1, keepdims=True) + NORM_EPS)
            o_ref[rows, :] = res_buf[rows, :] + gate * (acc * scale)


def _residual_projection(a, w, res, mod, g_post, gate_chunk, tm_target=SWEEP_ROWS, tk_target=MM_TILE,
                         name="residual_projection"):
    m, k = a.shape
    d = w.shape[1]
    tm, tk = _pick(m, tm_target, 8), _pick(k, tk_target)
    full = lambda z: pl.BlockSpec(z.shape, lambda i, kk: (0,) * z.ndim)
    gp = g_post.reshape(1, d)
    return pl.pallas_call(
        functools.partial(_res_proj_kernel, d=d, gate_chunk=gate_chunk),
        grid=(m // tm, k // tk),
        in_specs=[pl.BlockSpec((tm, tk), lambda i, kk: (i, kk)),
                  pl.BlockSpec((tk, d), lambda i, kk: (kk, 0)),
                  pl.BlockSpec(memory_space=pl.ANY), full(mod), full(gp)],
        out_specs=pl.BlockSpec((tm, d), lambda i, kk: (i, 0)),
        out_shape=jax.ShapeDtypeStruct((m, d), F32),
        scratch_shapes=[pltpu.VMEM((tm, d), F32), pltpu.SemaphoreType.DMA(())],
        compiler_params=_params(("arbitrary", "arbitrary"), VMEM_LIMIT),
        name=name,
    )(a, w, res, mod, gp)


RING = 4


def _rwkv_feat_kernel(pm_hbm, pl_hbm, sel_ref, mu_ref, sell_ref, mul_ref, w0_ref, w2_ref, a0_ref, a2_ref, g2_ref,
                      kk_ref, ka_ref, rk_ref, hs_ref, hst_ref,
                      r_o, v_o, a_o, lw0_o, lw1_o, k0_o, k1_o, b0_o, b1_o, bonus_o, gate_o, ring_m, ring_l, sems,
                      *, ncb, nrows, dr, rd, ra):
    i = pl.program_id(0)
    nb = pl.num_programs(0)
    c = ring_m.shape[1]
    sources = ((pm_hbm, ring_m), (pl_hbm, ring_l))

    def fetch(blk):
        slot = blk % RING
        rows = pl.ds(pl.multiple_of(blk * c, c), c)
        return [pltpu.make_async_copy(src.at[rows, :], ring.at[slot], sems.at[n, slot])
                for n, (src, ring) in enumerate(sources)]

    def start(blk):
        for cp in fetch(blk):
            cp.start()

    def wait(blk):
        for cp in fetch(blk):
            cp.wait()

    @pl.when(i == 0)
    def _():
        start(0)
        start(1)
        start(2)
        wait(0)
        wait(1)

    @pl.when((i > 0) & (i + 1 < nb))
    def _():
        wait(i + 1)

    @pl.when((i > 0) & (i + 2 < nb))
    def _():
        start(i + 2)

    is_ctx = i < ncb
    xi = i - ncb
    s_cur, s_prv, s_nxt = i % RING, jnp.maximum(i - 1, 0) % RING, jnp.minimum(i + 1, nb - 1) % RING

    def shift_lerp(ring, sel, mu):
        cur, prv, nxt = ring[s_cur], ring[s_prv], ring[s_nxt]
        row = lax.broadcasted_iota(jnp.int32, cur.shape, 0)
        prev_last = jnp.where(is_ctx & (i > 0), prv[c - 1:c, :], 0.0)
        next_first = jnp.where(is_ctx & (i < ncb - 1), nxt[0:1, :], 0.0)
        before = jnp.where(row == 0, prev_last, pltpu.roll(cur, 1, 0))
        after = jnp.where(row == c - 1, next_first, pltpu.roll(cur, c - 1, 0))
        above = jnp.where(is_ctx, before, jnp.where(xi > 0, prv, 0.0))
        below = jnp.where(is_ctx, after, jnp.where(xi < nrows - 1, nxt, 0.0))
        shifted = jnp.where(sel == 0, before, jnp.where(sel == 1, after, jnp.where(sel == 2, above, below)))
        return cur + mu * (shifted - cur)

    m = shift_lerp(ring_m, sel_ref[...], mu_ref[...])
    ml = shift_lerp(ring_l, sell_ref[...], mul_ref[...])
    r, k, v = m[:, 0:dr], m[:, dr:2 * dr], m[:, 2 * dr:3 * dr]
    wl = (ml[:, 0:rd], ml[:, rd:2 * rd])
    al = (ml[:, 2 * rd:2 * rd + ra], ml[:, 2 * rd + ra:2 * rd + 2 * ra])
    gl = ml[:, 2 * rd + 2 * ra:]

    hs, hst = hs_ref[...], hst_ref[...]

    def headsum(z):
        return _head_sum(z, hs, hst)

    kkf = k * kk_ref[...]
    kk = kkf * lax.rsqrt(headsum(kkf * kkf) + 1e-12)
    r_o[...] = r
    v_o[...] = v
    a_o[...] = -kk
    for d, (lw_o, k_o, b_o) in enumerate(((lw0_o, k0_o, b0_o), (lw1_o, k1_o, b1_o))):
        wd = _mxu(jnp.tanh(wl[d]).astype(BF16), w2_ref[d]) + w0_ref[d:d + 1, :]
        lw_o[...] = -EXP_M05 * _sigmoid(wd)
        ad = _sigmoid(_mxu(al[d].astype(BF16), a2_ref[d]) + a0_ref[d:d + 1, :])
        k_o[...] = k * (1.0 + (ad - 1.0) * ka_ref[...])
        b_o[...] = kk * ad
    bonus_o[...] = (headsum(r * k * rk_ref[...]) * v).astype(bonus_o.dtype)
    gate_o[...] = _mxu(_sigmoid(gl).astype(BF16), g2_ref[...]).astype(gate_o.dtype)


def _rwkv_features(p_main, p_lora, sel, mu, sel_l, mu_l, w0, w2p, a0, a2p, g2p, k_k, k_a, r_k, lc, l):
    n = p_main.shape[0]
    dr = k_k.shape[0]
    rd, ra = w2p.shape[1], a2p.shape[1]
    nb, ncb = n // CHUNK, lc // CHUNK
    nheads = dr // RWKV_HEAD
    assert nheads <= LANE
    hs, hst = _head_indicators(dr)
    full = lambda a: pl.BlockSpec(a.shape, lambda i: (0,) * a.ndim)
    row1 = lambda a: a.reshape(1, -1)
    consts = [sel, mu, sel_l, mu_l, w0, w2p.astype(BF16), a0, a2p.astype(BF16), g2p.astype(BF16), row1(k_k),
              row1(k_a), row1(r_k), hs, hst]
    assert nb >= 3
    outs = pl.pallas_call(
        functools.partial(_rwkv_feat_kernel, ncb=ncb, nrows=l // CHUNK, dr=dr, rd=rd, ra=ra),
        grid=(nb,),
        in_specs=[pl.BlockSpec(memory_space=pl.ANY)] * 2 + [full(a) for a in consts],
        out_specs=[pl.BlockSpec((CHUNK, dr), lambda i: (i, 0))] * 11,
        out_shape=[jax.ShapeDtypeStruct((n, dr), F32)] * 9 + [jax.ShapeDtypeStruct((n, dr), PROJ_DTYPE)] * 2,
        scratch_shapes=[pltpu.VMEM((RING, CHUNK, p_main.shape[1]), F32), pltpu.VMEM((RING, CHUNK, p_lora.shape[1]), F32),
                        pltpu.SemaphoreType.DMA((2, RING))],
        compiler_params=_params(("arbitrary",), VMEM_LIMIT),
        name="rwkv_features",
    )(p_main, p_lora, *consts)
    return outs


def _block_diag_mask(n, w):
    r = lax.broadcasted_iota(jnp.int32, (n, n), 0)
    c = lax.broadcasted_iota(jnp.int32, (n, n), 1)
    return (r // w) == (c // w)


def _cumsum_rows(z, rev):
    c = z.shape[0]
    t = lax.broadcasted_iota(jnp.int32, (c, c), 0)
    s = lax.broadcasted_iota(jnp.int32, (c, c), 1)
    tri = ((s >= t) if rev else (s <= t)).astype(BF16)
    hi, mid, lo = _split3(z)
    return _mxu(tri, hi) + _mxu(tri, mid) + _mxu(tri, lo)


def _rwkv_chunks(chains):
    c, w = chains[0][0].shape
    nh = w // c
    every = range(len(chains))
    revs = [ch[7] for ch in chains]
    bd = _block_diag_mask(w, c)
    t_i = lax.broadcasted_iota(jnp.int32, (c, w), 0)
    s_i = lax.broadcasted_iota(jnp.int32, (c, w), 1) % c
    eye = (s_i == t_i).astype(F32)
    strict = {False: s_i < t_i, True: s_i > t_i}
    incl = {False: s_i <= t_i, True: s_i >= t_i}

    def expand(z):
        return jnp.where(bd, jnp.concatenate([z] * nh, axis=0), jnp.zeros((), z.dtype))

    def packed_mm(lhs, rhs):
        return _mxu(lhs.astype(BF16), expand(rhs.astype(BF16)))

    bc = [_cumsum_rows(ch[1], ch[7]) for ch in chains]
    ar, bk, v16 = [], [], []
    for i, (r, lw, k, v, a, b, _, _) in enumerate(chains):
        e_in, e_ex, e_ng = jnp.exp(bc[i]), jnp.exp(bc[i] - lw), jnp.exp(-bc[i])
        ar.append(jnp.concatenate([a * e_ex, r * e_in], axis=0).astype(BF16))
        bk.append(jnp.concatenate([b * e_ng, k * e_ng], axis=0).astype(BF16))
        v16.append(v.astype(BF16))
    gb = [_mxu_nt(ar[i], expand(bk[i][:c])) for i in every]
    gk = [_mxu_nt(ar[i], expand(bk[i][c:])) for i in every]
    l_ab = [jnp.where(strict[revs[i]], gb[i][:c], 0.0) for i in every]
    m_rb = [jnp.where(incl[revs[i]], gb[i][c:], 0.0) for i in every]
    lmk = [jnp.concatenate([jnp.where(strict[revs[i]], gk[i][:c], 0.0),
                            jnp.where(incl[revs[i]], gk[i][c:], 0.0)], axis=0) for i in every]
    t_m = [eye + l_ab[i] for i in every]
    p_m = [packed_mm(l_ab[i], l_ab[i]) for i in every]
    for _ in range(int(np.log2(c)) - 2):
        tp = [packed_mm(jnp.concatenate([t_m[i], p_m[i]], axis=0), p_m[i]) for i in every]
        t_m = [t_m[i] + tp[i][:c] for i in every]
        p_m = [tp[i][c:] for i in every]
    t_m = [t_m[i] + packed_mm(t_m[i], p_m[i]) for i in every]

    ars = [_mxu_nt(ar[i], chains[i][6].astype(BF16)) for i in every]
    lm = [_mxu(lmk[i].astype(BF16), expand(v16[i])) for i in every]
    u16 = [packed_mm(t_m[i], ars[i][:c] + lm[i][:c]).astype(BF16) for i in every]
    ys = [ars[i][c:] + _mxu(m_rb[i].astype(BF16), expand(u16[i])) + lm[i][c:] for i in every]
    ds = [_mxu_tn(jnp.concatenate([u16[i], v16[i]], axis=0), bk[i]) for i in every]
    s_new = []
    for i in every:
        last = bc[i][0:1] if revs[i] else bc[i][c - 1:c]
        s_new.append((chains[i][6] + jnp.where(bd, ds[i], 0.0)) * jnp.exp(last))
    return ys, s_new


def _cast_jobs(weights, ng, nb):
    in_specs, out_specs, shapes = [], [], []
    for w, col0, cols in weights:
        rows = w.shape[0]
        n = max(m for m in range(1, ng * nb + 1) if rows % m == 0 and (rows // m) % 16 == 0)
        slab = lambda g, c, n=n: jnp.minimum(g * nb + c, n - 1)
        in_specs.append(pl.BlockSpec((pl.Element(rows // n), pl.Element(cols)),
                                     lambda g, c, s=slab, h=rows // n, col0=col0: (s(g, c) * h, col0)))
        out_specs.append(pl.BlockSpec((rows // n, cols), lambda g, c, s=slab: (s(g, c), 0)))
        shapes.append(jax.ShapeDtypeStruct((rows, cols), PROJ_DTYPE))
    return in_specs, out_specs, shapes


def _run_cast_jobs(src_refs, dst_refs):
    for src, dst in zip(src_refs, dst_refs):
        dst[...] = src[...].astype(dst.dtype)


def _rwkv_scan_kernel(rf, lwf, kf, vf, af, bf, rr, lwr, kr, vr, ar, br, *rest):
    n_cast = (len(rest) - 3) // 2
    yf_o, yr_o = rest[n_cast:n_cast + 2]
    s_ref = rest[-1]

    @pl.when(pl.program_id(1) == 0)
    def _():
        s_ref[...] = jnp.zeros_like(s_ref)

    ngs = rf.shape[1] // GROUP_W
    chains = []
    for d, refs in enumerate(((rf, lwf, kf, vf, af, bf), (rr, lwr, kr, vr, ar, br))):
        for g in range(ngs):
            sl = slice(g * GROUP_W, (g + 1) * GROUP_W)
            chains.append(tuple(ref[:, sl] for ref in refs) + (s_ref[d, g], d == 1))
    ys, s_new = _rwkv_chunks(chains)
    _run_cast_jobs(rest[:n_cast], rest[n_cast + 2:-1])
    for d, o_ref in enumerate((yf_o, yr_o)):
        for g in range(ngs):
            o_ref[:, g * GROUP_W:(g + 1) * GROUP_W] = ys[d * ngs + g]
            s_ref[d, g] = s_new[d * ngs + g]


def _rev_chunk(c, ncb, nb):
    return jnp.where(c < ncb, ncb - 1 - c, nb - 1 - (c - ncb))


def _rwkv_scan(r, v, a, lw0, lw1, k0, k1, b0, b1, lc, cast_weights):
    n, dr = r.shape
    nb, ncb = n // CHUNK, lc // CHUNK
    bw = _pick(dr, RWKV_GROUPS_PER_STEP * GROUP_W, GROUP_W)
    fwd = pl.BlockSpec((CHUNK, bw), lambda g, c: (c, g))
    rev = pl.BlockSpec((CHUNK, bw), lambda g, c: (_rev_chunk(c, ncb, nb), g))
    cast_in, cast_out, cast_shapes = _cast_jobs(cast_weights, dr // bw, nb)
    return pl.pallas_call(
        _rwkv_scan_kernel,
        grid=(dr // bw, nb),
        in_specs=[fwd] * 6 + [rev] * 6 + cast_in,
        out_specs=[fwd, rev] + cast_out,
        out_shape=[jax.ShapeDtypeStruct((n, dr), F32)] * 2 + cast_shapes,
        scratch_shapes=[pltpu.VMEM((2, bw // GROUP_W, GROUP_W, GROUP_W), F32)],
        compiler_params=_params(("arbitrary", "arbitrary"), VMEM_LIMIT),
        name="rwkv_scan",
    )(r, lw0, k0, v, a, b0, r, lw1, k1, v, a, b1, *[w for w, _, _ in cast_weights])


def _hgrn_chunk(q, k, v, lf, s_vk, rev):
    c, w = q.shape
    nsb = c // SUB
    bc = _cumsum_rows(lf, rev)
    last = bc[0:1] if rev else bc[c - 1:c]
    bdh = _block_diag_mask(w, HGRN_HEAD)
    v16 = v.astype(BF16)
    o_inter = _mxu_nt((q * jnp.exp(bc)).astype(BF16), s_vk.astype(BF16))
    s_new = s_vk * jnp.exp(last) + jnp.where(bdh, _mxu_tn(v16, (k * jnp.exp(last - bc)).astype(BF16)), 0.0)

    lane2 = lax.broadcasted_iota(jnp.int32, (c, w), 1)
    row2 = lax.broadcasted_iota(jnp.int32, (c, w), 0)
    zpad = jnp.zeros((HGRN_HEAD - c, w), BF16)

    def expand(z):
        parts = []
        for h in range(w // HGRN_HEAD):
            parts += [jnp.where(lane2 // HGRN_HEAD == h, z, jnp.zeros((), BF16)), zpad]
        return jnp.concatenate(parts, axis=0)

    ones_bd = bdh.astype(BF16)
    t3 = lax.broadcasted_iota(jnp.int32, (SUB, SUB, w), 0)
    s3 = lax.broadcasted_iota(jnp.int32, (SUB, SUB, w), 1)
    l3 = lax.broadcasted_iota(jnp.int32, (SUB, SUB, w), 2) % HGRN_HEAD
    causal = (s3 >= t3) if rev else (s3 <= t3)
    a_rows = []
    for i in range(nsb):
        lo, hi = SUB * i, SUB * (i + 1)
        qi, ki, bi = q[lo:hi], k[lo:hi], bc[lo:hi]
        d3 = bi[:, None, :] - bi[None, :, :]
        x3 = jnp.where(causal, (qi[:, None, :] * ki[None, :, :]) * jnp.exp(jnp.minimum(d3, 0.0)), 0.0)
        r3 = _mxu(x3.reshape(SUB * SUB, w).astype(BF16), ones_bd).reshape(SUB, SUB, w)
        a_i = jnp.sum(jnp.where(l3 == s3 + lo, r3, 0.0), axis=1)
        if (not rev and i > 0) or (rev and i < nsb - 1):
            beta = bc[hi:hi + 1] if rev else bc[lo - 1:lo]
            earlier = (row2 >= hi) if rev else (row2 < lo)
            kp = jnp.where(earlier, k * jnp.exp(jnp.minimum(beta - bc, 0.0)), 0.0)
            a_i = a_i + _mxu_nt((qi * jnp.exp(bi - beta)).astype(BF16), expand(kp.astype(BF16)))
        a_rows.append(a_i)
    o = o_inter + _mxu(jnp.concatenate(a_rows, axis=0).astype(BF16), expand(v16))
    return o, s_new


def _hgrn_chunks_bounded(chains):
    c, w = chains[0][0].shape
    every = range(len(chains))
    revs = [ch[5] for ch in chains]
    bdh = _block_diag_mask(w, HGRN_HEAD)
    lane2 = lax.broadcasted_iota(jnp.int32, (c, w), 1)
    t_i = lax.broadcasted_iota(jnp.int32, (c, w), 0)
    s_i = lane2 % HGRN_HEAD
    incl = {False: s_i <= t_i, True: s_i >= t_i}
    zpad = jnp.zeros((HGRN_HEAD - c, w), BF16)

    def expand(z):
        parts = []
        for h in range(w // HGRN_HEAD):
            parts += [jnp.where(lane2 // HGRN_HEAD == h, z, jnp.zeros((), BF16)), zpad]
        return jnp.concatenate(parts, axis=0)

    bc = [_cumsum_rows(ch[3], ch[5]) for ch in chains]
    last = [bc[i][0:1] if revs[i] else bc[i][c - 1:c] for i in every]
    qt = [(chains[i][0] * jnp.exp(bc[i])).astype(BF16) for i in every]
    kt = [(chains[i][1] * jnp.exp(-bc[i])).astype(BF16) for i in every]
    kd = [(chains[i][1] * jnp.exp(last[i] - bc[i])).astype(BF16) for i in every]
    v16 = [chains[i][2].astype(BF16) for i in every]
    o_inter = [_mxu_nt(qt[i], chains[i][4].astype(BF16)) for i in every]
    a = [jnp.where(incl[revs[i]], _mxu_nt(qt[i], expand(kt[i])), 0.0).astype(BF16) for i in every]
    o = [o_inter[i] + _mxu(a[i], expand(v16[i])) for i in every]
    ds = [_mxu_tn(v16[i], kd[i]) for i in every]
    s_new = [chains[i][4] * jnp.exp(last[i]) + jnp.where(bdh, ds[i], 0.0) for i in every]
    return o, s_new


def _hgrn_scan_kernel(qf, ff, vf, qr, fr, vr, lg_ref, *rest, layer):
    n_cast = (len(rest) - 3) // 2
    of_o, or_o = rest[n_cast:n_cast + 2]
    s_ref = rest[-1]

    @pl.when(pl.program_id(1) == 0)
    def _():
        s_ref[...] = jnp.zeros_like(s_ref)

    lg = lg_ref[...]
    e = jnp.exp(lg - jnp.max(lg, axis=0, keepdims=True))
    lb = jnp.sum(e[:layer + 1], axis=0) / jnp.sum(e, axis=0)
    ngs = qf.shape[1] // GROUP_W
    chunk = qf.shape[0]

    def chains():
        out = []
        for d, (q_ref, f_ref, v_ref) in enumerate(((qf, ff, vf), (qr, fr, vr))):
            fd = lb[d:d + 1, :] + (1.0 - lb[d:d + 1, :]) * _sigmoid(f_ref[...])
            qv = q_ref[...]
            qh, kh, vh, lf = qv * _sigmoid(qv), 1.0 - fd, v_ref[...], jnp.log(fd)
            for g in range(ngs):
                sl = slice(g * GROUP_W, (g + 1) * GROUP_W)
                out.append((qh[:, sl], kh[:, sl], vh[:, sl], lf[:, sl], s_ref[d, g], d == 1))
        return out

    def emit(o, s_new):
        for d, o_ref in enumerate((of_o, or_o)):
            for g in range(ngs):
                o_ref[:, g * GROUP_W:(g + 1) * GROUP_W] = o[d * ngs + g]
                s_ref[d, g] = s_new[d * ngs + g]
        _run_cast_jobs(rest[:n_cast], rest[n_cast + 2:-1])

    bounded = chunk * jnp.min(jnp.log(lb)) >= -MAX_CHUNK_LOG_DECAY

    @pl.when(bounded)
    def _():
        emit(*_hgrn_chunks_bounded(chains()))

    @pl.when(jnp.logical_not(bounded))
    def _():
        res = [_hgrn_chunk(*ch) for ch in chains()]
        emit([r[0] for r in res], [r[1] for r in res])


def _hgrn_scan(p_h, lb_logits, lc, layer, cast_weights):
    n = p_h.shape[0]
    dh = lb_logits.shape[2]
    nb, ncb = n // CHUNK, lc // CHUNK
    bw = _pick(dh, HGRN_GROUPS_PER_STEP * GROUP_W, GROUP_W)
    ng = dh // bw
    blk = (CHUNK, bw)
    fwd = lambda sec: pl.BlockSpec(blk, lambda g, c: (c, sec * ng + g))
    rev = lambda sec: pl.BlockSpec(blk, lambda g, c: (_rev_chunk(c, ncb, nb), sec * ng + g))
    cast_in, cast_out, cast_shapes = _cast_jobs(cast_weights, ng, nb)
    return pl.pallas_call(
        functools.partial(_hgrn_scan_kernel, layer=layer),
        grid=(ng, nb),
        in_specs=[fwd(0), fwd(1), fwd(3), rev(0), rev(2), rev(3),
                  pl.BlockSpec((lb_logits.shape[0], 2, bw), lambda g, c: (0, 0, g))] + cast_in,
        out_specs=[pl.BlockSpec(blk, lambda g, c: (c, g)),
                   pl.BlockSpec(blk, lambda g, c: (_rev_chunk(c, ncb, nb), g))] + cast_out,
        out_shape=[jax.ShapeDtypeStruct((n, dh), F32)] * 2 + cast_shapes,
        scratch_shapes=[pltpu.VMEM((2, bw // GROUP_W, GROUP_W, GROUP_W), F32)],
        compiler_params=_params(("arbitrary", "arbitrary"), VMEM_LIMIT),
        name="hgrn_scan",
    )(p_h, p_h, p_h, p_h, p_h, p_h, lb_logits, *[w for w, _, _ in cast_weights])


MIX_RING = 3


def _mix_out_kernel(yf_hbm, yr_hbm, bonus_hbm, gate_hbm, of_hbm, or_hbm, ph_hbm, lng, lnb, hng, hs_ref, hst_ref, u_o,
                    *scratch, dr, dh, off, g_col):
    rings, sems = scratch[:-1], scratch[-1]
    sources = (yf_hbm, yr_hbm, bonus_hbm, gate_hbm, of_hbm, or_hbm, ph_hbm)
    i, nb = pl.program_id(0), pl.num_programs(0)
    tb = u_o.shape[0]

    def fetch(blk):
        slot = blk % MIX_RING
        rows = pl.ds(pl.multiple_of((blk + off) * tb, tb), tb)
        copies = []
        for n, (src, ring) in enumerate(zip(sources, rings)):
            view = src.at[rows, pl.ds(g_col, dh)] if src is ph_hbm else src.at[rows, :]
            copies.append(pltpu.make_async_copy(view, ring.at[slot], sems.at[n, slot]))
        return copies

    @pl.when(i == 0)
    def _():
        for blk in range(MIX_RING - 1):
            for cp in fetch(blk):
                cp.start()

    @pl.when(i + MIX_RING - 1 < nb)
    def _():
        for cp in fetch(i + MIX_RING - 1):
            cp.start()

    for cp in fetch(i):
        cp.wait()
    slot = i % MIX_RING
    yf, yr, bonus, gate, of, orv, g_ref = (ring.at[slot] for ring in rings)

    hs, hst = hs_ref[...], hst_ref[...]

    def headmean(z):
        return _head_sum(z, hs, hst) * (1.0 / RWKV_HEAD)

    y = yf[...] + yr[...]
    yc = y - headmean(y)
    yn = yc * lax.rsqrt(headmean(yc * yc) + GN_EPS) * lng[...] + lnb[...]
    u_o[:, 0:dr] = ((yn + bonus[...].astype(F32)) * gate[...].astype(F32)).astype(u_o.dtype)
    o = of[...] + orv[...]
    g = g_ref[...]
    sg = g * _sigmoid(g)
    for h in range(dh // HGRN_HEAD):
        sl = slice(h * HGRN_HEAD, (h + 1) * HGRN_HEAD)
        oh = o[:, sl]
        on = oh * lax.rsqrt(jnp.mean(oh * oh, axis=-1, keepdims=True) + NORM_EPS) * hng[...]
        u_o[:, dr + h * HGRN_HEAD:dr + (h + 1) * HGRN_HEAD] = (on * sg[:, sl]).astype(u_o.dtype)


def _mix_out(yf, yr, bonus, gate, of, orv, p_h, ln_g, ln_b, hg, lc, l):
    dr, dh = yf.shape[1], of.shape[1]
    tb = _pick(int(np.gcd(lc, l)), EPILOGUE_ROWS, 8)
    off = lc // tb
    hs, hst = _head_indicators(dr)
    full = lambda a: pl.BlockSpec(a.shape, lambda i: (0,) * a.ndim)
    consts = [ln_g.reshape(1, dr), ln_b.reshape(1, dr), hg.reshape(1, HGRN_HEAD), hs, hst]
    streams = [yf, yr, bonus, gate, of, orv]
    assert l // tb >= MIX_RING - 1
    return pl.pallas_call(
        functools.partial(_mix_out_kernel, dr=dr, dh=dh, off=off, g_col=4 * dh),
        grid=(l // tb,),
        in_specs=[pl.BlockSpec(memory_space=pl.ANY)] * 7 + [full(a) for a in consts],
        out_specs=pl.BlockSpec((tb, dr + dh), lambda i: (i, 0)),
        out_shape=jax.ShapeDtypeStruct((l, dr + dh), PROJ_DTYPE),
        scratch_shapes=[pltpu.VMEM((MIX_RING, tb, s.shape[1]), s.dtype) for s in streams]
                       + [pltpu.VMEM((MIX_RING, tb, dh), p_h.dtype), pltpu.SemaphoreType.DMA((7, MIX_RING))],
        compiler_params=_params(("arbitrary",), VMEM_LIMIT),
        name="mix_out",
    )(*streams, p_h, *consts)


def _res_mix_kernel(ux_ref, x_ref, mod_ref, gpost_ref, gpre_ref, x1_o, h2_o, *, d):
    ux = ux_ref[...]
    nrm = ux * lax.rsqrt(jnp.mean(ux * ux, axis=-1, keepdims=True) + NORM_EPS) * gpost_ref[...]
    x1 = x_ref[...] + mod_ref[0:1, 2 * d:3 * d] * nrm
    x1_o[...] = x1
    hn = x1 * lax.rsqrt(jnp.mean(x1 * x1, axis=-1, keepdims=True) + NORM_EPS) * gpre_ref[...]
    h2_o[...] = (hn * (1.0 + mod_ref[0:1, 4 * d:5 * d]) + mod_ref[0:1, 3 * d:4 * d]).astype(h2_o.dtype)


def _res_mix(ux, x2, mod, g_post, g_pre):
    l, d = x2.shape
    tb = _pick(l, ROW_BLOCK, 8)
    rows = pl.BlockSpec((tb, d), lambda i: (i, 0))
    full = lambda a: pl.BlockSpec(a.shape, lambda i: (0,) * a.ndim)
    gp, gq = g_post.reshape(1, d), g_pre.reshape(1, d)
    return pl.pallas_call(
        functools.partial(_res_mix_kernel, d=d),
        grid=(l // tb,),
        in_specs=[rows, rows, full(mod), full(gp), full(gq)],
        out_specs=[rows, rows],
        out_shape=[jax.ShapeDtypeStruct((l, d), F32), jax.ShapeDtypeStruct((l, d), PROJ_DTYPE)],
        compiler_params=_params(("arbitrary",), VMEM_LIMIT),
        name="residual_mix",
    )(ux, x2, mod, gp, gq)


def _pad_cols(a, width):
    return jnp.pad(a, ((0, 0), (0, width - a.shape[1])))


def _pad_rows(a, height):
    return jnp.pad(a, ((0, height - a.shape[0]), (0, 0)))


def _round_up(n, m):
    return (n + m - 1) // m * m


def kernel(x, c, ctx, c_ctx, w_ada, b_ada, g_mix_pre, g_mix_post, g_ffn_pre, g_ffn_post, w_in, mu_shift, w0, w2, a0,
           a2, g2, k_k, k_a, r_k, ln_x_g, ln_x_b, hgrn_lb_logits, hgrn_norm_g, w_out, w_ff1, w_ff2):
    assert x.shape[0] == 1 and w_in.shape[0] == 1, "single batch, single layer"
    layer = 0
    x2, ctx2 = x[0], ctx[0]
    l, d = x2.shape
    lc = ctx2.shape[0]
    dr, dh = k_k.shape[1], hgrn_lb_logits.shape[2]
    rd_raw, ra_raw, rg_raw = w2.shape[2], a2.shape[2], g2.shape[1]
    rd, ra, rg = (_round_up(v, LANE) for v in (rd_raw, ra_raw, rg_raw))
    assert l % GRID_W == 0 and lc % CHUNK == 0 and dr % GROUP_W == 0 and dh % GROUP_W == 0

    cvec = jnp.concatenate([c, c_ctx[None, :], jnp.zeros((6, d), F32)], axis=0)
    mod = _ada(cvec, w_ada[layer], b_ada[layer])

    wi = w_in[layer]
    mu = mu_shift[layer][None, :]
    sel = (jnp.arange(mu.shape[1], dtype=jnp.int32) % 4)[None, :]
    o = 3 * dr
    cuts = []
    for raw, padded in ((rd_raw, rd), (rd_raw, rd), (ra_raw, ra), (ra_raw, ra), (rg_raw, rg)):
        cuts.append((o, o + raw, padded))
        o += raw
    rw_cols = o
    regroup = lambda a: jnp.concatenate([_pad_cols(a[:, s:e], wd) for s, e, wd in cuts], axis=1)
    w_lora = regroup(wi)
    w2p = jnp.stack([_pad_rows(w2[layer, dd], rd) for dd in range(2)])
    a2p = jnp.stack([_pad_rows(a2[layer, dd], ra) for dd in range(2)])
    g2p = _pad_rows(g2[layer], rg)

    h = _prep(ctx2, x2, g_mix_pre[layer], mod)
    p_main = _mm_wcast(h, wi, 3 * dr, F32, tm_target=IN_PROJ_TILE, tn_target=IN_PROJ_TILE, name="in_proj_rkv")
    p_lora = _mm_wcast(h, w_lora, w_lora.shape[1], F32, tm_target=IN_PROJ_TILE, name="in_proj_lora")

    (r, v, a, lw0, lw1, k0, k1, b0, b1, bonus, gate) = _rwkv_features(
        p_main, p_lora, sel[:, :3 * dr], mu[:, :3 * dr], regroup(sel), regroup(mu), w0[layer], w2p, a0[layer], a2p,
        g2p, k_k[layer], k_a[layer], r_k[layer].reshape(-1), lc, l)
    whole = lambda w: (w, 0, w.shape[1])
    yf, yr, w_h, w_up = _rwkv_scan(r, v, a, lw0, lw1, k0, k1, b0, b1, lc,
                                   [(wi, rw_cols, wi.shape[1] - rw_cols), whole(w_ff1[layer])])
    p_h = _mm(h, w_h, F32, tm_target=IN_PROJ_TILE, name="in_proj_hgrn")
    of, orv, w_down, w_o = _hgrn_scan(p_h, hgrn_lb_logits, lc, layer, [whole(w_ff2[layer]), whole(w_out[layer])])

    u = _mix_out(yf, yr, bonus, gate, of, orv, p_h, ln_x_g[layer], ln_x_b[layer], hgrn_norm_g[layer], lc, l)
    ux = _mm(u, w_o, F32, name="out_proj")
    x1, h2 = _res_mix(ux, x2, mod, g_mix_post[layer], g_ffn_pre[layer])
    act = _mm(h2, w_up, PROJ_DTYPE, act="relu2", name="ffn_up")
    out = _residual_projection(act, w_down, x1, mod, g_ffn_post[layer], 5, name="ffn_down")
    return out[None]
```
